```python
import math
import jax, jax.numpy as jnp
from jax import lax
import numpy as np

D_MODEL = 1024
BATCH = 8
SEQ = 2048
DEPTH = 2

A_HEADS = 4
A_DH = 64
A_DV = 2 * A_DH
A_QW = A_HEADS * 2 * A_DH
A_VW = A_HEADS * A_DV
A_BLOCK = 128
R_HEADS = 8
R_N = 64
R_W = R_HEADS * R_N
R_LORA_W = 64
R_LORA_A = 64
R_LORA_G = 128
R_COLS = 3 * R_W + R_LORA_W + R_LORA_A + R_LORA_G
RWKV_GN_EPS = 64e-5
G_HEADS = 4
G_DK = 64
G_DV = 128
G_KW = G_HEADS * G_DK
G_VW = G_HEADS * G_DV
G_LORA = 16
G_TAU = 16.0
G_CHUNK = 64
G_COLS = 2 * G_KW + G_VW + G_LORA + G_VW
A_COLS = 2 * A_QW + A_VW
GATE_COLS = 3 * D_MODEL
N_IN = A_COLS + R_COLS + G_COLS + GATE_COLS
N_GROUPS = 4
EXP_PER_GROUP = 8
N_EXPERTS = N_GROUPS * EXP_PER_GROUP
TOP_K = 2
D_EXPERT = 512
MOE_BLOCK = 128
EPS = 1e-6

kernel_name = 'hybrid_diffattn_rwkv7_gla_hmoe_adaln'


def _split(t, widths):
    cuts = [int(i) for i in np.cumsum(widths)[:-1]]
    return jnp.split(t, cuts, axis=-1)


def rms_norm(x, g, eps=EPS):
    xf = x.astype(jnp.float32)
    y = xf * lax.rsqrt(jnp.mean(xf * xf, axis=-1, keepdims=True) + eps)
    return (y * g.astype(jnp.float32)).astype(x.dtype)


def group_norm(x, g, b, eps):
    xf = x.astype(jnp.float32)
    mu = jnp.mean(xf, axis=-1, keepdims=True)
    var = jnp.mean(jnp.square(xf - mu), axis=-1, keepdims=True)
    return (xf - mu) * lax.rsqrt(var + eps) * g.astype(jnp.float32) + b.astype(jnp.float32)


def token_shift(t):
    return jnp.pad(t, ((0, 0), (1, 0), (0, 0)))[:, :-1]


def diff_attention(q, k, v, qn_g, kn_g, lam_vecs, subln_g, lambda_init):
    B, S = q.shape[:2]
    q = rms_norm(q, qn_g) * (A_DH ** -0.5)
    k = rms_norm(k, kn_g)
    q = q.transpose(0, 2, 3, 1, 4)
    k = k.transpose(0, 2, 3, 1, 4)
    v = v.transpose(0, 2, 1, 3)
    lv = lam_vecs.astype(jnp.float32)
    lam = jnp.exp(jnp.sum(lv[0] * lv[1])) - jnp.exp(jnp.sum(lv[2] * lv[3])) + lambda_init
    slopes = jnp.asarray([2.0 ** (-8.0 * (i + 1) / A_HEADS) for i in range(A_HEADS)], jnp.float32)
    kpos = jnp.arange(S)

    def block(i):
        q_blk = lax.dynamic_slice_in_dim(q, i * A_BLOCK, A_BLOCK, axis=3)
        s = jnp.einsum('bhcqd,bhckd->bhcqk', q_blk, k).astype(jnp.float32)
        qpos = i * A_BLOCK + jnp.arange(A_BLOCK)
        dist = qpos[:, None] - kpos[None, :]
        s = s - (slopes[:, None, None] * dist.astype(jnp.float32))[None, :, None]
        s = jnp.where(dist >= 0, s, -jnp.inf)
        p = jax.nn.softmax(s, axis=-1)
        a = p[:, :, 0] - lam * p[:, :, 1]
        return jnp.einsum('bhqk,bhkd->bhqd', a.astype(v.dtype), v)

    o = lax.map(block, jnp.arange(S // A_BLOCK))
    o = o.transpose(1, 0, 3, 2, 4).reshape(B, S, A_HEADS, A_DV)
    o = rms_norm(o, subln_g) * (1.0 - lambda_init)
    return o.reshape(B, S, A_VW)


def rwkv7_mix(slab, mu, w_up, w0, a_up, a0, g_up, k_k, k_a, r_k, lnx_g, lnx_b):
    B, S = slab.shape[:2]
    f32 = jnp.float32
    slab = slab + (token_shift(slab) - slab) * mu
    r, k, v, wd, ad, gd = _split(slab, [R_W, R_W, R_W, R_LORA_W, R_LORA_A, R_LORA_G])
    w = -jax.nn.softplus(-(w0 + jnp.tanh(wd) @ w_up)) - 0.5
    decay = jnp.exp(-jnp.exp(w.astype(f32)))
    a = jax.nn.sigmoid(a0 + ad @ a_up)
    g = jax.nn.sigmoid(gd) @ g_up
    hs = lambda t: t.reshape(B, S, R_HEADS, R_N).astype(f32)
    r, k, v, decay, a = hs(r), hs(k), hs(v), hs(decay), hs(a)
    kk = k * k_k.reshape(R_HEADS, R_N).astype(f32)
    kk = kk / jnp.maximum(jnp.sqrt(jnp.sum(kk * kk, axis=-1, keepdims=True)), 1e-12)
    k = k * (1.0 + (a - 1.0) * k_a.reshape(R_HEADS, R_N).astype(f32))

    def step(state, inp):
        r_t, w_t, k_t, v_t, kk_t, a_t = inp
        sa = jnp.einsum('bhvk,bhk->bhv', state, -kk_t)
        state = (state * w_t[:, :, None, :] + sa[..., None] * (kk_t * a_t)[:, :, None, :]
                 + v_t[..., None] * k_t[:, :, None, :])
        return state, jnp.einsum('bhvk,bhk->bhv', state, r_t)

    xs = tuple(t.swapaxes(0, 1) for t in (r, decay, k, v, kk, a))
    s0 = jnp.zeros((B, R_HEADS, R_N, R_N), f32)
    _, y = lax.scan(step, s0, xs)
    y = y.swapaxes(0, 1)
    y = group_norm(y, lnx_g.reshape(R_HEADS, R_N), lnx_b.reshape(R_HEADS, R_N), RWKV_GN_EPS)
    bonus = jnp.sum(r * k * r_k.astype(f32), axis=-1, keepdims=True) * v
    out = (y + bonus) * g.reshape(B, S, R_HEADS, R_N).astype(f32)
    return out.reshape(B, S, R_W).astype(slab.dtype)


def gla_mix(q, k, v, log_alpha, gate, norm_g):
    B, S = q.shape[:2]
    nc = S // G_CHUNK
    f32 = jnp.float32

    def chunks(t, d):
        return t.reshape(B, nc, G_CHUNK, G_HEADS, d).transpose(1, 0, 3, 2, 4).astype(f32)

    qc = chunks(q, G_DK) * (G_DK ** -0.5)
    kc, vc, lc = chunks(k, G_DK), chunks(v, G_DV), chunks(log_alpha, G_DK)
    causal = jnp.tril(jnp.ones((G_CHUNK, G_CHUNK), dtype=bool))[:, :, None]

    def step(state, inp):
        q_, k_, v_, l_ = inp
        b = jnp.cumsum(l_, axis=2)
        o_inter = jnp.einsum('bhid,bhde->bhie', q_ * jnp.exp(b), state)
        diff = b[:, :, :, None, :] - b[:, :, None, :, :]
        dec = jnp.exp(jnp.where(causal, diff, -jnp.inf))
        scores = jnp.einsum('bhid,bhjd,bhijd->bhij', q_, k_, dec)
        o_intra = jnp.einsum('bhij,bhje->bhie', scores, v_)
        b_last = b[:, :, -1:, :]
        state = (state * jnp.exp(b_last)[:, :, 0, :, None]
                 + jnp.einsum('bhjd,bhje->bhde', k_ * jnp.exp(b_last - b), v_))
        return state, o_inter + o_intra

    s0 = jnp.zeros((B, G_HEADS, G_DK, G_DV), f32)
    _, o = lax.scan(step, s0, (qc, kc, vc, lc))
    o = o.transpose(1, 0, 3, 2, 4).reshape(B, S, G_HEADS, G_DV)
    o = rms_norm(o, norm_g) * jax.nn.silu(gate.astype(f32))
    return o.reshape(B, S, G_VW).astype(q.dtype)


def hier_moe(h, rg_w, rg_b, re_w, re_b, w_gate, w_up, w_down):
    B, S, D = h.shape
    T = B * S
    A = T * TOP_K
    xt = h.reshape(T, D)
    g_logits = (xt @ rg_w + rg_b).astype(jnp.float32)
    grp = jnp.argmax(g_logits, axis=-1)
    g_prob = jnp.take_along_axis(jax.nn.softmax(g_logits, axis=-1), grp[:, None], axis=1)
    e_logits = (xt @ re_w + re_b).astype(jnp.float32).reshape(T, N_GROUPS, EXP_PER_GROUP)
    e_in = jnp.take_along_axis(e_logits, grp[:, None, None], axis=1)[:, 0]
    top_v, top_i = lax.top_k(e_in, TOP_K)
    w_tok = g_prob * jax.nn.softmax(top_v, axis=-1)
    eid = (grp[:, None] * EXP_PER_GROUP + top_i).reshape(-1).astype(jnp.int32)
    tok = jnp.repeat(jnp.arange(T, dtype=jnp.int32), TOP_K)
    wts = w_tok.reshape(-1)
    order = jnp.argsort(eid)
    eid_s, tok_s, w_s = eid[order], tok[order], wts[order]
    counts = jax.ops.segment_sum(jnp.ones_like(eid), eid, num_segments=N_EXPERTS)
    padded = (counts + MOE_BLOCK - 1) // MOE_BLOCK * MOE_BLOCK
    pad_end = jnp.cumsum(padded)
    pad_start = pad_end - padded
    start = jnp.cumsum(counts) - counts
    dest = pad_start[eid_s] + jnp.arange(A, dtype=jnp.int32) - start[eid_s]
    n_blocks = -(-A // MOE_BLOCK) + N_EXPERTS
    R = n_blocks * MOE_BLOCK
    row_tok = jnp.zeros((R,), jnp.int32).at[dest].set(tok_s)
    blk_exp = jnp.minimum(jnp.searchsorted(pad_end, jnp.arange(n_blocks) * MOE_BLOCK, side='right'),
                          N_EXPERTS - 1)
    xin = xt[row_tok].reshape(n_blocks, MOE_BLOCK, D)

    def run(args):
        xb, e = args
        hid = jax.nn.silu(xb @ w_gate[e]) * (xb @ w_up[e])
        return hid @ w_down[e]

    y = lax.map(run, (xin, blk_exp)).reshape(R, D)
    out = jnp.zeros((T, D), h.dtype).at[tok_s].add(y[dest] * w_s[:, None].astype(h.dtype))
    return out.reshape(B, S, D)


def setup_inputs(seed: int = 0) -> dict:
    key = jax.random.key(seed)
    ks = iter(jax.random.split(key, 64))
    L, D = DEPTH, D_MODEL
    nrm = lambda shape, s: jax.random.normal(next(ks), shape, jnp.float32) * s
    ones_n = lambda shape: 1.0 + nrm(shape, 0.02)
    ada_b = nrm((L, 6 * D), 0.05)
    ada_b = ada_b.at[:, 2 * D:3 * D].add(0.5).at[:, 5 * D:6 * D].add(0.5)
    return {
        'x': nrm((BATCH, SEQ, D), 1.0),
        'c': nrm((BATCH, D), 1.0),
        'ada_w': nrm((L, D, 6 * D), 0.2 * D ** -0.5),
        'ada_b': ada_b,
        'norm1_g': ones_n((L, D)),
        'norm2_g': ones_n((L, D)),
        'w_in': nrm((L, D, N_IN), D ** -0.5),
        'attn_qn_g': ones_n((L, A_DH)),
        'attn_kn_g': ones_n((L, A_DH)),
        'attn_lambda': nrm((L, 4, A_DH), 0.1),
        'attn_subln_g': ones_n((L, A_DV)),
        'rwkv_mu': jax.random.uniform(next(ks), (L, R_COLS), jnp.float32),
        'rwkv_w_up': nrm((L, R_LORA_W, R_W), 0.1 * R_LORA_W ** -0.5),
        'rwkv_w0': jax.random.uniform(next(ks), (L, R_W), jnp.float32, -6.5, -1.5),
        'rwkv_a_up': nrm((L, R_LORA_A, R_W), R_LORA_A ** -0.5),
        'rwkv_a0': nrm((L, R_W), 0.1),
        'rwkv_g_up': nrm((L, R_LORA_G, R_W), R_LORA_G ** -0.5),
        'rwkv_k_k': 0.85 + nrm((L, R_W), 0.05),
        'rwkv_k_a': 1.0 + nrm((L, R_W), 0.05),
        'rwkv_r_k': nrm((L, R_HEADS, R_N), 0.1),
        'rwkv_lnx_g': ones_n((L, R_W)),
        'rwkv_lnx_b': nrm((L, R_W), 0.02),
        'gla_alpha_up': nrm((L, G_LORA, G_KW), G_LORA ** -0.5),
        'gla_alpha_b': 1.0 + nrm((L, G_KW), 0.5),
        'gla_norm_g': ones_n((L, G_DV)),
        'proj_attn': nrm((L, A_VW, D), A_VW ** -0.5),
        'proj_rwkv': nrm((L, R_W, D), R_W ** -0.5),
        'proj_gla': nrm((L, G_VW, D), G_VW ** -0.5),
        'w_out': nrm((L, D, D), D ** -0.5),
        'router_grp_w': nrm((L, D, N_GROUPS), D ** -0.5),
        'router_grp_b': nrm((L, N_GROUPS), 0.01),
        'router_exp_w': nrm((L, D, N_EXPERTS), D ** -0.5),
        'router_exp_b': nrm((L, N_EXPERTS), 0.01),
        'exp_w_gate': nrm((L, N_EXPERTS, D, D_EXPERT), D ** -0.5),
        'exp_w_up': nrm((L, N_EXPERTS, D, D_EXPERT), D ** -0.5),
        'exp_w_down': nrm((L, N_EXPERTS, D_EXPERT, D), D_EXPERT ** -0.5),
    }


def reference(x, c, ada_w, ada_b, norm1_g, norm2_g, w_in, attn_qn_g, attn_kn_g, attn_lambda,
              attn_subln_g, rwkv_mu, rwkv_w_up, rwkv_w0, rwkv_a_up, rwkv_a0, rwkv_g_up, rwkv_k_k,
              rwkv_k_a, rwkv_r_k, rwkv_lnx_g, rwkv_lnx_b, gla_alpha_up, gla_alpha_b, gla_norm_g,
              proj_attn, proj_rwkv, proj_gla, w_out, router_grp_w, router_grp_b, router_exp_w,
              router_exp_b, exp_w_gate, exp_w_up, exp_w_down):
    B, S, D = x.shape
    c_act = jax.nn.silu(c)
    for l in range(DEPTH):
        lambda_init = 0.8 - 0.6 * math.exp(-0.3 * l)
        mod = (c_act @ ada_w[l] + ada_b[l])[:, None, :]
        sh1, sc1, gt1, sh2, sc2, gt2 = jnp.split(mod, 6, axis=-1)
        h = rms_norm(x, norm1_g[l]) * (1.0 + sc1) + sh1
        p = h @ w_in[l]
        p_attn, p_rwkv, p_gla, p_gate = _split(p, [A_COLS, R_COLS, G_COLS, GATE_COLS])
        aq, ak, av = _split(p_attn, [A_QW, A_QW, A_VW])
        o_a = diff_attention(aq.reshape(B, S, A_HEADS, 2, A_DH), ak.reshape(B, S, A_HEADS, 2, A_DH),
                             av.reshape(B, S, A_HEADS, A_DV), attn_qn_g[l], attn_kn_g[l],
                             attn_lambda[l], attn_subln_g[l], lambda_init)
        o_r = rwkv7_mix(p_rwkv, rwkv_mu[l], rwkv_w_up[l], rwkv_w0[l], rwkv_a_up[l], rwkv_a0[l],
                        rwkv_g_up[l], rwkv_k_k[l], rwkv_k_a[l], rwkv_r_k[l], rwkv_lnx_g[l],
                        rwkv_lnx_b[l])
        gq, gk, gv, gad, ggate = _split(p_gla, [G_KW, G_KW, G_VW, G_LORA, G_VW])
        log_alpha = jax.nn.log_sigmoid((gad @ gla_alpha_up[l] + gla_alpha_b[l]).astype(jnp.float32)) / G_TAU
        o_g = gla_mix(gq.reshape(B, S, G_HEADS, G_DK), gk.reshape(B, S, G_HEADS, G_DK),
                      gv.reshape(B, S, G_HEADS, G_DV), log_alpha.reshape(B, S, G_HEADS, G_DK),
                      ggate.reshape(B, S, G_HEADS, G_DV), gla_norm_g[l])
        g_a, g_r, g_g = jnp.split(p_gate, 3, axis=-1)
        merged = (jax.nn.sigmoid(g_a) * (o_a @ proj_attn[l])
                  + jax.nn.sigmoid(g_r) * (o_r @ proj_rwkv[l])
                  + jax.nn.sigmoid(g_g) * (o_g @ proj_gla[l]))
        x = x + gt1 * (merged @ w_out[l])
        h2 = rms_norm(x, norm2_g[l]) * (1.0 + sc2) + sh2
        x = x + gt2 * hier_moe(h2, router_grp_w[l], router_grp_b[l], router_exp_w[l], router_exp_b[l],
                               exp_w_gate[l], exp_w_up[l], exp_w_down[l])
    return x
```

```python
import functools
import math

import jax
import jax.numpy as jnp
from jax import lax
from jax.experimental import pallas as pl
from jax.experimental.pallas import tpu as pltpu

F32 = jnp.float32
BF16 = jnp.bfloat16
HIGHEST = lax.Precision.HIGHEST

D_MODEL = 1024
A_HEADS, A_DH, A_DV = 4, 64, 128
A_QW, A_VW = 512, 512
A_COLS = 1536
R_HEADS, R_N, R_W = 8, 64, 512
R_COLS = 1792
RWKV_GN_EPS = 64e-5
G_HEADS, G_DK, G_DV = 4, 64, 128
G_KW, G_VW, G_LORA = 256, 512, 16
G_TAU = 16.0
G_COLS = 1552
G_COLS_PAD = 1664
GATE_COLS = 3072
N_GROUPS, EXP_PER_GROUP, N_EXPERTS, TOP_K = 4, 8, 32, 2
D_EXPERT = 512
EPS = 1e-6

LANES = 128
SUBLANES = 8
CHUNK = 64
VMEM_LIMIT = 56 * 1024 * 1024


def _dot(a, b):
    return jnp.dot(a.astype(BF16), b.astype(BF16), preferred_element_type=F32)


def _dot_hi(a, b):
    return jnp.dot(a, b, precision=HIGHEST, preferred_element_type=F32)


def _dot_nt(a, b, precision=None):
    return lax.dot_general(a, b, (((1,), (1,)), ((), ())), precision=precision,
                           preferred_element_type=F32)


def _dot_tn(a, b, precision=None):
    return lax.dot_general(a, b, (((0,), (0,)), ((), ())), precision=precision,
                           preferred_element_type=F32)


def _sigmoid(x):
    return 1.0 / (1.0 + jnp.exp(-x))


def _softplus(x):
    return jnp.maximum(x, 0.0) + jnp.log(1.0 + jnp.exp(-jnp.abs(x)))


def _seg_ones(n, seg):
    r = lax.broadcasted_iota(jnp.int32, (n, n), 0) // seg
    c = lax.broadcasted_iota(jnp.int32, (n, n), 1) // seg
    return (r == c).astype(F32)


def _tri(n, strict):
    r = lax.broadcasted_iota(jnp.int32, (n, n), 0)
    c = lax.broadcasted_iota(jnp.int32, (n, n), 1)
    return (c < r) if strict else (c <= r)


def _adaln_kernel(c_ref, w_ref, b_ref, o_ref):
    c = c_ref[...]
    c_act = c * _sigmoid(c)
    o_ref[...] = _dot_hi(c_act, w_ref[...]) + b_ref[...]


def _adaln(c, ada_w, ada_b):
    L, D, N = ada_w.shape
    B = c.shape[0]
    tn = D
    return pl.pallas_call(
        _adaln_kernel,
        grid=(L, N // tn),
        in_specs=[
            pl.BlockSpec((B, D), lambda l, j: (0, 0)),
            pl.BlockSpec((None, D, tn), lambda l, j: (l, 0, j)),
            pl.BlockSpec((None, 1, tn), lambda l, j: (l, 0, j)),
        ],
        out_specs=pl.BlockSpec((None, B, tn), lambda l, j: (l, 0, j)),
        out_shape=jax.ShapeDtypeStruct((L, B, N), F32),
        name="adaln",
    )(c, ada_w, ada_b.reshape(L, 1, N))


_IN_SEGS = (A_COLS, R_COLS, G_COLS_PAD, GATE_COLS)
_IN_DTYPES = (BF16, F32, F32, BF16)
_IN_CHUNK = 512


def _inproj_kernel(x_ref, mod_ref, g_ref, w_ref, *o_refs):
    x = x_ref[...]
    D = x.shape[-1]
    ms = jnp.mean(x * x, axis=-1, keepdims=True)
    y = x * lax.rsqrt(ms + EPS) * g_ref[...]
    sh = mod_ref[:, 0:D]
    sc = mod_ref[:, D:2 * D]
    h = (y * (1.0 + sc) + sh).astype(BF16)
    base = 0
    for o_ref, width in zip(o_refs, _IN_SEGS):
        for c0 in range(0, width, _IN_CHUNK):
            c1 = min(c0 + _IN_CHUNK, width)
            o_ref[:, c0:c1] = jnp.dot(
                h, w_ref[:, base + c0:base + c1], preferred_element_type=F32
            ).astype(o_ref.dtype)
        base += width


def _inproj(x2, mod_l, norm_g, w_pad, S, tm):
    T, D = x2.shape
    NP = w_pad.shape[1]
    tiles_per_batch = S // tm
    return pl.pallas_call(
        _inproj_kernel,
        grid=(T // tm,),
        in_specs=[
            pl.BlockSpec((tm, D), lambda i: (i, 0)),
            pl.BlockSpec((None, 1, 2 * D), lambda i: (i // tiles_per_batch, 0, 0)),
            pl.BlockSpec((1, D), lambda i: (0, 0)),
            pl.BlockSpec((D, NP), lambda i: (0, 0), pipeline_mode=pl.Buffered(1)),
        ],
        out_specs=[pl.BlockSpec((tm, w), lambda i: (i, 0)) for w in _IN_SEGS],
        out_shape=[jax.ShapeDtypeStruct((T, w), dt) for w, dt in zip(_IN_SEGS, _IN_DTYPES)],
        compiler_params=pltpu.CompilerParams(
            dimension_semantics=("arbitrary",), vmem_limit_bytes=VMEM_LIMIT),
        name="inproj",
    )(x2, mod_l, norm_g.reshape(1, D), w_pad)


def _attn_kernel(q_ref, k_ref, v_ref, qg_ref, kg_ref, lam_ref, sg_ref, slope_ref, o_ref,
                 qn_s, kn_s, *, S, tq, lambda_init):
    seg = _seg_ones(LANES, A_DH)

    def qknorm(x, g, scale):
        xf = x.astype(F32)
        ms = _dot_hi(xf * xf, seg) * (1.0 / A_DH)
        return xf * lax.rsqrt(ms + EPS) * g * scale

    qn = qknorm(q_ref[...], qg_ref[...], A_DH ** -0.5).astype(BF16)
    kn = qknorm(k_ref[...], kg_ref[...], 1.0).astype(BF16)
    for c in range(2):
        qn_s[c] = qn[:, c * A_DH:(c + 1) * A_DH]
        kn_s[c] = kn[:, c * A_DH:(c + 1) * A_DH]

    lv = lam_ref[...]
    lam = (jnp.exp(jnp.sum(lv[0:1] * lv[1:2], axis=-1, keepdims=True))
           - jnp.exp(jnp.sum(lv[2:3] * lv[3:4], axis=-1, keepdims=True)) + lambda_init)
    slope = slope_ref[...]
    slope_row = jnp.concatenate([slope] * (tq // LANES), axis=-1)
    rel = (lax.broadcasted_iota(jnp.int32, (tq, tq), 1)
           - lax.broadcasted_iota(jnp.int32, (tq, tq), 0))

    def qblock(i, _):
        q0 = pl.multiple_of(i * tq, tq)
        qs = [qn_s[c, pl.ds(q0, tq), :] for c in range(2)]

        def kvstep(j, carry):
            k0 = pl.multiple_of(j * tq, tq)
            off = (j - i) * tq
            relj = rel + off
            bias = slope_row * (lax.broadcasted_iota(jnp.int32, (1, tq), 1) + off).astype(F32)
            vblk = v_ref[pl.ds(k0, tq), :]
            out = []
            for c in range(2):
                m, l, acc = carry[c]
                s = _dot_nt(qs[c], kn_s[c, pl.ds(k0, tq), :]) + bias
                s = jnp.where(relj <= 0, s, -jnp.inf)
                m_new = jnp.maximum(m, jnp.max(s, axis=-1, keepdims=True))
                alpha = jnp.exp(m - m_new)
                p = jnp.exp(s - m_new)
                l = alpha * l + jnp.sum(p, axis=-1, keepdims=True)
                acc = alpha * acc + jnp.dot(p.astype(BF16), vblk, preferred_element_type=F32)
                out.append((m_new, l, acc))
            return tuple(out)

        init = tuple((jnp.full((tq, 1), -jnp.inf, F32), jnp.zeros((tq, 1), F32),
                      jnp.zeros((tq, A_DV), F32)) for _ in range(2))
        (m0, l0, a0), (m1, l1, a1) = lax.fori_loop(0, i + 1, kvstep, init)
        o = a0 / l0 - lam * (a1 / l1)
        ms = jnp.mean(o * o, axis=-1, keepdims=True)
        o = o * lax.rsqrt(ms + EPS) * sg_ref[...] * (1.0 - lambda_init)
        o_ref[pl.ds(q0, tq), :] = o.astype(o_ref.dtype)
        return 0

    lax.fori_loop(0, S // tq, qblock, 0)


def _attention(p_attn, qn_g, kn_g, lam_vecs, subln_g, lambda_init, B, S):
    tq = 256
    pa = p_attn.reshape(B, S, A_COLS)
    dup = lambda g: jnp.concatenate([g, g]).reshape(1, LANES)
    slopes = jnp.asarray(
        [[2.0 ** (-8.0 * (i + 1) / A_HEADS)] * LANES for i in range(A_HEADS)], F32
    ).reshape(A_HEADS, 1, LANES)
    nqb = A_QW // LANES
    kern = functools.partial(_attn_kernel, S=S, tq=tq, lambda_init=lambda_init)
    out = pl.pallas_call(
        kern,
        grid=(B, A_HEADS),
        in_specs=[
            pl.BlockSpec((None, S, LANES), lambda b, h: (b, 0, h)),
            pl.BlockSpec((None, S, LANES), lambda b, h: (b, 0, nqb + h)),
            pl.BlockSpec((None, S, LANES), lambda b, h: (b, 0, 2 * nqb + h)),
            pl.BlockSpec((1, LANES), lambda b, h: (0, 0)),
            pl.BlockSpec((1, LANES), lambda b, h: (0, 0)),
            pl.BlockSpec((4, A_DH), lambda b, h: (0, 0)),
            pl.BlockSpec((1, A_DV), lambda b, h: (0, 0)),
            pl.BlockSpec((None, 1, LANES), lambda b, h: (h, 0, 0)),
        ],
        out_specs=pl.BlockSpec((None, S, A_DV), lambda b, h: (b, 0, h)),
        out_shape=jax.ShapeDtypeStruct((B, S, A_VW), BF16),
        scratch_shapes=[pltpu.VMEM((2, S, A_DH), BF16), pltpu.VMEM((2, S, A_DH), BF16)],
        compiler_params=pltpu.CompilerParams(
            dimension_semantics=("arbitrary", "arbitrary"), vmem_limit_bytes=VMEM_LIMIT),
        name="diff_attn",
    )(pa, pa, pa, dup(qn_g), dup(kn_g), lam_vecs, subln_g.reshape(1, A_DV), slopes)
    return out.reshape(B * S, A_VW)


def _rwkv_kernel(p_ref, mu_ref, wup_ref, w0_ref, aup_ref, a0_ref, gup_ref, kk_ref, ka_ref,
                 rk_ref, lg_ref, lb_ref, o_ref,
                 carry_ref, st_ref, r_s, k_s, v_s, kk_s, bv_s, lw_s, y_s, *, TB):
    @pl.when(pl.program_id(1) == 0)
    def _():
        carry_ref[...] = jnp.zeros_like(carry_ref)
        st_ref[...] = jnp.zeros_like(st_ref)

    xs = p_ref[...]
    prev = pltpu.roll(xs, 1, 0)
    row = lax.broadcasted_iota(jnp.int32, (TB, 1), 0)
    prev = jnp.where(row == 0, carry_ref[...], prev)
    carry_ref[...] = xs[TB - 1:TB, :]
    xm = xs + (prev - xs) * mu_ref[...]
    r = xm[:, 0:R_W]
    k = xm[:, R_W:2 * R_W]
    v = xm[:, 2 * R_W:3 * R_W]
    wa = xm[:, 3 * R_W:3 * R_W + LANES]
    gd = xm[:, 3 * R_W + LANES:3 * R_W + 2 * LANES]
    wz = w0_ref[...] + _dot_hi(jnp.tanh(wa), wup_ref[...])
    w_log = -_softplus(-wz) - 0.5
    lw_s[...] = -jnp.exp(w_log)
    a = _sigmoid(a0_ref[...] + _dot_hi(wa, aup_ref[...]))
    g = _dot_hi(_sigmoid(gd), gup_ref[...])
    seg = _seg_ones(R_W, R_N)
    kk = k * kk_ref[...]
    nrm = jnp.sqrt(_dot_hi(kk * kk, seg))
    kk = kk / jnp.maximum(nrm, 1e-12)
    k2 = k * (1.0 + (a - 1.0) * ka_ref[...])
    bonus = _dot_hi(r * k2 * rk_ref[...], seg) * v
    r_s[...] = r
    k_s[...] = k2
    v_s[...] = v
    kk_s[...] = kk
    bv_s[...] = kk * a

    tri_incl = _tri(CHUNK, False)
    tri_strict = _tri(CHUNK, True)
    tri_f = tri_incl.astype(F32)

    def chunk(ci, _):
        r0 = pl.multiple_of(ci * CHUNK, CHUNK)
        rows = pl.ds(r0, CHUNK)
        lw = lw_s[rows, :]
        Lg = _dot_hi(tri_f, lw)
        gam = jnp.exp(Lg)
        inv = jnp.exp(-Lg)
        gC = Lg[CHUNK - 1:CHUNK, :]
        tail = jnp.exp(gC - Lg)
        gamC = jnp.exp(gC)
        kkc = kk_s[rows, :]
        bvc = bv_s[rows, :]
        kc = k_s[rows, :]
        vc = v_s[rows, :]
        alpha = jnp.exp(Lg - lw) * kkc
        beta = bvc * inv
        kappa = kc * inv
        rho = gam * r_s[rows, :]
        beta_t = bvc * tail
        kappa_t = kc * tail
        for h in range(R_HEADS):
            sl = slice(h * R_N, (h + 1) * R_N)
            al, be, ka, rh, vh = alpha[:, sl], beta[:, sl], kappa[:, sl], rho[:, sl], vc[:, sl]
            S0 = st_ref[h]
            m_ab = jnp.where(tri_strict, _dot_nt(al, be, HIGHEST), 0.0)
            m_ak = jnp.where(tri_strict, _dot_nt(al, ka, HIGHEST), 0.0)
            m_rb = jnp.where(tri_incl, _dot_nt(rh, be, HIGHEST), 0.0)
            m_rk = jnp.where(tri_incl, _dot_nt(rh, ka, HIGHEST), 0.0)
            a_s = _dot_nt(al, S0, HIGHEST)
            r_st = _dot_nt(rh, S0, HIGHEST)
            u = -(a_s + _dot_hi(m_ak, vh))
            x = -m_ab
            n = 1
            while True:
                u = u + _dot_hi(x, u)
                n *= 2
                if n >= CHUNK:
                    break
                x = _dot_hi(x, x)
            y = r_st + _dot_hi(m_rb, u) + _dot_hi(m_rk, vh)
            st_ref[h] = (S0 * gamC[:, sl] + _dot_tn(u, beta_t[:, sl], HIGHEST)
                         + _dot_tn(vh, kappa_t[:, sl], HIGHEST))
            y_s[rows, sl] = y
        return 0

    lax.fori_loop(0, TB // CHUNK, chunk, 0)

    y = y_s[...]
    mean = _dot_hi(y, seg) * (1.0 / R_N)
    yc = y - mean
    var = _dot_hi(yc * yc, seg) * (1.0 / R_N)
    yn = yc * lax.rsqrt(var + RWKV_GN_EPS) * lg_ref[...] + lb_ref[...]
    o_ref[...] = ((yn + bonus) * g).astype(o_ref.dtype)


def _rwkv(p_rwkv, mu, w_up, w0, a_up, a0, g_up, k_k, k_a, r_k, lnx_g, lnx_b, B, S):
    TB = 256
    T = B * S
    nt = S // TB
    row = lambda t: t.reshape(1, -1)
    zeros = jnp.zeros((R_N, R_W), F32)
    wup_pad = jnp.concatenate([w_up, zeros], axis=0)
    aup_pad = jnp.concatenate([zeros, a_up], axis=0)
    vec = lambda n: pl.BlockSpec((1, n), lambda b, i: (0, 0))
    mat = lambda m, n: pl.BlockSpec((m, n), lambda b, i: (0, 0))
    kern = functools.partial(_rwkv_kernel, TB=TB)
    return pl.pallas_call(
        kern,
        grid=(B, nt),
        in_specs=[
            pl.BlockSpec((TB, R_COLS), lambda b, i: (b * nt + i, 0)),
            vec(R_COLS), mat(LANES, R_W), vec(R_W), mat(LANES, R_W), vec(R_W), mat(LANES, R_W),
            vec(R_W), vec(R_W), vec(R_W), vec(R_W), vec(R_W),
        ],
        out_specs=pl.BlockSpec((TB, R_W), lambda b, i: (b * nt + i, 0)),
        out_shape=jax.ShapeDtypeStruct((T, R_W), BF16),
        scratch_shapes=[
            pltpu.VMEM((1, R_COLS), F32),
            pltpu.VMEM((R_HEADS, R_N, R_N), F32),
        ] + [pltpu.VMEM((TB, R_W), F32)] * 7,
        compiler_params=pltpu.CompilerParams(
            dimension_semantics=("arbitrary", "arbitrary"), vmem_limit_bytes=VMEM_LIMIT),
        name="rwkv7",
    )(p_rwkv, row(mu), wup_pad, row(w0), aup_pad, row(a0), g_up, row(k_k), row(k_a),
      row(r_k), row(lnx_g), row(lnx_b))


def _gla_kernel(p_ref, aup_ref, ab_ref, ng_ref, o_ref, st_ref, *, TB):
    @pl.when(pl.program_id(1) == 0)
    def _():
        st_ref[...] = jnp.zeros_like(st_ref)

    tri_incl = _tri(CHUNK, False)
    tri_f = tri_incl.astype(F32)
    c_gv = 2 * G_KW
    c_ad = c_gv + G_VW
    c_gate = c_ad + LANES

    def chunk(ci, _):
        r0 = pl.multiple_of(ci * CHUNK, CHUNK)
        rows = pl.ds(r0, CHUNK)
        q = p_ref[rows, 0:G_KW] * (G_DK ** -0.5)
        k = p_ref[rows, G_KW:2 * G_KW]
        v = p_ref[rows, c_gv:c_gv + G_VW]
        gate = p_ref[rows, c_gate:c_gate + G_VW]
        z = _dot_hi(p_ref[rows, c_ad:c_ad + LANES], aup_ref[...]) + ab_ref[...]
        la = -_softplus(-z) * (1.0 / G_TAU)
        b = _dot_hi(tri_f, la)
        b_last = b[CHUNK - 1:CHUNK, :]
        qe = q * jnp.exp(b)
        ke = k * jnp.exp(-b)
        kt = k * jnp.exp(b_last - b)
        e_last = jnp.exp(b_last)
        for h in range(G_HEADS):
            sl = slice(h * G_DK, (h + 1) * G_DK)
            vs = slice(h * G_DV, (h + 1) * G_DV)
            S0 = st_ref[h]
            sc = jnp.where(tri_incl, _dot_nt(qe[:, sl], ke[:, sl], HIGHEST), 0.0)
            o = _dot_hi(sc, v[:, vs]) + _dot_nt(qe[:, sl], S0, HIGHEST)
            st_ref[h] = S0 * e_last[:, sl] + _dot_tn(v[:, vs], kt[:, sl], HIGHEST)
            ms = jnp.mean(o * o, axis=-1, keepdims=True)
            gt = gate[:, vs]
            o = o * lax.rsqrt(ms + EPS) * ng_ref[...] * (gt * _sigmoid(gt))
            o_ref[rows, vs] = o.astype(o_ref.dtype)
        return 0

    lax.fori_loop(0, TB // CHUNK, chunk, 0)


def _gla(p_gla, alpha_up, alpha_b, norm_g, B, S):
    TB = 256
    T = B * S
    nt = S // TB
    aup_pad = jnp.concatenate([alpha_up, jnp.zeros((LANES - G_LORA, G_KW), F32)], axis=0)
    kern = functools.partial(_gla_kernel, TB=TB)
    return pl.pallas_call(
        kern,
        grid=(B, nt),
        in_specs=[
            pl.BlockSpec((TB, G_COLS_PAD), lambda b, i: (b * nt + i, 0)),
            pl.BlockSpec((LANES, G_KW), lambda b, i: (0, 0)),
            pl.BlockSpec((1, G_KW), lambda b, i: (0, 0)),
            pl.BlockSpec((1, G_DV), lambda b, i: (0, 0)),
        ],
        out_specs=pl.BlockSpec((TB, G_VW), lambda b, i: (b * nt + i, 0)),
        out_shape=jax.ShapeDtypeStruct((T, G_VW), BF16),
        scratch_shapes=[pltpu.VMEM((G_HEADS, G_DV, G_DK), F32)],
        compiler_params=pltpu.CompilerParams(
            dimension_semantics=("arbitrary", "arbitrary"), vmem_limit_bytes=VMEM_LIMIT),
        name="gla",
    )(p_gla, aup_pad, alpha_b.reshape(1, G_KW), norm_g.reshape(1, G_DV))


def _merge_kernel(x_ref, oa_ref, or_ref, og_ref, gate_ref, pa_ref, pr_ref, pg_ref, wo_ref,
                  mod_ref, g2_ref, rw_ref, rb_ref, x1_ref, h2_ref, lg_ref):
    D = D_MODEL
    merged = (_sigmoid(gate_ref[:, 0:D].astype(F32))
              * jnp.dot(oa_ref[...], pa_ref[...], preferred_element_type=F32)
              + _sigmoid(gate_ref[:, D:2 * D].astype(F32))
              * jnp.dot(or_ref[...], pr_ref[...], preferred_element_type=F32)
              + _sigmoid(gate_ref[:, 2 * D:3 * D].astype(F32))
              * jnp.dot(og_ref[...], pg_ref[...], preferred_element_type=F32))
    gt1 = mod_ref[:, 2 * D:3 * D]
    sh2 = mod_ref[:, 3 * D:4 * D]
    sc2 = mod_ref[:, 4 * D:5 * D]
    x1 = x_ref[...] + gt1 * jnp.dot(merged.astype(BF16), wo_ref[...], preferred_element_type=F32)
    x1_ref[...] = x1
    ms = jnp.mean(x1 * x1, axis=-1, keepdims=True)
    h2 = x1 * lax.rsqrt(ms + EPS) * g2_ref[...] * (1.0 + sc2) + sh2
    for s in range(D // LANES):
        h2_ref[:, s, :] = h2[:, s * LANES:(s + 1) * LANES]
    lg_ref[...] = _dot_hi(h2, rw_ref[...]) + rb_ref[...]


def _merge(x2, o_a, o_r, o_g, p_gate, proj_a, proj_r, proj_g, w_out, mod_l, norm2_g,
           router_w, router_b, S, tm):
    T, D = x2.shape
    tiles_per_batch = S // tm
    tile = lambda w: pl.BlockSpec((tm, w), lambda i: (i, 0))
    const = lambda m, n: pl.BlockSpec((m, n), lambda i: (0, 0))
    return pl.pallas_call(
        _merge_kernel,
        grid=(T // tm,),
        in_specs=[
            tile(D), tile(A_VW), tile(R_W), tile(G_VW), tile(GATE_COLS),
            const(A_VW, D), const(R_W, D), const(G_VW, D), const(D, D),
            pl.BlockSpec((None, 1, 6 * D), lambda i: (i // tiles_per_batch, 0, 0)),
            const(1, D), const(D, LANES), const(1, LANES),
        ],
        out_specs=[
            tile(D),
            pl.BlockSpec((tm, D // LANES, LANES), lambda i: (i, 0, 0)),
            tile(LANES),
        ],
        out_shape=[
            jax.ShapeDtypeStruct((T, D), F32),
            jax.ShapeDtypeStruct((T, D // LANES, LANES), F32),
            jax.ShapeDtypeStruct((T, LANES), F32),
        ],
        compiler_params=pltpu.CompilerParams(
            dimension_semantics=("arbitrary",), vmem_limit_bytes=VMEM_LIMIT),
        name="merge",
    )(x2, o_a, o_r, o_g, p_gate, proj_a, proj_r, proj_g, w_out, mod_l,
      norm2_g.reshape(1, D), router_w, router_b)


MOE_ROWS = 256


def _moe_kernel(be_ref, tok_ref, dst_ref, h2_hbm, wg_ref, wu_ref, wd_ref, rw_ref, y_hbm,
                xbuf, ybuf, wgb, wub, wdb, gsem, ssem):
    i = pl.program_id(0)
    base = i * MOE_ROWS
    nsub = D_MODEL // LANES

    def gather(r):
        return pltpu.make_async_copy(h2_hbm.at[tok_ref[base + r]], xbuf.at[r], gsem)

    def scatter(r):
        return pltpu.make_async_copy(ybuf.at[r], y_hbm.at[dst_ref[base + r]], ssem)

    def start_gather(r, _):
        gather(r).start()
        return 0

    lax.fori_loop(0, MOE_ROWS, start_gather, 0)

    changed = jnp.logical_or(i == 0, be_ref[i] != be_ref[jnp.maximum(i - 1, 0)])

    @pl.when(changed)
    def _():
        wgb[...] = wg_ref[...].astype(BF16)
        wub[...] = wu_ref[...].astype(BF16)
        wdb[...] = wd_ref[...].astype(BF16)

    def wait_gather(r, _):
        gather(r).wait()
        return 0

    lax.fori_loop(0, MOE_ROWS, wait_gather, 0)

    hg = jnp.zeros((MOE_ROWS, D_EXPERT), F32)
    hu = jnp.zeros((MOE_ROWS, D_EXPERT), F32)
    for s in range(nsub):
        xs = xbuf[:, s, :].astype(BF16)
        hg = hg + jnp.dot(xs, wgb[s * LANES:(s + 1) * LANES, :], preferred_element_type=F32)
        hu = hu + jnp.dot(xs, wub[s * LANES:(s + 1) * LANES, :], preferred_element_type=F32)
    hid = (hg * _sigmoid(hg) * hu).astype(BF16)
    y = jnp.dot(hid, wdb[...], preferred_element_type=F32) * rw_ref[...]
    for s in range(nsub):
        ybuf[:, s, :] = y[:, s * LANES:(s + 1) * LANES]

    def start_scatter(r, _):
        scatter(r).start()
        return 0

    lax.fori_loop(0, MOE_ROWS, start_scatter, 0)

    def wait_scatter(r, _):
        scatter(r).wait()
        return 0

    lax.fori_loop(0, MOE_ROWS, wait_scatter, 0)


def _moe(h2_3d, blk_exp, row_tok, row_dst, row_w, w_gate, w_up, w_down, n_out_rows):
    R = row_tok.shape[0]
    n_blocks = R // MOE_ROWS
    nsub = D_MODEL // LANES
    wspec = lambda m, n: pl.BlockSpec((None, m, n), lambda i, be, tk, ds: (be[i], 0, 0))
    grid_spec = pltpu.PrefetchScalarGridSpec(
        num_scalar_prefetch=3,
        grid=(n_blocks,),
        in_specs=[
            pl.BlockSpec(memory_space=pl.ANY),
            wspec(D_MODEL, D_EXPERT), wspec(D_MODEL, D_EXPERT), wspec(D_EXPERT, D_MODEL),
            pl.BlockSpec((MOE_ROWS, 1), lambda i, be, tk, ds: (i, 0)),
        ],
        out_specs=pl.BlockSpec(memory_space=pl.ANY),
        scratch_shapes=[
            pltpu.VMEM((MOE_ROWS, nsub, LANES), F32),
            pltpu.VMEM((MOE_ROWS, nsub, LANES), F32),
            pltpu.VMEM((D_MODEL, D_EXPERT), BF16),
            pltpu.VMEM((D_MODEL, D_EXPERT), BF16),
            pltpu.VMEM((D_EXPERT, D_MODEL), BF16),
            pltpu.SemaphoreType.DMA(()),
            pltpu.SemaphoreType.DMA(()),
        ],
    )
    return pl.pallas_call(
        _moe_kernel,
        grid_spec=grid_spec,
        out_shape=jax.ShapeDtypeStruct((n_out_rows, nsub, LANES), F32),
        compiler_params=pltpu.CompilerParams(
            dimension_semantics=("arbitrary",), vmem_limit_bytes=VMEM_LIMIT),
        name="moe_ffn",
    )(blk_exp, row_tok, row_dst, h2_3d, w_gate, w_up, w_down, row_w)


def _combine_kernel(x1_ref, y_ref, mod_ref, o_ref):
    D = D_MODEL
    for s in range(D // LANES):
        cols = slice(s * LANES, (s + 1) * LANES)
        gt2 = mod_ref[:, 5 * D + s * LANES:5 * D + (s + 1) * LANES]
        o_ref[:, cols] = x1_ref[:, cols] + gt2 * (y_ref[:, 0, s, :] + y_ref[:, 1, s, :])


def _combine(x1, y4, mod_l, S, tm):
    T, D = x1.shape
    tiles_per_batch = S // tm
    nsub = D // LANES
    return pl.pallas_call(
        _combine_kernel,
        grid=(T // tm,),
        in_specs=[
            pl.BlockSpec((tm, D), lambda i: (i, 0)),
            pl.BlockSpec((tm, TOP_K, nsub, LANES), lambda i: (i, 0, 0, 0)),
            pl.BlockSpec((None, 1, 6 * D), lambda i: (i // tiles_per_batch, 0, 0)),
        ],
        out_specs=pl.BlockSpec((tm, D), lambda i: (i, 0)),
        out_shape=jax.ShapeDtypeStruct((T, D), F32),
        compiler_params=pltpu.CompilerParams(
            dimension_semantics=("arbitrary",), vmem_limit_bytes=VMEM_LIMIT),
        name="moe_combine",
    )(x1, y4, mod_l)


def _route(logits, T):
    g_logits = logits[:, :N_GROUPS]
    grp = jnp.argmax(g_logits, axis=-1)
    g_prob = jnp.take_along_axis(jax.nn.softmax(g_logits, axis=-1), grp[:, None], axis=1)
    e_logits = logits[:, N_GROUPS:N_GROUPS + N_EXPERTS].reshape(T, N_GROUPS, EXP_PER_GROUP)
    e_in = jnp.take_along_axis(e_logits, grp[:, None, None], axis=1)[:, 0]
    top_v, top_i = lax.top_k(e_in, TOP_K)
    w_tok = g_prob * jax.nn.softmax(top_v, axis=-1)
    eid = (grp[:, None] * EXP_PER_GROUP + top_i).reshape(-1).astype(jnp.int32)
    A = T * TOP_K
    wts = w_tok.reshape(-1)
    order = jnp.argsort(eid).astype(jnp.int32)
    eid_s = eid[order]
    counts = jnp.zeros((N_EXPERTS,), jnp.int32).at[eid].add(1)
    padded = (counts + MOE_ROWS - 1) // MOE_ROWS * MOE_ROWS
    pad_end = jnp.cumsum(padded)
    pad_start = pad_end - padded
    start = jnp.cumsum(counts) - counts
    dest = pad_start[eid_s] + jnp.arange(A, dtype=jnp.int32) - start[eid_s]
    n_blocks = -(-A // MOE_ROWS) + N_EXPERTS
    R = n_blocks * MOE_ROWS
    row_tok = jnp.zeros((R,), jnp.int32).at[dest].set(order // TOP_K)
    is_pad = jnp.ones((R,), jnp.int32).at[dest].set(0)
    row_dst = (A + jnp.cumsum(is_pad) - 1).astype(jnp.int32).at[dest].set(order)
    row_w = jnp.zeros((R,), F32).at[dest].set(wts[order])
    blk_exp = jnp.minimum(
        jnp.searchsorted(pad_end, jnp.arange(n_blocks, dtype=jnp.int32) * MOE_ROWS, side='right'),
        N_EXPERTS - 1).astype(jnp.int32)
    return blk_exp, row_tok, row_dst, row_w.reshape(R, 1), R


def _pad_w_in(w):
    D = w.shape[0]
    c0 = A_COLS + R_COLS
    c_ad = c0 + 2 * G_KW + G_VW
    c_gg = c_ad + G_LORA
    pad = jnp.zeros((D, LANES - G_LORA), w.dtype)
    return jnp.concatenate([w[:, :c_gg], pad, w[:, c_gg:]], axis=1).astype(BF16)


def kernel(x, c, ada_w, ada_b, norm1_g, norm2_g, w_in, attn_qn_g, attn_kn_g, attn_lambda,
           attn_subln_g, rwkv_mu, rwkv_w_up, rwkv_w0, rwkv_a_up, rwkv_a0, rwkv_g_up, rwkv_k_k,
           rwkv_k_a, rwkv_r_k, rwkv_lnx_g, rwkv_lnx_b, gla_alpha_up, gla_alpha_b, gla_norm_g,
           proj_attn, proj_rwkv, proj_gla, w_out, router_grp_w, router_grp_b, router_exp_w,
           router_exp_b, exp_w_gate, exp_w_up, exp_w_down):
    B, S, D = x.shape
    T = B * S
    L = ada_w.shape[0]
    tm = 256
    mod = _adaln(c, ada_w, ada_b).reshape(L, B, 1, 6 * D)
    x2 = x.reshape(T, D)
    for l in range(L):
        lambda_init = 0.8 - 0.6 * math.exp(-0.3 * l)
        p_attn, p_rwkv, p_gla, p_gate = _inproj(x2, mod[l], norm1_g[l], _pad_w_in(w_in[l]), S, tm)
        o_a = _attention(p_attn, attn_qn_g[l], attn_kn_g[l], attn_lambda[l], attn_subln_g[l],
                         lambda_init, B, S)
        o_r = _rwkv(p_rwkv, rwkv_mu[l], rwkv_w_up[l], rwkv_w0[l], rwkv_a_up[l], rwkv_a0[l],
                    rwkv_g_up[l], rwkv_k_k[l], rwkv_k_a[l], rwkv_r_k[l], rwkv_lnx_g[l],
                    rwkv_lnx_b[l], B, S)
        o_g = _gla(p_gla, gla_alpha_up[l], gla_alpha_b[l], gla_norm_g[l], B, S)
        n_r = N_GROUPS + N_EXPERTS
        router_w = jnp.concatenate(
            [router_grp_w[l], router_exp_w[l], jnp.zeros((D, LANES - n_r), F32)], axis=1)
        router_b = jnp.concatenate(
            [router_grp_b[l], router_exp_b[l], jnp.zeros((LANES - n_r,), F32)]).reshape(1, LANES)
        x1, h2, logits = _merge(
            x2, o_a, o_r, o_g, p_gate, proj_attn[l].astype(BF16), proj_rwkv[l].astype(BF16),
            proj_gla[l].astype(BF16), w_out[l].astype(BF16), mod[l], norm2_g[l],
            router_w, router_b, S, tm)
        blk_exp, row_tok, row_dst, row_w, n_out = _route(logits, T)
        y = _moe(h2, blk_exp, row_tok, row_dst, row_w, exp_w_gate[l], exp_w_up[l],
                 exp_w_down[l], n_out)
        y4 = y.reshape(n_out // TOP_K, TOP_K, D // LANES, LANES)
        x2 = _combine(x1, y4, mod[l], S, tm)
    return x2.reshape(B, S, D)
```

```python
import functools
import math

import jax
import jax.numpy as jnp
from jax import lax
from jax.experimental import pallas as pl
from jax.experimental.pallas import tpu as pltpu

F32 = jnp.float32
BF16 = jnp.bfloat16
HIGHEST = lax.Precision.HIGHEST

D_MODEL = 1024
A_HEADS, A_DH, A_DV = 4, 64, 128
A_QW, A_VW = 512, 512
A_COLS = 1536
R_HEADS, R_N, R_W = 8, 64, 512
R_COLS = 1792
RWKV_GN_EPS = 64e-5
G_HEADS, G_DK, G_DV = 4, 64, 128
G_KW, G_VW, G_LORA = 256, 512, 16
G_TAU = 16.0
G_COLS = 1552
G_COLS_PAD = 1664
GATE_COLS = 3072
N_GROUPS, EXP_PER_GROUP, N_EXPERTS, TOP_K = 4, 8, 32, 2
D_EXPERT = 512
EPS = 1e-6

LANES = 128
SUBLANES = 8
CHUNK = 64
VMEM_LIMIT = 56 * 1024 * 1024


def _dot(a, b):
    return jnp.dot(a.astype(BF16), b.astype(BF16), preferred_element_type=F32)


def _dot_hi(a, b):
    return jnp.dot(a, b, precision=HIGHEST, preferred_element_type=F32)


def _dot_nt(a, b, precision=None):
    return lax.dot_general(a, b, (((1,), (1,)), ((), ())), precision=precision,
                           preferred_element_type=F32)


def _dot_tn(a, b, precision=None):
    return lax.dot_general(a, b, (((0,), (0,)), ((), ())), precision=precision,
                           preferred_element_type=F32)


def _sigmoid(x):
    return 1.0 / (1.0 + jnp.exp(-x))


def _softplus(x):
    return jnp.maximum(x, 0.0) + jnp.log(1.0 + jnp.exp(-jnp.abs(x)))


def _seg_ones(n, seg):
    r = lax.broadcasted_iota(jnp.int32, (n, n), 0) // seg
    c = lax.broadcasted_iota(jnp.int32, (n, n), 1) // seg
    return (r == c).astype(F32)


def _tri(n, strict):
    r = lax.broadcasted_iota(jnp.int32, (n, n), 0)
    c = lax.broadcasted_iota(jnp.int32, (n, n), 1)
    return (c < r) if strict else (c <= r)


def _adaln_kernel(c_ref, w_ref, b_ref, o_ref):
    c = c_ref[...]
    c_act = c * _sigmoid(c)
    o_ref[...] = _dot_hi(c_act, w_ref[...]) + b_ref[...]


def _adaln(c, ada_w, ada_b):
    L, D, N = ada_w.shape
    B = c.shape[0]
    tn = D
    return pl.pallas_call(
        _adaln_kernel,
        grid=(L, N // tn),
        in_specs=[
            pl.BlockSpec((B, D), lambda l, j: (0, 0)),
            pl.BlockSpec((None, D, tn), lambda l, j: (l, 0, j)),
            pl.BlockSpec((None, 1, tn), lambda l, j: (l, 0, j)),
        ],
        out_specs=pl.BlockSpec((None, B, tn), lambda l, j: (l, 0, j)),
        out_shape=jax.ShapeDtypeStruct((L, B, N), F32),
        name="adaln",
    )(c, ada_w, ada_b.reshape(L, 1, N))


_IN_SEGS = (A_COLS, R_COLS, G_COLS_PAD, GATE_COLS)
_IN_DTYPES = (BF16, F32, F32, BF16)
_IN_CHUNK = 512


def _inproj_kernel(x_ref, mod_ref, g_ref, w_ref, *o_refs):
    x = x_ref[...]
    D = x.shape[-1]
    ms = jnp.mean(x * x, axis=-1, keepdims=True)
    y = x * lax.rsqrt(ms + EPS) * g_ref[...]
    sh = mod_ref[:, 0:D]
    sc = mod_ref[:, D:2 * D]
    h = (y * (1.0 + sc) + sh).astype(BF16)
    base = 0
    for o_ref, width in zip(o_refs, _IN_SEGS):
        for c0 in range(0, width, _IN_CHUNK):
            c1 = min(c0 + _IN_CHUNK, width)
            o_ref[:, c0:c1] = jnp.dot(
                h, w_ref[:, base + c0:base + c1], preferred_element_type=F32
            ).astype(o_ref.dtype)
        base += width


def _inproj(x2, mod_l, norm_g, w_pad, S, tm):
    T, D = x2.shape
    NP = w_pad.shape[1]
    tiles_per_batch = S // tm
    return pl.pallas_call(
        _inproj_kernel,
        grid=(T // tm,),
        in_specs=[
            pl.BlockSpec((tm, D), lambda i: (i, 0)),
            pl.BlockSpec((None, 1, 2 * D), lambda i: (i // tiles_per_batch, 0, 0)),
            pl.BlockSpec((1, D), lambda i: (0, 0)),
            pl.BlockSpec((D, NP), lambda i: (0, 0), pipeline_mode=pl.Buffered(1)),
        ],
        out_specs=[pl.BlockSpec((tm, w), lambda i: (i, 0)) for w in _IN_SEGS],
        out_shape=[jax.ShapeDtypeStruct((T, w), dt) for w, dt in zip(_IN_SEGS, _IN_DTYPES)],
        compiler_params=pltpu.CompilerParams(
            dimension_semantics=("arbitrary",), vmem_limit_bytes=VMEM_LIMIT),
        name="inproj",
    )(x2, mod_l, norm_g.reshape(1, D), w_pad)


def _attn_kernel(q_ref, k_ref, v_ref, qg_ref, kg_ref, lam_ref, sg_ref, slope_ref, o_ref,
                 qn_s, kn_s, *, S, tq, lambda_init):
    seg = _seg_ones(LANES, A_DH)

    def qknorm(x, g, scale):
        xf = x.astype(F32)
        ms = _dot_hi(xf * xf, seg) * (1.0 / A_DH)
        return xf * lax.rsqrt(ms + EPS) * g * scale

    qn = qknorm(q_ref[...], qg_ref[...], A_DH ** -0.5).astype(BF16)
    kn = qknorm(k_ref[...], kg_ref[...], 1.0).astype(BF16)
    for c in range(2):
        qn_s[c] = qn[:, c * A_DH:(c + 1) * A_DH]
        kn_s[c] = kn[:, c * A_DH:(c + 1) * A_DH]

    lv = lam_ref[...]
    lam = (jnp.exp(jnp.sum(lv[0:1] * lv[1:2], axis=-1, keepdims=True))
           - jnp.exp(jnp.sum(lv[2:3] * lv[3:4], axis=-1, keepdims=True)) + lambda_init)
    slope = slope_ref[...]
    slope_row = jnp.concatenate([slope] * (tq // LANES), axis=-1)
    rel = (lax.broadcasted_iota(jnp.int32, (tq, tq), 1)
           - lax.broadcasted_iota(jnp.int32, (tq, tq), 0))

    def qblock(i, _):
        q0 = pl.multiple_of(i * tq, tq)
        qs = [qn_s[c, pl.ds(q0, tq), :] for c in range(2)]

        def kvstep(j, carry):
            k0 = pl.multiple_of(j * tq, tq)
            off = (j - i) * tq
            relj = rel + off
            bias = slope_row * (lax.broadcasted_iota(jnp.int32, (1, tq), 1) + off).astype(F32)
            vblk = v_ref[pl.ds(k0, tq), :]
            out = []
            for c in range(2):
                m, l, acc = carry[c]
                s = _dot_nt(qs[c], kn_s[c, pl.ds(k0, tq), :]) + bias
                s = jnp.where(relj <= 0, s, -jnp.inf)
                m_new = jnp.maximum(m, jnp.max(s, axis=-1, keepdims=True))
                alpha = jnp.exp(m - m_new)
                p = jnp.exp(s - m_new)
                l = alpha * l + jnp.sum(p, axis=-1, keepdims=True)
                acc = alpha * acc + jnp.dot(p.astype(BF16), vblk, preferred_element_type=F32)
                out.append((m_new, l, acc))
            return tuple(out)

        init = tuple((jnp.full((tq, 1), -jnp.inf, F32), jnp.zeros((tq, 1), F32),
                      jnp.zeros((tq, A_DV), F32)) for _ in range(2))
        (m0, l0, a0), (m1, l1, a1) = lax.fori_loop(0, i + 1, kvstep, init)
        o = a0 / l0 - lam * (a1 / l1)
        ms = jnp.mean(o * o, axis=-1, keepdims=True)
        o = o * lax.rsqrt(ms + EPS) * sg_ref[...] * (1.0 - lambda_init)
        o_ref[pl.ds(q0, tq), :] = o.astype(o_ref.dtype)
        return 0

    lax.fori_loop(0, S // tq, qblock, 0)


def _attention(p_attn, qn_g, kn_g, lam_vecs, subln_g, lambda_init, B, S):
    tq = 256
    pa = p_attn.reshape(B, S, A_COLS)
    dup = lambda g: jnp.concatenate([g, g]).reshape(1, LANES)
    slopes = jnp.asarray(
        [[2.0 ** (-8.0 * (i + 1) / A_HEADS)] * LANES for i in range(A_HEADS)], F32
    ).reshape(A_HEADS, 1, LANES)
    nqb = A_QW // LANES
    kern = functools.partial(_attn_kernel, S=S, tq=tq, lambda_init=lambda_init)
    out = pl.pallas_call(
        kern,
        grid=(B, A_HEADS),
        in_specs=[
            pl.BlockSpec((None, S, LANES), lambda b, h: (b, 0, h)),
            pl.BlockSpec((None, S, LANES), lambda b, h: (b, 0, nqb + h)),
            pl.BlockSpec((None, S, LANES), lambda b, h: (b, 0, 2 * nqb + h)),
            pl.BlockSpec((1, LANES), lambda b, h: (0, 0)),
            pl.BlockSpec((1, LANES), lambda b, h: (0, 0)),
            pl.BlockSpec((4, A_DH), lambda b, h: (0, 0)),
            pl.BlockSpec((1, A_DV), lambda b, h: (0, 0)),
            pl.BlockSpec((None, 1, LANES), lambda b, h: (h, 0, 0)),
        ],
        out_specs=pl.BlockSpec((None, S, A_DV), lambda b, h: (b, 0, h)),
        out_shape=jax.ShapeDtypeStruct((B, S, A_VW), BF16),
        scratch_shapes=[pltpu.VMEM((2, S, A_DH), BF16), pltpu.VMEM((2, S, A_DH), BF16)],
        compiler_params=pltpu.CompilerParams(
            dimension_semantics=("arbitrary", "arbitrary"), vmem_limit_bytes=VMEM_LIMIT),
        name="diff_attn",
    )(pa, pa, pa, dup(qn_g), dup(kn_g), lam_vecs, subln_g.reshape(1, A_DV), slopes)
    return out.reshape(B * S, A_VW)


R_GROUP = 4
R_GW = R_GROUP * R_N


def _split_dot(x, ones):
    hi = x.astype(BF16)
    lo = (x - hi.astype(F32)).astype(BF16)
    return (jnp.dot(hi, ones, preferred_element_type=F32)
            + jnp.dot(lo, ones, preferred_element_type=F32))


def _rwkv_kernel(p_ref, mu_ref, wup_ref, w0_ref, aup_ref, a0_ref, gup_ref, kk_ref, ka_ref,
                 rk_ref, lg_ref, lb_ref, o_ref,
                 carry_ref, st_ref, al_s, be_s, ka_s, rh_s, bt_s, kt_s, v_s, gc_s, y_s, *, TB):
    @pl.when(pl.program_id(1) == 0)
    def _():
        carry_ref[...] = jnp.zeros_like(carry_ref)
        st_ref[...] = jnp.zeros_like(st_ref)

    xs = p_ref[...]
    prev = pltpu.roll(xs, 1, 0)
    row = lax.broadcasted_iota(jnp.int32, (TB, 1), 0)
    prev = jnp.where(row == 0, carry_ref[...], prev)
    carry_ref[...] = xs[TB - 1:TB, :]
    xm = xs + (prev - xs) * mu_ref[...]
    r = xm[:, 0:R_W]
    k = xm[:, R_W:2 * R_W]
    v = xm[:, 2 * R_W:3 * R_W]
    wa = xm[:, 3 * R_W:3 * R_W + LANES]
    gd = xm[:, 3 * R_W + LANES:3 * R_W + 2 * LANES]
    wz = w0_ref[...] + _dot_hi(jnp.tanh(wa), wup_ref[...])
    w_log = -_softplus(-wz) - 0.5
    lw = -jnp.exp(w_log)
    a = _sigmoid(a0_ref[...] + _dot_hi(wa, aup_ref[...]))
    g = _dot_hi(_sigmoid(gd), gup_ref[...])
    seg = _seg_ones(R_W, R_N).astype(BF16)
    kk = k * kk_ref[...]
    nrm = jnp.sqrt(_split_dot(kk * kk, seg))
    kk = kk / jnp.maximum(nrm, 1e-12)
    k2 = k * (1.0 + (a - 1.0) * ka_ref[...])
    bonus = _split_dot(r * k2 * rk_ref[...], seg) * v
    bv = kk * a

    rr = lax.broadcasted_iota(jnp.int32, (TB, TB), 0)
    cc = lax.broadcasted_iota(jnp.int32, (TB, TB), 1)
    tril_blk = jnp.logical_and(rr // CHUNK == cc // CHUNK, cc <= rr).astype(F32)
    Lg = _dot_hi(tril_blk, lw)
    inv = jnp.exp(-Lg)
    al_s[...] = (jnp.exp(Lg - lw) * kk).astype(BF16)
    be_s[...] = (bv * inv).astype(BF16)
    ka_s[...] = (k2 * inv).astype(BF16)
    rh_s[...] = (jnp.exp(Lg) * r).astype(BF16)
    v_s[...] = v.astype(BF16)
    for c in range(TB // CHUNK):
        rows = slice(c * CHUNK, (c + 1) * CHUNK)
        gC = Lg[(c + 1) * CHUNK - 1:(c + 1) * CHUNK, :]
        tail = jnp.exp(gC - Lg[rows, :])
        bt_s[rows, :] = (bv[rows, :] * tail).astype(BF16)
        kt_s[rows, :] = (k2[rows, :] * tail).astype(BF16)
        gc_s[c * SUBLANES:(c + 1) * SUBLANES, :] = jnp.broadcast_to(jnp.exp(gC), (SUBLANES, R_W))

    ri = lax.broadcasted_iota(jnp.int32, (R_GW, R_GW), 0)
    ci = lax.broadcasted_iota(jnp.int32, (R_GW, R_GW), 1)
    blk = ri // R_N == ci // R_N
    strict = ci % R_N < ri % R_N
    incl = ci % R_N <= ri % R_N
    zero = jnp.zeros((), BF16)

    def expand(x):
        return jnp.where(blk, jnp.concatenate([x] * R_GROUP, axis=0), zero)

    def chunk(c, _):
        r0 = pl.multiple_of(c * CHUNK, CHUNK)
        rows = pl.ds(r0, CHUNK)
        g0 = pl.multiple_of(c * SUBLANES, SUBLANES)
        for gi in range(R_HEADS // R_GROUP):
            cols = slice(gi * R_GW, (gi + 1) * R_GW)
            A_ = expand(al_s[rows, cols])
            R_ = expand(rh_s[rows, cols])
            B_ = expand(be_s[rows, cols])
            K_ = expand(ka_s[rows, cols])
            Bt = expand(bt_s[rows, cols])
            Kt = expand(kt_s[rows, cols])
            vc = v_s[rows, cols]
            Vb = jnp.concatenate([vc[:, h * R_N:(h + 1) * R_N] for h in range(R_GROUP)], axis=0)
            m_ab = jnp.where(strict, _dot_nt(A_, B_), 0.0)
            m_ak = jnp.where(strict, _dot_nt(A_, K_), 0.0).astype(BF16)
            m_rb = jnp.where(incl, _dot_nt(R_, B_), 0.0).astype(BF16)
            m_rk = jnp.where(incl, _dot_nt(R_, K_), 0.0).astype(BF16)
            Z = A_.astype(F32)
            W = jnp.dot(m_ak, Vb, preferred_element_type=F32)
            X = -m_ab
            n = 1
            while True:
                Xb = X.astype(BF16)
                Z = Z + jnp.dot(Xb, Z.astype(BF16), preferred_element_type=F32)
                W = W + jnp.dot(Xb, W.astype(BF16), preferred_element_type=F32)
                n *= 2
                if n >= CHUNK:
                    break
                X = jnp.dot(Xb, Xb, preferred_element_type=F32)
            Zb = Z.astype(BF16)
            Wb = W.astype(BF16)
            y_a = R_.astype(F32) - jnp.dot(m_rb, Zb, preferred_element_type=F32)
            y_b = (jnp.dot(m_rk, Vb, preferred_element_type=F32)
                   - jnp.dot(m_rb, Wb, preferred_element_type=F32))
            p_neg = _dot_tn(Zb, Bt)
            q = _dot_tn(Vb, Kt) - _dot_tn(Wb, Bt)
            S0 = st_ref[gi]
            S0b = S0.astype(BF16)
            y = _dot_nt(y_a.astype(BF16), S0b) + y_b
            st_ref[gi] = (S0 * gc_s[pl.ds(g0, SUBLANES), cols][0:1, :]
                          - jnp.dot(S0b, p_neg.astype(BF16), preferred_element_type=F32) + q)
            for h in range(R_GROUP):
                hh = gi * R_GROUP + h
                y_s[rows, hh * R_N:(hh + 1) * R_N] = y[h * R_N:(h + 1) * R_N, :]
        return 0

    lax.fori_loop(0, TB // CHUNK, chunk, 0)

    y = y_s[...]
    mean = _split_dot(y, seg) * (1.0 / R_N)
    yc = y - mean
    var = _split_dot(yc * yc, seg) * (1.0 / R_N)
    yn = yc * lax.rsqrt(var + RWKV_GN_EPS) * lg_ref[...] + lb_ref[...]
    o_ref[...] = ((yn + bonus) * g).astype(o_ref.dtype)


def _rwkv(p_rwkv, mu, w_up, w0, a_up, a0, g_up, k_k, k_a, r_k, lnx_g, lnx_b, B, S):
    TB = 256
    T = B * S
    nt = S // TB
    row = lambda t: t.reshape(1, -1)
    zeros = jnp.zeros((R_N, R_W), F32)
    wup_pad = jnp.concatenate([w_up, zeros], axis=0)
    aup_pad = jnp.concatenate([zeros, a_up], axis=0)
    vec = lambda n: pl.BlockSpec((1, n), lambda b, i: (0, 0))
    mat = lambda m, n: pl.BlockSpec((m, n), lambda b, i: (0, 0))
    kern = functools.partial(_rwkv_kernel, TB=TB)
    return pl.pallas_call(
        kern,
        grid=(B, nt),
        in_specs=[
            pl.BlockSpec((TB, R_COLS), lambda b, i: (b * nt + i, 0)),
            vec(R_COLS), mat(LANES, R_W), vec(R_W), mat(LANES, R_W), vec(R_W), mat(LANES, R_W),
            vec(R_W), vec(R_W), vec(R_W), vec(R_W), vec(R_W),
        ],
        out_specs=pl.BlockSpec((TB, R_W), lambda b, i: (b * nt + i, 0)),
        out_shape=jax.ShapeDtypeStruct((T, R_W), BF16),
        scratch_shapes=[
            pltpu.VMEM((1, R_COLS), F32),
            pltpu.VMEM((R_HEADS // R_GROUP, R_N, R_GW), F32),
        ] + [pltpu.VMEM((TB, R_W), BF16)] * 7 + [
            pltpu.VMEM((TB // CHUNK * SUBLANES, R_W), F32),
            pltpu.VMEM((TB, R_W), F32),
        ],
        compiler_params=pltpu.CompilerParams(
            dimension_semantics=("arbitrary", "arbitrary"), vmem_limit_bytes=VMEM_LIMIT),
        name="rwkv7",
    )(p_rwkv, row(mu), wup_pad, row(w0), aup_pad, row(a0), g_up, row(k_k), row(k_a),
      row(r_k), row(lnx_g), row(lnx_b))


def _gla_kernel(p_ref, aup_ref, ab_ref, ng_ref, o_ref, st_ref, *, TB):
    @pl.when(pl.program_id(1) == 0)
    def _():
        st_ref[...] = jnp.zeros_like(st_ref)

    tri_incl = _tri(CHUNK, False)
    tri_f = tri_incl.astype(F32)
    c_gv = 2 * G_KW
    c_ad = c_gv + G_VW
    c_gate = c_ad + LANES

    def chunk(ci, _):
        r0 = pl.multiple_of(ci * CHUNK, CHUNK)
        rows = pl.ds(r0, CHUNK)
        q = p_ref[rows, 0:G_KW] * (G_DK ** -0.5)
        k = p_ref[rows, G_KW:2 * G_KW]
        v = p_ref[rows, c_gv:c_gv + G_VW]
        gate = p_ref[rows, c_gate:c_gate + G_VW]
        z = _dot_hi(p_ref[rows, c_ad:c_ad + LANES], aup_ref[...]) + ab_ref[...]
        la = -_softplus(-z) * (1.0 / G_TAU)
        b = _dot_hi(tri_f, la)
        b_last = b[CHUNK - 1:CHUNK, :]
        qe = (q * jnp.exp(b)).astype(BF16)
        ke = (k * jnp.exp(-b)).astype(BF16)
        kt = (k * jnp.exp(b_last - b)).astype(BF16)
        e_last = jnp.exp(b_last)
        vb = v.astype(BF16)
        for h in range(G_HEADS):
            sl = slice(h * G_DK, (h + 1) * G_DK)
            vs = slice(h * G_DV, (h + 1) * G_DV)
            S0 = st_ref[h]
            sc = jnp.where(tri_incl, _dot_nt(qe[:, sl], ke[:, sl]), 0.0)
            o = _dot(sc, vb[:, vs]) + _dot_nt(qe[:, sl], S0.astype(BF16))
            st_ref[h] = S0 * e_last[:, sl] + _dot_tn(vb[:, vs], kt[:, sl])
            ms = jnp.mean(o * o, axis=-1, keepdims=True)
            gt = gate[:, vs]
            o = o * lax.rsqrt(ms + EPS) * ng_ref[...] * (gt * _sigmoid(gt))
            o_ref[rows, vs] = o.astype(o_ref.dtype)
        return 0

    lax.fori_loop(0, TB // CHUNK, chunk, 0)


def _gla(p_gla, alpha_up, alpha_b, norm_g, B, S):
    TB = 256
    T = B * S
    nt = S // TB
    aup_pad = jnp.concatenate([alpha_up, jnp.zeros((LANES - G_LORA, G_KW), F32)], axis=0)
    kern = functools.partial(_gla_kernel, TB=TB)
    return pl.pallas_call(
        kern,
        grid=(B, nt),
        in_specs=[
            pl.BlockSpec((TB, G_COLS_PAD), lambda b, i: (b * nt + i, 0)),
            pl.BlockSpec((LANES, G_KW), lambda b, i: (0, 0)),
            pl.BlockSpec((1, G_KW), lambda b, i: (0, 0)),
            pl.BlockSpec((1, G_DV), lambda b, i: (0, 0)),
        ],
        out_specs=pl.BlockSpec((TB, G_VW), lambda b, i: (b * nt + i, 0)),
        out_shape=jax.ShapeDtypeStruct((T, G_VW), BF16),
        scratch_shapes=[pltpu.VMEM((G_HEADS, G_DV, G_DK), F32)],
        compiler_params=pltpu.CompilerParams(
            dimension_semantics=("arbitrary", "arbitrary"), vmem_limit_bytes=VMEM_LIMIT),
        name="gla",
    )(p_gla, aup_pad, alpha_b.reshape(1, G_KW), norm_g.reshape(1, G_DV))


def _merge_kernel(x_ref, oa_ref, or_ref, og_ref, gate_ref, pa_ref, pr_ref, pg_ref, wo_ref,
                  mod_ref, g2_ref, rw_ref, rb_ref, x1_ref, h2_ref, lg_ref):
    D = D_MODEL
    merged = (_sigmoid(gate_ref[:, 0:D].astype(F32))
              * jnp.dot(oa_ref[...], pa_ref[...], preferred_element_type=F32)
              + _sigmoid(gate_ref[:, D:2 * D].astype(F32))
              * jnp.dot(or_ref[...], pr_ref[...], preferred_element_type=F32)
              + _sigmoid(gate_ref[:, 2 * D:3 * D].astype(F32))
              * jnp.dot(og_ref[...], pg_ref[...], preferred_element_type=F32))
    gt1 = mod_ref[:, 2 * D:3 * D]
    sh2 = mod_ref[:, 3 * D:4 * D]
    sc2 = mod_ref[:, 4 * D:5 * D]
    x1 = x_ref[...] + gt1 * jnp.dot(merged.astype(BF16), wo_ref[...], preferred_element_type=F32)
    x1_ref[...] = x1
    ms = jnp.mean(x1 * x1, axis=-1, keepdims=True)
    h2 = x1 * lax.rsqrt(ms + EPS) * g2_ref[...] * (1.0 + sc2) + sh2
    for s in range(D // LANES):
        h2_ref[:, s, :] = h2[:, s * LANES:(s + 1) * LANES]
    lg_ref[...] = _dot_hi(h2, rw_ref[...]) + rb_ref[...]


def _merge(x2, o_a, o_r, o_g, p_gate, proj_a, proj_r, proj_g, w_out, mod_l, norm2_g,
           router_w, router_b, S, tm):
    T, D = x2.shape
    tiles_per_batch = S // tm
    tile = lambda w: pl.BlockSpec((tm, w), lambda i: (i, 0))
    const = lambda m, n: pl.BlockSpec((m, n), lambda i: (0, 0))
    return pl.pallas_call(
        _merge_kernel,
        grid=(T // tm,),
        in_specs=[
            tile(D), tile(A_VW), tile(R_W), tile(G_VW), tile(GATE_COLS),
            const(A_VW, D), const(R_W, D), const(G_VW, D), const(D, D),
            pl.BlockSpec((None, 1, 6 * D), lambda i: (i // tiles_per_batch, 0, 0)),
            const(1, D), const(D, LANES), const(1, LANES),
        ],
        out_specs=[
            tile(D),
            pl.BlockSpec((tm, D // LANES, LANES), lambda i: (i, 0, 0)),
            tile(LANES),
        ],
        out_shape=[
            jax.ShapeDtypeStruct((T, D), F32),
            jax.ShapeDtypeStruct((T, D // LANES, LANES), F32),
            jax.ShapeDtypeStruct((T, LANES), F32),
        ],
        compiler_params=pltpu.CompilerParams(
            dimension_semantics=("arbitrary",), vmem_limit_bytes=VMEM_LIMIT),
        name="merge",
    )(x2, o_a, o_r, o_g, p_gate, proj_a, proj_r, proj_g, w_out, mod_l,
      norm2_g.reshape(1, D), router_w, router_b)


MOE_ROWS = 256


def _moe_kernel(be_ref, tok_ref, dst_ref, nu_ref, h2_hbm, wg_ref, wu_ref, wd_ref, rw_ref, y_hbm,
                xbuf, ybuf, wgb, wub, wdb, gsem, ssem):
    i = pl.program_id(0)
    n = pl.num_programs(0)
    slot = i % 2
    n_used = nu_ref[0]
    nsub = D_MODEL // LANES

    def start_gather(blk, sl):
        def body(r, _):
            pltpu.make_async_copy(
                h2_hbm.at[tok_ref[blk * MOE_ROWS + r]], xbuf.at[sl, r], gsem.at[sl]).start()
            return 0
        lax.fori_loop(0, MOE_ROWS, body, 0, unroll=8)

    def wait_gather(sl):
        pltpu.make_async_copy(h2_hbm.at[pl.ds(0, MOE_ROWS)], xbuf.at[sl], gsem.at[sl]).wait()

    def start_scatter(blk, sl):
        def body(r, _):
            pltpu.make_async_copy(
                ybuf.at[sl, r], y_hbm.at[dst_ref[blk * MOE_ROWS + r]], ssem.at[sl]).start()
            return 0
        lax.fori_loop(0, MOE_ROWS, body, 0, unroll=8)

    def wait_scatter(sl):
        pltpu.make_async_copy(ybuf.at[sl], y_hbm.at[pl.ds(0, MOE_ROWS)], ssem.at[sl]).wait()

    @pl.when(i == 0)
    def _():
        start_gather(0, 0)

    @pl.when(i + 1 < n_used)
    def _():
        start_gather(i + 1, 1 - slot)

    @pl.when(i >= 2)
    def _():
        wait_scatter(slot)

    @pl.when(i < n_used)
    def _():
        changed = jnp.logical_or(i == 0, be_ref[i] != be_ref[jnp.maximum(i - 1, 0)])

        @pl.when(changed)
        def _():
            wgb[...] = wg_ref[...].astype(BF16)
            wub[...] = wu_ref[...].astype(BF16)
            wdb[...] = wd_ref[...].astype(BF16)

        wait_gather(slot)
        hg = jnp.zeros((MOE_ROWS, D_EXPERT), F32)
        hu = jnp.zeros((MOE_ROWS, D_EXPERT), F32)
        for s in range(nsub):
            xs = xbuf[slot, :, s, :].astype(BF16)
            hg = hg + jnp.dot(xs, wgb[s * LANES:(s + 1) * LANES, :], preferred_element_type=F32)
            hu = hu + jnp.dot(xs, wub[s * LANES:(s + 1) * LANES, :], preferred_element_type=F32)
        hid = (hg * _sigmoid(hg) * hu).astype(BF16)
        y = jnp.dot(hid, wdb[...], preferred_element_type=F32) * rw_ref[...]
        for s in range(nsub):
            ybuf[slot, :, s, :] = y[:, s * LANES:(s + 1) * LANES]

    @pl.when(i >= n_used)
    def _():
        ybuf[slot] = jnp.zeros((MOE_ROWS, nsub, LANES), F32)

    start_scatter(i, slot)

    @pl.when(i == n - 1)
    def _():
        wait_scatter(1 - slot)
        wait_scatter(slot)


def _moe(h2_3d, blk_exp, row_tok, row_dst, n_used, row_w, w_gate, w_up, w_down, n_out_rows):
    R = row_tok.shape[0]
    n_blocks = R // MOE_ROWS
    assert n_blocks >= 2
    nsub = D_MODEL // LANES
    wspec = lambda m, n: pl.BlockSpec((None, m, n), lambda i, be, tk, ds, nu: (be[i], 0, 0))
    grid_spec = pltpu.PrefetchScalarGridSpec(
        num_scalar_prefetch=4,
        grid=(n_blocks,),
        in_specs=[
            pl.BlockSpec(memory_space=pl.ANY),
            wspec(D_MODEL, D_EXPERT), wspec(D_MODEL, D_EXPERT), wspec(D_EXPERT, D_MODEL),
            pl.BlockSpec((MOE_ROWS, 1), lambda i, be, tk, ds, nu: (i, 0)),
        ],
        out_specs=pl.BlockSpec(memory_space=pl.ANY),
        scratch_shapes=[
            pltpu.VMEM((2, MOE_ROWS, nsub, LANES), F32),
            pltpu.VMEM((2, MOE_ROWS, nsub, LANES), F32),
            pltpu.VMEM((D_MODEL, D_EXPERT), BF16),
            pltpu.VMEM((D_MODEL, D_EXPERT), BF16),
            pltpu.VMEM((D_EXPERT, D_MODEL), BF16),
            pltpu.SemaphoreType.DMA((2,)),
            pltpu.SemaphoreType.DMA((2,)),
        ],
    )
    return pl.pallas_call(
        _moe_kernel,
        grid_spec=grid_spec,
        out_shape=jax.ShapeDtypeStruct((n_out_rows, nsub, LANES), F32),
        compiler_params=pltpu.CompilerParams(
            dimension_semantics=("arbitrary",), vmem_limit_bytes=VMEM_LIMIT),
        name="moe_ffn",
    )(blk_exp, row_tok, row_dst, n_used, h2_3d, w_gate, w_up, w_down, row_w)


def _combine_kernel(x1_ref, y_ref, mod_ref, o_ref):
    D = D_MODEL
    for s in range(D // LANES):
        cols = slice(s * LANES, (s + 1) * LANES)
        gt2 = mod_ref[:, 5 * D + s * LANES:5 * D + (s + 1) * LANES]
        o_ref[:, cols] = x1_ref[:, cols] + gt2 * (y_ref[:, 0, s, :] + y_ref[:, 1, s, :])


def _combine(x1, y4, mod_l, S, tm):
    T, D = x1.shape
    tiles_per_batch = S // tm
    nsub = D // LANES
    return pl.pallas_call(
        _combine_kernel,
        grid=(T // tm,),
        in_specs=[
            pl.BlockSpec((tm, D), lambda i: (i, 0)),
            pl.BlockSpec((tm, TOP_K, nsub, LANES), lambda i: (i, 0, 0, 0)),
            pl.BlockSpec((None, 1, 6 * D), lambda i: (i // tiles_per_batch, 0, 0)),
        ],
        out_specs=pl.BlockSpec((tm, D), lambda i: (i, 0)),
        out_shape=jax.ShapeDtypeStruct((T, D), F32),
        compiler_params=pltpu.CompilerParams(
            dimension_semantics=("arbitrary",), vmem_limit_bytes=VMEM_LIMIT),
        name="moe_combine",
    )(x1, y4, mod_l)


def _route(logits, T):
    g_logits = logits[:, :N_GROUPS]
    grp = jnp.argmax(g_logits, axis=-1)
    g_prob = jnp.take_along_axis(jax.nn.softmax(g_logits, axis=-1), grp[:, None], axis=1)
    e_logits = logits[:, N_GROUPS:N_GROUPS + N_EXPERTS].reshape(T, N_GROUPS, EXP_PER_GROUP)
    e_in = jnp.take_along_axis(e_logits, grp[:, None, None], axis=1)[:, 0]
    top_v, top_i = lax.top_k(e_in, TOP_K)
    w_tok = g_prob * jax.nn.softmax(top_v, axis=-1)
    eid = (grp[:, None] * EXP_PER_GROUP + top_i).reshape(-1).astype(jnp.int32)
    A = T * TOP_K
    wts = w_tok.reshape(-1)
    order = jnp.argsort(eid).astype(jnp.int32)
    eid_s = eid[order]
    counts = jnp.zeros((N_EXPERTS,), jnp.int32).at[eid].add(1)
    padded = (counts + MOE_ROWS - 1) // MOE_ROWS * MOE_ROWS
    pad_end = jnp.cumsum(padded)
    pad_start = pad_end - padded
    start = jnp.cumsum(counts) - counts
    dest = pad_start[eid_s] + jnp.arange(A, dtype=jnp.int32) - start[eid_s]
    n_blocks = -(-A // MOE_ROWS) + N_EXPERTS
    R = n_blocks * MOE_ROWS
    row_tok = jnp.zeros((R,), jnp.int32).at[dest].set(order // TOP_K)
    is_pad = jnp.ones((R,), jnp.int32).at[dest].set(0)
    row_dst = (A + jnp.cumsum(is_pad) - 1).astype(jnp.int32).at[dest].set(order)
    row_w = jnp.zeros((R,), F32).at[dest].set(wts[order])
    blk_exp = jnp.minimum(
        jnp.searchsorted(pad_end, jnp.arange(n_blocks, dtype=jnp.int32) * MOE_ROWS, side='right'),
        N_EXPERTS - 1).astype(jnp.int32)
    n_used = (pad_end[-1:] // MOE_ROWS).astype(jnp.int32)
    return blk_exp, row_tok, row_dst, n_used, row_w.reshape(R, 1), R


def _pad_w_in(w):
    D = w.shape[0]
    c0 = A_COLS + R_COLS
    c_ad = c0 + 2 * G_KW + G_VW
    c_gg = c_ad + G_LORA
    pad = jnp.zeros((D, LANES - G_LORA), w.dtype)
    return jnp.concatenate([w[:, :c_gg], pad, w[:, c_gg:]], axis=1).astype(BF16)


def kernel(x, c, ada_w, ada_b, norm1_g, norm2_g, w_in, attn_qn_g, attn_kn_g, attn_lambda,
           attn_subln_g, rwkv_mu, rwkv_w_up, rwkv_w0, rwkv_a_up, rwkv_a0, rwkv_g_up, rwkv_k_k,
           rwkv_k_a, rwkv_r_k, rwkv_lnx_g, rwkv_lnx_b, gla_alpha_up, gla_alpha_b, gla_norm_g,
           proj_attn, proj_rwkv, proj_gla, w_out, router_grp_w, router_grp_b, router_exp_w,
           router_exp_b, exp_w_gate, exp_w_up, exp_w_down):
    B, S, D = x.shape
    T = B * S
    L = ada_w.shape[0]
    tm = 256
    mod = _adaln(c, ada_w, ada_b).reshape(L, B, 1, 6 * D)
    x2 = x.reshape(T, D)
    for l in range(L):
        lambda_init = 0.8 - 0.6 * math.exp(-0.3 * l)
        p_attn, p_rwkv, p_gla, p_gate = _inproj(x2, mod[l], norm1_g[l], _pad_w_in(w_in[l]), S, tm)
        o_a = _attention(p_attn, attn_qn_g[l], attn_kn_g[l], attn_lambda[l], attn_subln_g[l],
                         lambda_init, B, S)
        o_r = _rwkv(p_rwkv, rwkv_mu[l], rwkv_w_up[l], rwkv_w0[l], rwkv_a_up[l], rwkv_a0[l],
                    rwkv_g_up[l], rwkv_k_k[l], rwkv_k_a[l], rwkv_r_k[l], rwkv_lnx_g[l],
                    rwkv_lnx_b[l], B, S)
        o_g = _gla(p_gla, gla_alpha_up[l], gla_alpha_b[l], gla_norm_g[l], B, S)
        n_r = N_GROUPS + N_EXPERTS
        router_w = jnp.concatenate(
            [router_grp_w[l], router_exp_w[l], jnp.zeros((D, LANES - n_r), F32)], axis=1)
        router_b = jnp.concatenate(
            [router_grp_b[l], router_exp_b[l], jnp.zeros((LANES - n_r,), F32)]).reshape(1, LANES)
        x1, h2, logits = _merge(
            x2, o_a, o_r, o_g, p_gate, proj_attn[l].astype(BF16), proj_rwkv[l].astype(BF16),
            proj_gla[l].astype(BF16), w_out[l].astype(BF16), mod[l], norm2_g[l],
            router_w, router_b, S, tm)
        blk_exp, row_tok, row_dst, n_used, row_w, n_out = _route(logits, T)
        y = _moe(h2, blk_exp, row_tok, row_dst, n_used, row_w, exp_w_gate[l], exp_w_up[l],
                 exp_w_down[l], n_out)
        y4 = y.reshape(n_out // TOP_K, TOP_K, D // LANES, LANES)
        x2 = _combine(x1, y4, mod[l], S, tm)
    return x2.reshape(B, S, D)
```

```python
import functools
import math

import jax
import jax.numpy as jnp
from jax import lax
from jax.experimental import pallas as pl
from jax.experimental.pallas import tpu as pltpu

F32 = jnp.float32
BF16 = jnp.bfloat16
HIGHEST = lax.Precision.HIGHEST

D_MODEL = 1024
A_HEADS, A_DH, A_DV = 4, 64, 128
A_QW, A_VW = 512, 512
A_COLS = 1536
R_HEADS, R_N, R_W = 8, 64, 512
R_COLS = 1792
RWKV_GN_EPS = 64e-5
G_HEADS, G_DK, G_DV = 4, 64, 128
G_KW, G_VW, G_LORA = 256, 512, 16
G_TAU = 16.0
G_COLS = 1552
G_COLS_PAD = 1664
GATE_COLS = 3072
N_GROUPS, EXP_PER_GROUP, N_EXPERTS, TOP_K = 4, 8, 32, 2
D_EXPERT = 512
EPS = 1e-6

LANES = 128
SUBLANES = 8
CHUNK = 64
VMEM_LIMIT = 56 * 1024 * 1024


def _dot(a, b):
    return jnp.dot(a.astype(BF16), b.astype(BF16), preferred_element_type=F32)


def _dot_hi(a, b):
    return jnp.dot(a, b, precision=HIGHEST, preferred_element_type=F32)


def _dot_nt(a, b, precision=None):
    return lax.dot_general(a, b, (((1,), (1,)), ((), ())), precision=precision,
                           preferred_element_type=F32)


def _dot_tn(a, b, precision=None):
    return lax.dot_general(a, b, (((0,), (0,)), ((), ())), precision=precision,
                           preferred_element_type=F32)


def _sigmoid(x):
    return 1.0 / (1.0 + jnp.exp(-x))


def _softplus(x):
    return jnp.maximum(x, 0.0) + jnp.log(1.0 + jnp.exp(-jnp.abs(x)))


def _seg_ones(n, seg):
    r = lax.broadcasted_iota(jnp.int32, (n, n), 0) // seg
    c = lax.broadcasted_iota(jnp.int32, (n, n), 1) // seg
    return (r == c).astype(F32)


def _tri(n, strict):
    r = lax.broadcasted_iota(jnp.int32, (n, n), 0)
    c = lax.broadcasted_iota(jnp.int32, (n, n), 1)
    return (c < r) if strict else (c <= r)


def _adaln_kernel(c_ref, w_ref, b_ref, o_ref):
    c = c_ref[...]
    c_act = c * _sigmoid(c)
    o_ref[...] = _dot_hi(c_act, w_ref[...]) + b_ref[...]


def _adaln(c, ada_w, ada_b):
    L, D, N = ada_w.shape
    B = c.shape[0]
    tn = D
    return pl.pallas_call(
        _adaln_kernel,
        grid=(L, N // tn),
        in_specs=[
            pl.BlockSpec((B, D), lambda l, j: (0, 0)),
            pl.BlockSpec((None, D, tn), lambda l, j: (l, 0, j)),
            pl.BlockSpec((None, 1, tn), lambda l, j: (l, 0, j)),
        ],
        out_specs=pl.BlockSpec((None, B, tn), lambda l, j: (l, 0, j)),
        out_shape=jax.ShapeDtypeStruct((L, B, N), F32),
        name="adaln",
    )(c, ada_w, ada_b.reshape(L, 1, N))


_IN_SEGS = (A_COLS, R_COLS, G_COLS_PAD, GATE_COLS)
_IN_DTYPES = (BF16, F32, F32, BF16)
_IN_CHUNK = 512


def _inproj_kernel(x_ref, mod_ref, g_ref, w_ref, *o_refs):
    x = x_ref[...]
    D = x.shape[-1]
    ms = jnp.mean(x * x, axis=-1, keepdims=True)
    y = x * lax.rsqrt(ms + EPS) * g_ref[...]
    sh = mod_ref[:, 0:D]
    sc = mod_ref[:, D:2 * D]
    h = (y * (1.0 + sc) + sh).astype(BF16)
    base = 0
    for o_ref, width in zip(o_refs, _IN_SEGS):
        for c0 in range(0, width, _IN_CHUNK):
            c1 = min(c0 + _IN_CHUNK, width)
            o_ref[:, c0:c1] = jnp.dot(
                h, w_ref[:, base + c0:base + c1], preferred_element_type=F32
            ).astype(o_ref.dtype)
        base += width


def _inproj(x2, mod_l, norm_g, w_pad, S, tm):
    T, D = x2.shape
    NP = w_pad.shape[1]
    tiles_per_batch = S // tm
    return pl.pallas_call(
        _inproj_kernel,
        grid=(T // tm,),
        in_specs=[
            pl.BlockSpec((tm, D), lambda i: (i, 0)),
            pl.BlockSpec((None, 1, 2 * D), lambda i: (i // tiles_per_batch, 0, 0)),
            pl.BlockSpec((1, D), lambda i: (0, 0)),
            pl.BlockSpec((D, NP), lambda i: (0, 0), pipeline_mode=pl.Buffered(1)),
        ],
        out_specs=[pl.BlockSpec((tm, w), lambda i: (i, 0)) for w in _IN_SEGS],
        out_shape=[jax.ShapeDtypeStruct((T, w), dt) for w, dt in zip(_IN_SEGS, _IN_DTYPES)],
        compiler_params=pltpu.CompilerParams(
            dimension_semantics=("arbitrary",), vmem_limit_bytes=VMEM_LIMIT),
        name="inproj",
    )(x2, mod_l, norm_g.reshape(1, D), w_pad)


def _attn_kernel(q_ref, k_ref, v_ref, qg_ref, kg_ref, lam_ref, sg_ref, slope_ref, o_ref,
                 qn_s, kn_s, *, S, tq, lambda_init):
    seg = _seg_ones(LANES, A_DH)

    def qknorm(x, g, scale):
        xf = x.astype(F32)
        ms = _dot_hi(xf * xf, seg) * (1.0 / A_DH)
        return xf * lax.rsqrt(ms + EPS) * g * scale

    qn = qknorm(q_ref[...], qg_ref[...], A_DH ** -0.5).astype(BF16)
    kn = qknorm(k_ref[...], kg_ref[...], 1.0).astype(BF16)
    for c in range(2):
        qn_s[c] = qn[:, c * A_DH:(c + 1) * A_DH]
        kn_s[c] = kn[:, c * A_DH:(c + 1) * A_DH]

    lv = lam_ref[...]
    lam = (jnp.exp(jnp.sum(lv[0:1] * lv[1:2], axis=-1, keepdims=True))
           - jnp.exp(jnp.sum(lv[2:3] * lv[3:4], axis=-1, keepdims=True)) + lambda_init)
    slope = slope_ref[...]
    slope_row = jnp.concatenate([slope] * (tq // LANES), axis=-1)
    rel = (lax.broadcasted_iota(jnp.int32, (tq, tq), 1)
           - lax.broadcasted_iota(jnp.int32, (tq, tq), 0))

    def qblock(i, _):
        q0 = pl.multiple_of(i * tq, tq)
        qs = [qn_s[c, pl.ds(q0, tq), :] for c in range(2)]

        def kvstep(j, carry):
            k0 = pl.multiple_of(j * tq, tq)
            off = (j - i) * tq
            relj = rel + off
            bias = slope_row * (lax.broadcasted_iota(jnp.int32, (1, tq), 1) + off).astype(F32)
            vblk = v_ref[pl.ds(k0, tq), :]
            out = []
            for c in range(2):
                m, l, acc = carry[c]
                s = _dot_nt(qs[c], kn_s[c, pl.ds(k0, tq), :]) + bias
                s = jnp.where(relj <= 0, s, -jnp.inf)
                m_new = jnp.maximum(m, jnp.max(s, axis=-1, keepdims=True))
                alpha = jnp.exp(m - m_new)
                p = jnp.exp(s - m_new)
                l = alpha * l + jnp.sum(p, axis=-1, keepdims=True)
                acc = alpha * acc + jnp.dot(p.astype(BF16), vblk, preferred_element_type=F32)
                out.append((m_new, l, acc))
            return tuple(out)

        init = tuple((jnp.full((tq, 1), -jnp.inf, F32), jnp.zeros((tq, 1), F32),
                      jnp.zeros((tq, A_DV), F32)) for _ in range(2))
        (m0, l0, a0), (m1, l1, a1) = lax.fori_loop(0, i + 1, kvstep, init)
        o = a0 / l0 - lam * (a1 / l1)
        ms = jnp.mean(o * o, axis=-1, keepdims=True)
        o = o * lax.rsqrt(ms + EPS) * sg_ref[...] * (1.0 - lambda_init)
        o_ref[pl.ds(q0, tq), :] = o.astype(o_ref.dtype)
        return 0

    lax.fori_loop(0, S // tq, qblock, 0)


def _attention(p_attn, qn_g, kn_g, lam_vecs, subln_g, lambda_init, B, S):
    tq = 256
    pa = p_attn.reshape(B, S, A_COLS)
    dup = lambda g: jnp.concatenate([g, g]).reshape(1, LANES)
    slopes = jnp.asarray(
        [[2.0 ** (-8.0 * (i + 1) / A_HEADS)] * LANES for i in range(A_HEADS)], F32
    ).reshape(A_HEADS, 1, LANES)
    nqb = A_QW // LANES
    kern = functools.partial(_attn_kernel, S=S, tq=tq, lambda_init=lambda_init)
    out = pl.pallas_call(
        kern,
        grid=(B, A_HEADS),
        in_specs=[
            pl.BlockSpec((None, S, LANES), lambda b, h: (b, 0, h)),
            pl.BlockSpec((None, S, LANES), lambda b, h: (b, 0, nqb + h)),
            pl.BlockSpec((None, S, LANES), lambda b, h: (b, 0, 2 * nqb + h)),
            pl.BlockSpec((1, LANES), lambda b, h: (0, 0)),
            pl.BlockSpec((1, LANES), lambda b, h: (0, 0)),
            pl.BlockSpec((4, A_DH), lambda b, h: (0, 0)),
            pl.BlockSpec((1, A_DV), lambda b, h: (0, 0)),
            pl.BlockSpec((None, 1, LANES), lambda b, h: (h, 0, 0)),
        ],
        out_specs=pl.BlockSpec((None, S, A_DV), lambda b, h: (b, 0, h)),
        out_shape=jax.ShapeDtypeStruct((B, S, A_VW), BF16),
        scratch_shapes=[pltpu.VMEM((2, S, A_DH), BF16), pltpu.VMEM((2, S, A_DH), BF16)],
        compiler_params=pltpu.CompilerParams(
            dimension_semantics=("arbitrary", "arbitrary"), vmem_limit_bytes=VMEM_LIMIT),
        name="diff_attn",
    )(pa, pa, pa, dup(qn_g), dup(kn_g), lam_vecs, subln_g.reshape(1, A_DV), slopes)
    return out.reshape(B * S, A_VW)


R_GROUP = 4
R_GW = R_GROUP * R_N


def _split_dot(x, ones):
    hi = x.astype(BF16)
    lo = (x - hi.astype(F32)).astype(BF16)
    return (jnp.dot(hi, ones, preferred_element_type=F32)
            + jnp.dot(lo, ones, preferred_element_type=F32))


def _rwkv_kernel(p_ref, mu_ref, wup_ref, w0_ref, aup_ref, a0_ref, gup_ref, kk_ref, ka_ref,
                 rk_ref, lg_ref, lb_ref, o_ref,
                 carry_ref, st_ref, al_s, be_s, ka_s, rh_s, bt_s, kt_s, v_s, gc_s, y_s, *, TB):
    @pl.when(pl.program_id(1) == 0)
    def _():
        carry_ref[...] = jnp.zeros_like(carry_ref)
        st_ref[...] = jnp.zeros_like(st_ref)

    xs = p_ref[...]
    prev = pltpu.roll(xs, 1, 0)
    row = lax.broadcasted_iota(jnp.int32, (TB, 1), 0)
    prev = jnp.where(row == 0, carry_ref[...], prev)
    carry_ref[...] = xs[TB - 1:TB, :]
    xm = xs + (prev - xs) * mu_ref[...]
    r = xm[:, 0:R_W]
    k = xm[:, R_W:2 * R_W]
    v = xm[:, 2 * R_W:3 * R_W]
    wa = xm[:, 3 * R_W:3 * R_W + LANES]
    gd = xm[:, 3 * R_W + LANES:3 * R_W + 2 * LANES]
    wz = w0_ref[...] + _dot_hi(jnp.tanh(wa), wup_ref[...])
    w_log = -_softplus(-wz) - 0.5
    lw = -jnp.exp(w_log)
    a = _sigmoid(a0_ref[...] + _dot_hi(wa, aup_ref[...]))
    g = _dot_hi(_sigmoid(gd), gup_ref[...])
    seg = _seg_ones(R_W, R_N).astype(BF16)
    kk = k * kk_ref[...]
    nrm = jnp.sqrt(_split_dot(kk * kk, seg))
    kk = kk / jnp.maximum(nrm, 1e-12)
    k2 = k * (1.0 + (a - 1.0) * ka_ref[...])
    bonus = _split_dot(r * k2 * rk_ref[...], seg) * v
    bv = kk * a

    rr = lax.broadcasted_iota(jnp.int32, (TB, TB), 0)
    cc = lax.broadcasted_iota(jnp.int32, (TB, TB), 1)
    tril_blk = jnp.logical_and(rr // CHUNK == cc // CHUNK, cc <= rr).astype(F32)
    Lg = _dot_hi(tril_blk, lw)
    inv = jnp.exp(-Lg)
    al_s[...] = (jnp.exp(Lg - lw) * kk).astype(BF16)
    be_s[...] = (bv * inv).astype(BF16)
    ka_s[...] = (k2 * inv).astype(BF16)
    rh_s[...] = (jnp.exp(Lg) * r).astype(BF16)
    v_s[...] = v.astype(BF16)
    for c in range(TB // CHUNK):
        rows = slice(c * CHUNK, (c + 1) * CHUNK)
        gC = Lg[(c + 1) * CHUNK - 1:(c + 1) * CHUNK, :]
        tail = jnp.exp(gC - Lg[rows, :])
        bt_s[rows, :] = (bv[rows, :] * tail).astype(BF16)
        kt_s[rows, :] = (k2[rows, :] * tail).astype(BF16)
        gc_s[c * SUBLANES:(c + 1) * SUBLANES, :] = jnp.broadcast_to(jnp.exp(gC), (SUBLANES, R_W))

    ri = lax.broadcasted_iota(jnp.int32, (R_GW, R_GW), 0)
    ci = lax.broadcasted_iota(jnp.int32, (R_GW, R_GW), 1)
    blk = ri // R_N == ci // R_N
    strict = ci % R_N < ri % R_N
    incl = ci % R_N <= ri % R_N
    zero = jnp.zeros((), BF16)

    def expand(x):
        return jnp.where(blk, jnp.concatenate([x] * R_GROUP, axis=0), zero)

    def chunk(c, _):
        r0 = pl.multiple_of(c * CHUNK, CHUNK)
        rows = pl.ds(r0, CHUNK)
        g0 = pl.multiple_of(c * SUBLANES, SUBLANES)
        for gi in range(R_HEADS // R_GROUP):
            cols = slice(gi * R_GW, (gi + 1) * R_GW)
            A_ = expand(al_s[rows, cols])
            R_ = expand(rh_s[rows, cols])
            B_ = expand(be_s[rows, cols])
            K_ = expand(ka_s[rows, cols])
            Bt = expand(bt_s[rows, cols])
            Kt = expand(kt_s[rows, cols])
            vc = v_s[rows, cols]
            Vb = jnp.concatenate([vc[:, h * R_N:(h + 1) * R_N] for h in range(R_GROUP)], axis=0)
            m_ab = jnp.where(strict, _dot_nt(A_, B_), 0.0)
            m_ak = jnp.where(strict, _dot_nt(A_, K_), 0.0).astype(BF16)
            m_rb = jnp.where(incl, _dot_nt(R_, B_), 0.0).astype(BF16)
            m_rk = jnp.where(incl, _dot_nt(R_, K_), 0.0).astype(BF16)
            Z = A_.astype(F32)
            W = jnp.dot(m_ak, Vb, preferred_element_type=F32)
            X = -m_ab
            n = 1
            while True:
                Xb = X.astype(BF16)
                Z = Z + jnp.dot(Xb, Z.astype(BF16), preferred_element_type=F32)
                W = W + jnp.dot(Xb, W.astype(BF16), preferred_element_type=F32)
                n *= 2
                if n >= CHUNK:
                    break
                X = jnp.dot(Xb, Xb, preferred_element_type=F32)
            Zb = Z.astype(BF16)
            Wb = W.astype(BF16)
            y_a = R_.astype(F32) - jnp.dot(m_rb, Zb, preferred_element_type=F32)
            y_b = (jnp.dot(m_rk, Vb, preferred_element_type=F32)
                   - jnp.dot(m_rb, Wb, preferred_element_type=F32))
            p_neg = _dot_tn(Zb, Bt)
            q = _dot_tn(Vb, Kt) - _dot_tn(Wb, Bt)
            S0 = st_ref[gi]
            S0b = S0.astype(BF16)
            y = _dot_nt(y_a.astype(BF16), S0b) + y_b
            st_ref[gi] = (S0 * gc_s[pl.ds(g0, SUBLANES), cols][0:1, :]
                          - jnp.dot(S0b, p_neg.astype(BF16), preferred_element_type=F32) + q)
            for h in range(R_GROUP):
                hh = gi * R_GROUP + h
                y_s[rows, hh * R_N:(hh + 1) * R_N] = y[h * R_N:(h + 1) * R_N, :]
        return 0

    lax.fori_loop(0, TB // CHUNK, chunk, 0)

    y = y_s[...]
    mean = _split_dot(y, seg) * (1.0 / R_N)
    yc = y - mean
    var = _split_dot(yc * yc, seg) * (1.0 / R_N)
    yn = yc * lax.rsqrt(var + RWKV_GN_EPS) * lg_ref[...] + lb_ref[...]
    o_ref[...] = ((yn + bonus) * g).astype(o_ref.dtype)


def _rwkv(p_rwkv, mu, w_up, w0, a_up, a0, g_up, k_k, k_a, r_k, lnx_g, lnx_b, B, S):
    TB = 256
    T = B * S
    nt = S // TB
    row = lambda t: t.reshape(1, -1)
    zeros = jnp.zeros((R_N, R_W), F32)
    wup_pad = jnp.concatenate([w_up, zeros], axis=0)
    aup_pad = jnp.concatenate([zeros, a_up], axis=0)
    vec = lambda n: pl.BlockSpec((1, n), lambda b, i: (0, 0))
    mat = lambda m, n: pl.BlockSpec((m, n), lambda b, i: (0, 0))
    kern = functools.partial(_rwkv_kernel, TB=TB)
    return pl.pallas_call(
        kern,
        grid=(B, nt),
        in_specs=[
            pl.BlockSpec((TB, R_COLS), lambda b, i: (b * nt + i, 0)),
            vec(R_COLS), mat(LANES, R_W), vec(R_W), mat(LANES, R_W), vec(R_W), mat(LANES, R_W),
            vec(R_W), vec(R_W), vec(R_W), vec(R_W), vec(R_W),
        ],
        out_specs=pl.BlockSpec((TB, R_W), lambda b, i: (b * nt + i, 0)),
        out_shape=jax.ShapeDtypeStruct((T, R_W), BF16),
        scratch_shapes=[
            pltpu.VMEM((1, R_COLS), F32),
            pltpu.VMEM((R_HEADS // R_GROUP, R_N, R_GW), F32),
        ] + [pltpu.VMEM((TB, R_W), BF16)] * 7 + [
            pltpu.VMEM((TB // CHUNK * SUBLANES, R_W), F32),
            pltpu.VMEM((TB, R_W), F32),
        ],
        compiler_params=pltpu.CompilerParams(
            dimension_semantics=("arbitrary", "arbitrary"), vmem_limit_bytes=VMEM_LIMIT),
        name="rwkv7",
    )(p_rwkv, row(mu), wup_pad, row(w0), aup_pad, row(a0), g_up, row(k_k), row(k_a),
      row(r_k), row(lnx_g), row(lnx_b))


def _gla_kernel(p_ref, aup_ref, ab_ref, ng_ref, o_ref, st_ref, *, TB):
    @pl.when(pl.program_id(1) == 0)
    def _():
        st_ref[...] = jnp.zeros_like(st_ref)

    tri_incl = _tri(CHUNK, False)
    tri_f = tri_incl.astype(F32)
    c_gv = 2 * G_KW
    c_ad = c_gv + G_VW
    c_gate = c_ad + LANES

    def chunk(ci, _):
        r0 = pl.multiple_of(ci * CHUNK, CHUNK)
        rows = pl.ds(r0, CHUNK)
        q = p_ref[rows, 0:G_KW] * (G_DK ** -0.5)
        k = p_ref[rows, G_KW:2 * G_KW]
        v = p_ref[rows, c_gv:c_gv + G_VW]
        gate = p_ref[rows, c_gate:c_gate + G_VW]
        z = _dot_hi(p_ref[rows, c_ad:c_ad + LANES], aup_ref[...]) + ab_ref[...]
        la = -_softplus(-z) * (1.0 / G_TAU)
        b = _dot_hi(tri_f, la)
        b_last = b[CHUNK - 1:CHUNK, :]
        qe = (q * jnp.exp(b)).astype(BF16)
        ke = (k * jnp.exp(-b)).astype(BF16)
        kt = (k * jnp.exp(b_last - b)).astype(BF16)
        e_last = jnp.exp(b_last)
        vb = v.astype(BF16)
        for h in range(G_HEADS):
            sl = slice(h * G_DK, (h + 1) * G_DK)
            vs = slice(h * G_DV, (h + 1) * G_DV)
            S0 = st_ref[h]
            sc = jnp.where(tri_incl, _dot_nt(qe[:, sl], ke[:, sl]), 0.0)
            o = _dot(sc, vb[:, vs]) + _dot_nt(qe[:, sl], S0.astype(BF16))
            st_ref[h] = S0 * e_last[:, sl] + _dot_tn(vb[:, vs], kt[:, sl])
            ms = jnp.mean(o * o, axis=-1, keepdims=True)
            gt = gate[:, vs]
            o = o * lax.rsqrt(ms + EPS) * ng_ref[...] * (gt * _sigmoid(gt))
            o_ref[rows, vs] = o.astype(o_ref.dtype)
        return 0

    lax.fori_loop(0, TB // CHUNK, chunk, 0)


def _gla(p_gla, alpha_up, alpha_b, norm_g, B, S):
    TB = 256
    T = B * S
    nt = S // TB
    aup_pad = jnp.concatenate([alpha_up, jnp.zeros((LANES - G_LORA, G_KW), F32)], axis=0)
    kern = functools.partial(_gla_kernel, TB=TB)
    return pl.pallas_call(
        kern,
        grid=(B, nt),
        in_specs=[
            pl.BlockSpec((TB, G_COLS_PAD), lambda b, i: (b * nt + i, 0)),
            pl.BlockSpec((LANES, G_KW), lambda b, i: (0, 0)),
            pl.BlockSpec((1, G_KW), lambda b, i: (0, 0)),
            pl.BlockSpec((1, G_DV), lambda b, i: (0, 0)),
        ],
        out_specs=pl.BlockSpec((TB, G_VW), lambda b, i: (b * nt + i, 0)),
        out_shape=jax.ShapeDtypeStruct((T, G_VW), BF16),
        scratch_shapes=[pltpu.VMEM((G_HEADS, G_DV, G_DK), F32)],
        compiler_params=pltpu.CompilerParams(
            dimension_semantics=("arbitrary", "arbitrary"), vmem_limit_bytes=VMEM_LIMIT),
        name="gla",
    )(p_gla, aup_pad, alpha_b.reshape(1, G_KW), norm_g.reshape(1, G_DV))


def _merge_kernel(x_ref, oa_ref, or_ref, og_ref, gate_ref, pa_ref, pr_ref, pg_ref, wo_ref,
                  mod_ref, g2_ref, rw_ref, rb_ref, x1_ref, h2_ref, lg_ref):
    D = D_MODEL
    merged = (_sigmoid(gate_ref[:, 0:D].astype(F32))
              * jnp.dot(oa_ref[...], pa_ref[...], preferred_element_type=F32)
              + _sigmoid(gate_ref[:, D:2 * D].astype(F32))
              * jnp.dot(or_ref[...], pr_ref[...], preferred_element_type=F32)
              + _sigmoid(gate_ref[:, 2 * D:3 * D].astype(F32))
              * jnp.dot(og_ref[...], pg_ref[...], preferred_element_type=F32))
    gt1 = mod_ref[:, 2 * D:3 * D]
    sh2 = mod_ref[:, 3 * D:4 * D]
    sc2 = mod_ref[:, 4 * D:5 * D]
    x1 = x_ref[...] + gt1 * jnp.dot(merged.astype(BF16), wo_ref[...], preferred_element_type=F32)
    x1_ref[...] = x1
    ms = jnp.mean(x1 * x1, axis=-1, keepdims=True)
    h2 = x1 * lax.rsqrt(ms + EPS) * g2_ref[...] * (1.0 + sc2) + sh2
    for s in range(D // LANES):
        h2_ref[:, s, :] = h2[:, s * LANES:(s + 1) * LANES]
    lg_ref[...] = _dot_hi(h2, rw_ref[...]) + rb_ref[...]


def _merge(x2, o_a, o_r, o_g, p_gate, proj_a, proj_r, proj_g, w_out, mod_l, norm2_g,
           router_w, router_b, S, tm):
    T, D = x2.shape
    tiles_per_batch = S // tm
    tile = lambda w: pl.BlockSpec((tm, w), lambda i: (i, 0))
    const = lambda m, n: pl.BlockSpec((m, n), lambda i: (0, 0))
    return pl.pallas_call(
        _merge_kernel,
        grid=(T // tm,),
        in_specs=[
            tile(D), tile(A_VW), tile(R_W), tile(G_VW), tile(GATE_COLS),
            const(A_VW, D), const(R_W, D), const(G_VW, D), const(D, D),
            pl.BlockSpec((None, 1, 6 * D), lambda i: (i // tiles_per_batch, 0, 0)),
            const(1, D), const(D, LANES), const(1, LANES),
        ],
        out_specs=[
            tile(D),
            pl.BlockSpec((tm, D // LANES, LANES), lambda i: (i, 0, 0)),
            tile(LANES),
        ],
        out_shape=[
            jax.ShapeDtypeStruct((T, D), F32),
            jax.ShapeDtypeStruct((T, D // LANES, LANES), F32),
            jax.ShapeDtypeStruct((T, LANES), F32),
        ],
        compiler_params=pltpu.CompilerParams(
            dimension_semantics=("arbitrary",), vmem_limit_bytes=VMEM_LIMIT),
        name="merge",
    )(x2, o_a, o_r, o_g, p_gate, proj_a, proj_r, proj_g, w_out, mod_l,
      norm2_g.reshape(1, D), router_w, router_b)


MOE_ROWS = 256
ROUTE_ROWS = 512
E_LANE0 = N_GROUPS


def _route_kernel(lg_ref, info_ref, cnt_ref, carry_ref):
    @pl.when(pl.program_id(0) == 0)
    def _():
        carry_ref[...] = jnp.zeros_like(carry_ref)

    lg = lg_ref[...]
    n = lg.shape[0]
    lane = lax.broadcasted_iota(jnp.int32, (n, LANES), 1).astype(F32)
    neg = -jnp.inf
    big = float(LANES)

    def first_max(vals):
        m = jnp.max(vals, axis=-1, keepdims=True)
        idx = jnp.min(jnp.where(vals == m, lane, big), axis=-1, keepdims=True)
        return m, idx

    in_grp = lane < N_GROUPS
    gm, grp = first_max(jnp.where(in_grp, lg, neg))
    g_prob = 1.0 / jnp.sum(jnp.where(in_grp, jnp.exp(lg - gm), 0.0), axis=-1, keepdims=True)
    lo = E_LANE0 + grp * EXP_PER_GROUP
    el = jnp.where(jnp.logical_and(lane >= lo, lane < lo + EXP_PER_GROUP), lg, neg)
    v1, i1 = first_max(el)
    v2, i2 = first_max(jnp.where(lane == i1, neg, el))
    e21 = jnp.exp(v2 - v1)
    w0 = g_prob / (1.0 + e21)
    w1 = g_prob * e21 / (1.0 + e21)
    oh0 = lane == i1
    oh1 = lane == i2
    oh = jnp.logical_or(oh0, oh1).astype(F32)
    before = _tri(n, True).astype(BF16)
    cnt = jnp.dot(before, oh.astype(BF16), preferred_element_type=F32) + carry_ref[...]
    rank0 = jnp.sum(jnp.where(oh0, cnt, 0.0), axis=-1, keepdims=True)
    rank1 = jnp.sum(jnp.where(oh1, cnt, 0.0), axis=-1, keepdims=True)
    carry = carry_ref[...] + jnp.sum(oh, axis=0, keepdims=True)
    carry_ref[...] = carry
    cnt_ref[...] = carry
    cols = (i1 - E_LANE0, i2 - E_LANE0, rank0, rank1, w0, w1)
    info = jnp.zeros((n, LANES), F32)
    for j, col in enumerate(cols):
        info = jnp.where(lane == j, col, info)
    info_ref[...] = info


def _route(logits):
    T = logits.shape[0]
    tr = min(ROUTE_ROWS, T)
    info, cnt = pl.pallas_call(
        _route_kernel,
        grid=(T // tr,),
        in_specs=[pl.BlockSpec((tr, LANES), lambda i: (i, 0))],
        out_specs=[pl.BlockSpec((tr, LANES), lambda i: (i, 0)),
                   pl.BlockSpec((1, LANES), lambda i: (0, 0))],
        out_shape=[jax.ShapeDtypeStruct((T, LANES), F32), jax.ShapeDtypeStruct((1, LANES), F32)],
        scratch_shapes=[pltpu.VMEM((1, LANES), F32)],
        compiler_params=pltpu.CompilerParams(dimension_semantics=("arbitrary",)),
        name="moe_route",
    )(logits)
    A = T * TOP_K
    counts = cnt[0, E_LANE0:E_LANE0 + N_EXPERTS].astype(jnp.int32)
    padded = (counts + MOE_ROWS - 1) // MOE_ROWS * MOE_ROWS
    pad_end = jnp.cumsum(padded)
    pad_start = pad_end - padded
    n_blocks = -(-A // MOE_ROWS) + N_EXPERTS
    eid = info[:, 0:2].astype(jnp.int32)
    dest = jnp.take(pad_start, eid) + info[:, 2:4].astype(jnp.int32)
    blk_start = jnp.arange(n_blocks, dtype=jnp.int32) * MOE_ROWS
    blk_exp = jnp.minimum(jnp.sum(pad_end[None, :] <= blk_start[:, None], axis=1),
                          N_EXPERTS - 1).astype(jnp.int32)
    n_used = (pad_end[-1:] // MOE_ROWS).astype(jnp.int32)
    return info, dest[:, 0], dest[:, 1], blk_exp, n_used, n_blocks * MOE_ROWS


def _dispatch_kernel(d0_ref, d1_ref, h2_ref, xin_in, xin_hbm, sem):
    del xin_in
    n = h2_ref.shape[0]
    base = pl.program_id(0) * n

    def body(r, _):
        pltpu.make_async_copy(h2_ref.at[r], xin_hbm.at[d0_ref[base + r]], sem).start()
        pltpu.make_async_copy(h2_ref.at[r], xin_hbm.at[d1_ref[base + r]], sem).start()
        return 0

    lax.fori_loop(0, n, body, 0, unroll=8)
    for _ in range(TOP_K):
        pltpu.make_async_copy(h2_ref, xin_hbm.at[pl.ds(0, n)], sem).wait()


def _dispatch(h2_3d, dest0, dest1, n_rows, td):
    T, nsub, _ = h2_3d.shape
    grid_spec = pltpu.PrefetchScalarGridSpec(
        num_scalar_prefetch=2,
        grid=(T // td,),
        in_specs=[pl.BlockSpec((td, nsub, LANES), lambda i, d0, d1: (i, 0, 0)),
                  pl.BlockSpec(memory_space=pl.ANY)],
        out_specs=pl.BlockSpec(memory_space=pl.ANY),
        scratch_shapes=[pltpu.SemaphoreType.DMA(())],
    )
    return pl.pallas_call(
        _dispatch_kernel,
        grid_spec=grid_spec,
        out_shape=jax.ShapeDtypeStruct((n_rows, nsub, LANES), F32),
        input_output_aliases={3: 0},
        compiler_params=pltpu.CompilerParams(dimension_semantics=("arbitrary",)),
        name="moe_dispatch",
    )(dest0, dest1, h2_3d, jnp.zeros((n_rows, nsub, LANES), F32))


def _moe_kernel(be_ref, nu_ref, x_ref, wg_ref, wu_ref, wd_ref, y_ref, wgb, wub, wdb):
    i = pl.program_id(0)
    nsub = D_MODEL // LANES

    @pl.when(i < nu_ref[0])
    def _():
        changed = jnp.logical_or(i == 0, be_ref[i] != be_ref[jnp.maximum(i - 1, 0)])

        @pl.when(changed)
        def _():
            wgb[...] = wg_ref[...].astype(BF16)
            wub[...] = wu_ref[...].astype(BF16)
            wdb[...] = wd_ref[...].astype(BF16)

        hg = jnp.zeros((MOE_ROWS, D_EXPERT), F32)
        hu = jnp.zeros((MOE_ROWS, D_EXPERT), F32)
        for s in range(nsub):
            xs = x_ref[:, s, :].astype(BF16)
            hg = hg + jnp.dot(xs, wgb[s * LANES:(s + 1) * LANES, :], preferred_element_type=F32)
            hu = hu + jnp.dot(xs, wub[s * LANES:(s + 1) * LANES, :], preferred_element_type=F32)
        hid = (hg * _sigmoid(hg) * hu).astype(BF16)
        y = jnp.dot(hid, wdb[...], preferred_element_type=F32)
        for s in range(nsub):
            y_ref[:, s, :] = y[:, s * LANES:(s + 1) * LANES]

    @pl.when(i >= nu_ref[0])
    def _():
        y_ref[...] = jnp.zeros_like(y_ref)


def _moe(xin, blk_exp, n_used, w_gate, w_up, w_down, layer):
    R, nsub, _ = xin.shape
    n_blocks = R // MOE_ROWS
    wspec = lambda m, n: pl.BlockSpec((None, None, m, n), lambda i, be, nu: (layer, be[i], 0, 0))
    last = lambda i, nu: jnp.minimum(i, nu[0] - 1)
    grid_spec = pltpu.PrefetchScalarGridSpec(
        num_scalar_prefetch=2,
        grid=(n_blocks,),
        in_specs=[
            pl.BlockSpec((MOE_ROWS, nsub, LANES), lambda i, be, nu: (last(i, nu), 0, 0)),
            wspec(D_MODEL, D_EXPERT), wspec(D_MODEL, D_EXPERT), wspec(D_EXPERT, D_MODEL),
        ],
        out_specs=pl.BlockSpec((MOE_ROWS, nsub, LANES), lambda i, be, nu: (i, 0, 0)),
        scratch_shapes=[
            pltpu.VMEM((D_MODEL, D_EXPERT), BF16),
            pltpu.VMEM((D_MODEL, D_EXPERT), BF16),
            pltpu.VMEM((D_EXPERT, D_MODEL), BF16),
        ],
    )
    return pl.pallas_call(
        _moe_kernel,
        grid_spec=grid_spec,
        out_shape=jax.ShapeDtypeStruct((R, nsub, LANES), F32),
        compiler_params=pltpu.CompilerParams(
            dimension_semantics=("arbitrary",), vmem_limit_bytes=VMEM_LIMIT),
        name="moe_ffn",
    )(blk_exp, n_used, xin, w_gate, w_up, w_down)


def _combine_kernel(d0_ref, d1_ref, y_hbm, x1_ref, info_ref, mod_ref, o_ref, ybuf, sem):
    i = pl.program_id(0)
    nsteps = pl.num_programs(0)
    n = x1_ref.shape[0]
    slot = i % 2
    D = D_MODEL

    def start_gather(step, sl):
        base = step * n

        def body(r, _):
            pltpu.make_async_copy(y_hbm.at[d0_ref[base + r]], ybuf.at[sl, 0, r], sem.at[sl]).start()
            pltpu.make_async_copy(y_hbm.at[d1_ref[base + r]], ybuf.at[sl, 1, r], sem.at[sl]).start()
            return 0
        lax.fori_loop(0, n, body, 0, unroll=8)

    @pl.when(i == 0)
    def _():
        start_gather(0, 0)

    @pl.when(i + 1 < nsteps)
    def _():
        start_gather(i + 1, 1 - slot)

    for k in range(TOP_K):
        pltpu.make_async_copy(y_hbm.at[pl.ds(0, n)], ybuf.at[slot, k], sem.at[slot]).wait()

    w0 = info_ref[:, 4:5]
    w1 = info_ref[:, 5:6]
    for s in range(D // LANES):
        cols = slice(s * LANES, (s + 1) * LANES)
        gt2 = mod_ref[:, 5 * D + s * LANES:5 * D + (s + 1) * LANES]
        moe = w0 * ybuf[slot, 0, :, s, :] + w1 * ybuf[slot, 1, :, s, :]
        o_ref[:, cols] = x1_ref[:, cols] + gt2 * moe


def _combine(x1, y, info, dest0, dest1, mod_l, S, tm):
    T, D = x1.shape
    tiles_per_batch = S // tm
    nsub = D // LANES
    grid_spec = pltpu.PrefetchScalarGridSpec(
        num_scalar_prefetch=2,
        grid=(T // tm,),
        in_specs=[
            pl.BlockSpec(memory_space=pl.ANY),
            pl.BlockSpec((tm, D), lambda i, d0, d1: (i, 0)),
            pl.BlockSpec((tm, LANES), lambda i, d0, d1: (i, 0)),
            pl.BlockSpec((None, 1, 6 * D), lambda i, d0, d1: (i // tiles_per_batch, 0, 0)),
        ],
        out_specs=pl.BlockSpec((tm, D), lambda i, d0, d1: (i, 0)),
        scratch_shapes=[pltpu.VMEM((2, TOP_K, tm, nsub, LANES), F32),
                        pltpu.SemaphoreType.DMA((2,))],
    )
    return pl.pallas_call(
        _combine_kernel,
        grid_spec=grid_spec,
        out_shape=jax.ShapeDtypeStruct((T, D), F32),
        compiler_params=pltpu.CompilerParams(
            dimension_semantics=("arbitrary",), vmem_limit_bytes=VMEM_LIMIT),
        name="moe_combine",
    )(dest0, dest1, y, x1, info, mod_l)


def _pad_w_in(w):
    D = w.shape[0]
    c0 = A_COLS + R_COLS
    c_ad = c0 + 2 * G_KW + G_VW
    c_gg = c_ad + G_LORA
    pad = jnp.zeros((D, LANES - G_LORA), w.dtype)
    return jnp.concatenate([w[:, :c_gg], pad, w[:, c_gg:]], axis=1).astype(BF16)


def kernel(x, c, ada_w, ada_b, norm1_g, norm2_g, w_in, attn_qn_g, attn_kn_g, attn_lambda,
           attn_subln_g, rwkv_mu, rwkv_w_up, rwkv_w0, rwkv_a_up, rwkv_a0, rwkv_g_up, rwkv_k_k,
           rwkv_k_a, rwkv_r_k, rwkv_lnx_g, rwkv_lnx_b, gla_alpha_up, gla_alpha_b, gla_norm_g,
           proj_attn, proj_rwkv, proj_gla, w_out, router_grp_w, router_grp_b, router_exp_w,
           router_exp_b, exp_w_gate, exp_w_up, exp_w_down):
    B, S, D = x.shape
    T = B * S
    L = ada_w.shape[0]
    tm = 256
    mod = _adaln(c, ada_w, ada_b).reshape(L, B, 1, 6 * D)
    x2 = x.reshape(T, D)
    for l in range(L):
        lambda_init = 0.8 - 0.6 * math.exp(-0.3 * l)
        p_attn, p_rwkv, p_gla, p_gate = _inproj(x2, mod[l], norm1_g[l], _pad_w_in(w_in[l]), S, tm)
        o_a = _attention(p_attn, attn_qn_g[l], attn_kn_g[l], attn_lambda[l], attn_subln_g[l],
                         lambda_init, B, S)
        o_r = _rwkv(p_rwkv, rwkv_mu[l], rwkv_w_up[l], rwkv_w0[l], rwkv_a_up[l], rwkv_a0[l],
                    rwkv_g_up[l], rwkv_k_k[l], rwkv_k_a[l], rwkv_r_k[l], rwkv_lnx_g[l],
                    rwkv_lnx_b[l], B, S)
        o_g = _gla(p_gla, gla_alpha_up[l], gla_alpha_b[l], gla_norm_g[l], B, S)
        n_r = N_GROUPS + N_EXPERTS
        router_w = jnp.concatenate(
            [router_grp_w[l], router_exp_w[l], jnp.zeros((D, LANES - n_r), F32)], axis=1)
        router_b = jnp.concatenate(
            [router_grp_b[l], router_exp_b[l], jnp.zeros((LANES - n_r,), F32)]).reshape(1, LANES)
        x1, h2, logits = _merge(
            x2, o_a, o_r, o_g, p_gate, proj_attn[l].astype(BF16), proj_rwkv[l].astype(BF16),
            proj_gla[l].astype(BF16), w_out[l].astype(BF16), mod[l], norm2_g[l],
            router_w, router_b, S, tm)
        info, dest0, dest1, blk_exp, n_used, n_rows = _route(logits)
        xin = _dispatch(h2, dest0, dest1, n_rows, tm)
        y = _moe(xin, blk_exp, n_used, exp_w_gate, exp_w_up, exp_w_down, l)
        x2 = _combine(x1, y, info, dest0, dest1, mod[l], S, tm)
    return x2.reshape(B, S, D)
```

```python
import functools
import math

import jax
import jax.numpy as jnp
from jax import lax
from jax.experimental import pallas as pl
from jax.experimental.pallas import tpu as pltpu

F32 = jnp.float32
BF16 = jnp.bfloat16
HIGHEST = lax.Precision.HIGHEST

D_MODEL = 1024
A_HEADS, A_DH, A_DV = 4, 64, 128
A_QW, A_VW = 512, 512
A_COLS = 1536
R_HEADS, R_N, R_W = 8, 64, 512
R_COLS = 1792
RWKV_GN_EPS = 64e-5
G_HEADS, G_DK, G_DV = 4, 64, 128
G_KW, G_VW, G_LORA = 256, 512, 16
G_TAU = 16.0
G_COLS = 1552
G_COLS_PAD = 1664
GATE_COLS = 3072
N_GROUPS, EXP_PER_GROUP, N_EXPERTS, TOP_K = 4, 8, 32, 2
D_EXPERT = 512
EPS = 1e-6

LANES = 128
SUBLANES = 8
CHUNK = 64
VMEM_LIMIT = 56 * 1024 * 1024


def _dot(a, b):
    return jnp.dot(a.astype(BF16), b.astype(BF16), preferred_element_type=F32)


def _dot_hi(a, b):
    return jnp.dot(a, b, precision=HIGHEST, preferred_element_type=F32)


def _dot_nt(a, b, precision=None):
    return lax.dot_general(a, b, (((1,), (1,)), ((), ())), precision=precision,
                           preferred_element_type=F32)


def _dot_tn(a, b, precision=None):
    return lax.dot_general(a, b, (((0,), (0,)), ((), ())), precision=precision,
                           preferred_element_type=F32)


def _sigmoid(x):
    return 1.0 / (1.0 + jnp.exp(-x))


def _softplus(x):
    return jnp.maximum(x, 0.0) + jnp.log(1.0 + jnp.exp(-jnp.abs(x)))


def _seg_ones(n, seg):
    r = lax.broadcasted_iota(jnp.int32, (n, n), 0) // seg
    c = lax.broadcasted_iota(jnp.int32, (n, n), 1) // seg
    return (r == c).astype(F32)


def _tri(n, strict):
    r = lax.broadcasted_iota(jnp.int32, (n, n), 0)
    c = lax.broadcasted_iota(jnp.int32, (n, n), 1)
    return (c < r) if strict else (c <= r)


def _adaln_kernel(c_ref, w_ref, b_ref, o_ref):
    c = c_ref[...]
    c_act = c * _sigmoid(c)
    o_ref[...] = _dot_hi(c_act, w_ref[...]) + b_ref[...]


def _adaln(c, ada_w, ada_b):
    L, D, N = ada_w.shape
    B = c.shape[0]
    tn = D
    return pl.pallas_call(
        _adaln_kernel,
        grid=(L, N // tn),
        in_specs=[
            pl.BlockSpec((B, D), lambda l, j: (0, 0)),
            pl.BlockSpec((None, D, tn), lambda l, j: (l, 0, j)),
            pl.BlockSpec((None, 1, tn), lambda l, j: (l, 0, j)),
        ],
        out_specs=pl.BlockSpec((None, B, tn), lambda l, j: (l, 0, j)),
        out_shape=jax.ShapeDtypeStruct((L, B, N), F32),
        name="adaln",
    )(c, ada_w, ada_b.reshape(L, 1, N))


_IN_SEGS = (A_COLS, R_COLS, G_COLS_PAD, GATE_COLS)
_IN_DTYPES = (BF16, F32, F32, BF16)
_IN_CHUNK = 512


def _inproj_kernel(x_ref, mod_ref, g_ref, w_ref, *o_refs):
    x = x_ref[...]
    D = x.shape[-1]
    ms = jnp.mean(x * x, axis=-1, keepdims=True)
    y = x * lax.rsqrt(ms + EPS) * g_ref[...]
    sh = mod_ref[:, 0:D]
    sc = mod_ref[:, D:2 * D]
    h = (y * (1.0 + sc) + sh).astype(BF16)
    base = 0
    for o_ref, width in zip(o_refs, _IN_SEGS):
        for c0 in range(0, width, _IN_CHUNK):
            c1 = min(c0 + _IN_CHUNK, width)
            o_ref[:, c0:c1] = jnp.dot(
                h, w_ref[:, base + c0:base + c1], preferred_element_type=F32
            ).astype(o_ref.dtype)
        base += width


def _inproj(x2, mod_l, norm_g, w_pad, S, tm):
    T, D = x2.shape
    NP = w_pad.shape[1]
    tiles_per_batch = S // tm
    return pl.pallas_call(
        _inproj_kernel,
        grid=(T // tm,),
        in_specs=[
            pl.BlockSpec((tm, D), lambda i: (i, 0)),
            pl.BlockSpec((None, 1, 2 * D), lambda i: (i // tiles_per_batch, 0, 0)),
            pl.BlockSpec((1, D), lambda i: (0, 0)),
            pl.BlockSpec((D, NP), lambda i: (0, 0), pipeline_mode=pl.Buffered(1)),
        ],
        out_specs=[pl.BlockSpec((tm, w), lambda i: (i, 0)) for w in _IN_SEGS],
        out_shape=[jax.ShapeDtypeStruct((T, w), dt) for w, dt in zip(_IN_SEGS, _IN_DTYPES)],
        compiler_params=pltpu.CompilerParams(
            dimension_semantics=("arbitrary",), vmem_limit_bytes=VMEM_LIMIT),
        name="inproj",
    )(x2, mod_l, norm_g.reshape(1, D), w_pad)


A_TILE = 256
A_POS_SPLIT = 64


def _split_dot(x, ones):
    hi = x.astype(BF16)
    lo = (x - hi.astype(F32)).astype(BF16)
    return (jnp.dot(hi, ones, preferred_element_type=F32)
            + jnp.dot(lo, ones, preferred_element_type=F32))


def _eye(n):
    r = lax.broadcasted_iota(jnp.int32, (n, n), 0)
    c = lax.broadcasted_iota(jnp.int32, (n, n), 1)
    return (r == c).astype(BF16)


def _attn_kernel(q_ref, k_ref, v_ref, qg_ref, kg_ref, lam_ref, sg_ref, slope_ref, o_ref,
                 qt_s, ka_s, vt_s, s_s, *, S, lambda_init):
    t = A_TILE
    seg = _seg_ones(LANES, A_DH).astype(BF16)
    eye_t = _eye(t)
    eye_v = _eye(A_DV)
    slope = slope_ref[...]
    lane = lax.broadcasted_iota(jnp.int32, (t, LANES), 1)
    row = lax.broadcasted_iota(jnp.int32, (t, LANES), 0)

    def qknorm(x, g, scale):
        xf = x.astype(F32)
        ms = _split_dot(xf * xf, seg) * (1.0 / A_DH)
        return xf * lax.rsqrt(ms + EPS) * g * scale

    for b in range(S // t):
        rows = slice(b * t, (b + 1) * t)
        pos = row + b * t
        hi = (pos // A_POS_SPLIT).astype(F32)
        lo = (pos % A_POS_SPLIT).astype(F32)
        q_aug = jnp.where(lane == A_DH, A_POS_SPLIT * slope,
                jnp.where(lane == A_DH + 1, slope,
                jnp.where(lane == A_DH + 2, -A_POS_SPLIT * slope * hi,
                jnp.where(lane == A_DH + 3, -slope * lo, 0.0))))
        k_aug = jnp.where(lane == A_DH, hi,
                jnp.where(lane == A_DH + 1, lo,
                jnp.where(jnp.logical_or(lane == A_DH + 2, lane == A_DH + 3), 1.0, 0.0)))
        qn = qknorm(q_ref[rows, :], qg_ref[...], A_DH ** -0.5)
        kn = qknorm(k_ref[rows, :], kg_ref[...], 1.0)
        for c in range(2):
            qc = qn if c == 0 else pltpu.roll(qn, A_DH, 1)
            kc = kn if c == 0 else pltpu.roll(kn, A_DH, 1)
            qa = jnp.where(lane < A_DH, qc, q_aug).astype(BF16)
            qt_s[c, :, rows] = _dot_tn(qa, eye_t).astype(BF16)
            ka_s[c, rows, :] = jnp.where(lane < A_DH, kc, k_aug).astype(BF16)
        vt_s[:, rows] = _dot_tn(v_ref[rows, :], eye_t).astype(BF16)

    lv = lam_ref[...]
    lam = (jnp.exp(jnp.sum(lv[0:1] * lv[1:2], axis=-1, keepdims=True))
           - jnp.exp(jnp.sum(lv[2:3] * lv[3:4], axis=-1, keepdims=True)) + lambda_init)
    causal = (lax.broadcasted_iota(jnp.int32, (t, t), 0)
              <= lax.broadcasted_iota(jnp.int32, (t, t), 1))

    for i in range(S // t):
        qcols = slice(i * t, (i + 1) * t)
        res = []
        for c in range(2):
            qt = qt_s[c, :, qcols]
            m = None
            for j in range(i + 1):
                s = jnp.dot(ka_s[c, j * t:(j + 1) * t, :], qt, preferred_element_type=F32)
                if j == i:
                    s = jnp.where(causal, s, -jnp.inf)
                s_s[c, j * t:(j + 1) * t, :] = s
                mj = jnp.max(s, axis=0, keepdims=True)
                m = mj if m is None else jnp.maximum(m, mj)
            l = jnp.zeros((1, t), F32)
            acc = jnp.zeros((A_DV, t), F32)
            for j in range(i + 1):
                p = jnp.exp(s_s[c, j * t:(j + 1) * t, :] - m)
                l = l + jnp.sum(p, axis=0, keepdims=True)
                acc = acc + jnp.dot(vt_s[:, j * t:(j + 1) * t], p.astype(BF16),
                                    preferred_element_type=F32)
            res.append(acc / l)
        o = res[0] - lam * res[1]
        ms = jnp.mean(o * o, axis=0, keepdims=True)
        o = o * lax.rsqrt(ms + EPS) * sg_ref[...] * (1.0 - lambda_init)
        o_ref[qcols, :] = _dot_tn(o.astype(BF16), eye_v).astype(o_ref.dtype)


def _attention(p_attn, qn_g, kn_g, lam_vecs, subln_g, lambda_init, B, S):
    pa = p_attn.reshape(B, S, A_COLS)
    dup = lambda g: jnp.concatenate([g, g]).reshape(1, LANES)
    slopes = jnp.asarray(
        [[2.0 ** (-8.0 * (i + 1) / A_HEADS)] * LANES for i in range(A_HEADS)], F32
    ).reshape(A_HEADS, 1, LANES)
    nqb = A_QW // LANES
    kern = functools.partial(_attn_kernel, S=S, lambda_init=lambda_init)
    out = pl.pallas_call(
        kern,
        grid=(B, A_HEADS),
        in_specs=[
            pl.BlockSpec((None, S, LANES), lambda b, h: (b, 0, h)),
            pl.BlockSpec((None, S, LANES), lambda b, h: (b, 0, nqb + h)),
            pl.BlockSpec((None, S, LANES), lambda b, h: (b, 0, 2 * nqb + h)),
            pl.BlockSpec((1, LANES), lambda b, h: (0, 0)),
            pl.BlockSpec((1, LANES), lambda b, h: (0, 0)),
            pl.BlockSpec((4, A_DH), lambda b, h: (0, 0)),
            pl.BlockSpec((A_DV, 1), lambda b, h: (0, 0)),
            pl.BlockSpec((None, 1, LANES), lambda b, h: (h, 0, 0)),
        ],
        out_specs=pl.BlockSpec((None, S, A_DV), lambda b, h: (b, 0, h)),
        out_shape=jax.ShapeDtypeStruct((B, S, A_VW), BF16),
        scratch_shapes=[pltpu.VMEM((2, LANES, S), BF16), pltpu.VMEM((2, S, LANES), BF16),
                        pltpu.VMEM((A_DV, S), BF16), pltpu.VMEM((2, S, A_TILE), F32)],
        compiler_params=pltpu.CompilerParams(
            dimension_semantics=("arbitrary", "arbitrary"), vmem_limit_bytes=VMEM_LIMIT),
        name="diff_attn",
    )(pa, pa, pa, dup(qn_g), dup(kn_g), lam_vecs, subln_g.reshape(A_DV, 1), slopes)
    return out.reshape(B * S, A_VW)


R_GROUP = 4
R_GW = R_GROUP * R_N


def _rwkv_kernel(p_ref, mu_ref, wup_ref, w0_ref, aup_ref, a0_ref, gup_ref, kk_ref, ka_ref,
                 rk_ref, lg_ref, lb_ref, o_ref,
                 carry_ref, st_ref, al_s, be_s, ka_s, rh_s, bt_s, kt_s, v_s, gc_s, y_s, *, TB):
    @pl.when(pl.program_id(1) == 0)
    def _():
        carry_ref[...] = jnp.zeros_like(carry_ref)
        st_ref[...] = jnp.zeros_like(st_ref)

    xs = p_ref[...]
    prev = pltpu.roll(xs, 1, 0)
    row = lax.broadcasted_iota(jnp.int32, (TB, 1), 0)
    prev = jnp.where(row == 0, carry_ref[...], prev)
    carry_ref[...] = xs[TB - 1:TB, :]
    xm = xs + (prev - xs) * mu_ref[...]
    r = xm[:, 0:R_W]
    k = xm[:, R_W:2 * R_W]
    v = xm[:, 2 * R_W:3 * R_W]
    wa = xm[:, 3 * R_W:3 * R_W + LANES]
    gd = xm[:, 3 * R_W + LANES:3 * R_W + 2 * LANES]
    wz = w0_ref[...] + _dot_hi(jnp.tanh(wa), wup_ref[...])
    w_log = -_softplus(-wz) - 0.5
    lw = -jnp.exp(w_log)
    a = _sigmoid(a0_ref[...] + _dot_hi(wa, aup_ref[...]))
    g = _dot_hi(_sigmoid(gd), gup_ref[...])
    seg = _seg_ones(R_W, R_N).astype(BF16)
    kk = k * kk_ref[...]
    nrm = jnp.sqrt(_split_dot(kk * kk, seg))
    kk = kk / jnp.maximum(nrm, 1e-12)
    k2 = k * (1.0 + (a - 1.0) * ka_ref[...])
    bonus = _split_dot(r * k2 * rk_ref[...], seg) * v
    bv = kk * a

    rr = lax.broadcasted_iota(jnp.int32, (TB, TB), 0)
    cc = lax.broadcasted_iota(jnp.int32, (TB, TB), 1)
    tril_blk = jnp.logical_and(rr // CHUNK == cc // CHUNK, cc <= rr).astype(F32)
    Lg = _dot_hi(tril_blk, lw)
    inv = jnp.exp(-Lg)
    al_s[...] = (jnp.exp(Lg - lw) * kk).astype(BF16)
    be_s[...] = (bv * inv).astype(BF16)
    ka_s[...] = (k2 * inv).astype(BF16)
    rh_s[...] = (jnp.exp(Lg) * r).astype(BF16)
    v_s[...] = v.astype(BF16)
    for c in range(TB // CHUNK):
        rows = slice(c * CHUNK, (c + 1) * CHUNK)
        gC = Lg[(c + 1) * CHUNK - 1:(c + 1) * CHUNK, :]
        tail = jnp.exp(gC - Lg[rows, :])
        bt_s[rows, :] = (bv[rows, :] * tail).astype(BF16)
        kt_s[rows, :] = (k2[rows, :] * tail).astype(BF16)
        gc_s[c * SUBLANES:(c + 1) * SUBLANES, :] = jnp.broadcast_to(jnp.exp(gC), (SUBLANES, R_W))

    ri = lax.broadcasted_iota(jnp.int32, (R_GW, R_GW), 0)
    ci = lax.broadcasted_iota(jnp.int32, (R_GW, R_GW), 1)
    blk = ri // R_N == ci // R_N
    strict = ci % R_N < ri % R_N
    incl = ci % R_N <= ri % R_N
    zero = jnp.zeros((), BF16)

    def expand(x):
        return jnp.where(blk, jnp.concatenate([x] * R_GROUP, axis=0), zero)

    def chunk(c, _):
        r0 = pl.multiple_of(c * CHUNK, CHUNK)
        rows = pl.ds(r0, CHUNK)
        g0 = pl.multiple_of(c * SUBLANES, SUBLANES)
        for gi in range(R_HEADS // R_GROUP):
            cols = slice(gi * R_GW, (gi + 1) * R_GW)
            A_ = expand(al_s[rows, cols])
            R_ = expand(rh_s[rows, cols])
            B_ = expand(be_s[rows, cols])
            K_ = expand(ka_s[rows, cols])
            Bt = expand(bt_s[rows, cols])
            Kt = expand(kt_s[rows, cols])
            vc = v_s[rows, cols]
            Vb = jnp.concatenate([vc[:, h * R_N:(h + 1) * R_N] for h in range(R_GROUP)], axis=0)
            m_ab = jnp.where(strict, _dot_nt(A_, B_), 0.0)
            m_ak = jnp.where(strict, _dot_nt(A_, K_), 0.0).astype(BF16)
            m_rb = jnp.where(incl, _dot_nt(R_, B_), 0.0).astype(BF16)
            m_rk = jnp.where(incl, _dot_nt(R_, K_), 0.0).astype(BF16)
            Z = A_.astype(F32)
            W = jnp.dot(m_ak, Vb, preferred_element_type=F32)
            X = -m_ab
            n = 1
            while True:
                Xb = X.astype(BF16)
                Z = Z + jnp.dot(Xb, Z.astype(BF16), preferred_element_type=F32)
                W = W + jnp.dot(Xb, W.astype(BF16), preferred_element_type=F32)
                n *= 2
                if n >= CHUNK:
                    break
                X = jnp.dot(Xb, Xb, preferred_element_type=F32)
            Zb = Z.astype(BF16)
            Wb = W.astype(BF16)
            y_a = R_.astype(F32) - jnp.dot(m_rb, Zb, preferred_element_type=F32)
            y_b = (jnp.dot(m_rk, Vb, preferred_element_type=F32)
                   - jnp.dot(m_rb, Wb, preferred_element_type=F32))
            p_neg = _dot_tn(Zb, Bt)
            q = _dot_tn(Vb, Kt) - _dot_tn(Wb, Bt)
            S0 = st_ref[gi]
            S0b = S0.astype(BF16)
            y = _dot_nt(y_a.astype(BF16), S0b) + y_b
            st_ref[gi] = (S0 * gc_s[pl.ds(g0, SUBLANES), cols][0:1, :]
                          - jnp.dot(S0b, p_neg.astype(BF16), preferred_element_type=F32) + q)
            for h in range(R_GROUP):
                hh = gi * R_GROUP + h
                y_s[rows, hh * R_N:(hh + 1) * R_N] = y[h * R_N:(h + 1) * R_N, :]
        return 0

    lax.fori_loop(0, TB // CHUNK, chunk, 0)

    y = y_s[...]
    mean = _split_dot(y, seg) * (1.0 / R_N)
    yc = y - mean
    var = _split_dot(yc * yc, seg) * (1.0 / R_N)
    yn = yc * lax.rsqrt(var + RWKV_GN_EPS) * lg_ref[...] + lb_ref[...]
    o_ref[...] = ((yn + bonus) * g).astype(o_ref.dtype)


def _rwkv(p_rwkv, mu, w_up, w0, a_up, a0, g_up, k_k, k_a, r_k, lnx_g, lnx_b, B, S):
    TB = 256
    T = B * S
    nt = S // TB
    row = lambda t: t.reshape(1, -1)
    zeros = jnp.zeros((R_N, R_W), F32)
    wup_pad = jnp.concatenate([w_up, zeros], axis=0)
    aup_pad = jnp.concatenate([zeros, a_up], axis=0)
    vec = lambda n: pl.BlockSpec((1, n), lambda b, i: (0, 0))
    mat = lambda m, n: pl.BlockSpec((m, n), lambda b, i: (0, 0))
    kern = functools.partial(_rwkv_kernel, TB=TB)
    return pl.pallas_call(
        kern,
        grid=(B, nt),
        in_specs=[
            pl.BlockSpec((TB, R_COLS), lambda b, i: (b * nt + i, 0)),
            vec(R_COLS), mat(LANES, R_W), vec(R_W), mat(LANES, R_W), vec(R_W), mat(LANES, R_W),
            vec(R_W), vec(R_W), vec(R_W), vec(R_W), vec(R_W),
        ],
        out_specs=pl.BlockSpec((TB, R_W), lambda b, i: (b * nt + i, 0)),
        out_shape=jax.ShapeDtypeStruct((T, R_W), BF16),
        scratch_shapes=[
            pltpu.VMEM((1, R_COLS), F32),
            pltpu.VMEM((R_HEADS // R_GROUP, R_N, R_GW), F32),
        ] + [pltpu.VMEM((TB, R_W), BF16)] * 7 + [
            pltpu.VMEM((TB // CHUNK * SUBLANES, R_W), F32),
            pltpu.VMEM((TB, R_W), F32),
        ],
        compiler_params=pltpu.CompilerParams(
            dimension_semantics=("arbitrary", "arbitrary"), vmem_limit_bytes=VMEM_LIMIT),
        name="rwkv7",
    )(p_rwkv, row(mu), wup_pad, row(w0), aup_pad, row(a0), g_up, row(k_k), row(k_a),
      row(r_k), row(lnx_g), row(lnx_b))


def _gla_kernel(p_ref, aup_ref, ab_ref, ng_ref, o_ref, st_ref, *, TB):
    @pl.when(pl.program_id(1) == 0)
    def _():
        st_ref[...] = jnp.zeros_like(st_ref)

    tri_incl = _tri(CHUNK, False)
    tri_f = tri_incl.astype(F32)
    c_gv = 2 * G_KW
    c_ad = c_gv + G_VW
    c_gate = c_ad + LANES

    def chunk(ci, _):
        r0 = pl.multiple_of(ci * CHUNK, CHUNK)
        rows = pl.ds(r0, CHUNK)
        q = p_ref[rows, 0:G_KW] * (G_DK ** -0.5)
        k = p_ref[rows, G_KW:2 * G_KW]
        v = p_ref[rows, c_gv:c_gv + G_VW]
        gate = p_ref[rows, c_gate:c_gate + G_VW]
        z = _dot_hi(p_ref[rows, c_ad:c_ad + LANES], aup_ref[...]) + ab_ref[...]
        la = -_softplus(-z) * (1.0 / G_TAU)
        b = _dot_hi(tri_f, la)
        b_last = b[CHUNK - 1:CHUNK, :]
        qe = (q * jnp.exp(b)).astype(BF16)
        ke = (k * jnp.exp(-b)).astype(BF16)
        kt = (k * jnp.exp(b_last - b)).astype(BF16)
        e_last = jnp.exp(b_last)
        vb = v.astype(BF16)
        for h in range(G_HEADS):
            sl = slice(h * G_DK, (h + 1) * G_DK)
            vs = slice(h * G_DV, (h + 1) * G_DV)
            S0 = st_ref[h]
            sc = jnp.where(tri_incl, _dot_nt(qe[:, sl], ke[:, sl]), 0.0)
            o = _dot(sc, vb[:, vs]) + _dot_nt(qe[:, sl], S0.astype(BF16))
            st_ref[h] = S0 * e_last[:, sl] + _dot_tn(vb[:, vs], kt[:, sl])
            ms = jnp.mean(o * o, axis=-1, keepdims=True)
            gt = gate[:, vs]
            o = o * lax.rsqrt(ms + EPS) * ng_ref[...] * (gt * _sigmoid(gt))
            o_ref[rows, vs] = o.astype(o_ref.dtype)
        return 0

    lax.fori_loop(0, TB // CHUNK, chunk, 0)


def _gla(p_gla, alpha_up, alpha_b, norm_g, B, S):
    TB = 256
    T = B * S
    nt = S // TB
    aup_pad = jnp.concatenate([alpha_up, jnp.zeros((LANES - G_LORA, G_KW), F32)], axis=0)
    kern = functools.partial(_gla_kernel, TB=TB)
    return pl.pallas_call(
        kern,
        grid=(B, nt),
        in_specs=[
            pl.BlockSpec((TB, G_COLS_PAD), lambda b, i: (b * nt + i, 0)),
            pl.BlockSpec((LANES, G_KW), lambda b, i: (0, 0)),
            pl.BlockSpec((1, G_KW), lambda b, i: (0, 0)),
            pl.BlockSpec((1, G_DV), lambda b, i: (0, 0)),
        ],
        out_specs=pl.BlockSpec((TB, G_VW), lambda b, i: (b * nt + i, 0)),
        out_shape=jax.ShapeDtypeStruct((T, G_VW), BF16),
        scratch_shapes=[pltpu.VMEM((G_HEADS, G_DV, G_DK), F32)],
        compiler_params=pltpu.CompilerParams(
            dimension_semantics=("arbitrary", "arbitrary"), vmem_limit_bytes=VMEM_LIMIT),
        name="gla",
    )(p_gla, aup_pad, alpha_b.reshape(1, G_KW), norm_g.reshape(1, G_DV))


def _merge_kernel(x_ref, oa_ref, or_ref, og_ref, gate_ref, pa_ref, pr_ref, pg_ref, wo_ref,
                  mod_ref, g2_ref, rw_ref, rb_ref, x1_ref, h2_ref, lg_ref):
    D = D_MODEL
    merged = (_sigmoid(gate_ref[:, 0:D].astype(F32))
              * jnp.dot(oa_ref[...], pa_ref[...], preferred_element_type=F32)
              + _sigmoid(gate_ref[:, D:2 * D].astype(F32))
              * jnp.dot(or_ref[...], pr_ref[...], preferred_element_type=F32)
              + _sigmoid(gate_ref[:, 2 * D:3 * D].astype(F32))
              * jnp.dot(og_ref[...], pg_ref[...], preferred_element_type=F32))
    gt1 = mod_ref[:, 2 * D:3 * D]
    sh2 = mod_ref[:, 3 * D:4 * D]
    sc2 = mod_ref[:, 4 * D:5 * D]
    x1 = x_ref[...] + gt1 * jnp.dot(merged.astype(BF16), wo_ref[...], preferred_element_type=F32)
    x1_ref[...] = x1
    ms = jnp.mean(x1 * x1, axis=-1, keepdims=True)
    h2 = x1 * lax.rsqrt(ms + EPS) * g2_ref[...] * (1.0 + sc2) + sh2
    for s in range(D // LANES):
        h2_ref[:, s, :] = h2[:, s * LANES:(s + 1) * LANES]
    lg_ref[...] = _dot_hi(h2, rw_ref[...]) + rb_ref[...]


def _merge(x2, o_a, o_r, o_g, p_gate, proj_a, proj_r, proj_g, w_out, mod_l, norm2_g,
           router_w, router_b, S, tm):
    T, D = x2.shape
    tiles_per_batch = S // tm
    tile = lambda w: pl.BlockSpec((tm, w), lambda i: (i, 0))
    const = lambda m, n: pl.BlockSpec((m, n), lambda i: (0, 0))
    return pl.pallas_call(
        _merge_kernel,
        grid=(T // tm,),
        in_specs=[
            tile(D), tile(A_VW), tile(R_W), tile(G_VW), tile(GATE_COLS),
            const(A_VW, D), const(R_W, D), const(G_VW, D), const(D, D),
            pl.BlockSpec((None, 1, 6 * D), lambda i: (i // tiles_per_batch, 0, 0)),
            const(1, D), const(D, LANES), const(1, LANES),
        ],
        out_specs=[
            tile(D),
            pl.BlockSpec((tm, D // LANES, LANES), lambda i: (i, 0, 0)),
            tile(LANES),
        ],
        out_shape=[
            jax.ShapeDtypeStruct((T, D), F32),
            jax.ShapeDtypeStruct((T, D // LANES, LANES), F32),
            jax.ShapeDtypeStruct((T, LANES), F32),
        ],
        compiler_params=pltpu.CompilerParams(
            dimension_semantics=("arbitrary",), vmem_limit_bytes=VMEM_LIMIT),
        name="merge",
    )(x2, o_a, o_r, o_g, p_gate, proj_a, proj_r, proj_g, w_out, mod_l,
      norm2_g.reshape(1, D), router_w, router_b)


MOE_ROWS = 256
ROUTE_ROWS = 512
E_LANE0 = N_GROUPS


def _route_kernel(lg_ref, info_ref, cnt_ref, carry_ref):
    @pl.when(pl.program_id(0) == 0)
    def _():
        carry_ref[...] = jnp.zeros_like(carry_ref)

    lg = lg_ref[...]
    n = lg.shape[0]
    lane = lax.broadcasted_iota(jnp.int32, (n, LANES), 1).astype(F32)
    neg = -jnp.inf
    big = float(LANES)

    def first_max(vals):
        m = jnp.max(vals, axis=-1, keepdims=True)
        idx = jnp.min(jnp.where(vals == m, lane, big), axis=-1, keepdims=True)
        return m, idx

    in_grp = lane < N_GROUPS
    gm, grp = first_max(jnp.where(in_grp, lg, neg))
    g_prob = 1.0 / jnp.sum(jnp.where(in_grp, jnp.exp(lg - gm), 0.0), axis=-1, keepdims=True)
    lo = E_LANE0 + grp * EXP_PER_GROUP
    el = jnp.where(jnp.logical_and(lane >= lo, lane < lo + EXP_PER_GROUP), lg, neg)
    v1, i1 = first_max(el)
    v2, i2 = first_max(jnp.where(lane == i1, neg, el))
    e21 = jnp.exp(v2 - v1)
    w0 = g_prob / (1.0 + e21)
    w1 = g_prob * e21 / (1.0 + e21)
    oh0 = lane == i1
    oh1 = lane == i2
    oh = jnp.logical_or(oh0, oh1).astype(F32)
    before = _tri(n, True).astype(BF16)
    cnt = jnp.dot(before, oh.astype(BF16), preferred_element_type=F32) + carry_ref[...]
    rank0 = jnp.sum(jnp.where(oh0, cnt, 0.0), axis=-1, keepdims=True)
    rank1 = jnp.sum(jnp.where(oh1, cnt, 0.0), axis=-1, keepdims=True)
    carry = carry_ref[...] + jnp.sum(oh, axis=0, keepdims=True)
    carry_ref[...] = carry
    cnt_ref[...] = carry
    cols = (i1 - E_LANE0, i2 - E_LANE0, rank0, rank1, w0, w1)
    info = jnp.zeros((n, LANES), F32)
    for j, col in enumerate(cols):
        info = jnp.where(lane == j, col, info)
    info_ref[...] = info


def _route(logits):
    T = logits.shape[0]
    tr = min(ROUTE_ROWS, T)
    info, cnt = pl.pallas_call(
        _route_kernel,
        grid=(T // tr,),
        in_specs=[pl.BlockSpec((tr, LANES), lambda i: (i, 0))],
        out_specs=[pl.BlockSpec((tr, LANES), lambda i: (i, 0)),
                   pl.BlockSpec((1, LANES), lambda i: (0, 0))],
        out_shape=[jax.ShapeDtypeStruct((T, LANES), F32), jax.ShapeDtypeStruct((1, LANES), F32)],
        scratch_shapes=[pltpu.VMEM((1, LANES), F32)],
        compiler_params=pltpu.CompilerParams(dimension_semantics=("arbitrary",)),
        name="moe_route",
    )(logits)
    A = T * TOP_K
    counts = cnt[0, E_LANE0:E_LANE0 + N_EXPERTS].astype(jnp.int32)
    padded = (counts + MOE_ROWS - 1) // MOE_ROWS * MOE_ROWS
    pad_end = jnp.cumsum(padded)
    pad_start = pad_end - padded
    n_blocks = -(-A // MOE_ROWS) + N_EXPERTS
    eid = info[:, 0:2].astype(jnp.int32)
    dest = jnp.take(pad_start, eid) + info[:, 2:4].astype(jnp.int32)
    blk_start = jnp.arange(n_blocks, dtype=jnp.int32) * MOE_ROWS
    blk_exp = jnp.minimum(jnp.sum(pad_end[None, :] <= blk_start[:, None], axis=1),
                          N_EXPERTS - 1).astype(jnp.int32)
    n_used = (pad_end[-1:] // MOE_ROWS).astype(jnp.int32)
    return info, dest[:, 0], dest[:, 1], blk_exp, n_used, n_blocks * MOE_ROWS


def _dispatch_kernel(d0_ref, d1_ref, h2_ref, xin_in, xin_hbm, sem):
    del xin_in
    n = h2_ref.shape[0]
    base = pl.program_id(0) * n

    def body(r, _):
        pltpu.make_async_copy(h2_ref.at[r], xin_hbm.at[d0_ref[base + r]], sem).start()
        pltpu.make_async_copy(h2_ref.at[r], xin_hbm.at[d1_ref[base + r]], sem).start()
        return 0

    lax.fori_loop(0, n, body, 0, unroll=8)
    for _ in range(TOP_K):
        pltpu.make_async_copy(h2_ref, xin_hbm.at[pl.ds(0, n)], sem).wait()


def _dispatch(h2_3d, dest0, dest1, n_rows, td):
    T, nsub, _ = h2_3d.shape
    grid_spec = pltpu.PrefetchScalarGridSpec(
        num_scalar_prefetch=2,
        grid=(T // td,),
        in_specs=[pl.BlockSpec((td, nsub, LANES), lambda i, d0, d1: (i, 0, 0)),
                  pl.BlockSpec(memory_space=pl.ANY)],
        out_specs=pl.BlockSpec(memory_space=pl.ANY),
        scratch_shapes=[pltpu.SemaphoreType.DMA(())],
    )
    return pl.pallas_call(
        _dispatch_kernel,
        grid_spec=grid_spec,
        out_shape=jax.ShapeDtypeStruct((n_rows, nsub, LANES), F32),
        input_output_aliases={3: 0},
        compiler_params=pltpu.CompilerParams(dimension_semantics=("arbitrary",)),
        name="moe_dispatch",
    )(dest0, dest1, h2_3d, jnp.zeros((n_rows, nsub, LANES), F32))


def _moe_kernel(be_ref, nu_ref, x_ref, wg_ref, wu_ref, wd_ref, y_ref, wgb, wub, wdb):
    i = pl.program_id(0)
    nsub = D_MODEL // LANES

    @pl.when(i < nu_ref[0])
    def _():
        changed = jnp.logical_or(i == 0, be_ref[i] != be_ref[jnp.maximum(i - 1, 0)])

        @pl.when(changed)
        def _():
            wgb[...] = wg_ref[...].astype(BF16)
            wub[...] = wu_ref[...].astype(BF16)
            wdb[...] = wd_ref[...].astype(BF16)

        hg = jnp.zeros((MOE_ROWS, D_EXPERT), F32)
        hu = jnp.zeros((MOE_ROWS, D_EXPERT), F32)
        for s in range(nsub):
            xs = x_ref[:, s, :].astype(BF16)
            hg = hg + jnp.dot(xs, wgb[s * LANES:(s + 1) * LANES, :], preferred_element_type=F32)
            hu = hu + jnp.dot(xs, wub[s * LANES:(s + 1) * LANES, :], preferred_element_type=F32)
        hid = (hg * _sigmoid(hg) * hu).astype(BF16)
        y = jnp.dot(hid, wdb[...], preferred_element_type=F32)
        for s in range(nsub):
            y_ref[:, s, :] = y[:, s * LANES:(s + 1) * LANES]

    @pl.when(i >= nu_ref[0])
    def _():
        y_ref[...] = jnp.zeros_like(y_ref)


def _moe(xin, blk_exp, n_used, w_gate, w_up, w_down, layer):
    R, nsub, _ = xin.shape
    n_blocks = R // MOE_ROWS
    wspec = lambda m, n: pl.BlockSpec((None, None, m, n), lambda i, be, nu: (layer, be[i], 0, 0))
    last = lambda i, nu: jnp.minimum(i, nu[0] - 1)
    grid_spec = pltpu.PrefetchScalarGridSpec(
        num_scalar_prefetch=2,
        grid=(n_blocks,),
        in_specs=[
            pl.BlockSpec((MOE_ROWS, nsub, LANES), lambda i, be, nu: (last(i, nu), 0, 0)),
            wspec(D_MODEL, D_EXPERT), wspec(D_MODEL, D_EXPERT), wspec(D_EXPERT, D_MODEL),
        ],
        out_specs=pl.BlockSpec((MOE_ROWS, nsub, LANES), lambda i, be, nu: (i, 0, 0)),
        scratch_shapes=[
            pltpu.VMEM((D_MODEL, D_EXPERT), BF16),
            pltpu.VMEM((D_MODEL, D_EXPERT), BF16),
            pltpu.VMEM((D_EXPERT, D_MODEL), BF16),
        ],
    )
    return pl.pallas_call(
        _moe_kernel,
        grid_spec=grid_spec,
        out_shape=jax.ShapeDtypeStruct((R, nsub, LANES), F32),
        compiler_params=pltpu.CompilerParams(
            dimension_semantics=("arbitrary",), vmem_limit_bytes=VMEM_LIMIT),
        name="moe_ffn",
    )(blk_exp, n_used, xin, w_gate, w_up, w_down)


def _combine_kernel(d0_ref, d1_ref, y_hbm, x1_ref, info_ref, mod_ref, o_ref, ybuf, sem):
    i = pl.program_id(0)
    nsteps = pl.num_programs(0)
    n = x1_ref.shape[0]
    slot = i % 2
    D = D_MODEL

    def start_gather(step, sl):
        base = step * n

        def body(r, _):
            pltpu.make_async_copy(y_hbm.at[d0_ref[base + r]], ybuf.at[sl, 0, r], sem.at[sl]).start()
            pltpu.make_async_copy(y_hbm.at[d1_ref[base + r]], ybuf.at[sl, 1, r], sem.at[sl]).start()
            return 0
        lax.fori_loop(0, n, body, 0, unroll=8)

    @pl.when(i == 0)
    def _():
        start_gather(0, 0)

    @pl.when(i + 1 < nsteps)
    def _():
        start_gather(i + 1, 1 - slot)

    for k in range(TOP_K):
        pltpu.make_async_copy(y_hbm.at[pl.ds(0, n)], ybuf.at[slot, k], sem.at[slot]).wait()

    w0 = info_ref[:, 4:5]
    w1 = info_ref[:, 5:6]
    for s in range(D // LANES):
        cols = slice(s * LANES, (s + 1) * LANES)
        gt2 = mod_ref[:, 5 * D + s * LANES:5 * D + (s + 1) * LANES]
        moe = w0 * ybuf[slot, 0, :, s, :] + w1 * ybuf[slot, 1, :, s, :]
        o_ref[:, cols] = x1_ref[:, cols] + gt2 * moe


def _combine(x1, y, info, dest0, dest1, mod_l, S, tm):
    T, D = x1.shape
    tiles_per_batch = S // tm
    nsub = D // LANES
    grid_spec = pltpu.PrefetchScalarGridSpec(
        num_scalar_prefetch=2,
        grid=(T // tm,),
        in_specs=[
            pl.BlockSpec(memory_space=pl.ANY),
            pl.BlockSpec((tm, D), lambda i, d0, d1: (i, 0)),
            pl.BlockSpec((tm, LANES), lambda i, d0, d1: (i, 0)),
            pl.BlockSpec((None, 1, 6 * D), lambda i, d0, d1: (i // tiles_per_batch, 0, 0)),
        ],
        out_specs=pl.BlockSpec((tm, D), lambda i, d0, d1: (i, 0)),
        scratch_shapes=[pltpu.VMEM((2, TOP_K, tm, nsub, LANES), F32),
                        pltpu.SemaphoreType.DMA((2,))],
    )
    return pl.pallas_call(
        _combine_kernel,
        grid_spec=grid_spec,
        out_shape=jax.ShapeDtypeStruct((T, D), F32),
        compiler_params=pltpu.CompilerParams(
            dimension_semantics=("arbitrary",), vmem_limit_bytes=VMEM_LIMIT),
        name="moe_combine",
    )(dest0, dest1, y, x1, info, mod_l)


def _pad_w_in(w):
    D = w.shape[0]
    c0 = A_COLS + R_COLS
    c_ad = c0 + 2 * G_KW + G_VW
    c_gg = c_ad + G_LORA
    pad = jnp.zeros((D, LANES - G_LORA), w.dtype)
    return jnp.concatenate([w[:, :c_gg], pad, w[:, c_gg:]], axis=1).astype(BF16)


def kernel(x, c, ada_w, ada_b, norm1_g, norm2_g, w_in, attn_qn_g, attn_kn_g, attn_lambda,
           attn_subln_g, rwkv_mu, rwkv_w_up, rwkv_w0, rwkv_a_up, rwkv_a0, rwkv_g_up, rwkv_k_k,
           rwkv_k_a, rwkv_r_k, rwkv_lnx_g, rwkv_lnx_b, gla_alpha_up, gla_alpha_b, gla_norm_g,
           proj_attn, proj_rwkv, proj_gla, w_out, router_grp_w, router_grp_b, router_exp_w,
           router_exp_b, exp_w_gate, exp_w_up, exp_w_down):
    B, S, D = x.shape
    T = B * S
    L = ada_w.shape[0]
    tm = 256
    mod = _adaln(c, ada_w, ada_b).reshape(L, B, 1, 6 * D)
    x2 = x.reshape(T, D)
    for l in range(L):
        lambda_init = 0.8 - 0.6 * math.exp(-0.3 * l)
        p_attn, p_rwkv, p_gla, p_gate = _inproj(x2, mod[l], norm1_g[l], _pad_w_in(w_in[l]), S, tm)
        o_a = _attention(p_attn, attn_qn_g[l], attn_kn_g[l], attn_lambda[l], attn_subln_g[l],
                         lambda_init, B, S)
        o_r = _rwkv(p_rwkv, rwkv_mu[l], rwkv_w_up[l], rwkv_w0[l], rwkv_a_up[l], rwkv_a0[l],
                    rwkv_g_up[l], rwkv_k_k[l], rwkv_k_a[l], rwkv_r_k[l], rwkv_lnx_g[l],
                    rwkv_lnx_b[l], B, S)
        o_g = _gla(p_gla, gla_alpha_up[l], gla_alpha_b[l], gla_norm_g[l], B, S)
        n_r = N_GROUPS + N_EXPERTS
        router_w = jnp.concatenate(
            [router_grp_w[l], router_exp_w[l], jnp.zeros((D, LANES - n_r), F32)], axis=1)
        router_b = jnp.concatenate(
            [router_grp_b[l], router_exp_b[l], jnp.zeros((LANES - n_r,), F32)]).reshape(1, LANES)
        x1, h2, logits = _merge(
            x2, o_a, o_r, o_g, p_gate, proj_attn[l].astype(BF16), proj_rwkv[l].astype(BF16),
            proj_gla[l].astype(BF16), w_out[l].astype(BF16), mod[l], norm2_g[l],
            router_w, router_b, S, tm)
        info, dest0, dest1, blk_exp, n_used, n_rows = _route(logits)
        xin = _dispatch(h2, dest0, dest1, n_rows, tm)
        y = _moe(xin, blk_exp, n_used, exp_w_gate, exp_w_up, exp_w_down, l)
        x2 = _combine(x1, y, info, dest0, dest1, mod[l], S, tm)
    return x2.reshape(B, S, D)
```

```python
import functools
import math

import jax
import jax.numpy as jnp
from jax import lax
from jax.experimental import pallas as pl
from jax.experimental.pallas import tpu as pltpu

F32 = jnp.float32
BF16 = jnp.bfloat16
HIGHEST = lax.Precision.HIGHEST

D_MODEL = 1024
A_HEADS, A_DH, A_DV = 4, 64, 128
A_QW, A_VW = 512, 512
A_COLS = 1536
R_HEADS, R_N, R_W = 8, 64, 512
R_COLS = 1792
RWKV_GN_EPS = 64e-5
G_HEADS, G_DK, G_DV = 4, 64, 128
G_KW, G_VW, G_LORA = 256, 512, 16
G_TAU = 16.0
G_COLS = 1552
G_COLS_PAD = 1664
GATE_COLS = 3072
N_GROUPS, EXP_PER_GROUP, N_EXPERTS, TOP_K = 4, 8, 32, 2
D_EXPERT = 512
EPS = 1e-6

LANES = 128
SUBLANES = 8
NSUB = D_MODEL // LANES
CHUNK = 64
VMEM_LIMIT = 56 * 1024 * 1024


def _dot(a, b):
    return jnp.dot(a.astype(BF16), b.astype(BF16), preferred_element_type=F32)


def _dot_hi(a, b):
    return jnp.dot(a, b, precision=HIGHEST, preferred_element_type=F32)


def _dot_nt(a, b, precision=None):
    return lax.dot_general(a, b, (((1,), (1,)), ((), ())), precision=precision,
                           preferred_element_type=F32)


def _dot_tn(a, b, precision=None):
    return lax.dot_general(a, b, (((0,), (0,)), ((), ())), precision=precision,
                           preferred_element_type=F32)


def _sigmoid(x):
    return 1.0 / (1.0 + jnp.exp(-x))


def _softplus(x):
    return jnp.maximum(x, 0.0) + jnp.log(1.0 + jnp.exp(-jnp.abs(x)))


def _seg_ones(n, seg):
    r = lax.broadcasted_iota(jnp.int32, (n, n), 0) // seg
    c = lax.broadcasted_iota(jnp.int32, (n, n), 1) // seg
    return (r == c).astype(F32)


def _tri(n, strict):
    r = lax.broadcasted_iota(jnp.int32, (n, n), 0)
    c = lax.broadcasted_iota(jnp.int32, (n, n), 1)
    return (c < r) if strict else (c <= r)


def _adaln_kernel(c_ref, w_ref, b_ref, o_ref):
    c = c_ref[...]
    c_act = c * _sigmoid(c)
    o_ref[...] = _dot_hi(c_act, w_ref[...]) + b_ref[...]


def _adaln(c, ada_w, ada_b):
    L, D, N = ada_w.shape
    B = c.shape[0]
    tn = D
    return pl.pallas_call(
        _adaln_kernel,
        grid=(L, N // tn),
        in_specs=[
            pl.BlockSpec((B, D), lambda l, j: (0, 0)),
            pl.BlockSpec((None, D, tn), lambda l, j: (l, 0, j)),
            pl.BlockSpec((None, 1, tn), lambda l, j: (l, 0, j)),
        ],
        out_specs=pl.BlockSpec((None, B, tn), lambda l, j: (l, 0, j)),
        out_shape=jax.ShapeDtypeStruct((L, B, N), F32),
        name="adaln",
    )(c, ada_w, ada_b.reshape(L, 1, N))


_IN_SEGS = (A_COLS, R_COLS, G_COLS_PAD, GATE_COLS)
_IN_DTYPES = (BF16, F32, F32, BF16)
_IN_CHUNK = 512


def _inproj_kernel(x_ref, mod_ref, g_ref, w_ref, *o_refs):
    x = x_ref[...]
    D = x.shape[-1]
    ms = jnp.mean(x * x, axis=-1, keepdims=True)
    y = x * lax.rsqrt(ms + EPS) * g_ref[...]
    sh = mod_ref[:, 0:D]
    sc = mod_ref[:, D:2 * D]
    h = (y * (1.0 + sc) + sh).astype(BF16)
    base = 0
    for o_ref, width in zip(o_refs, _IN_SEGS):
        for c0 in range(0, width, _IN_CHUNK):
            c1 = min(c0 + _IN_CHUNK, width)
            o_ref[:, c0:c1] = jnp.dot(
                h, w_ref[:, base + c0:base + c1], preferred_element_type=F32
            ).astype(o_ref.dtype)
        base += width


def _inproj(x2, mod_l, norm_g, w_pad, S, tm):
    T, D = x2.shape
    NP = w_pad.shape[1]
    tiles_per_batch = S // tm
    return pl.pallas_call(
        _inproj_kernel,
        grid=(T // tm,),
        in_specs=[
            pl.BlockSpec((tm, D), lambda i: (i, 0)),
            pl.BlockSpec((None, 1, 2 * D), lambda i: (i // tiles_per_batch, 0, 0)),
            pl.BlockSpec((1, D), lambda i: (0, 0)),
            pl.BlockSpec((D, NP), lambda i: (0, 0), pipeline_mode=pl.Buffered(1)),
        ],
        out_specs=[pl.BlockSpec((tm, w), lambda i: (i, 0)) for w in _IN_SEGS],
        out_shape=[jax.ShapeDtypeStruct((T, w), dt) for w, dt in zip(_IN_SEGS, _IN_DTYPES)],
        compiler_params=pltpu.CompilerParams(
            dimension_semantics=("arbitrary",), vmem_limit_bytes=VMEM_LIMIT),
        name="inproj",
    )(x2, mod_l, norm_g.reshape(1, D), w_pad)


A_TILE = 256
A_POS_SPLIT = 64


def _split_dot(x, ones):
    hi = x.astype(BF16)
    lo = (x - hi.astype(F32)).astype(BF16)
    return (jnp.dot(hi, ones, preferred_element_type=F32)
            + jnp.dot(lo, ones, preferred_element_type=F32))


def _eye(n):
    r = lax.broadcasted_iota(jnp.int32, (n, n), 0)
    c = lax.broadcasted_iota(jnp.int32, (n, n), 1)
    return (r == c).astype(BF16)


def _attn_kernel(q_ref, k_ref, v_ref, qg_ref, kg_ref, lam_ref, sg_ref, slope_ref, o_ref,
                 qt_s, ka_s, vt_s, s_s, *, S, lambda_init):
    t = A_TILE
    seg = _seg_ones(LANES, A_DH).astype(BF16)
    eye_t = _eye(t)
    eye_v = _eye(A_DV)
    slope = slope_ref[...]
    lane = lax.broadcasted_iota(jnp.int32, (t, LANES), 1)
    row = lax.broadcasted_iota(jnp.int32, (t, LANES), 0)

    def qknorm(x, g, scale):
        xf = x.astype(F32)
        ms = _split_dot(xf * xf, seg) * (1.0 / A_DH)
        return xf * lax.rsqrt(ms + EPS) * g * scale

    for b in range(S // t):
        rows = slice(b * t, (b + 1) * t)
        pos = row + b * t
        hi = (pos // A_POS_SPLIT).astype(F32)
        lo = (pos % A_POS_SPLIT).astype(F32)
        q_aug = jnp.where(lane == A_DH, A_POS_SPLIT * slope,
                jnp.where(lane == A_DH + 1, slope,
                jnp.where(lane == A_DH + 2, -A_POS_SPLIT * slope * hi,
                jnp.where(lane == A_DH + 3, -slope * lo, 0.0))))
        k_aug = jnp.where(lane == A_DH, hi,
                jnp.where(lane == A_DH + 1, lo,
                jnp.where(jnp.logical_or(lane == A_DH + 2, lane == A_DH + 3), 1.0, 0.0)))
        qn = qknorm(q_ref[rows, :], qg_ref[...], A_DH ** -0.5)
        kn = qknorm(k_ref[rows, :], kg_ref[...], 1.0)
        for c in range(2):
            qc = qn if c == 0 else pltpu.roll(qn, A_DH, 1)
            kc = kn if c == 0 else pltpu.roll(kn, A_DH, 1)
            qa = jnp.where(lane < A_DH, qc, q_aug).astype(BF16)
            qt_s[c, :, rows] = _dot_tn(qa, eye_t).astype(BF16)
            ka_s[c, rows, :] = jnp.where(lane < A_DH, kc, k_aug).astype(BF16)
        vt_s[:, rows] = _dot_tn(v_ref[rows, :], eye_t).astype(BF16)

    lv = lam_ref[...]
    lam = (jnp.exp(jnp.sum(lv[0:1] * lv[1:2], axis=-1, keepdims=True))
           - jnp.exp(jnp.sum(lv[2:3] * lv[3:4], axis=-1, keepdims=True)) + lambda_init)
    causal = (lax.broadcasted_iota(jnp.int32, (t, t), 0)
              <= lax.broadcasted_iota(jnp.int32, (t, t), 1))

    for i in range(S // t):
        qcols = slice(i * t, (i + 1) * t)
        res = []
        for c in range(2):
            qt = qt_s[c, :, qcols]
            m = None
            for j in range(i + 1):
                s = jnp.dot(ka_s[c, j * t:(j + 1) * t, :], qt, preferred_element_type=F32)
                if j == i:
                    s = jnp.where(causal, s, -jnp.inf)
                s_s[c, j * t:(j + 1) * t, :] = s
                mj = jnp.max(s, axis=0, keepdims=True)
                m = mj if m is None else jnp.maximum(m, mj)
            l = jnp.zeros((1, t), F32)
            acc = jnp.zeros((A_DV, t), F32)
            for j in range(i + 1):
                p = jnp.exp(s_s[c, j * t:(j + 1) * t, :] - m)
                l = l + jnp.sum(p, axis=0, keepdims=True)
                acc = acc + jnp.dot(vt_s[:, j * t:(j + 1) * t], p.astype(BF16),
                                    preferred_element_type=F32)
            res.append(acc / l)
        o = res[0] - lam * res[1]
        ms = jnp.mean(o * o, axis=0, keepdims=True)
        o = o * lax.rsqrt(ms + EPS) * sg_ref[...] * (1.0 - lambda_init)
        o_ref[qcols, :] = _dot_tn(o.astype(BF16), eye_v).astype(o_ref.dtype)


def _attention(p_attn, qn_g, kn_g, lam_vecs, subln_g, lambda_init, B, S):
    pa = p_attn.reshape(B, S, A_COLS)
    dup = lambda g: jnp.concatenate([g, g]).reshape(1, LANES)
    slopes = jnp.asarray(
        [[2.0 ** (-8.0 * (i + 1) / A_HEADS)] * LANES for i in range(A_HEADS)], F32
    ).reshape(A_HEADS, 1, LANES)
    nqb = A_QW // LANES
    kern = functools.partial(_attn_kernel, S=S, lambda_init=lambda_init)
    out = pl.pallas_call(
        kern,
        grid=(B, A_HEADS),
        in_specs=[
            pl.BlockSpec((None, S, LANES), lambda b, h: (b, 0, h)),
            pl.BlockSpec((None, S, LANES), lambda b, h: (b, 0, nqb + h)),
            pl.BlockSpec((None, S, LANES), lambda b, h: (b, 0, 2 * nqb + h)),
            pl.BlockSpec((1, LANES), lambda b, h: (0, 0)),
            pl.BlockSpec((1, LANES), lambda b, h: (0, 0)),
            pl.BlockSpec((4, A_DH), lambda b, h: (0, 0)),
            pl.BlockSpec((A_DV, 1), lambda b, h: (0, 0)),
            pl.BlockSpec((None, 1, LANES), lambda b, h: (h, 0, 0)),
        ],
        out_specs=pl.BlockSpec((None, S, A_DV), lambda b, h: (b, 0, h)),
        out_shape=jax.ShapeDtypeStruct((B, S, A_VW), BF16),
        scratch_shapes=[pltpu.VMEM((2, LANES, S), BF16), pltpu.VMEM((2, S, LANES), BF16),
                        pltpu.VMEM((A_DV, S), BF16), pltpu.VMEM((2, S, A_TILE), F32)],
        compiler_params=pltpu.CompilerParams(
            dimension_semantics=("arbitrary", "arbitrary"), vmem_limit_bytes=VMEM_LIMIT),
        name="diff_attn",
    )(pa, pa, pa, dup(qn_g), dup(kn_g), lam_vecs, subln_g.reshape(A_DV, 1), slopes)
    return out.reshape(B * S, A_VW)


R_GROUP = 4
R_GW = R_GROUP * R_N


def _rwkv_kernel(p_ref, mu_ref, wup_ref, w0_ref, aup_ref, a0_ref, gup_ref, kk_ref, ka_ref,
                 rk_ref, lg_ref, lb_ref, o_ref,
                 carry_ref, st_ref, al_s, be_s, ka_s, rh_s, bt_s, kt_s, v_s, gc_s, y_s, *, TB):
    @pl.when(pl.program_id(1) == 0)
    def _():
        carry_ref[...] = jnp.zeros_like(carry_ref)
        st_ref[...] = jnp.zeros_like(st_ref)

    xs = p_ref[...]
    prev = pltpu.roll(xs, 1, 0)
    row = lax.broadcasted_iota(jnp.int32, (TB, 1), 0)
    prev = jnp.where(row == 0, carry_ref[...], prev)
    carry_ref[...] = xs[TB - 1:TB, :]
    xm = xs + (prev - xs) * mu_ref[...]
    r = xm[:, 0:R_W]
    k = xm[:, R_W:2 * R_W]
    v = xm[:, 2 * R_W:3 * R_W]
    wa = xm[:, 3 * R_W:3 * R_W + LANES]
    gd = xm[:, 3 * R_W + LANES:3 * R_W + 2 * LANES]
    wz = w0_ref[...] + _dot(jnp.tanh(wa), wup_ref[...])
    w_log = -_softplus(-wz) - 0.5
    lw = -jnp.exp(w_log)
    a = _sigmoid(a0_ref[...] + _dot(wa, aup_ref[...]))
    g = _dot(_sigmoid(gd), gup_ref[...])
    seg = _seg_ones(R_W, R_N).astype(BF16)
    kk = k * kk_ref[...]
    nrm = jnp.sqrt(_split_dot(kk * kk, seg))
    kk = kk / jnp.maximum(nrm, 1e-12)
    k2 = k * (1.0 + (a - 1.0) * ka_ref[...])
    bonus = _split_dot(r * k2 * rk_ref[...], seg) * v
    bv = kk * a

    rr = lax.broadcasted_iota(jnp.int32, (TB, TB), 0)
    cc = lax.broadcasted_iota(jnp.int32, (TB, TB), 1)
    tril_blk = jnp.logical_and(rr // CHUNK == cc // CHUNK, cc <= rr).astype(F32)
    Lg = _dot_hi(tril_blk, lw)
    inv = jnp.exp(-Lg)
    al_s[...] = (jnp.exp(Lg - lw) * kk).astype(BF16)
    be_s[...] = (bv * inv).astype(BF16)
    ka_s[...] = (k2 * inv).astype(BF16)
    rh_s[...] = (jnp.exp(Lg) * r).astype(BF16)
    v_s[...] = v.astype(BF16)
    for c in range(TB // CHUNK):
        rows = slice(c * CHUNK, (c + 1) * CHUNK)
        gC = Lg[(c + 1) * CHUNK - 1:(c + 1) * CHUNK, :]
        tail = jnp.exp(gC - Lg[rows, :])
        bt_s[rows, :] = (bv[rows, :] * tail).astype(BF16)
        kt_s[rows, :] = (k2[rows, :] * tail).astype(BF16)
        gc_s[c * SUBLANES:(c + 1) * SUBLANES, :] = jnp.broadcast_to(jnp.exp(gC), (SUBLANES, R_W))

    ri = lax.broadcasted_iota(jnp.int32, (R_GW, R_GW), 0)
    ci = lax.broadcasted_iota(jnp.int32, (R_GW, R_GW), 1)
    blk = ri // R_N == ci // R_N
    strict = ci % R_N < ri % R_N
    strict_t = ri % R_N < ci % R_N
    incl = ci % R_N <= ri % R_N
    zero = jnp.zeros((), BF16)

    def expand(x):
        return jnp.where(blk, jnp.concatenate([x] * R_GROUP, axis=0), zero)

    def chunk(c, _):
        r0 = pl.multiple_of(c * CHUNK, CHUNK)
        rows = pl.ds(r0, CHUNK)
        g0 = pl.multiple_of(c * SUBLANES, SUBLANES)
        for gi in range(R_HEADS // R_GROUP):
            cols = slice(gi * R_GW, (gi + 1) * R_GW)
            A_ = expand(al_s[rows, cols])
            R_ = expand(rh_s[rows, cols])
            B_ = expand(be_s[rows, cols])
            K_ = expand(ka_s[rows, cols])
            Bt = expand(bt_s[rows, cols])
            Kt = expand(kt_s[rows, cols])
            vc = v_s[rows, cols]
            Vb = jnp.concatenate([vc[:, h * R_N:(h + 1) * R_N] for h in range(R_GROUP)], axis=0)
            m_ab = jnp.where(strict, _dot_nt(A_, B_), 0.0)
            m_ak_t = jnp.where(strict_t, _dot_nt(K_, A_), 0.0).astype(BF16)
            m_rb = jnp.where(incl, _dot_nt(R_, B_), 0.0).astype(BF16)
            m_rk = jnp.where(incl, _dot_nt(R_, K_), 0.0).astype(BF16)
            Z = A_.astype(F32)
            Wt = _dot_tn(Vb, m_ak_t)
            X = -m_ab
            n = 1
            while True:
                Xb = X.astype(BF16)
                Z = Z + jnp.dot(Xb, Z.astype(BF16), preferred_element_type=F32)
                Wt = Wt + _dot_nt(Wt.astype(BF16), Xb)
                n *= 2
                if n >= CHUNK:
                    break
                X = jnp.dot(Xb, Xb, preferred_element_type=F32)
            Zb = Z.astype(BF16)
            Wtb = Wt.astype(BF16)
            y_a = R_.astype(F32) - jnp.dot(m_rb, Zb, preferred_element_type=F32)
            y_b = jnp.dot(m_rk, Vb, preferred_element_type=F32) - _dot_nt(m_rb, Wtb)
            p_neg = _dot_tn(Zb, Bt)
            q = _dot_tn(Vb, Kt) - jnp.dot(Wtb, Bt, preferred_element_type=F32)
            S0 = st_ref[gi]
            S0b = S0.astype(BF16)
            y = _dot_nt(y_a.astype(BF16), S0b) + y_b
            st_ref[gi] = (S0 * gc_s[pl.ds(g0, SUBLANES), cols][0:1, :]
                          - jnp.dot(S0b, p_neg.astype(BF16), preferred_element_type=F32) + q)
            for h in range(R_GROUP):
                hh = gi * R_GROUP + h
                y_s[rows, hh * R_N:(hh + 1) * R_N] = y[h * R_N:(h + 1) * R_N, :]
        return 0

    lax.fori_loop(0, TB // CHUNK, chunk, 0)

    y = y_s[...]
    mean = _split_dot(y, seg) * (1.0 / R_N)
    yc = y - mean
    var = _split_dot(yc * yc, seg) * (1.0 / R_N)
    yn = yc * lax.rsqrt(var + RWKV_GN_EPS) * lg_ref[...] + lb_ref[...]
    o_ref[...] = ((yn + bonus) * g).astype(o_ref.dtype)


def _rwkv(p_rwkv, mu, w_up, w0, a_up, a0, g_up, k_k, k_a, r_k, lnx_g, lnx_b, B, S):
    TB = 256
    T = B * S
    nt = S // TB
    row = lambda t: t.reshape(1, -1)
    zeros = jnp.zeros((R_N, R_W), F32)
    wup_pad = jnp.concatenate([w_up, zeros], axis=0)
    aup_pad = jnp.concatenate([zeros, a_up], axis=0)
    vec = lambda n: pl.BlockSpec((1, n), lambda b, i: (0, 0))
    mat = lambda m, n: pl.BlockSpec((m, n), lambda b, i: (0, 0))
    kern = functools.partial(_rwkv_kernel, TB=TB)
    return pl.pallas_call(
        kern,
        grid=(B, nt),
        in_specs=[
            pl.BlockSpec((TB, R_COLS), lambda b, i: (b * nt + i, 0)),
            vec(R_COLS), mat(LANES, R_W), vec(R_W), mat(LANES, R_W), vec(R_W), mat(LANES, R_W),
            vec(R_W), vec(R_W), vec(R_W), vec(R_W), vec(R_W),
        ],
        out_specs=pl.BlockSpec((TB, R_W), lambda b, i: (b * nt + i, 0)),
        out_shape=jax.ShapeDtypeStruct((T, R_W), BF16),
        scratch_shapes=[
            pltpu.VMEM((1, R_COLS), F32),
            pltpu.VMEM((R_HEADS // R_GROUP, R_N, R_GW), F32),
        ] + [pltpu.VMEM((TB, R_W), BF16)] * 7 + [
            pltpu.VMEM((TB // CHUNK * SUBLANES, R_W), F32),
            pltpu.VMEM((TB, R_W), F32),
        ],
        compiler_params=pltpu.CompilerParams(
            dimension_semantics=("arbitrary", "arbitrary"), vmem_limit_bytes=VMEM_LIMIT),
        name="rwkv7",
    )(p_rwkv, row(mu), wup_pad, row(w0), aup_pad, row(a0), g_up, row(k_k), row(k_a),
      row(r_k), row(lnx_g), row(lnx_b))


def _gla_kernel(p_ref, aup_ref, ab_ref, ng_ref, o_ref, st_ref, *, TB):
    @pl.when(pl.program_id(1) == 0)
    def _():
        st_ref[...] = jnp.zeros_like(st_ref)

    c_gv = 2 * G_KW
    c_ad = c_gv + G_VW
    c_gate = c_ad + LANES
    nchunk = TB // CHUNK
    rr = lax.broadcasted_iota(jnp.int32, (TB, TB), 0)
    cc = lax.broadcasted_iota(jnp.int32, (TB, TB), 1)
    causal = jnp.logical_and(rr // CHUNK == cc // CHUNK, cc <= rr)

    q = p_ref[:, 0:G_KW] * (G_DK ** -0.5)
    k = p_ref[:, G_KW:2 * G_KW]
    vb = p_ref[:, c_gv:c_gv + G_VW].astype(BF16)
    z = _dot_hi(p_ref[:, c_ad:c_ad + LANES], aup_ref[...]) + ab_ref[...]
    la = -_softplus(-z) * (1.0 / G_TAU)
    b = _dot_hi(causal.astype(F32), la)
    qe = (q * jnp.exp(b)).astype(BF16)
    ke = (k * jnp.exp(-b)).astype(BF16)
    kts, e_lasts = [], []
    for c in range(nchunk):
        rows = slice(c * CHUNK, (c + 1) * CHUNK)
        b_last = b[(c + 1) * CHUNK - 1:(c + 1) * CHUNK, :]
        kts.append((k[rows, :] * jnp.exp(b_last - b[rows, :])).astype(BF16))
        e_lasts.append(jnp.exp(b_last))
    for h in range(G_HEADS):
        sl = slice(h * G_DK, (h + 1) * G_DK)
        vs = slice(h * G_DV, (h + 1) * G_DV)
        sc = jnp.where(causal, _dot_nt(qe[:, sl], ke[:, sl]), 0.0)
        o_intra = _dot(sc, vb[:, vs])
        S = st_ref[h]
        parts = []
        for c in range(nchunk):
            rows = slice(c * CHUNK, (c + 1) * CHUNK)
            parts.append(o_intra[rows, :] + _dot_nt(qe[rows, sl], S.astype(BF16)))
            S = S * e_lasts[c][:, sl] + _dot_tn(vb[rows, vs], kts[c][:, sl])
        st_ref[h] = S
        o = jnp.concatenate(parts, axis=0)
        ms = jnp.mean(o * o, axis=-1, keepdims=True)
        gt = p_ref[:, c_gate + h * G_DV:c_gate + (h + 1) * G_DV]
        o = o * lax.rsqrt(ms + EPS) * ng_ref[...] * (gt * _sigmoid(gt))
        o_ref[:, vs] = o.astype(o_ref.dtype)


def _gla(p_gla, alpha_up, alpha_b, norm_g, B, S):
    TB = 256
    T = B * S
    nt = S // TB
    aup_pad = jnp.concatenate([alpha_up, jnp.zeros((LANES - G_LORA, G_KW), F32)], axis=0)
    kern = functools.partial(_gla_kernel, TB=TB)
    return pl.pallas_call(
        kern,
        grid=(B, nt),
        in_specs=[
            pl.BlockSpec((TB, G_COLS_PAD), lambda b, i: (b * nt + i, 0)),
            pl.BlockSpec((LANES, G_KW), lambda b, i: (0, 0)),
            pl.BlockSpec((1, G_KW), lambda b, i: (0, 0)),
            pl.BlockSpec((1, G_DV), lambda b, i: (0, 0)),
        ],
        out_specs=pl.BlockSpec((TB, G_VW), lambda b, i: (b * nt + i, 0)),
        out_shape=jax.ShapeDtypeStruct((T, G_VW), BF16),
        scratch_shapes=[pltpu.VMEM((G_HEADS, G_DV, G_DK), F32)],
        compiler_params=pltpu.CompilerParams(
            dimension_semantics=("arbitrary", "arbitrary"), vmem_limit_bytes=VMEM_LIMIT),
        name="gla",
    )(p_gla, aup_pad, alpha_b.reshape(1, G_KW), norm_g.reshape(1, G_DV))


def _merge_kernel(x_ref, oa_ref, or_ref, og_ref, gate_ref, pa_ref, pr_ref, pg_ref, wo_ref,
                  mod_ref, g2_ref, rw_ref, rb_ref, x1_ref, h2_ref, lg_ref):
    D = D_MODEL
    merged = (_sigmoid(gate_ref[:, 0:D].astype(F32))
              * jnp.dot(oa_ref[...], pa_ref[...], preferred_element_type=F32)
              + _sigmoid(gate_ref[:, D:2 * D].astype(F32))
              * jnp.dot(or_ref[...], pr_ref[...], preferred_element_type=F32)
              + _sigmoid(gate_ref[:, 2 * D:3 * D].astype(F32))
              * jnp.dot(og_ref[...], pg_ref[...], preferred_element_type=F32))
    gt1 = mod_ref[:, 2 * D:3 * D]
    sh2 = mod_ref[:, 3 * D:4 * D]
    sc2 = mod_ref[:, 4 * D:5 * D]
    x1 = x_ref[...] + gt1 * jnp.dot(merged.astype(BF16), wo_ref[...], preferred_element_type=F32)
    x1_ref[...] = x1
    ms = jnp.mean(x1 * x1, axis=-1, keepdims=True)
    h2 = x1 * lax.rsqrt(ms + EPS) * g2_ref[...] * (1.0 + sc2) + sh2
    for s in range(NSUB):
        h2_ref[pl.ds(s, h2.shape[0], stride=NSUB), :] = h2[:, s * LANES:(s + 1) * LANES]
    h_hi = h2.astype(BF16)
    h_lo = (h2 - h_hi.astype(F32)).astype(BF16)
    lg_ref[...] = (jnp.dot(h_hi, rw_ref[0], preferred_element_type=F32)
                   + jnp.dot(h_lo, rw_ref[0], preferred_element_type=F32)
                   + jnp.dot(h_hi, rw_ref[1], preferred_element_type=F32) + rb_ref[...])


def _merge(x2, o_a, o_r, o_g, p_gate, proj_a, proj_r, proj_g, w_out, mod_l, norm2_g,
           router_w, router_b, S, tm):
    T, D = x2.shape
    tiles_per_batch = S // tm
    tile = lambda w: pl.BlockSpec((tm, w), lambda i: (i, 0))
    const = lambda m, n: pl.BlockSpec((m, n), lambda i: (0, 0))
    return pl.pallas_call(
        _merge_kernel,
        grid=(T // tm,),
        in_specs=[
            tile(D), tile(A_VW), tile(R_W), tile(G_VW), tile(GATE_COLS),
            const(A_VW, D), const(R_W, D), const(G_VW, D), const(D, D),
            pl.BlockSpec((None, 1, 6 * D), lambda i: (i // tiles_per_batch, 0, 0)),
            const(1, D), pl.BlockSpec((2, D, LANES), lambda i: (0, 0, 0)), const(1, LANES),
        ],
        out_specs=[
            tile(D),
            pl.BlockSpec((tm * NSUB, LANES), lambda i: (i, 0)),
            tile(LANES),
        ],
        out_shape=[
            jax.ShapeDtypeStruct((T, D), F32),
            jax.ShapeDtypeStruct((T * NSUB, LANES), F32),
            jax.ShapeDtypeStruct((T, LANES), F32),
        ],
        compiler_params=pltpu.CompilerParams(
            dimension_semantics=("arbitrary",), vmem_limit_bytes=VMEM_LIMIT),
        name="merge",
    )(x2, o_a, o_r, o_g, p_gate, proj_a, proj_r, proj_g, w_out, mod_l,
      norm2_g.reshape(1, D), router_w, router_b)


MOE_ROWS = 256
ROUTE_ROWS = 512
E_LANE0 = N_GROUPS


def _route_kernel(lg_ref, info_ref, cnt_ref, carry_ref):
    @pl.when(pl.program_id(0) == 0)
    def _():
        carry_ref[...] = jnp.zeros_like(carry_ref)

    lg = lg_ref[...]
    n = lg.shape[0]
    lane = lax.broadcasted_iota(jnp.int32, (n, LANES), 1).astype(F32)
    neg = -jnp.inf
    big = float(LANES)

    def first_max(vals):
        m = jnp.max(vals, axis=-1, keepdims=True)
        idx = jnp.min(jnp.where(vals == m, lane, big), axis=-1, keepdims=True)
        return m, idx

    in_grp = lane < N_GROUPS
    gm, grp = first_max(jnp.where(in_grp, lg, neg))
    g_prob = 1.0 / jnp.sum(jnp.where(in_grp, jnp.exp(lg - gm), 0.0), axis=-1, keepdims=True)
    lo = E_LANE0 + grp * EXP_PER_GROUP
    el = jnp.where(jnp.logical_and(lane >= lo, lane < lo + EXP_PER_GROUP), lg, neg)
    v1, i1 = first_max(el)
    v2, i2 = first_max(jnp.where(lane == i1, neg, el))
    e21 = jnp.exp(v2 - v1)
    w0 = g_prob / (1.0 + e21)
    w1 = g_prob * e21 / (1.0 + e21)
    oh0 = lane == i1
    oh1 = lane == i2
    oh = jnp.logical_or(oh0, oh1).astype(F32)
    before = _tri(n, True).astype(BF16)
    cnt = jnp.dot(before, oh.astype(BF16), preferred_element_type=F32) + carry_ref[...]
    rank0 = jnp.sum(jnp.where(oh0, cnt, 0.0), axis=-1, keepdims=True)
    rank1 = jnp.sum(jnp.where(oh1, cnt, 0.0), axis=-1, keepdims=True)
    carry = carry_ref[...] + jnp.sum(oh, axis=0, keepdims=True)
    carry_ref[...] = carry
    cnt_ref[...] = carry
    cols = (i1 - E_LANE0, i2 - E_LANE0, rank0, rank1, w0, w1)
    info = jnp.zeros((n, LANES), F32)
    for j, col in enumerate(cols):
        info = jnp.where(lane == j, col, info)
    info_ref[...] = info


def _route(logits):
    T = logits.shape[0]
    tr = min(ROUTE_ROWS, T)
    info, cnt = pl.pallas_call(
        _route_kernel,
        grid=(T // tr,),
        in_specs=[pl.BlockSpec((tr, LANES), lambda i: (i, 0))],
        out_specs=[pl.BlockSpec((tr, LANES), lambda i: (i, 0)),
                   pl.BlockSpec((1, LANES), lambda i: (0, 0))],
        out_shape=[jax.ShapeDtypeStruct((T, LANES), F32), jax.ShapeDtypeStruct((1, LANES), F32)],
        scratch_shapes=[pltpu.VMEM((1, LANES), F32)],
        compiler_params=pltpu.CompilerParams(dimension_semantics=("arbitrary",)),
        name="moe_route",
    )(logits)
    A = T * TOP_K
    counts = cnt[0, E_LANE0:E_LANE0 + N_EXPERTS].astype(jnp.int32)
    padded = (counts + MOE_ROWS - 1) // MOE_ROWS * MOE_ROWS
    pad_end = jnp.cumsum(padded)
    pad_start = pad_end - padded
    n_blocks = -(-A // MOE_ROWS) + N_EXPERTS
    eid = info[:, 0:2].astype(jnp.int32)
    dest = jnp.take(pad_start, eid) + info[:, 2:4].astype(jnp.int32)
    blk_start = jnp.arange(n_blocks, dtype=jnp.int32) * MOE_ROWS
    blk_exp = jnp.minimum(jnp.sum(pad_end[None, :] <= blk_start[:, None], axis=1),
                          N_EXPERTS - 1).astype(jnp.int32)
    n_used = (pad_end[-1:] // MOE_ROWS).astype(jnp.int32)
    return info, dest[:, 0], dest[:, 1], blk_exp, n_used, n_blocks * MOE_ROWS


def _dispatch_kernel(d0_ref, d1_ref, h2_ref, xin_in, xin_hbm, sem):
    del xin_in
    n = h2_ref.shape[0] // NSUB
    base = pl.program_id(0) * n

    def slab(ref, row):
        return ref.at[pl.ds(pl.multiple_of(row * NSUB, NSUB), NSUB)]

    def body(r, _):
        pltpu.make_async_copy(slab(h2_ref, r), slab(xin_hbm, d0_ref[base + r]), sem).start()
        pltpu.make_async_copy(slab(h2_ref, r), slab(xin_hbm, d1_ref[base + r]), sem).start()
        return 0

    lax.fori_loop(0, n, body, 0, unroll=8)
    for _ in range(TOP_K):
        pltpu.make_async_copy(h2_ref, xin_hbm.at[pl.ds(0, n * NSUB)], sem).wait()


def _dispatch(h2_slab, dest0, dest1, n_rows, td):
    T = h2_slab.shape[0] // NSUB
    grid_spec = pltpu.PrefetchScalarGridSpec(
        num_scalar_prefetch=2,
        grid=(T // td,),
        in_specs=[pl.BlockSpec((td * NSUB, LANES), lambda i, d0, d1: (i, 0)),
                  pl.BlockSpec(memory_space=pl.ANY)],
        out_specs=pl.BlockSpec(memory_space=pl.ANY),
        scratch_shapes=[pltpu.SemaphoreType.DMA(())],
    )
    return pl.pallas_call(
        _dispatch_kernel,
        grid_spec=grid_spec,
        out_shape=jax.ShapeDtypeStruct((n_rows * NSUB, LANES), F32),
        input_output_aliases={3: 0},
        compiler_params=pltpu.CompilerParams(dimension_semantics=("arbitrary",)),
        name="moe_dispatch",
    )(dest0, dest1, h2_slab, jnp.zeros((n_rows * NSUB, LANES), F32))


def _moe_kernel(be_ref, nu_ref, x_ref, wg_ref, wu_ref, wd_ref, y_ref, wgb, wub, wdb):
    i = pl.program_id(0)

    @pl.when(i < nu_ref[0])
    def _():
        changed = jnp.logical_or(i == 0, be_ref[i] != be_ref[jnp.maximum(i - 1, 0)])

        @pl.when(changed)
        def _():
            wgb[...] = wg_ref[...].astype(BF16)
            wub[...] = wu_ref[...].astype(BF16)
            wdb[...] = wd_ref[...].astype(BF16)

        x = jnp.concatenate(
            [x_ref[pl.ds(s, MOE_ROWS, stride=NSUB), :].astype(BF16) for s in range(NSUB)], axis=-1)
        hg = jnp.dot(x, wgb[...], preferred_element_type=F32)
        hu = jnp.dot(x, wub[...], preferred_element_type=F32)
        hid = (hg * _sigmoid(hg) * hu).astype(BF16)
        y = jnp.dot(hid, wdb[...], preferred_element_type=F32)
        for s in range(NSUB):
            y_ref[pl.ds(s, MOE_ROWS, stride=NSUB), :] = y[:, s * LANES:(s + 1) * LANES]

    @pl.when(i >= nu_ref[0])
    def _():
        y_ref[...] = jnp.zeros_like(y_ref)


def _moe(xin, blk_exp, n_used, w_gate, w_up, w_down, layer):
    blk_rows = MOE_ROWS * NSUB
    n_blocks = xin.shape[0] // blk_rows
    wspec = lambda m, n: pl.BlockSpec((None, None, m, n), lambda i, be, nu: (layer, be[i], 0, 0))
    last = lambda i, nu: jnp.minimum(i, nu[0] - 1)
    grid_spec = pltpu.PrefetchScalarGridSpec(
        num_scalar_prefetch=2,
        grid=(n_blocks,),
        in_specs=[
            pl.BlockSpec((blk_rows, LANES), lambda i, be, nu: (last(i, nu), 0)),
            wspec(D_MODEL, D_EXPERT), wspec(D_MODEL, D_EXPERT), wspec(D_EXPERT, D_MODEL),
        ],
        out_specs=pl.BlockSpec((blk_rows, LANES), lambda i, be, nu: (i, 0)),
        scratch_shapes=[
            pltpu.VMEM((D_MODEL, D_EXPERT), BF16),
            pltpu.VMEM((D_MODEL, D_EXPERT), BF16),
            pltpu.VMEM((D_EXPERT, D_MODEL), BF16),
        ],
    )
    return pl.pallas_call(
        _moe_kernel,
        grid_spec=grid_spec,
        out_shape=jax.ShapeDtypeStruct(xin.shape, F32),
        compiler_params=pltpu.CompilerParams(
            dimension_semantics=("arbitrary",), vmem_limit_bytes=VMEM_LIMIT),
        name="moe_ffn",
    )(blk_exp, n_used, xin, w_gate, w_up, w_down)


def _combine_kernel(d0_ref, d1_ref, y_hbm, x1_ref, info_ref, mod_ref, o_ref, ybuf, sem):
    i = pl.program_id(0)
    nsteps = pl.num_programs(0)
    n = x1_ref.shape[0]
    slot = i % 2
    D = D_MODEL

    def slab(ref, row):
        return ref.at[pl.ds(pl.multiple_of(row * NSUB, NSUB), NSUB)]

    def start_gather(step, sl):
        base = step * n

        def body(r, _):
            pltpu.make_async_copy(
                slab(y_hbm, d0_ref[base + r]), slab(ybuf.at[sl, 0], r), sem.at[sl]).start()
            pltpu.make_async_copy(
                slab(y_hbm, d1_ref[base + r]), slab(ybuf.at[sl, 1], r), sem.at[sl]).start()
            return 0
        lax.fori_loop(0, n, body, 0, unroll=8)

    @pl.when(i == 0)
    def _():
        start_gather(0, 0)

    @pl.when(i + 1 < nsteps)
    def _():
        start_gather(i + 1, 1 - slot)

    for k in range(TOP_K):
        pltpu.make_async_copy(
            y_hbm.at[pl.ds(0, n * NSUB)], ybuf.at[slot, k], sem.at[slot]).wait()

    w0 = info_ref[:, 4:5]
    w1 = info_ref[:, 5:6]
    for s in range(NSUB):
        cols = slice(s * LANES, (s + 1) * LANES)
        gt2 = mod_ref[:, 5 * D + s * LANES:5 * D + (s + 1) * LANES]
        piece = pl.ds(s, n, stride=NSUB)
        moe = w0 * ybuf[slot, 0, piece, :] + w1 * ybuf[slot, 1, piece, :]
        o_ref[:, cols] = x1_ref[:, cols] + gt2 * moe


def _combine(x1, y, info, dest0, dest1, mod_l, S, tm):
    T, D = x1.shape
    tiles_per_batch = S // tm
    grid_spec = pltpu.PrefetchScalarGridSpec(
        num_scalar_prefetch=2,
        grid=(T // tm,),
        in_specs=[
            pl.BlockSpec(memory_space=pl.ANY),
            pl.BlockSpec((tm, D), lambda i, d0, d1: (i, 0)),
            pl.BlockSpec((tm, LANES), lambda i, d0, d1: (i, 0)),
            pl.BlockSpec((None, 1, 6 * D), lambda i, d0, d1: (i // tiles_per_batch, 0, 0)),
        ],
        out_specs=pl.BlockSpec((tm, D), lambda i, d0, d1: (i, 0)),
        scratch_shapes=[pltpu.VMEM((2, TOP_K, tm * NSUB, LANES), F32),
                        pltpu.SemaphoreType.DMA((2,))],
    )
    return pl.pallas_call(
        _combine_kernel,
        grid_spec=grid_spec,
        out_shape=jax.ShapeDtypeStruct((T, D), F32),
        compiler_params=pltpu.CompilerParams(
            dimension_semantics=("arbitrary",), vmem_limit_bytes=VMEM_LIMIT),
        name="moe_combine",
    )(dest0, dest1, y, x1, info, mod_l)


def _pad_w_in(w):
    D = w.shape[0]
    c0 = A_COLS + R_COLS
    c_ad = c0 + 2 * G_KW + G_VW
    c_gg = c_ad + G_LORA
    pad = jnp.zeros((D, LANES - G_LORA), w.dtype)
    return jnp.concatenate([w[:, :c_gg], pad, w[:, c_gg:]], axis=1).astype(BF16)


def kernel(x, c, ada_w, ada_b, norm1_g, norm2_g, w_in, attn_qn_g, attn_kn_g, attn_lambda,
           attn_subln_g, rwkv_mu, rwkv_w_up, rwkv_w0, rwkv_a_up, rwkv_a0, rwkv_g_up, rwkv_k_k,
           rwkv_k_a, rwkv_r_k, rwkv_lnx_g, rwkv_lnx_b, gla_alpha_up, gla_alpha_b, gla_norm_g,
           proj_attn, proj_rwkv, proj_gla, w_out, router_grp_w, router_grp_b, router_exp_w,
           router_exp_b, exp_w_gate, exp_w_up, exp_w_down):
    B, S, D = x.shape
    T = B * S
    L = ada_w.shape[0]
    tm = 256
    mod = _adaln(c, ada_w, ada_b).reshape(L, B, 1, 6 * D)
    x2 = x.reshape(T, D)
    for l in range(L):
        lambda_init = 0.8 - 0.6 * math.exp(-0.3 * l)
        p_attn, p_rwkv, p_gla, p_gate = _inproj(x2, mod[l], norm1_g[l], _pad_w_in(w_in[l]), S, tm)
        o_a = _attention(p_attn, attn_qn_g[l], attn_kn_g[l], attn_lambda[l], attn_subln_g[l],
                         lambda_init, B, S)
        o_r = _rwkv(p_rwkv, rwkv_mu[l], rwkv_w_up[l], rwkv_w0[l], rwkv_a_up[l], rwkv_a0[l],
                    rwkv_g_up[l], rwkv_k_k[l], rwkv_k_a[l], rwkv_r_k[l], rwkv_lnx_g[l],
                    rwkv_lnx_b[l], B, S)
        o_g = _gla(p_gla, gla_alpha_up[l], gla_alpha_b[l], gla_norm_g[l], B, S)
        n_r = N_GROUPS + N_EXPERTS
        router_w = jnp.concatenate(
            [router_grp_w[l], router_exp_w[l], jnp.zeros((D, LANES - n_r), F32)], axis=1)
        rw_hi = router_w.astype(BF16)
        router_w = jnp.stack([rw_hi, (router_w - rw_hi.astype(F32)).astype(BF16)])
        router_b = jnp.concatenate(
            [router_grp_b[l], router_exp_b[l], jnp.zeros((LANES - n_r,), F32)]).reshape(1, LANES)
        x1, h2, logits = _merge(
            x2, o_a, o_r, o_g, p_gate, proj_attn[l].astype(BF16), proj_rwkv[l].astype(BF16),
            proj_gla[l].astype(BF16), w_out[l].astype(BF16), mod[l], norm2_g[l],
            router_w, router_b, S, tm)
        info, dest0, dest1, blk_exp, n_used, n_rows = _route(logits)
        xin = _dispatch(h2, dest0, dest1, n_rows, tm)
        y = _moe(xin, blk_exp, n_used, exp_w_gate, exp_w_up, exp_w_down, l)
        x2 = _combine(x1, y, info, dest0, dest1, mod[l], S, tm)
    return x2.reshape(B, S, D)
```

```python
import functools
import math

import jax
import jax.numpy as jnp
from jax import lax
from jax.experimental import pallas as pl
from jax.experimental.pallas import tpu as pltpu

F32 = jnp.float32
BF16 = jnp.bfloat16
HIGHEST = lax.Precision.HIGHEST

D_MODEL = 1024
A_HEADS, A_DH, A_DV = 4, 64, 128
A_QW, A_VW = 512, 512
A_COLS = 1536
R_HEADS, R_N, R_W = 8, 64, 512
R_COLS = 1792
RWKV_GN_EPS = 64e-5
G_HEADS, G_DK, G_DV = 4, 64, 128
G_KW, G_VW, G_LORA = 256, 512, 16
G_TAU = 16.0
G_COLS = 1552
G_COLS_PAD = 1664
GATE_COLS = 3072
N_GROUPS, EXP_PER_GROUP, N_EXPERTS, TOP_K = 4, 8, 32, 2
D_EXPERT = 512
EPS = 1e-6

LANES = 128
SUBLANES = 8
NSUB = D_MODEL // LANES
CHUNK = 64
VMEM_LIMIT = 56 * 1024 * 1024


def _dot(a, b):
    return jnp.dot(a.astype(BF16), b.astype(BF16), preferred_element_type=F32)


def _dot_hi(a, b):
    return jnp.dot(a, b, precision=HIGHEST, preferred_element_type=F32)


def _dot_nt(a, b, precision=None):
    return lax.dot_general(a, b, (((1,), (1,)), ((), ())), precision=precision,
                           preferred_element_type=F32)


def _dot_tn(a, b, precision=None):
    return lax.dot_general(a, b, (((0,), (0,)), ((), ())), precision=precision,
                           preferred_element_type=F32)


def _sigmoid(x):
    return 1.0 / (1.0 + jnp.exp(-x))


def _softplus(x):
    return jnp.maximum(x, 0.0) + jnp.log(1.0 + jnp.exp(-jnp.abs(x)))


def _seg_ones(n, seg):
    r = lax.broadcasted_iota(jnp.int32, (n, n), 0) // seg
    c = lax.broadcasted_iota(jnp.int32, (n, n), 1) // seg
    return (r == c).astype(F32)


def _tri(n, strict):
    r = lax.broadcasted_iota(jnp.int32, (n, n), 0)
    c = lax.broadcasted_iota(jnp.int32, (n, n), 1)
    return (c < r) if strict else (c <= r)


def _adaln_kernel(c_ref, w_ref, b_ref, o_ref):
    c = c_ref[...]
    c_act = c * _sigmoid(c)
    o_ref[...] = _dot_hi(c_act, w_ref[...]) + b_ref[...]


def _adaln(c, ada_w, ada_b):
    L, D, N = ada_w.shape
    B = c.shape[0]
    tn = D
    return pl.pallas_call(
        _adaln_kernel,
        grid=(L, N // tn),
        in_specs=[
            pl.BlockSpec((B, D), lambda l, j: (0, 0)),
            pl.BlockSpec((None, D, tn), lambda l, j: (l, 0, j)),
            pl.BlockSpec((None, 1, tn), lambda l, j: (l, 0, j)),
        ],
        out_specs=pl.BlockSpec((None, B, tn), lambda l, j: (l, 0, j)),
        out_shape=jax.ShapeDtypeStruct((L, B, N), F32),
        name="adaln",
    )(c, ada_w, ada_b.reshape(L, 1, N))


_IN_SEGS = (A_COLS, R_COLS, G_COLS_PAD, GATE_COLS)
_IN_DTYPES = (BF16, F32, F32, BF16)
_IN_CHUNK = 512


def _inproj_kernel(x_ref, mod_ref, g_ref, w_ref, *o_refs):
    x = x_ref[...]
    D = x.shape[-1]
    ms = jnp.mean(x * x, axis=-1, keepdims=True)
    y = x * lax.rsqrt(ms + EPS) * g_ref[...]
    sh = mod_ref[:, 0:D]
    sc = mod_ref[:, D:2 * D]
    h = (y * (1.0 + sc) + sh).astype(BF16)
    base = 0
    for o_ref, width in zip(o_refs, _IN_SEGS):
        for c0 in range(0, width, _IN_CHUNK):
            c1 = min(c0 + _IN_CHUNK, width)
            o_ref[:, c0:c1] = jnp.dot(
                h, w_ref[:, base + c0:base + c1], preferred_element_type=F32
            ).astype(o_ref.dtype)
        base += width


def _inproj(x2, mod_l, norm_g, w_pad, S, tm):
    T, D = x2.shape
    NP = w_pad.shape[1]
    tiles_per_batch = S // tm
    return pl.pallas_call(
        _inproj_kernel,
        grid=(T // tm,),
        in_specs=[
            pl.BlockSpec((tm, D), lambda i: (i, 0)),
            pl.BlockSpec((None, 1, 2 * D), lambda i: (i // tiles_per_batch, 0, 0)),
            pl.BlockSpec((1, D), lambda i: (0, 0)),
            pl.BlockSpec((D, NP), lambda i: (0, 0), pipeline_mode=pl.Buffered(1)),
        ],
        out_specs=[pl.BlockSpec((tm, w), lambda i: (i, 0)) for w in _IN_SEGS],
        out_shape=[jax.ShapeDtypeStruct((T, w), dt) for w, dt in zip(_IN_SEGS, _IN_DTYPES)],
        compiler_params=pltpu.CompilerParams(
            dimension_semantics=("arbitrary",), vmem_limit_bytes=VMEM_LIMIT),
        name="inproj",
    )(x2, mod_l, norm_g.reshape(1, D), w_pad)


A_TILE = 256
A_POS_SPLIT = 64


def _split_dot(x, ones):
    hi = x.astype(BF16)
    lo = (x - hi.astype(F32)).astype(BF16)
    return (jnp.dot(hi, ones, preferred_element_type=F32)
            + jnp.dot(lo, ones, preferred_element_type=F32))


def _eye(n):
    r = lax.broadcasted_iota(jnp.int32, (n, n), 0)
    c = lax.broadcasted_iota(jnp.int32, (n, n), 1)
    return (r == c).astype(BF16)


def _attn_kernel(q_ref, k_ref, v_ref, qg_ref, kg_ref, lam_ref, sg_ref, slope_ref, o_ref,
                 qt_s, ka_s, vt_s, s_s, *, S, lambda_init):
    t = A_TILE
    seg = _seg_ones(LANES, A_DH).astype(BF16)
    eye_t = _eye(t)
    eye_v = _eye(A_DV)
    slope = slope_ref[...]
    lane = lax.broadcasted_iota(jnp.int32, (t, LANES), 1)
    row = lax.broadcasted_iota(jnp.int32, (t, LANES), 0)

    def qknorm(x, g, scale):
        xf = x.astype(F32)
        ms = _split_dot(xf * xf, seg) * (1.0 / A_DH)
        return xf * lax.rsqrt(ms + EPS) * g * scale

    for b in range(S // t):
        rows = slice(b * t, (b + 1) * t)
        pos = row + b * t
        hi = (pos // A_POS_SPLIT).astype(F32)
        lo = (pos % A_POS_SPLIT).astype(F32)
        q_aug = jnp.where(lane == A_DH, A_POS_SPLIT * slope,
                jnp.where(lane == A_DH + 1, slope,
                jnp.where(lane == A_DH + 2, -A_POS_SPLIT * slope * hi,
                jnp.where(lane == A_DH + 3, -slope * lo, 0.0))))
        k_aug = jnp.where(lane == A_DH, hi,
                jnp.where(lane == A_DH + 1, lo,
                jnp.where(jnp.logical_or(lane == A_DH + 2, lane == A_DH + 3), 1.0, 0.0)))
        qn = qknorm(q_ref[rows, :], qg_ref[...], A_DH ** -0.5)
        kn = qknorm(k_ref[rows, :], kg_ref[...], 1.0)
        for c in range(2):
            qc = qn if c == 0 else pltpu.roll(qn, A_DH, 1)
            kc = kn if c == 0 else pltpu.roll(kn, A_DH, 1)
            qa = jnp.where(lane < A_DH, qc, q_aug).astype(BF16)
            qt_s[c, :, rows] = _dot_tn(qa, eye_t).astype(BF16)
            ka_s[c, rows, :] = jnp.where(lane < A_DH, kc, k_aug).astype(BF16)
        vt_s[:, rows] = _dot_tn(v_ref[rows, :], eye_t).astype(BF16)

    lv = lam_ref[...]
    lam = (jnp.exp(jnp.sum(lv[0:1] * lv[1:2], axis=-1, keepdims=True))
           - jnp.exp(jnp.sum(lv[2:3] * lv[3:4], axis=-1, keepdims=True)) + lambda_init)
    causal = (lax.broadcasted_iota(jnp.int32, (t, t), 0)
              <= lax.broadcasted_iota(jnp.int32, (t, t), 1))

    nt = S // t

    def score_block(i, c, j, st):
        s = jnp.dot(ka_s[c, j * t:(j + 1) * t, :], qt_s[c, :, i * t:(i + 1) * t],
                    preferred_element_type=F32)
        if j == i:
            s = jnp.where(causal, s, -jnp.inf)
        s_s[i % 2, c, j * t:(j + 1) * t, :] = s
        mj = jnp.max(s, axis=0, keepdims=True)
        st["m"][c] = mj if st["m"][c] is None else jnp.maximum(st["m"][c], mj)

    def value_block(i, c, j, st):
        p = jnp.exp(s_s[i % 2, c, j * t:(j + 1) * t, :] - st["m"][c])
        st["l"][c] = st["l"][c] + jnp.sum(p, axis=0, keepdims=True)
        st["acc"][c] = st["acc"][c] + jnp.dot(vt_s[:, j * t:(j + 1) * t], p.astype(BF16),
                                              preferred_element_type=F32)

    def finish(i, st):
        o = st["acc"][0] / st["l"][0] - lam * (st["acc"][1] / st["l"][1])
        ms = jnp.mean(o * o, axis=0, keepdims=True)
        o = o * lax.rsqrt(ms + EPS) * sg_ref[...] * (1.0 - lambda_init)
        o_ref[i * t:(i + 1) * t, :] = _dot_tn(o.astype(BF16), eye_v).astype(o_ref.dtype)

    prev = None
    for i in range(nt + 1):
        cur = None
        first = []
        if i < nt:
            cur = dict(m=[None, None], l=[jnp.zeros((1, t), F32)] * 2,
                       acc=[jnp.zeros((A_DV, t), F32)] * 2)
            first = [(c, j) for j in range(i + 1) for c in range(2)]
        second = [(c, j) for j in range(i) for c in range(2)] if prev is not None else []
        for n in range(max(len(first), len(second))):
            if n < len(first):
                score_block(i, first[n][0], first[n][1], cur)
            if n < len(second):
                value_block(i - 1, second[n][0], second[n][1], prev)
        if prev is not None:
            finish(i - 1, prev)
        prev = cur


def _attention(p_attn, qn_g, kn_g, lam_vecs, subln_g, lambda_init, B, S):
    pa = p_attn.reshape(B, S, A_COLS)
    dup = lambda g: jnp.concatenate([g, g]).reshape(1, LANES)
    slopes = jnp.asarray(
        [[2.0 ** (-8.0 * (i + 1) / A_HEADS)] * LANES for i in range(A_HEADS)], F32
    ).reshape(A_HEADS, 1, LANES)
    nqb = A_QW // LANES
    kern = functools.partial(_attn_kernel, S=S, lambda_init=lambda_init)
    out = pl.pallas_call(
        kern,
        grid=(B, A_HEADS),
        in_specs=[
            pl.BlockSpec((None, S, LANES), lambda b, h: (b, 0, h)),
            pl.BlockSpec((None, S, LANES), lambda b, h: (b, 0, nqb + h)),
            pl.BlockSpec((None, S, LANES), lambda b, h: (b, 0, 2 * nqb + h)),
            pl.BlockSpec((1, LANES), lambda b, h: (0, 0)),
            pl.BlockSpec((1, LANES), lambda b, h: (0, 0)),
            pl.BlockSpec((4, A_DH), lambda b, h: (0, 0)),
            pl.BlockSpec((A_DV, 1), lambda b, h: (0, 0)),
            pl.BlockSpec((None, 1, LANES), lambda b, h: (h, 0, 0)),
        ],
        out_specs=pl.BlockSpec((None, S, A_DV), lambda b, h: (b, 0, h)),
        out_shape=jax.ShapeDtypeStruct((B, S, A_VW), BF16),
        scratch_shapes=[pltpu.VMEM((2, LANES, S), BF16), pltpu.VMEM((2, S, LANES), BF16),
                        pltpu.VMEM((A_DV, S), BF16), pltpu.VMEM((2, 2, S, A_TILE), F32)],
        compiler_params=pltpu.CompilerParams(
            dimension_semantics=("arbitrary", "arbitrary"), vmem_limit_bytes=VMEM_LIMIT),
        name="diff_attn",
    )(pa, pa, pa, dup(qn_g), dup(kn_g), lam_vecs, subln_g.reshape(A_DV, 1), slopes)
    return out.reshape(B * S, A_VW)


R_GROUP = 4
R_GW = R_GROUP * R_N


def _rwkv_kernel(p_ref, mu_ref, wup_ref, w0_ref, aup_ref, a0_ref, gup_ref, kk_ref, ka_ref,
                 rk_ref, lg_ref, lb_ref, o_ref,
                 carry_ref, st_ref, al_s, be_s, ka_s, rh_s, bt_s, kt_s, v_s, gc_s, y_s, *, TB):
    @pl.when(pl.program_id(1) == 0)
    def _():
        carry_ref[...] = jnp.zeros_like(carry_ref)
        st_ref[...] = jnp.zeros_like(st_ref)

    xs = p_ref[...]
    prev = pltpu.roll(xs, 1, 0)
    row = lax.broadcasted_iota(jnp.int32, (TB, 1), 0)
    prev = jnp.where(row == 0, carry_ref[...], prev)
    carry_ref[...] = xs[TB - 1:TB, :]
    xm = xs + (prev - xs) * mu_ref[...]
    r = xm[:, 0:R_W]
    k = xm[:, R_W:2 * R_W]
    v = xm[:, 2 * R_W:3 * R_W]
    wa = xm[:, 3 * R_W:3 * R_W + LANES]
    gd = xm[:, 3 * R_W + LANES:3 * R_W + 2 * LANES]
    wz = w0_ref[...] + _dot(jnp.tanh(wa), wup_ref[...])
    w_log = -_softplus(-wz) - 0.5
    lw = -jnp.exp(w_log)
    a = _sigmoid(a0_ref[...] + _dot(wa, aup_ref[...]))
    g = _dot(_sigmoid(gd), gup_ref[...])
    seg = _seg_ones(R_W, R_N).astype(BF16)
    kk = k * kk_ref[...]
    nrm = jnp.sqrt(_split_dot(kk * kk, seg))
    kk = kk / jnp.maximum(nrm, 1e-12)
    k2 = k * (1.0 + (a - 1.0) * ka_ref[...])
    bonus = _split_dot(r * k2 * rk_ref[...], seg) * v
    bv = kk * a

    rr = lax.broadcasted_iota(jnp.int32, (TB, TB), 0)
    cc = lax.broadcasted_iota(jnp.int32, (TB, TB), 1)
    tril_blk = jnp.logical_and(rr // CHUNK == cc // CHUNK, cc <= rr).astype(F32)
    Lg = _dot_hi(tril_blk, lw)
    inv = jnp.exp(-Lg)
    al_s[...] = (jnp.exp(Lg - lw) * kk).astype(BF16)
    be_s[...] = (bv * inv).astype(BF16)
    ka_s[...] = (k2 * inv).astype(BF16)
    rh_s[...] = (jnp.exp(Lg) * r).astype(BF16)
    v_s[...] = v.astype(BF16)
    for c in range(TB // CHUNK):
        rows = slice(c * CHUNK, (c + 1) * CHUNK)
        gC = Lg[(c + 1) * CHUNK - 1:(c + 1) * CHUNK, :]
        tail = jnp.exp(gC - Lg[rows, :])
        bt_s[rows, :] = (bv[rows, :] * tail).astype(BF16)
        kt_s[rows, :] = (k2[rows, :] * tail).astype(BF16)
        gc_s[c * SUBLANES:(c + 1) * SUBLANES, :] = jnp.broadcast_to(jnp.exp(gC), (SUBLANES, R_W))

    ri = lax.broadcasted_iota(jnp.int32, (R_GW, R_GW), 0)
    ci = lax.broadcasted_iota(jnp.int32, (R_GW, R_GW), 1)
    blk = ri // R_N == ci // R_N
    strict = ci % R_N < ri % R_N
    strict_t = ri % R_N < ci % R_N
    incl = ci % R_N <= ri % R_N
    zero = jnp.zeros((), BF16)

    def expand(x):
        return jnp.where(blk, jnp.concatenate([x] * R_GROUP, axis=0), zero)

    n_groups = R_HEADS // R_GROUP
    pair = 2

    def chunk_pair(ip, _):
        chains = []
        for dc in range(pair):
            c = ip * pair + dc
            rows = pl.ds(pl.multiple_of(c * CHUNK, CHUNK), CHUNK)
            for gi in range(n_groups):
                chains.append(dict(c=c, rows=rows, gi=gi, cols=slice(gi * R_GW, (gi + 1) * R_GW)))
        for ch in chains:
            rows, cols = ch["rows"], ch["cols"]
            ch["A"] = expand(al_s[rows, cols])
            ch["R"] = expand(rh_s[rows, cols])
            ch["B"] = expand(be_s[rows, cols])
            ch["K"] = expand(ka_s[rows, cols])
            vc = v_s[rows, cols]
            ch["V"] = jnp.concatenate(
                [vc[:, h * R_N:(h + 1) * R_N] for h in range(R_GROUP)], axis=0)
        for ch in chains:
            ch["X"] = -jnp.where(strict, _dot_nt(ch["A"], ch["B"]), 0.0)
            ch["m_ak_t"] = jnp.where(strict_t, _dot_nt(ch["K"], ch["A"]), 0.0).astype(BF16)
            ch["m_rb"] = jnp.where(incl, _dot_nt(ch["R"], ch["B"]), 0.0).astype(BF16)
            ch["m_rk"] = jnp.where(incl, _dot_nt(ch["R"], ch["K"]), 0.0).astype(BF16)
        for ch in chains:
            ch["Z"] = ch["A"].astype(F32)
            ch["Wt"] = _dot_tn(ch["V"], ch["m_ak_t"])
        n = 1
        while True:
            last = 2 * n >= CHUNK
            for ch in chains:
                Xb = ch["X"].astype(BF16)
                ch["Z"] = ch["Z"] + jnp.dot(Xb, ch["Z"].astype(BF16), preferred_element_type=F32)
                ch["Wt"] = ch["Wt"] + _dot_nt(ch["Wt"].astype(BF16), Xb)
                if not last:
                    ch["X"] = jnp.dot(Xb, Xb, preferred_element_type=F32)
            n *= 2
            if last:
                break
        for ch in chains:
            rows, cols = ch["rows"], ch["cols"]
            Bt = expand(bt_s[rows, cols])
            Kt = expand(kt_s[rows, cols])
            Zb = ch["Z"].astype(BF16)
            Wtb = ch["Wt"].astype(BF16)
            ch["y_a"] = (ch["R"].astype(F32)
                         - jnp.dot(ch["m_rb"], Zb, preferred_element_type=F32)).astype(BF16)
            ch["y_b"] = (jnp.dot(ch["m_rk"], ch["V"], preferred_element_type=F32)
                         - _dot_nt(ch["m_rb"], Wtb))
            ch["p_neg"] = _dot_tn(Zb, Bt).astype(BF16)
            ch["q"] = _dot_tn(ch["V"], Kt) - jnp.dot(Wtb, Bt, preferred_element_type=F32)
        for ch in chains:
            rows, cols, gi = ch["rows"], ch["cols"], ch["gi"]
            g0 = pl.multiple_of(ch["c"] * SUBLANES, SUBLANES)
            S0 = st_ref[gi]
            S0b = S0.astype(BF16)
            y = _dot_nt(ch["y_a"], S0b) + ch["y_b"]
            st_ref[gi] = (S0 * gc_s[pl.ds(g0, SUBLANES), cols][0:1, :]
                          - jnp.dot(S0b, ch["p_neg"], preferred_element_type=F32) + ch["q"])
            for h in range(R_GROUP):
                hh = gi * R_GROUP + h
                y_s[rows, hh * R_N:(hh + 1) * R_N] = y[h * R_N:(h + 1) * R_N, :]
        return 0

    lax.fori_loop(0, TB // CHUNK // pair, chunk_pair, 0)

    y = y_s[...]
    mean = _split_dot(y, seg) * (1.0 / R_N)
    yc = y - mean
    var = _split_dot(yc * yc, seg) * (1.0 / R_N)
    yn = yc * lax.rsqrt(var + RWKV_GN_EPS) * lg_ref[...] + lb_ref[...]
    o_ref[...] = ((yn + bonus) * g).astype(o_ref.dtype)


def _rwkv(p_rwkv, mu, w_up, w0, a_up, a0, g_up, k_k, k_a, r_k, lnx_g, lnx_b, B, S):
    TB = 256
    T = B * S
    nt = S // TB
    row = lambda t: t.reshape(1, -1)
    zeros = jnp.zeros((R_N, R_W), F32)
    wup_pad = jnp.concatenate([w_up, zeros], axis=0)
    aup_pad = jnp.concatenate([zeros, a_up], axis=0)
    vec = lambda n: pl.BlockSpec((1, n), lambda b, i: (0, 0))
    mat = lambda m, n: pl.BlockSpec((m, n), lambda b, i: (0, 0))
    kern = functools.partial(_rwkv_kernel, TB=TB)
    return pl.pallas_call(
        kern,
        grid=(B, nt),
        in_specs=[
            pl.BlockSpec((TB, R_COLS), lambda b, i: (b * nt + i, 0)),
            vec(R_COLS), mat(LANES, R_W), vec(R_W), mat(LANES, R_W), vec(R_W), mat(LANES, R_W),
            vec(R_W), vec(R_W), vec(R_W), vec(R_W), vec(R_W),
        ],
        out_specs=pl.BlockSpec((TB, R_W), lambda b, i: (b * nt + i, 0)),
        out_shape=jax.ShapeDtypeStruct((T, R_W), BF16),
        scratch_shapes=[
            pltpu.VMEM((1, R_COLS), F32),
            pltpu.VMEM((R_HEADS // R_GROUP, R_N, R_GW), F32),
        ] + [pltpu.VMEM((TB, R_W), BF16)] * 7 + [
            pltpu.VMEM((TB // CHUNK * SUBLANES, R_W), F32),
            pltpu.VMEM((TB, R_W), F32),
        ],
        compiler_params=pltpu.CompilerParams(
            dimension_semantics=("arbitrary", "arbitrary"), vmem_limit_bytes=VMEM_LIMIT),
        name="rwkv7",
    )(p_rwkv, row(mu), wup_pad, row(w0), aup_pad, row(a0), g_up, row(k_k), row(k_a),
      row(r_k), row(lnx_g), row(lnx_b))


def _gla_kernel(p_ref, aup_ref, ab_ref, ng_ref, o_ref, st_ref, *, TB):
    @pl.when(pl.program_id(1) == 0)
    def _():
        st_ref[...] = jnp.zeros_like(st_ref)

    c_gv = 2 * G_KW
    c_ad = c_gv + G_VW
    c_gate = c_ad + LANES
    nchunk = TB // CHUNK
    rr = lax.broadcasted_iota(jnp.int32, (TB, TB), 0)
    cc = lax.broadcasted_iota(jnp.int32, (TB, TB), 1)
    causal = jnp.logical_and(rr // CHUNK == cc // CHUNK, cc <= rr)

    q = p_ref[:, 0:G_KW] * (G_DK ** -0.5)
    k = p_ref[:, G_KW:2 * G_KW]
    vb = p_ref[:, c_gv:c_gv + G_VW].astype(BF16)
    z = _dot_hi(p_ref[:, c_ad:c_ad + LANES], aup_ref[...]) + ab_ref[...]
    la = -_softplus(-z) * (1.0 / G_TAU)
    b = _dot_hi(causal.astype(F32), la)
    qe = (q * jnp.exp(b)).astype(BF16)
    ke = (k * jnp.exp(-b)).astype(BF16)
    kts, e_lasts = [], []
    for c in range(nchunk):
        rows = slice(c * CHUNK, (c + 1) * CHUNK)
        b_last = b[(c + 1) * CHUNK - 1:(c + 1) * CHUNK, :]
        kts.append((k[rows, :] * jnp.exp(b_last - b[rows, :])).astype(BF16))
        e_lasts.append(jnp.exp(b_last))
    heads = range(G_HEADS)
    sls = [slice(h * G_DK, (h + 1) * G_DK) for h in heads]
    vss = [slice(h * G_DV, (h + 1) * G_DV) for h in heads]
    scs = [jnp.where(causal, _dot_nt(qe[:, sls[h]], ke[:, sls[h]]), 0.0) for h in heads]
    o_intra = [_dot(scs[h], vb[:, vss[h]]) for h in heads]
    kvs = [[_dot_tn(vb[c * CHUNK:(c + 1) * CHUNK, vss[h]], kts[c][:, sls[h]]) for h in heads]
           for c in range(nchunk)]
    states = [st_ref[h] for h in heads]
    parts = [[] for _ in heads]
    for c in range(nchunk):
        rows = slice(c * CHUNK, (c + 1) * CHUNK)
        for h in heads:
            parts[h].append(o_intra[h][rows, :] + _dot_nt(qe[rows, sls[h]], states[h].astype(BF16)))
            states[h] = states[h] * e_lasts[c][:, sls[h]] + kvs[c][h]
    for h in heads:
        vs = vss[h]
        st_ref[h] = states[h]
        o = jnp.concatenate(parts[h], axis=0)
        ms = jnp.mean(o * o, axis=-1, keepdims=True)
        gt = p_ref[:, c_gate + h * G_DV:c_gate + (h + 1) * G_DV]
        o = o * lax.rsqrt(ms + EPS) * ng_ref[...] * (gt * _sigmoid(gt))
        o_ref[:, vs] = o.astype(o_ref.dtype)


def _gla(p_gla, alpha_up, alpha_b, norm_g, B, S):
    TB = 256
    T = B * S
    nt = S // TB
    aup_pad = jnp.concatenate([alpha_up, jnp.zeros((LANES - G_LORA, G_KW), F32)], axis=0)
    kern = functools.partial(_gla_kernel, TB=TB)
    return pl.pallas_call(
        kern,
        grid=(B, nt),
        in_specs=[
            pl.BlockSpec((TB, G_COLS_PAD), lambda b, i: (b * nt + i, 0)),
            pl.BlockSpec((LANES, G_KW), lambda b, i: (0, 0)),
            pl.BlockSpec((1, G_KW), lambda b, i: (0, 0)),
            pl.BlockSpec((1, G_DV), lambda b, i: (0, 0)),
        ],
        out_specs=pl.BlockSpec((TB, G_VW), lambda b, i: (b * nt + i, 0)),
        out_shape=jax.ShapeDtypeStruct((T, G_VW), BF16),
        scratch_shapes=[pltpu.VMEM((G_HEADS, G_DV, G_DK), F32)],
        compiler_params=pltpu.CompilerParams(
            dimension_semantics=("arbitrary", "arbitrary"), vmem_limit_bytes=VMEM_LIMIT),
        name="gla",
    )(p_gla, aup_pad, alpha_b.reshape(1, G_KW), norm_g.reshape(1, G_DV))


def _merge_kernel(x_ref, oa_ref, or_ref, og_ref, gate_ref, pa_ref, pr_ref, pg_ref, wo_ref,
                  mod_ref, g2_ref, rw_ref, rb_ref, x1_ref, h2_ref, lg_ref):
    D = D_MODEL
    merged = (_sigmoid(gate_ref[:, 0:D].astype(F32))
              * jnp.dot(oa_ref[...], pa_ref[...], preferred_element_type=F32)
              + _sigmoid(gate_ref[:, D:2 * D].astype(F32))
              * jnp.dot(or_ref[...], pr_ref[...], preferred_element_type=F32)
              + _sigmoid(gate_ref[:, 2 * D:3 * D].astype(F32))
              * jnp.dot(og_ref[...], pg_ref[...], preferred_element_type=F32))
    gt1 = mod_ref[:, 2 * D:3 * D]
    sh2 = mod_ref[:, 3 * D:4 * D]
    sc2 = mod_ref[:, 4 * D:5 * D]
    x1 = x_ref[...] + gt1 * jnp.dot(merged.astype(BF16), wo_ref[...], preferred_element_type=F32)
    x1_ref[...] = x1
    ms = jnp.mean(x1 * x1, axis=-1, keepdims=True)
    h2 = x1 * lax.rsqrt(ms + EPS) * g2_ref[...] * (1.0 + sc2) + sh2
    for s in range(NSUB):
        h2_ref[pl.ds(s, h2.shape[0], stride=NSUB), :] = h2[:, s * LANES:(s + 1) * LANES]
    h_hi = h2.astype(BF16)
    h_lo = (h2 - h_hi.astype(F32)).astype(BF16)
    lg_ref[...] = (jnp.dot(h_hi, rw_ref[0], preferred_element_type=F32)
                   + jnp.dot(h_lo, rw_ref[0], preferred_element_type=F32)
                   + jnp.dot(h_hi, rw_ref[1], preferred_element_type=F32) + rb_ref[...])


def _merge(x2, o_a, o_r, o_g, p_gate, proj_a, proj_r, proj_g, w_out, mod_l, norm2_g,
           router_w, router_b, S, tm):
    T, D = x2.shape
    tiles_per_batch = S // tm
    tile = lambda w: pl.BlockSpec((tm, w), lambda i: (i, 0))
    const = lambda m, n: pl.BlockSpec((m, n), lambda i: (0, 0))
    return pl.pallas_call(
        _merge_kernel,
        grid=(T // tm,),
        in_specs=[
            tile(D), tile(A_VW), tile(R_W), tile(G_VW), tile(GATE_COLS),
            const(A_VW, D), const(R_W, D), const(G_VW, D), const(D, D),
            pl.BlockSpec((None, 1, 6 * D), lambda i: (i // tiles_per_batch, 0, 0)),
            const(1, D), pl.BlockSpec((2, D, LANES), lambda i: (0, 0, 0)), const(1, LANES),
        ],
        out_specs=[
            tile(D),
            pl.BlockSpec((tm * NSUB, LANES), lambda i: (i, 0)),
            tile(LANES),
        ],
        out_shape=[
            jax.ShapeDtypeStruct((T, D), F32),
            jax.ShapeDtypeStruct((T * NSUB, LANES), F32),
            jax.ShapeDtypeStruct((T, LANES), F32),
        ],
        compiler_params=pltpu.CompilerParams(
            dimension_semantics=("arbitrary",), vmem_limit_bytes=VMEM_LIMIT),
        name="merge",
    )(x2, o_a, o_r, o_g, p_gate, proj_a, proj_r, proj_g, w_out, mod_l,
      norm2_g.reshape(1, D), router_w, router_b)


MOE_ROWS = 256
ROUTE_ROWS = 512
E_LANE0 = N_GROUPS


def _route_kernel(lg_ref, info_ref, cnt_ref, carry_ref):
    @pl.when(pl.program_id(0) == 0)
    def _():
        carry_ref[...] = jnp.zeros_like(carry_ref)

    lg = lg_ref[...]
    n = lg.shape[0]
    lane = lax.broadcasted_iota(jnp.int32, (n, LANES), 1).astype(F32)
    neg = -jnp.inf
    big = float(LANES)

    def first_max(vals):
        m = jnp.max(vals, axis=-1, keepdims=True)
        idx = jnp.min(jnp.where(vals == m, lane, big), axis=-1, keepdims=True)
        return m, idx

    in_grp = lane < N_GROUPS
    gm, grp = first_max(jnp.where(in_grp, lg, neg))
    g_prob = 1.0 / jnp.sum(jnp.where(in_grp, jnp.exp(lg - gm), 0.0), axis=-1, keepdims=True)
    lo = E_LANE0 + grp * EXP_PER_GROUP
    el = jnp.where(jnp.logical_and(lane >= lo, lane < lo + EXP_PER_GROUP), lg, neg)
    v1, i1 = first_max(el)
    v2, i2 = first_max(jnp.where(lane == i1, neg, el))
    e21 = jnp.exp(v2 - v1)
    w0 = g_prob / (1.0 + e21)
    w1 = g_prob * e21 / (1.0 + e21)
    oh0 = lane == i1
    oh1 = lane == i2
    oh = jnp.logical_or(oh0, oh1).astype(F32)
    before = _tri(n, True).astype(BF16)
    cnt = jnp.dot(before, oh.astype(BF16), preferred_element_type=F32) + carry_ref[...]
    rank0 = jnp.sum(jnp.where(oh0, cnt, 0.0), axis=-1, keepdims=True)
    rank1 = jnp.sum(jnp.where(oh1, cnt, 0.0), axis=-1, keepdims=True)
    carry = carry_ref[...] + jnp.sum(oh, axis=0, keepdims=True)
    carry_ref[...] = carry
    cnt_ref[...] = carry
    cols = (i1 - E_LANE0, i2 - E_LANE0, rank0, rank1, w0, w1)
    info = jnp.zeros((n, LANES), F32)
    for j, col in enumerate(cols):
        info = jnp.where(lane == j, col, info)
    info_ref[...] = info


def _route(logits):
    T = logits.shape[0]
    tr = min(ROUTE_ROWS, T)
    info, cnt = pl.pallas_call(
        _route_kernel,
        grid=(T // tr,),
        in_specs=[pl.BlockSpec((tr, LANES), lambda i: (i, 0))],
        out_specs=[pl.BlockSpec((tr, LANES), lambda i: (i, 0)),
                   pl.BlockSpec((1, LANES), lambda i: (0, 0))],
        out_shape=[jax.ShapeDtypeStruct((T, LANES), F32), jax.ShapeDtypeStruct((1, LANES), F32)],
        scratch_shapes=[pltpu.VMEM((1, LANES), F32)],
        compiler_params=pltpu.CompilerParams(dimension_semantics=("arbitrary",)),
        name="moe_route",
    )(logits)
    A = T * TOP_K
    counts = cnt[0, E_LANE0:E_LANE0 + N_EXPERTS].astype(jnp.int32)
    padded = (counts + MOE_ROWS - 1) // MOE_ROWS * MOE_ROWS
    pad_end = jnp.cumsum(padded)
    pad_start = pad_end - padded
    n_blocks = -(-A // MOE_ROWS) + N_EXPERTS
    eid = info[:, 0:2].astype(jnp.int32)
    dest = jnp.take(pad_start, eid) + info[:, 2:4].astype(jnp.int32)
    blk_start = jnp.arange(n_blocks, dtype=jnp.int32) * MOE_ROWS
    blk_exp = jnp.minimum(jnp.sum(pad_end[None, :] <= blk_start[:, None], axis=1),
                          N_EXPERTS - 1).astype(jnp.int32)
    n_used = (pad_end[-1:] // MOE_ROWS).astype(jnp.int32)
    return info, dest[:, 0], dest[:, 1], blk_exp, n_used, n_blocks * MOE_ROWS


def _dispatch_kernel(d0_ref, d1_ref, h2_ref, xin_in, xin_hbm, sem):
    del xin_in
    n = h2_ref.shape[0] // NSUB
    base = pl.program_id(0) * n

    def slab(ref, row):
        return ref.at[pl.ds(pl.multiple_of(row * NSUB, NSUB), NSUB)]

    def body(r, _):
        pltpu.make_async_copy(slab(h2_ref, r), slab(xin_hbm, d0_ref[base + r]), sem).start()
        pltpu.make_async_copy(slab(h2_ref, r), slab(xin_hbm, d1_ref[base + r]), sem).start()
        return 0

    lax.fori_loop(0, n, body, 0, unroll=8)
    for _ in range(TOP_K):
        pltpu.make_async_copy(h2_ref, xin_hbm.at[pl.ds(0, n * NSUB)], sem).wait()


def _dispatch(h2_slab, dest0, dest1, n_rows, td):
    T = h2_slab.shape[0] // NSUB
    grid_spec = pltpu.PrefetchScalarGridSpec(
        num_scalar_prefetch=2,
        grid=(T // td,),
        in_specs=[pl.BlockSpec((td * NSUB, LANES), lambda i, d0, d1: (i, 0)),
                  pl.BlockSpec(memory_space=pl.ANY)],
        out_specs=pl.BlockSpec(memory_space=pl.ANY),
        scratch_shapes=[pltpu.SemaphoreType.DMA(())],
    )
    return pl.pallas_call(
        _dispatch_kernel,
        grid_spec=grid_spec,
        out_shape=jax.ShapeDtypeStruct((n_rows * NSUB, LANES), F32),
        input_output_aliases={3: 0},
        compiler_params=pltpu.CompilerParams(dimension_semantics=("arbitrary",)),
        name="moe_dispatch",
    )(dest0, dest1, h2_slab, jnp.zeros((n_rows * NSUB, LANES), F32))


def _moe_kernel(be_ref, nu_ref, x_ref, wg_ref, wu_ref, wd_ref, y_ref, wgb, wub, wdb):
    i = pl.program_id(0)

    @pl.when(i < nu_ref[0])
    def _():
        changed = jnp.logical_or(i == 0, be_ref[i] != be_ref[jnp.maximum(i - 1, 0)])

        @pl.when(changed)
        def _():
            wgb[...] = wg_ref[...].astype(BF16)
            wub[...] = wu_ref[...].astype(BF16)
            wdb[...] = wd_ref[...].astype(BF16)

        halves = range(2)
        hr = MOE_ROWS // 2
        xs = [jnp.concatenate(
            [x_ref[pl.ds(p * hr * NSUB + s, hr, stride=NSUB), :].astype(BF16)
             for s in range(NSUB)], axis=-1) for p in halves]
        hgs = [jnp.dot(xs[p], wgb[...], preferred_element_type=F32) for p in halves]
        hus = [jnp.dot(xs[p], wub[...], preferred_element_type=F32) for p in halves]
        hids = [(hgs[p] * _sigmoid(hgs[p]) * hus[p]).astype(BF16) for p in halves]
        ys = [jnp.dot(hids[p], wdb[...], preferred_element_type=F32) for p in halves]
        for p in halves:
            for s in range(NSUB):
                y_ref[pl.ds(p * hr * NSUB + s, hr, stride=NSUB), :] = (
                    ys[p][:, s * LANES:(s + 1) * LANES])

    @pl.when(i >= nu_ref[0])
    def _():
        y_ref[...] = jnp.zeros_like(y_ref)


def _moe(xin, blk_exp, n_used, w_gate, w_up, w_down, layer):
    blk_rows = MOE_ROWS * NSUB
    n_blocks = xin.shape[0] // blk_rows
    wspec = lambda m, n: pl.BlockSpec((None, None, m, n), lambda i, be, nu: (layer, be[i], 0, 0))
    last = lambda i, nu: jnp.minimum(i, nu[0] - 1)
    grid_spec = pltpu.PrefetchScalarGridSpec(
        num_scalar_prefetch=2,
        grid=(n_blocks,),
        in_specs=[
            pl.BlockSpec((blk_rows, LANES), lambda i, be, nu: (last(i, nu), 0)),
            wspec(D_MODEL, D_EXPERT), wspec(D_MODEL, D_EXPERT), wspec(D_EXPERT, D_MODEL),
        ],
        out_specs=pl.BlockSpec((blk_rows, LANES), lambda i, be, nu: (i, 0)),
        scratch_shapes=[
            pltpu.VMEM((D_MODEL, D_EXPERT), BF16),
            pltpu.VMEM((D_MODEL, D_EXPERT), BF16),
            pltpu.VMEM((D_EXPERT, D_MODEL), BF16),
        ],
    )
    return pl.pallas_call(
        _moe_kernel,
        grid_spec=grid_spec,
        out_shape=jax.ShapeDtypeStruct(xin.shape, F32),
        compiler_params=pltpu.CompilerParams(
            dimension_semantics=("arbitrary",), vmem_limit_bytes=VMEM_LIMIT),
        name="moe_ffn",
    )(blk_exp, n_used, xin, w_gate, w_up, w_down)


def _combine_kernel(d0_ref, d1_ref, y_hbm, x1_ref, info_ref, mod_ref, o_ref, ybuf, sem):
    i = pl.program_id(0)
    nsteps = pl.num_programs(0)
    n = x1_ref.shape[0]
    slot = i % 2
    D = D_MODEL

    def slab(ref, row):
        return ref.at[pl.ds(pl.multiple_of(row * NSUB, NSUB), NSUB)]

    def start_gather(step, sl):
        base = step * n

        def body(r, _):
            pltpu.make_async_copy(
                slab(y_hbm, d0_ref[base + r]), slab(ybuf.at[sl, 0], r), sem.at[sl]).start()
            pltpu.make_async_copy(
                slab(y_hbm, d1_ref[base + r]), slab(ybuf.at[sl, 1], r), sem.at[sl]).start()
            return 0
        lax.fori_loop(0, n, body, 0, unroll=8)

    @pl.when(i == 0)
    def _():
        start_gather(0, 0)

    @pl.when(i + 1 < nsteps)
    def _():
        start_gather(i + 1, 1 - slot)

    for k in range(TOP_K):
        pltpu.make_async_copy(
            y_hbm.at[pl.ds(0, n * NSUB)], ybuf.at[slot, k], sem.at[slot]).wait()

    w0 = info_ref[:, 4:5]
    w1 = info_ref[:, 5:6]
    for s in range(NSUB):
        cols = slice(s * LANES, (s + 1) * LANES)
        gt2 = mod_ref[:, 5 * D + s * LANES:5 * D + (s + 1) * LANES]
        piece = pl.ds(s, n, stride=NSUB)
        moe = w0 * ybuf[slot, 0, piece, :] + w1 * ybuf[slot, 1, piece, :]
        o_ref[:, cols] = x1_ref[:, cols] + gt2 * moe


def _combine(x1, y, info, dest0, dest1, mod_l, S, tm):
    T, D = x1.shape
    tiles_per_batch = S // tm
    grid_spec = pltpu.PrefetchScalarGridSpec(
        num_scalar_prefetch=2,
        grid=(T // tm,),
        in_specs=[
            pl.BlockSpec(memory_space=pl.ANY),
            pl.BlockSpec((tm, D), lambda i, d0, d1: (i, 0)),
            pl.BlockSpec((tm, LANES), lambda i, d0, d1: (i, 0)),
            pl.BlockSpec((None, 1, 6 * D), lambda i, d0, d1: (i // tiles_per_batch, 0, 0)),
        ],
        out_specs=pl.BlockSpec((tm, D), lambda i, d0, d1: (i, 0)),
        scratch_shapes=[pltpu.VMEM((2, TOP_K, tm * NSUB, LANES), F32),
                        pltpu.SemaphoreType.DMA((2,))],
    )
    return pl.pallas_call(
        _combine_kernel,
        grid_spec=grid_spec,
        out_shape=jax.ShapeDtypeStruct((T, D), F32),
        compiler_params=pltpu.CompilerParams(
            dimension_semantics=("arbitrary",), vmem_limit_bytes=VMEM_LIMIT),
        name="moe_combine",
    )(dest0, dest1, y, x1, info, mod_l)


def _pad_w_in(w):
    D = w.shape[0]
    c0 = A_COLS + R_COLS
    c_ad = c0 + 2 * G_KW + G_VW
    c_gg = c_ad + G_LORA
    pad = jnp.zeros((D, LANES - G_LORA), w.dtype)
    return jnp.concatenate([w[:, :c_gg], pad, w[:, c_gg:]], axis=1).astype(BF16)


def kernel(x, c, ada_w, ada_b, norm1_g, norm2_g, w_in, attn_qn_g, attn_kn_g, attn_lambda,
           attn_subln_g, rwkv_mu, rwkv_w_up, rwkv_w0, rwkv_a_up, rwkv_a0, rwkv_g_up, rwkv_k_k,
           rwkv_k_a, rwkv_r_k, rwkv_lnx_g, rwkv_lnx_b, gla_alpha_up, gla_alpha_b, gla_norm_g,
           proj_attn, proj_rwkv, proj_gla, w_out, router_grp_w, router_grp_b, router_exp_w,
           router_exp_b, exp_w_gate, exp_w_up, exp_w_down):
    B, S, D = x.shape
    T = B * S
    L = ada_w.shape[0]
    tm = 256
    mod = _adaln(c, ada_w, ada_b).reshape(L, B, 1, 6 * D)
    x2 = x.reshape(T, D)
    for l in range(L):
        lambda_init = 0.8 - 0.6 * math.exp(-0.3 * l)
        p_attn, p_rwkv, p_gla, p_gate = _inproj(x2, mod[l], norm1_g[l], _pad_w_in(w_in[l]), S, tm)
        o_a = _attention(p_attn, attn_qn_g[l], attn_kn_g[l], attn_lambda[l], attn_subln_g[l],
                         lambda_init, B, S)
        o_r = _rwkv(p_rwkv, rwkv_mu[l], rwkv_w_up[l], rwkv_w0[l], rwkv_a_up[l], rwkv_a0[l],
                    rwkv_g_up[l], rwkv_k_k[l], rwkv_k_a[l], rwkv_r_k[l], rwkv_lnx_g[l],
                    rwkv_lnx_b[l], B, S)
        o_g = _gla(p_gla, gla_alpha_up[l], gla_alpha_b[l], gla_norm_g[l], B, S)
        n_r = N_GROUPS + N_EXPERTS
        router_w = jnp.concatenate(
            [router_grp_w[l], router_exp_w[l], jnp.zeros((D, LANES - n_r), F32)], axis=1)
        rw_hi = router_w.astype(BF16)
        router_w = jnp.stack([rw_hi, (router_w - rw_hi.astype(F32)).astype(BF16)])
        router_b = jnp.concatenate(
            [router_grp_b[l], router_exp_b[l], jnp.zeros((LANES - n_r,), F32)]).reshape(1, LANES)
        x1, h2, logits = _merge(
            x2, o_a, o_r, o_g, p_gate, proj_attn[l].astype(BF16), proj_rwkv[l].astype(BF16),
            proj_gla[l].astype(BF16), w_out[l].astype(BF16), mod[l], norm2_g[l],
            router_w, router_b, S, tm)
        info, dest0, dest1, blk_exp, n_used, n_rows = _route(logits)
        xin = _dispatch(h2, dest0, dest1, n_rows, tm)
        y = _moe(xin, blk_exp, n_used, exp_w_gate, exp_w_up, exp_w_down, l)
        x2 = _combine(x1, y, info, dest0, dest1, mod[l], S, tm)
    return x2.reshape(B, S, D)
```

```python
import functools
import math

import jax
import jax.numpy as jnp
from jax import lax
from jax.experimental import pallas as pl
from jax.experimental.pallas import tpu as pltpu

F32 = jnp.float32
BF16 = jnp.bfloat16
HIGHEST = lax.Precision.HIGHEST

D_MODEL = 1024
A_HEADS, A_DH, A_DV = 4, 64, 128
A_QW, A_VW = 512, 512
A_COLS = 1536
R_HEADS, R_N, R_W = 8, 64, 512
R_COLS = 1792
RWKV_GN_EPS = 64e-5
G_HEADS, G_DK, G_DV = 4, 64, 128
G_KW, G_VW, G_LORA = 256, 512, 16
G_TAU = 16.0
G_COLS = 1552
G_COLS_PAD = 1664
GATE_COLS = 3072
N_GROUPS, EXP_PER_GROUP, N_EXPERTS, TOP_K = 4, 8, 32, 2
D_EXPERT = 512
EPS = 1e-6

LANES = 128
SUBLANES = 8
NSUB = D_MODEL // LANES
CHUNK = 64
VMEM_LIMIT = 56 * 1024 * 1024


def _dot(a, b):
    return jnp.dot(a.astype(BF16), b.astype(BF16), preferred_element_type=F32)


def _dot_hi(a, b):
    return jnp.dot(a, b, precision=HIGHEST, preferred_element_type=F32)


def _dot_nt(a, b, precision=None):
    return lax.dot_general(a, b, (((1,), (1,)), ((), ())), precision=precision,
                           preferred_element_type=F32)


def _dot_tn(a, b, precision=None):
    return lax.dot_general(a, b, (((0,), (0,)), ((), ())), precision=precision,
                           preferred_element_type=F32)


def _sigmoid(x):
    return 1.0 / (1.0 + jnp.exp(-x))


def _softplus(x):
    return jnp.maximum(x, 0.0) + jnp.log(1.0 + jnp.exp(-jnp.abs(x)))


def _seg_ones(n, seg):
    r = lax.broadcasted_iota(jnp.int32, (n, n), 0) // seg
    c = lax.broadcasted_iota(jnp.int32, (n, n), 1) // seg
    return (r == c).astype(F32)


def _tri(n, strict):
    r = lax.broadcasted_iota(jnp.int32, (n, n), 0)
    c = lax.broadcasted_iota(jnp.int32, (n, n), 1)
    return (c < r) if strict else (c <= r)


def _adaln_kernel(c_ref, w_ref, b_ref, o_ref):
    c = c_ref[...]
    c_act = c * _sigmoid(c)
    o_ref[...] = _dot_hi(c_act, w_ref[...]) + b_ref[...]


def _adaln(c, ada_w, ada_b):
    L, D, N = ada_w.shape
    B = c.shape[0]
    tn = D
    return pl.pallas_call(
        _adaln_kernel,
        grid=(L, N // tn),
        in_specs=[
            pl.BlockSpec((B, D), lambda l, j: (0, 0)),
            pl.BlockSpec((None, D, tn), lambda l, j: (l, 0, j)),
            pl.BlockSpec((None, 1, tn), lambda l, j: (l, 0, j)),
        ],
        out_specs=pl.BlockSpec((None, B, tn), lambda l, j: (l, 0, j)),
        out_shape=jax.ShapeDtypeStruct((L, B, N), F32),
        name="adaln",
    )(c, ada_w, ada_b.reshape(L, 1, N))


_IN_SEGS = (A_COLS, R_COLS, G_COLS_PAD, GATE_COLS)
_IN_DTYPES = (BF16, F32, F32, BF16)
_IN_CHUNK = 512


def _inproj_kernel(x_ref, mod_ref, g_ref, w_ref, *o_refs):
    x = x_ref[...]
    D = x.shape[-1]
    ms = jnp.mean(x * x, axis=-1, keepdims=True)
    y = x * lax.rsqrt(ms + EPS) * g_ref[...]
    sh = mod_ref[:, 0:D]
    sc = mod_ref[:, D:2 * D]
    h = (y * (1.0 + sc) + sh).astype(BF16)
    base = 0
    for o_ref, width in zip(o_refs, _IN_SEGS):
        for c0 in range(0, width, _IN_CHUNK):
            c1 = min(c0 + _IN_CHUNK, width)
            o_ref[:, c0:c1] = jnp.dot(
                h, w_ref[:, base + c0:base + c1], preferred_element_type=F32
            ).astype(o_ref.dtype)
        base += width


def _inproj(x2, mod_l, norm_g, w_pad, S, tm):
    T, D = x2.shape
    NP = w_pad.shape[1]
    tiles_per_batch = S // tm
    return pl.pallas_call(
        _inproj_kernel,
        grid=(T // tm,),
        in_specs=[
            pl.BlockSpec((tm, D), lambda i: (i, 0)),
            pl.BlockSpec((None, 1, 2 * D), lambda i: (i // tiles_per_batch, 0, 0)),
            pl.BlockSpec((1, D), lambda i: (0, 0)),
            pl.BlockSpec((D, NP), lambda i: (0, 0), pipeline_mode=pl.Buffered(1)),
        ],
        out_specs=[pl.BlockSpec((tm, w), lambda i: (i, 0)) for w in _IN_SEGS],
        out_shape=[jax.ShapeDtypeStruct((T, w), dt) for w, dt in zip(_IN_SEGS, _IN_DTYPES)],
        compiler_params=pltpu.CompilerParams(
            dimension_semantics=("arbitrary",), vmem_limit_bytes=VMEM_LIMIT),
        name="inproj",
    )(x2, mod_l, norm_g.reshape(1, D), w_pad)


A_TILE = 256
A_POS_SPLIT = 64


def _split_dot(x, ones):
    hi = x.astype(BF16)
    lo = (x - hi.astype(F32)).astype(BF16)
    return (jnp.dot(hi, ones, preferred_element_type=F32)
            + jnp.dot(lo, ones, preferred_element_type=F32))


def _eye(n):
    r = lax.broadcasted_iota(jnp.int32, (n, n), 0)
    c = lax.broadcasted_iota(jnp.int32, (n, n), 1)
    return (r == c).astype(BF16)


def _attn_kernel(q_ref, k_ref, v_ref, qg_ref, kg_ref, lam_ref, sg_ref, slope_ref, o_ref,
                 qt_s, ka_s, vt_s, s_s, *, S, lambda_init):
    t = A_TILE
    seg = _seg_ones(LANES, A_DH).astype(BF16)
    eye_t = _eye(t)
    eye_v = _eye(A_DV)
    slope = slope_ref[...]
    lane = lax.broadcasted_iota(jnp.int32, (t, LANES), 1)
    row = lax.broadcasted_iota(jnp.int32, (t, LANES), 0)

    def qknorm(x, g, scale):
        xf = x.astype(F32)
        ms = _split_dot(xf * xf, seg) * (1.0 / A_DH)
        return xf * lax.rsqrt(ms + EPS) * g * scale

    for b in range(S // t):
        rows = slice(b * t, (b + 1) * t)
        pos = row + b * t
        hi = (pos // A_POS_SPLIT).astype(F32)
        lo = (pos % A_POS_SPLIT).astype(F32)
        q_aug = jnp.where(lane == A_DH, A_POS_SPLIT * slope,
                jnp.where(lane == A_DH + 1, slope,
                jnp.where(lane == A_DH + 2, -A_POS_SPLIT * slope * hi,
                jnp.where(lane == A_DH + 3, -slope * lo, 0.0))))
        k_aug = jnp.where(lane == A_DH, hi,
                jnp.where(lane == A_DH + 1, lo,
                jnp.where(jnp.logical_or(lane == A_DH + 2, lane == A_DH + 3), 1.0, 0.0)))
        qn = qknorm(q_ref[rows, :], qg_ref[...], A_DH ** -0.5)
        kn = qknorm(k_ref[rows, :], kg_ref[...], 1.0)
        for c in range(2):
            qc = qn if c == 0 else pltpu.roll(qn, A_DH, 1)
            kc = kn if c == 0 else pltpu.roll(kn, A_DH, 1)
            qa = jnp.where(lane < A_DH, qc, q_aug).astype(BF16)
            qt_s[c, :, rows] = _dot_tn(qa, eye_t).astype(BF16)
            ka_s[c, rows, :] = jnp.where(lane < A_DH, kc, k_aug).astype(BF16)
        vt_s[:, rows] = _dot_tn(v_ref[rows, :], eye_t).astype(BF16)

    lv = lam_ref[...]
    lam = (jnp.exp(jnp.sum(lv[0:1] * lv[1:2], axis=-1, keepdims=True))
           - jnp.exp(jnp.sum(lv[2:3] * lv[3:4], axis=-1, keepdims=True)) + lambda_init)
    causal = (lax.broadcasted_iota(jnp.int32, (t, t), 0)
              <= lax.broadcasted_iota(jnp.int32, (t, t), 1))

    nt = S // t

    def score_block(i, c, j, st):
        s = jnp.dot(ka_s[c, j * t:(j + 1) * t, :], qt_s[c, :, i * t:(i + 1) * t],
                    preferred_element_type=F32)
        if j == i:
            s = jnp.where(causal, s, -jnp.inf)
        s_s[i % 2, c, j * t:(j + 1) * t, :] = s
        mj = jnp.max(s, axis=0, keepdims=True)
        st["m"][c] = mj if st["m"][c] is None else jnp.maximum(st["m"][c], mj)

    def value_block(i, c, j, st):
        p = jnp.exp(s_s[i % 2, c, j * t:(j + 1) * t, :] - st["m"][c])
        st["l"][c] = st["l"][c] + jnp.sum(p, axis=0, keepdims=True)
        st["acc"][c] = st["acc"][c] + jnp.dot(vt_s[:, j * t:(j + 1) * t], p.astype(BF16),
                                              preferred_element_type=F32)

    def finish(i, st):
        o = st["acc"][0] / st["l"][0] - lam * (st["acc"][1] / st["l"][1])
        ms = jnp.mean(o * o, axis=0, keepdims=True)
        o = o * lax.rsqrt(ms + EPS) * sg_ref[...] * (1.0 - lambda_init)
        o_ref[i * t:(i + 1) * t, :] = _dot_tn(o.astype(BF16), eye_v).astype(o_ref.dtype)

    prev = None
    for i in range(nt + 1):
        cur = None
        first = []
        if i < nt:
            cur = dict(m=[None, None], l=[jnp.zeros((1, t), F32)] * 2,
                       acc=[jnp.zeros((A_DV, t), F32)] * 2)
            first = [(c, j) for j in range(i + 1) for c in range(2)]
        second = [(c, j) for j in range(i) for c in range(2)] if prev is not None else []
        for n in range(max(len(first), len(second))):
            if n < len(first):
                score_block(i, first[n][0], first[n][1], cur)
            if n < len(second):
                value_block(i - 1, second[n][0], second[n][1], prev)
        if prev is not None:
            finish(i - 1, prev)
        prev = cur


def _attention(p_attn, qn_g, kn_g, lam_vecs, subln_g, lambda_init, B, S):
    pa = p_attn.reshape(B, S, A_COLS)
    dup = lambda g: jnp.concatenate([g, g]).reshape(1, LANES)
    slopes = jnp.asarray(
        [[2.0 ** (-8.0 * (i + 1) / A_HEADS)] * LANES for i in range(A_HEADS)], F32
    ).reshape(A_HEADS, 1, LANES)
    nqb = A_QW // LANES
    kern = functools.partial(_attn_kernel, S=S, lambda_init=lambda_init)
    out = pl.pallas_call(
        kern,
        grid=(B, A_HEADS),
        in_specs=[
            pl.BlockSpec((None, S, LANES), lambda b, h: (b, 0, h)),
            pl.BlockSpec((None, S, LANES), lambda b, h: (b, 0, nqb + h)),
            pl.BlockSpec((None, S, LANES), lambda b, h: (b, 0, 2 * nqb + h)),
            pl.BlockSpec((1, LANES), lambda b, h: (0, 0)),
            pl.BlockSpec((1, LANES), lambda b, h: (0, 0)),
            pl.BlockSpec((4, A_DH), lambda b, h: (0, 0)),
            pl.BlockSpec((A_DV, 1), lambda b, h: (0, 0)),
            pl.BlockSpec((None, 1, LANES), lambda b, h: (h, 0, 0)),
        ],
        out_specs=pl.BlockSpec((None, S, A_DV), lambda b, h: (b, 0, h)),
        out_shape=jax.ShapeDtypeStruct((B, S, A_VW), BF16),
        scratch_shapes=[pltpu.VMEM((2, LANES, S), BF16), pltpu.VMEM((2, S, LANES), BF16),
                        pltpu.VMEM((A_DV, S), BF16), pltpu.VMEM((2, 2, S, A_TILE), F32)],
        compiler_params=pltpu.CompilerParams(
            dimension_semantics=("arbitrary", "arbitrary"), vmem_limit_bytes=VMEM_LIMIT),
        name="diff_attn",
    )(pa, pa, pa, dup(qn_g), dup(kn_g), lam_vecs, subln_g.reshape(A_DV, 1), slopes)
    return out.reshape(B * S, A_VW)


R_GROUP = 4
R_GW = R_GROUP * R_N


def _rwkv_kernel(p_ref, mu_ref, wup_ref, w0_ref, aup_ref, a0_ref, gup_ref, kk_ref, ka_ref,
                 rk_ref, lg_ref, lb_ref, o_ref,
                 carry_ref, st_ref, al_s, be_s, ka_s, rh_s, bt_s, kt_s, v_s, gc_s, y_s, *, TB):
    @pl.when(pl.program_id(1) == 0)
    def _():
        carry_ref[...] = jnp.zeros_like(carry_ref)
        st_ref[...] = jnp.zeros_like(st_ref)

    xs = p_ref[...]
    prev = pltpu.roll(xs, 1, 0)
    row = lax.broadcasted_iota(jnp.int32, (TB, 1), 0)
    prev = jnp.where(row == 0, carry_ref[...], prev)
    carry_ref[...] = xs[TB - 1:TB, :]
    xm = xs + (prev - xs) * mu_ref[...]
    r = xm[:, 0:R_W]
    k = xm[:, R_W:2 * R_W]
    v = xm[:, 2 * R_W:3 * R_W]
    wa = xm[:, 3 * R_W:3 * R_W + LANES]
    gd = xm[:, 3 * R_W + LANES:3 * R_W + 2 * LANES]
    wz = w0_ref[...] + _dot(jnp.tanh(wa), wup_ref[...])
    w_log = -_softplus(-wz) - 0.5
    lw = -jnp.exp(w_log)
    a = _sigmoid(a0_ref[...] + _dot(wa, aup_ref[...]))
    g = _dot(_sigmoid(gd), gup_ref[...])
    seg = _seg_ones(R_W, R_N).astype(BF16)
    kk = k * kk_ref[...]
    nrm = jnp.sqrt(_split_dot(kk * kk, seg))
    kk = kk / jnp.maximum(nrm, 1e-12)
    k2 = k * (1.0 + (a - 1.0) * ka_ref[...])
    bonus = _split_dot(r * k2 * rk_ref[...], seg) * v
    bv = kk * a

    rr = lax.broadcasted_iota(jnp.int32, (TB, TB), 0)
    cc = lax.broadcasted_iota(jnp.int32, (TB, TB), 1)
    tril_blk = jnp.logical_and(rr // CHUNK == cc // CHUNK, cc <= rr).astype(F32)
    Lg = _dot_hi(tril_blk, lw)
    inv = jnp.exp(-Lg)
    al_s[...] = (jnp.exp(Lg - lw) * kk).astype(BF16)
    be_s[...] = (bv * inv).astype(BF16)
    ka_s[...] = (k2 * inv).astype(BF16)
    rh_s[...] = (jnp.exp(Lg) * r).astype(BF16)
    v_s[...] = v.astype(BF16)
    for c in range(TB // CHUNK):
        rows = slice(c * CHUNK, (c + 1) * CHUNK)
        gC = Lg[(c + 1) * CHUNK - 1:(c + 1) * CHUNK, :]
        tail = jnp.exp(gC - Lg[rows, :])
        bt_s[rows, :] = (bv[rows, :] * tail).astype(BF16)
        kt_s[rows, :] = (k2[rows, :] * tail).astype(BF16)
        gc_s[c * SUBLANES:(c + 1) * SUBLANES, :] = jnp.broadcast_to(jnp.exp(gC), (SUBLANES, R_W))

    ri = lax.broadcasted_iota(jnp.int32, (R_GW, R_GW), 0)
    ci = lax.broadcasted_iota(jnp.int32, (R_GW, R_GW), 1)
    blk = ri // R_N == ci // R_N
    strict = ci % R_N < ri % R_N
    strict_t = ri % R_N < ci % R_N
    incl = ci % R_N <= ri % R_N
    zero = jnp.zeros((), BF16)

    def expand(x):
        return jnp.where(blk, jnp.concatenate([x] * R_GROUP, axis=0), zero)

    n_groups = R_HEADS // R_GROUP
    pair = 2

    def chunk_pair(ip, _):
        chains = []
        for dc in range(pair):
            c = ip * pair + dc
            rows = pl.ds(pl.multiple_of(c * CHUNK, CHUNK), CHUNK)
            for gi in range(n_groups):
                chains.append(dict(c=c, rows=rows, gi=gi, cols=slice(gi * R_GW, (gi + 1) * R_GW)))
        for ch in chains:
            rows, cols = ch["rows"], ch["cols"]
            ch["A"] = expand(al_s[rows, cols])
            ch["R"] = expand(rh_s[rows, cols])
            ch["B"] = expand(be_s[rows, cols])
            ch["K"] = expand(ka_s[rows, cols])
            vc = v_s[rows, cols]
            ch["V"] = jnp.concatenate(
                [vc[:, h * R_N:(h + 1) * R_N] for h in range(R_GROUP)], axis=0)
        for ch in chains:
            ch["X"] = -jnp.where(strict, _dot_nt(ch["A"], ch["B"]), 0.0)
            ch["m_ak_t"] = jnp.where(strict_t, _dot_nt(ch["K"], ch["A"]), 0.0).astype(BF16)
            ch["m_rb"] = jnp.where(incl, _dot_nt(ch["R"], ch["B"]), 0.0).astype(BF16)
            ch["m_rk"] = jnp.where(incl, _dot_nt(ch["R"], ch["K"]), 0.0).astype(BF16)
        for ch in chains:
            ch["Z"] = ch["A"].astype(F32)
            ch["Wt"] = _dot_tn(ch["V"], ch["m_ak_t"])
        n = 1
        while True:
            last = 2 * n >= CHUNK
            for ch in chains:
                Xb = ch["X"].astype(BF16)
                ch["Z"] = ch["Z"] + jnp.dot(Xb, ch["Z"].astype(BF16), preferred_element_type=F32)
                ch["Wt"] = ch["Wt"] + _dot_nt(ch["Wt"].astype(BF16), Xb)
                if not last:
                    ch["X"] = jnp.dot(Xb, Xb, preferred_element_type=F32)
            n *= 2
            if last:
                break
        for ch in chains:
            rows, cols = ch["rows"], ch["cols"]
            Bt = expand(bt_s[rows, cols])
            Kt = expand(kt_s[rows, cols])
            Zb = ch["Z"].astype(BF16)
            Wtb = ch["Wt"].astype(BF16)
            ch["y_a"] = (ch["R"].astype(F32)
                         - jnp.dot(ch["m_rb"], Zb, preferred_element_type=F32)).astype(BF16)
            ch["y_b"] = (jnp.dot(ch["m_rk"], ch["V"], preferred_element_type=F32)
                         - _dot_nt(ch["m_rb"], Wtb))
            ch["p_neg"] = _dot_tn(Zb, Bt).astype(BF16)
            ch["q"] = _dot_tn(ch["V"], Kt) - jnp.dot(Wtb, Bt, preferred_element_type=F32)
        for ch in chains:
            rows, cols, gi = ch["rows"], ch["cols"], ch["gi"]
            g0 = pl.multiple_of(ch["c"] * SUBLANES, SUBLANES)
            S0 = st_ref[gi]
            S0b = S0.astype(BF16)
            y = _dot_nt(ch["y_a"], S0b) + ch["y_b"]
            st_ref[gi] = (S0 * gc_s[pl.ds(g0, SUBLANES), cols][0:1, :]
                          - jnp.dot(S0b, ch["p_neg"], preferred_element_type=F32) + ch["q"])
            for h in range(R_GROUP):
                hh = gi * R_GROUP + h
                y_s[rows, hh * R_N:(hh + 1) * R_N] = y[h * R_N:(h + 1) * R_N, :]
        return 0

    lax.fori_loop(0, TB // CHUNK // pair, chunk_pair, 0)

    y = y_s[...]
    mean = _split_dot(y, seg) * (1.0 / R_N)
    yc = y - mean
    var = _split_dot(yc * yc, seg) * (1.0 / R_N)
    yn = yc * lax.rsqrt(var + RWKV_GN_EPS) * lg_ref[...] + lb_ref[...]
    o_ref[...] = ((yn + bonus) * g).astype(o_ref.dtype)


def _rwkv(p_rwkv, mu, w_up, w0, a_up, a0, g_up, k_k, k_a, r_k, lnx_g, lnx_b, B, S):
    TB = 256
    T = B * S
    nt = S // TB
    row = lambda t: t.reshape(1, -1)
    zeros = jnp.zeros((R_N, R_W), F32)
    wup_pad = jnp.concatenate([w_up, zeros], axis=0)
    aup_pad = jnp.concatenate([zeros, a_up], axis=0)
    vec = lambda n: pl.BlockSpec((1, n), lambda b, i: (0, 0))
    mat = lambda m, n: pl.BlockSpec((m, n), lambda b, i: (0, 0))
    kern = functools.partial(_rwkv_kernel, TB=TB)
    return pl.pallas_call(
        kern,
        grid=(B, nt),
        in_specs=[
            pl.BlockSpec((TB, R_COLS), lambda b, i: (b * nt + i, 0)),
            vec(R_COLS), mat(LANES, R_W), vec(R_W), mat(LANES, R_W), vec(R_W), mat(LANES, R_W),
            vec(R_W), vec(R_W), vec(R_W), vec(R_W), vec(R_W),
        ],
        out_specs=pl.BlockSpec((TB, R_W), lambda b, i: (b * nt + i, 0)),
        out_shape=jax.ShapeDtypeStruct((T, R_W), BF16),
        scratch_shapes=[
            pltpu.VMEM((1, R_COLS), F32),
            pltpu.VMEM((R_HEADS // R_GROUP, R_N, R_GW), F32),
        ] + [pltpu.VMEM((TB, R_W), BF16)] * 7 + [
            pltpu.VMEM((TB // CHUNK * SUBLANES, R_W), F32),
            pltpu.VMEM((TB, R_W), F32),
        ],
        compiler_params=pltpu.CompilerParams(
            dimension_semantics=("arbitrary", "arbitrary"), vmem_limit_bytes=VMEM_LIMIT),
        name="rwkv7",
    )(p_rwkv, row(mu), wup_pad, row(w0), aup_pad, row(a0), g_up, row(k_k), row(k_a),
      row(r_k), row(lnx_g), row(lnx_b))


def _gla_kernel(p_ref, aup_ref, ab_ref, ng_ref, o_ref, st_ref, *, TB):
    @pl.when(pl.program_id(1) == 0)
    def _():
        st_ref[...] = jnp.zeros_like(st_ref)

    c_gv = 2 * G_KW
    c_ad = c_gv + G_VW
    c_gate = c_ad + LANES
    nchunk = TB // CHUNK
    rr = lax.broadcasted_iota(jnp.int32, (TB, TB), 0)
    cc = lax.broadcasted_iota(jnp.int32, (TB, TB), 1)
    causal = jnp.logical_and(rr // CHUNK == cc // CHUNK, cc <= rr)

    q = p_ref[:, 0:G_KW] * (G_DK ** -0.5)
    k = p_ref[:, G_KW:2 * G_KW]
    vb = p_ref[:, c_gv:c_gv + G_VW].astype(BF16)
    z = _dot_hi(p_ref[:, c_ad:c_ad + LANES], aup_ref[...]) + ab_ref[...]
    la = -_softplus(-z) * (1.0 / G_TAU)
    b = _dot_hi(causal.astype(F32), la)
    qe = (q * jnp.exp(b)).astype(BF16)
    ke = (k * jnp.exp(-b)).astype(BF16)
    kts, e_lasts = [], []
    for c in range(nchunk):
        rows = slice(c * CHUNK, (c + 1) * CHUNK)
        b_last = b[(c + 1) * CHUNK - 1:(c + 1) * CHUNK, :]
        kts.append((k[rows, :] * jnp.exp(b_last - b[rows, :])).astype(BF16))
        e_lasts.append(jnp.exp(b_last))
    heads = range(G_HEADS)
    sls = [slice(h * G_DK, (h + 1) * G_DK) for h in heads]
    vss = [slice(h * G_DV, (h + 1) * G_DV) for h in heads]
    scs = [jnp.where(causal, _dot_nt(qe[:, sls[h]], ke[:, sls[h]]), 0.0) for h in heads]
    o_intra = [_dot(scs[h], vb[:, vss[h]]) for h in heads]
    kvs = [[_dot_tn(vb[c * CHUNK:(c + 1) * CHUNK, vss[h]], kts[c][:, sls[h]]) for h in heads]
           for c in range(nchunk)]
    states = [st_ref[h] for h in heads]
    parts = [[] for _ in heads]
    for c in range(nchunk):
        rows = slice(c * CHUNK, (c + 1) * CHUNK)
        for h in heads:
            parts[h].append(o_intra[h][rows, :] + _dot_nt(qe[rows, sls[h]], states[h].astype(BF16)))
            states[h] = states[h] * e_lasts[c][:, sls[h]] + kvs[c][h]
    for h in heads:
        vs = vss[h]
        st_ref[h] = states[h]
        o = jnp.concatenate(parts[h], axis=0)
        ms = jnp.mean(o * o, axis=-1, keepdims=True)
        gt = p_ref[:, c_gate + h * G_DV:c_gate + (h + 1) * G_DV]
        o = o * lax.rsqrt(ms + EPS) * ng_ref[...] * (gt * _sigmoid(gt))
        o_ref[:, vs] = o.astype(o_ref.dtype)


def _gla(p_gla, alpha_up, alpha_b, norm_g, B, S):
    TB = 256
    T = B * S
    nt = S // TB
    aup_pad = jnp.concatenate([alpha_up, jnp.zeros((LANES - G_LORA, G_KW), F32)], axis=0)
    kern = functools.partial(_gla_kernel, TB=TB)
    return pl.pallas_call(
        kern,
        grid=(B, nt),
        in_specs=[
            pl.BlockSpec((TB, G_COLS_PAD), lambda b, i: (b * nt + i, 0)),
            pl.BlockSpec((LANES, G_KW), lambda b, i: (0, 0)),
            pl.BlockSpec((1, G_KW), lambda b, i: (0, 0)),
            pl.BlockSpec((1, G_DV), lambda b, i: (0, 0)),
        ],
        out_specs=pl.BlockSpec((TB, G_VW), lambda b, i: (b * nt + i, 0)),
        out_shape=jax.ShapeDtypeStruct((T, G_VW), BF16),
        scratch_shapes=[pltpu.VMEM((G_HEADS, G_DV, G_DK), F32)],
        compiler_params=pltpu.CompilerParams(
            dimension_semantics=("arbitrary", "arbitrary"), vmem_limit_bytes=VMEM_LIMIT),
        name="gla",
    )(p_gla, aup_pad, alpha_b.reshape(1, G_KW), norm_g.reshape(1, G_DV))


def _merge_kernel(x_ref, oa_ref, or_ref, og_ref, gate_ref, pa_ref, pr_ref, pg_ref, wo_ref,
                  mod_ref, g2_ref, rw_ref, rb_ref, x1_ref, h2_ref, lg_ref):
    D = D_MODEL
    merged = (_sigmoid(gate_ref[:, 0:D].astype(F32))
              * jnp.dot(oa_ref[...], pa_ref[...], preferred_element_type=F32)
              + _sigmoid(gate_ref[:, D:2 * D].astype(F32))
              * jnp.dot(or_ref[...], pr_ref[...], preferred_element_type=F32)
              + _sigmoid(gate_ref[:, 2 * D:3 * D].astype(F32))
              * jnp.dot(og_ref[...], pg_ref[...], preferred_element_type=F32))
    gt1 = mod_ref[:, 2 * D:3 * D]
    sh2 = mod_ref[:, 3 * D:4 * D]
    sc2 = mod_ref[:, 4 * D:5 * D]
    x1 = x_ref[...] + gt1 * jnp.dot(merged.astype(BF16), wo_ref[...], preferred_element_type=F32)
    x1_ref[...] = x1
    ms = jnp.mean(x1 * x1, axis=-1, keepdims=True)
    h2 = x1 * lax.rsqrt(ms + EPS) * g2_ref[...] * (1.0 + sc2) + sh2
    for s in range(NSUB):
        h2_ref[pl.ds(s, h2.shape[0], stride=NSUB), :] = h2[:, s * LANES:(s + 1) * LANES]
    h_hi = h2.astype(BF16)
    h_lo = (h2 - h_hi.astype(F32)).astype(BF16)
    lg_ref[...] = (jnp.dot(h_hi, rw_ref[0], preferred_element_type=F32)
                   + jnp.dot(h_lo, rw_ref[0], preferred_element_type=F32)
                   + jnp.dot(h_hi, rw_ref[1], preferred_element_type=F32) + rb_ref[...])


def _merge(x2, o_a, o_r, o_g, p_gate, proj_a, proj_r, proj_g, w_out, mod_l, norm2_g,
           router_w, router_b, S, tm):
    T, D = x2.shape
    tiles_per_batch = S // tm
    tile = lambda w: pl.BlockSpec((tm, w), lambda i: (i, 0))
    const = lambda m, n: pl.BlockSpec((m, n), lambda i: (0, 0))
    return pl.pallas_call(
        _merge_kernel,
        grid=(T // tm,),
        in_specs=[
            tile(D), tile(A_VW), tile(R_W), tile(G_VW), tile(GATE_COLS),
            const(A_VW, D), const(R_W, D), const(G_VW, D), const(D, D),
            pl.BlockSpec((None, 1, 6 * D), lambda i: (i // tiles_per_batch, 0, 0)),
            const(1, D), pl.BlockSpec((2, D, LANES), lambda i: (0, 0, 0)), const(1, LANES),
        ],
        out_specs=[
            tile(D),
            pl.BlockSpec((tm * NSUB, LANES), lambda i: (i, 0)),
            tile(LANES),
        ],
        out_shape=[
            jax.ShapeDtypeStruct((T, D), F32),
            jax.ShapeDtypeStruct((T * NSUB, LANES), F32),
            jax.ShapeDtypeStruct((T, LANES), F32),
        ],
        compiler_params=pltpu.CompilerParams(
            dimension_semantics=("arbitrary",), vmem_limit_bytes=VMEM_LIMIT),
        name="merge",
    )(x2, o_a, o_r, o_g, p_gate, proj_a, proj_r, proj_g, w_out, mod_l,
      norm2_g.reshape(1, D), router_w, router_b)


MOE_ROWS = 256
ROUTE_ROWS = 512
E_LANE0 = N_GROUPS


def _route_kernel(lg_ref, info_ref, cnt_ref, carry_ref):
    @pl.when(pl.program_id(0) == 0)
    def _():
        carry_ref[...] = jnp.zeros_like(carry_ref)

    lg = lg_ref[...]
    n = lg.shape[0]
    lane = lax.broadcasted_iota(jnp.int32, (n, LANES), 1).astype(F32)
    neg = -jnp.inf
    big = float(LANES)

    def first_max(vals):
        m = jnp.max(vals, axis=-1, keepdims=True)
        idx = jnp.min(jnp.where(vals == m, lane, big), axis=-1, keepdims=True)
        return m, idx

    in_grp = lane < N_GROUPS
    gm, grp = first_max(jnp.where(in_grp, lg, neg))
    g_prob = 1.0 / jnp.sum(jnp.where(in_grp, jnp.exp(lg - gm), 0.0), axis=-1, keepdims=True)
    lo = E_LANE0 + grp * EXP_PER_GROUP
    el = jnp.where(jnp.logical_and(lane >= lo, lane < lo + EXP_PER_GROUP), lg, neg)
    v1, i1 = first_max(el)
    v2, i2 = first_max(jnp.where(lane == i1, neg, el))
    e21 = jnp.exp(v2 - v1)
    w0 = g_prob / (1.0 + e21)
    w1 = g_prob * e21 / (1.0 + e21)
    oh0 = lane == i1
    oh1 = lane == i2
    oh = jnp.logical_or(oh0, oh1).astype(F32)
    before = _tri(n, True).astype(BF16)
    cnt = jnp.dot(before, oh.astype(BF16), preferred_element_type=F32) + carry_ref[...]
    rank0 = jnp.sum(jnp.where(oh0, cnt, 0.0), axis=-1, keepdims=True)
    rank1 = jnp.sum(jnp.where(oh1, cnt, 0.0), axis=-1, keepdims=True)
    carry = carry_ref[...] + jnp.sum(oh, axis=0, keepdims=True)
    carry_ref[...] = carry
    cnt_ref[...] = carry
    cols = (i1 - E_LANE0, i2 - E_LANE0, rank0, rank1, w0, w1)
    info = jnp.zeros((n, LANES), F32)
    for j, col in enumerate(cols):
        info = jnp.where(lane == j, col, info)
    info_ref[...] = info


def _route(logits):
    T = logits.shape[0]
    tr = min(ROUTE_ROWS, T)
    info, cnt = pl.pallas_call(
        _route_kernel,
        grid=(T // tr,),
        in_specs=[pl.BlockSpec((tr, LANES), lambda i: (i, 0))],
        out_specs=[pl.BlockSpec((tr, LANES), lambda i: (i, 0)),
                   pl.BlockSpec((1, LANES), lambda i: (0, 0))],
        out_shape=[jax.ShapeDtypeStruct((T, LANES), F32), jax.ShapeDtypeStruct((1, LANES), F32)],
        scratch_shapes=[pltpu.VMEM((1, LANES), F32)],
        compiler_params=pltpu.CompilerParams(dimension_semantics=("arbitrary",)),
        name="moe_route",
    )(logits)
    A = T * TOP_K
    counts = cnt[0, E_LANE0:E_LANE0 + N_EXPERTS].astype(jnp.int32)
    padded = (counts + MOE_ROWS - 1) // MOE_ROWS * MOE_ROWS
    pad_end = jnp.cumsum(padded)
    pad_start = pad_end - padded
    n_blocks = -(-A // MOE_ROWS) + N_EXPERTS
    eid = info[:, 0:2].astype(jnp.int32)
    is_e = eid[:, :, None] == jnp.arange(N_EXPERTS, dtype=jnp.int32)
    dest = (jnp.sum(jnp.where(is_e, pad_start, 0), axis=-1)
            + info[:, 2:4].astype(jnp.int32))
    blk_start = jnp.arange(n_blocks, dtype=jnp.int32) * MOE_ROWS
    blk_exp = jnp.minimum(jnp.sum(pad_end[None, :] <= blk_start[:, None], axis=1),
                          N_EXPERTS - 1).astype(jnp.int32)
    n_used = (pad_end[-1:] // MOE_ROWS).astype(jnp.int32)
    return info, dest[:, 0], dest[:, 1], blk_exp, n_used, n_blocks * MOE_ROWS


def _dispatch_kernel(d0_ref, d1_ref, h2_ref, xin_in, xin_hbm, sem):
    del xin_in
    n = h2_ref.shape[0] // NSUB
    base = pl.program_id(0) * n

    def slab(ref, row):
        return ref.at[pl.ds(pl.multiple_of(row * NSUB, NSUB), NSUB)]

    def body(r, _):
        pltpu.make_async_copy(slab(h2_ref, r), slab(xin_hbm, d0_ref[base + r]), sem).start(0)
        pltpu.make_async_copy(slab(h2_ref, r), slab(xin_hbm, d1_ref[base + r]), sem).start(1)
        return 0

    lax.fori_loop(0, n, body, 0, unroll=8)
    for _ in range(TOP_K):
        pltpu.make_async_copy(h2_ref, xin_hbm.at[pl.ds(0, n * NSUB)], sem).wait()


def _dispatch(h2_slab, dest0, dest1, xin_init, td):
    T = h2_slab.shape[0] // NSUB
    grid_spec = pltpu.PrefetchScalarGridSpec(
        num_scalar_prefetch=2,
        grid=(T // td,),
        in_specs=[pl.BlockSpec((td * NSUB, LANES), lambda i, d0, d1: (i, 0)),
                  pl.BlockSpec(memory_space=pl.ANY)],
        out_specs=pl.BlockSpec(memory_space=pl.ANY),
        scratch_shapes=[pltpu.SemaphoreType.DMA(())],
    )
    return pl.pallas_call(
        _dispatch_kernel,
        grid_spec=grid_spec,
        out_shape=jax.ShapeDtypeStruct(xin_init.shape, F32),
        input_output_aliases={3: 0},
        compiler_params=pltpu.CompilerParams(dimension_semantics=("arbitrary",)),
        name="moe_dispatch",
    )(dest0, dest1, h2_slab, xin_init)


def _moe_kernel(be_ref, nu_ref, x_ref, wg_ref, wu_ref, wd_ref, y_ref, wgb, wub, wdb):
    i = pl.program_id(0)

    @pl.when(i < nu_ref[0])
    def _():
        changed = jnp.logical_or(i == 0, be_ref[i] != be_ref[jnp.maximum(i - 1, 0)])

        @pl.when(changed)
        def _():
            wgb[...] = wg_ref[...].astype(BF16)
            wub[...] = wu_ref[...].astype(BF16)
            wdb[...] = wd_ref[...].astype(BF16)

        halves = range(2)
        hr = MOE_ROWS // 2
        xs = [jnp.concatenate(
            [x_ref[pl.ds(p * hr * NSUB + s, hr, stride=NSUB), :].astype(BF16)
             for s in range(NSUB)], axis=-1) for p in halves]
        hgs = [jnp.dot(xs[p], wgb[...], preferred_element_type=F32) for p in halves]
        hus = [jnp.dot(xs[p], wub[...], preferred_element_type=F32) for p in halves]
        hids = [(hgs[p] * _sigmoid(hgs[p]) * hus[p]).astype(BF16) for p in halves]
        ys = [jnp.dot(hids[p], wdb[...], preferred_element_type=F32) for p in halves]
        for p in halves:
            for s in range(NSUB):
                y_ref[pl.ds(p * hr * NSUB + s, hr, stride=NSUB), :] = (
                    ys[p][:, s * LANES:(s + 1) * LANES])

    @pl.when(i >= nu_ref[0])
    def _():
        y_ref[...] = jnp.zeros_like(y_ref)


def _moe(xin, blk_exp, n_used, w_gate, w_up, w_down, layer):
    blk_rows = MOE_ROWS * NSUB
    n_blocks = xin.shape[0] // blk_rows
    wspec = lambda m, n: pl.BlockSpec((None, None, m, n), lambda i, be, nu: (layer, be[i], 0, 0))
    last = lambda i, nu: jnp.minimum(i, nu[0] - 1)
    grid_spec = pltpu.PrefetchScalarGridSpec(
        num_scalar_prefetch=2,
        grid=(n_blocks,),
        in_specs=[
            pl.BlockSpec((blk_rows, LANES), lambda i, be, nu: (last(i, nu), 0)),
            wspec(D_MODEL, D_EXPERT), wspec(D_MODEL, D_EXPERT), wspec(D_EXPERT, D_MODEL),
        ],
        out_specs=pl.BlockSpec((blk_rows, LANES), lambda i, be, nu: (i, 0)),
        scratch_shapes=[
            pltpu.VMEM((D_MODEL, D_EXPERT), BF16),
            pltpu.VMEM((D_MODEL, D_EXPERT), BF16),
            pltpu.VMEM((D_EXPERT, D_MODEL), BF16),
        ],
    )
    return pl.pallas_call(
        _moe_kernel,
        grid_spec=grid_spec,
        out_shape=jax.ShapeDtypeStruct(xin.shape, F32),
        compiler_params=pltpu.CompilerParams(
            dimension_semantics=("arbitrary",), vmem_limit_bytes=VMEM_LIMIT),
        name="moe_ffn",
    )(blk_exp, n_used, xin, w_gate, w_up, w_down)


def _combine_kernel(d0_ref, d1_ref, y_hbm, x1_ref, info_ref, mod_ref, o_ref, ybuf, sem):
    i = pl.program_id(0)
    nsteps = pl.num_programs(0)
    n = x1_ref.shape[0]
    slot = i % 2
    D = D_MODEL

    def slab(ref, row):
        return ref.at[pl.ds(pl.multiple_of(row * NSUB, NSUB), NSUB)]

    def start_gather(step, sl):
        base = step * n

        def body(r, _):
            pltpu.make_async_copy(
                slab(y_hbm, d0_ref[base + r]), slab(ybuf.at[sl, 0], r), sem.at[sl]).start(0)
            pltpu.make_async_copy(
                slab(y_hbm, d1_ref[base + r]), slab(ybuf.at[sl, 1], r), sem.at[sl]).start(1)
            return 0
        lax.fori_loop(0, n, body, 0, unroll=8)

    @pl.when(i == 0)
    def _():
        start_gather(0, 0)

    @pl.when(i + 1 < nsteps)
    def _():
        start_gather(i + 1, 1 - slot)

    for k in range(TOP_K):
        pltpu.make_async_copy(
            y_hbm.at[pl.ds(0, n * NSUB)], ybuf.at[slot, k], sem.at[slot]).wait()

    w0 = info_ref[:, 4:5]
    w1 = info_ref[:, 5:6]
    for s in range(NSUB):
        cols = slice(s * LANES, (s + 1) * LANES)
        gt2 = mod_ref[:, 5 * D + s * LANES:5 * D + (s + 1) * LANES]
        piece = pl.ds(s, n, stride=NSUB)
        moe = w0 * ybuf[slot, 0, piece, :] + w1 * ybuf[slot, 1, piece, :]
        o_ref[:, cols] = x1_ref[:, cols] + gt2 * moe


def _combine(x1, y, info, dest0, dest1, mod_l, S, tm):
    T, D = x1.shape
    tiles_per_batch = S // tm
    grid_spec = pltpu.PrefetchScalarGridSpec(
        num_scalar_prefetch=2,
        grid=(T // tm,),
        in_specs=[
            pl.BlockSpec(memory_space=pl.ANY),
            pl.BlockSpec((tm, D), lambda i, d0, d1: (i, 0)),
            pl.BlockSpec((tm, LANES), lambda i, d0, d1: (i, 0)),
            pl.BlockSpec((None, 1, 6 * D), lambda i, d0, d1: (i // tiles_per_batch, 0, 0)),
        ],
        out_specs=pl.BlockSpec((tm, D), lambda i, d0, d1: (i, 0)),
        scratch_shapes=[pltpu.VMEM((2, TOP_K, tm * NSUB, LANES), F32),
                        pltpu.SemaphoreType.DMA((2,))],
    )
    return pl.pallas_call(
        _combine_kernel,
        grid_spec=grid_spec,
        out_shape=jax.ShapeDtypeStruct((T, D), F32),
        compiler_params=pltpu.CompilerParams(
            dimension_semantics=("arbitrary",), vmem_limit_bytes=VMEM_LIMIT),
        name="moe_combine",
    )(dest0, dest1, y, x1, info, mod_l)


def _pad_w_in(w):
    D = w.shape[0]
    c0 = A_COLS + R_COLS
    c_ad = c0 + 2 * G_KW + G_VW
    c_gg = c_ad + G_LORA
    pad = jnp.zeros((D, LANES - G_LORA), w.dtype)
    return jnp.concatenate([w[:, :c_gg], pad, w[:, c_gg:]], axis=1).astype(BF16)


def kernel(x, c, ada_w, ada_b, norm1_g, norm2_g, w_in, attn_qn_g, attn_kn_g, attn_lambda,
           attn_subln_g, rwkv_mu, rwkv_w_up, rwkv_w0, rwkv_a_up, rwkv_a0, rwkv_g_up, rwkv_k_k,
           rwkv_k_a, rwkv_r_k, rwkv_lnx_g, rwkv_lnx_b, gla_alpha_up, gla_alpha_b, gla_norm_g,
           proj_attn, proj_rwkv, proj_gla, w_out, router_grp_w, router_grp_b, router_exp_w,
           router_exp_b, exp_w_gate, exp_w_up, exp_w_down):
    B, S, D = x.shape
    T = B * S
    L = ada_w.shape[0]
    tm = 256
    mod = _adaln(c, ada_w, ada_b).reshape(L, B, 1, 6 * D)
    x2 = x.reshape(T, D)
    xin = None
    for l in range(L):
        lambda_init = 0.8 - 0.6 * math.exp(-0.3 * l)
        p_attn, p_rwkv, p_gla, p_gate = _inproj(x2, mod[l], norm1_g[l], _pad_w_in(w_in[l]), S, tm)
        o_a = _attention(p_attn, attn_qn_g[l], attn_kn_g[l], attn_lambda[l], attn_subln_g[l],
                         lambda_init, B, S)
        o_r = _rwkv(p_rwkv, rwkv_mu[l], rwkv_w_up[l], rwkv_w0[l], rwkv_a_up[l], rwkv_a0[l],
                    rwkv_g_up[l], rwkv_k_k[l], rwkv_k_a[l], rwkv_r_k[l], rwkv_lnx_g[l],
                    rwkv_lnx_b[l], B, S)
        o_g = _gla(p_gla, gla_alpha_up[l], gla_alpha_b[l], gla_norm_g[l], B, S)
        n_r = N_GROUPS + N_EXPERTS
        router_w = jnp.concatenate(
            [router_grp_w[l], router_exp_w[l], jnp.zeros((D, LANES - n_r), F32)], axis=1)
        rw_hi = router_w.astype(BF16)
        router_w = jnp.stack([rw_hi, (router_w - rw_hi.astype(F32)).astype(BF16)])
        router_b = jnp.concatenate(
            [router_grp_b[l], router_exp_b[l], jnp.zeros((LANES - n_r,), F32)]).reshape(1, LANES)
        x1, h2, logits = _merge(
            x2, o_a, o_r, o_g, p_gate, proj_attn[l].astype(BF16), proj_rwkv[l].astype(BF16),
            proj_gla[l].astype(BF16), w_out[l].astype(BF16), mod[l], norm2_g[l],
            router_w, router_b, S, tm)
        info, dest0, dest1, blk_exp, n_used, n_rows = _route(logits)
        if xin is None:
            xin = jnp.zeros((n_rows * NSUB, LANES), F32)
        xin = _dispatch(h2, dest0, dest1, xin, tm)
        y = _moe(xin, blk_exp, n_used, exp_w_gate, exp_w_up, exp_w_down, l)
        x2 = _combine(x1, y, info, dest0, dest1, mod[l], S, tm)
    return x2.reshape(B, S, D)
```

```python
import functools
import math

import jax
import jax.numpy as jnp
from jax import lax
from jax.experimental import pallas as pl
from jax.experimental.pallas import tpu as pltpu

F32 = jnp.float32
BF16 = jnp.bfloat16
HIGHEST = lax.Precision.HIGHEST

D_MODEL = 1024
A_HEADS, A_DH, A_DV = 4, 64, 128
A_QW, A_VW = 512, 512
A_COLS = 1536
R_HEADS, R_N, R_W = 8, 64, 512
R_COLS = 1792
RWKV_GN_EPS = 64e-5
G_HEADS, G_DK, G_DV = 4, 64, 128
G_KW, G_VW, G_LORA = 256, 512, 16
G_TAU = 16.0
G_COLS = 1552
G_COLS_PAD = 1664
GATE_COLS = 3072
N_GROUPS, EXP_PER_GROUP, N_EXPERTS, TOP_K = 4, 8, 32, 2
D_EXPERT = 512
EPS = 1e-6

LANES = 128
SUBLANES = 8
NSUB = D_MODEL // LANES
CHUNK = 64
VMEM_LIMIT = 56 * 1024 * 1024


def _dot(a, b):
    return jnp.dot(a.astype(BF16), b.astype(BF16), preferred_element_type=F32)


def _dot_hi(a, b):
    return jnp.dot(a, b, precision=HIGHEST, preferred_element_type=F32)


def _dot_nt(a, b, precision=None):
    return lax.dot_general(a, b, (((1,), (1,)), ((), ())), precision=precision,
                           preferred_element_type=F32)


def _dot_tn(a, b, precision=None):
    return lax.dot_general(a, b, (((0,), (0,)), ((), ())), precision=precision,
                           preferred_element_type=F32)


def _cumsum3(tri, x):
    x1 = x.astype(BF16)
    res = x - x1.astype(F32)
    x2 = res.astype(BF16)
    x3 = (res - x2.astype(F32)).astype(BF16)
    return (jnp.dot(tri, x1, preferred_element_type=F32)
            + jnp.dot(tri, x2, preferred_element_type=F32)
            + jnp.dot(tri, x3, preferred_element_type=F32))


def _sigmoid(x):
    return 1.0 / (1.0 + jnp.exp(-x))


def _softplus(x):
    return jnp.maximum(x, 0.0) + jnp.log(1.0 + jnp.exp(-jnp.abs(x)))


def _seg_ones(n, seg):
    r = lax.broadcasted_iota(jnp.int32, (n, n), 0) // seg
    c = lax.broadcasted_iota(jnp.int32, (n, n), 1) // seg
    return (r == c).astype(F32)


def _tri(n, strict):
    r = lax.broadcasted_iota(jnp.int32, (n, n), 0)
    c = lax.broadcasted_iota(jnp.int32, (n, n), 1)
    return (c < r) if strict else (c <= r)


def _adaln_kernel(c_ref, w_ref, b_ref, o_ref):
    c = c_ref[...]
    c_act = c * _sigmoid(c)
    o_ref[...] = _dot_hi(c_act, w_ref[...]) + b_ref[...]


def _adaln(c, ada_w, ada_b):
    L, D, N = ada_w.shape
    B = c.shape[0]
    tn = D
    return pl.pallas_call(
        _adaln_kernel,
        grid=(L, N // tn),
        in_specs=[
            pl.BlockSpec((B, D), lambda l, j: (0, 0)),
            pl.BlockSpec((None, D, tn), lambda l, j: (l, 0, j)),
            pl.BlockSpec((None, 1, tn), lambda l, j: (l, 0, j)),
        ],
        out_specs=pl.BlockSpec((None, B, tn), lambda l, j: (l, 0, j)),
        out_shape=jax.ShapeDtypeStruct((L, B, N), F32),
        name="adaln",
    )(c, ada_w, ada_b.reshape(L, 1, N))


_IN_SEGS = (A_COLS, R_COLS, G_COLS_PAD, GATE_COLS)
_IN_DTYPES = (BF16, F32, F32, BF16)
_IN_CHUNK = 512


def _inproj_kernel(x_ref, mod_ref, g_ref, w_ref, *o_refs):
    x = x_ref[...]
    D = x.shape[-1]
    ms = jnp.mean(x * x, axis=-1, keepdims=True)
    y = x * lax.rsqrt(ms + EPS) * g_ref[...]
    sh = mod_ref[:, 0:D]
    sc = mod_ref[:, D:2 * D]
    h = (y * (1.0 + sc) + sh).astype(BF16)
    base = 0
    for o_ref, width in zip(o_refs, _IN_SEGS):
        for c0 in range(0, width, _IN_CHUNK):
            c1 = min(c0 + _IN_CHUNK, width)
            o_ref[:, c0:c1] = jnp.dot(
                h, w_ref[:, base + c0:base + c1], preferred_element_type=F32
            ).astype(o_ref.dtype)
        base += width


def _inproj(x2, mod_l, norm_g, w_pad, S, tm):
    T, D = x2.shape
    NP = w_pad.shape[1]
    tiles_per_batch = S // tm
    return pl.pallas_call(
        _inproj_kernel,
        grid=(T // tm,),
        in_specs=[
            pl.BlockSpec((tm, D), lambda i: (i, 0)),
            pl.BlockSpec((None, 1, 2 * D), lambda i: (i // tiles_per_batch, 0, 0)),
            pl.BlockSpec((1, D), lambda i: (0, 0)),
            pl.BlockSpec((D, NP), lambda i: (0, 0), pipeline_mode=pl.Buffered(1)),
        ],
        out_specs=[pl.BlockSpec((tm, w), lambda i: (i, 0)) for w in _IN_SEGS],
        out_shape=[jax.ShapeDtypeStruct((T, w), dt) for w, dt in zip(_IN_SEGS, _IN_DTYPES)],
        compiler_params=pltpu.CompilerParams(
            dimension_semantics=("arbitrary",), vmem_limit_bytes=VMEM_LIMIT),
        name="inproj",
    )(x2, mod_l, norm_g.reshape(1, D), w_pad)


A_TILE = 256
A_POS_SPLIT = 64
LOG2E = math.log2(math.e)


def _split_dot(x, ones):
    hi = x.astype(BF16)
    lo = (x - hi.astype(F32)).astype(BF16)
    return (jnp.dot(hi, ones, preferred_element_type=F32)
            + jnp.dot(lo, ones, preferred_element_type=F32))


def _eye(n):
    r = lax.broadcasted_iota(jnp.int32, (n, n), 0)
    c = lax.broadcasted_iota(jnp.int32, (n, n), 1)
    return (r == c).astype(BF16)


def _attn_kernel(q_ref, k_ref, v_ref, qg_ref, kg_ref, lam_ref, sg_ref, slope_ref, o_ref,
                 qt_s, ka_s, vt_s, s_s, *, S, lambda_init):
    t = A_TILE
    seg = _seg_ones(LANES, A_DH).astype(BF16)
    eye_t = _eye(t)
    eye_v = _eye(A_DV)
    slope = slope_ref[...]
    lane = lax.broadcasted_iota(jnp.int32, (t, LANES), 1)
    row = lax.broadcasted_iota(jnp.int32, (t, LANES), 0)

    def qknorm(x, g, scale):
        xf = x.astype(F32)
        ms = _split_dot(xf * xf, seg) * (1.0 / A_DH)
        return xf * lax.rsqrt(ms + EPS) * g * scale

    for b in range(S // t):
        rows = slice(b * t, (b + 1) * t)
        pos = row + b * t
        hi = (pos // A_POS_SPLIT).astype(F32)
        lo = (pos % A_POS_SPLIT).astype(F32)
        kl = lane - A_DH
        term = kl // 2
        s2 = slope * LOG2E
        qv = jnp.where(term == 0, A_POS_SPLIT * s2,
             jnp.where(term == 1, s2,
             jnp.where(term == 2, -A_POS_SPLIT * s2 * hi, -s2 * lo)))
        qv_hi = qv.astype(BF16).astype(F32)
        q_aug = jnp.where(kl < 8, jnp.where(kl % 2 == 0, qv_hi, qv - qv_hi), 0.0)
        k_aug = jnp.where(kl >= 8, 0.0,
                jnp.where(term == 0, hi, jnp.where(term == 1, lo, 1.0)))
        qn = qknorm(q_ref[rows, :], qg_ref[...], A_DH ** -0.5 * LOG2E)
        kn = qknorm(k_ref[rows, :], kg_ref[...], 1.0)
        for c in range(2):
            qc = qn if c == 0 else pltpu.roll(qn, A_DH, 1)
            kc = kn if c == 0 else pltpu.roll(kn, A_DH, 1)
            qa = jnp.where(lane < A_DH, qc, q_aug).astype(BF16)
            qt_s[c, :, rows] = _dot_tn(qa, eye_t).astype(BF16)
            ka_s[c, rows, :] = jnp.where(lane < A_DH, kc, k_aug).astype(BF16)
        vt_s[:, rows] = _dot_tn(v_ref[rows, :], eye_t).astype(BF16)

    lv = lam_ref[...]
    lam = (jnp.exp(jnp.sum(lv[0:1] * lv[1:2], axis=-1, keepdims=True))
           - jnp.exp(jnp.sum(lv[2:3] * lv[3:4], axis=-1, keepdims=True)) + lambda_init)
    causal = (lax.broadcasted_iota(jnp.int32, (t, t), 0)
              <= lax.broadcasted_iota(jnp.int32, (t, t), 1))

    nt = S // t

    def score_block(i, c, j, st):
        s = jnp.dot(ka_s[c, j * t:(j + 1) * t, :], qt_s[c, :, i * t:(i + 1) * t],
                    preferred_element_type=F32)
        if j == i:
            s = jnp.where(causal, s, -jnp.inf)
        s_s[i % 2, c, j * t:(j + 1) * t, :] = s
        mj = jnp.max(s, axis=0, keepdims=True)
        st["m"][c] = mj if st["m"][c] is None else jnp.maximum(st["m"][c], mj)

    def value_block(i, c, j, st):
        p = jnp.exp2(s_s[i % 2, c, j * t:(j + 1) * t, :] - st["m"][c])
        st["l"][c] = st["l"][c] + jnp.sum(p, axis=0, keepdims=True)
        st["acc"][c] = st["acc"][c] + jnp.dot(vt_s[:, j * t:(j + 1) * t], p.astype(BF16),
                                              preferred_element_type=F32)

    def finish(i, st):
        o = st["acc"][0] / st["l"][0] - lam * (st["acc"][1] / st["l"][1])
        ms = jnp.mean(o * o, axis=0, keepdims=True)
        o = o * lax.rsqrt(ms + EPS) * sg_ref[...] * (1.0 - lambda_init)
        o_ref[i * t:(i + 1) * t, :] = _dot_tn(o.astype(BF16), eye_v).astype(o_ref.dtype)

    prev = None
    for i in range(nt + 1):
        cur = None
        first = []
        if i < nt:
            cur = dict(m=[None, None], l=[jnp.zeros((1, t), F32)] * 2,
                       acc=[jnp.zeros((A_DV, t), F32)] * 2)
            first = [(c, j) for j in range(i + 1) for c in range(2)]
        second = [(c, j) for j in range(i) for c in range(2)] if prev is not None else []
        for n in range(max(len(first), len(second))):
            if n < len(first):
                score_block(i, first[n][0], first[n][1], cur)
            if n < len(second):
                value_block(i - 1, second[n][0], second[n][1], prev)
        if prev is not None:
            finish(i - 1, prev)
        prev = cur


def _attention(p_attn, qn_g, kn_g, lam_vecs, subln_g, lambda_init, B, S):
    pa = p_attn.reshape(B, S, A_COLS)
    dup = lambda g: jnp.concatenate([g, g]).reshape(1, LANES)
    slopes = jnp.asarray(
        [[2.0 ** (-8.0 * (i + 1) / A_HEADS)] * LANES for i in range(A_HEADS)], F32
    ).reshape(A_HEADS, 1, LANES)
    nqb = A_QW // LANES
    kern = functools.partial(_attn_kernel, S=S, lambda_init=lambda_init)
    out = pl.pallas_call(
        kern,
        grid=(B, A_HEADS),
        in_specs=[
            pl.BlockSpec((None, S, LANES), lambda b, h: (b, 0, h)),
            pl.BlockSpec((None, S, LANES), lambda b, h: (b, 0, nqb + h)),
            pl.BlockSpec((None, S, LANES), lambda b, h: (b, 0, 2 * nqb + h)),
            pl.BlockSpec((1, LANES), lambda b, h: (0, 0)),
            pl.BlockSpec((1, LANES), lambda b, h: (0, 0)),
            pl.BlockSpec((4, A_DH), lambda b, h: (0, 0)),
            pl.BlockSpec((A_DV, 1), lambda b, h: (0, 0)),
            pl.BlockSpec((None, 1, LANES), lambda b, h: (h, 0, 0)),
        ],
        out_specs=pl.BlockSpec((None, S, A_DV), lambda b, h: (b, 0, h)),
        out_shape=jax.ShapeDtypeStruct((B, S, A_VW), BF16),
        scratch_shapes=[pltpu.VMEM((2, LANES, S), BF16), pltpu.VMEM((2, S, LANES), BF16),
                        pltpu.VMEM((A_DV, S), BF16), pltpu.VMEM((2, 2, S, A_TILE), F32)],
        compiler_params=pltpu.CompilerParams(
            dimension_semantics=("arbitrary", "arbitrary"), vmem_limit_bytes=VMEM_LIMIT),
        name="diff_attn",
    )(pa, pa, pa, dup(qn_g), dup(kn_g), lam_vecs, subln_g.reshape(A_DV, 1), slopes)
    return out.reshape(B * S, A_VW)


R_GROUP = 4
R_GW = R_GROUP * R_N


def _rwkv_kernel(p_ref, mu_ref, wup_ref, w0_ref, aup_ref, a0_ref, gup_ref, kk_ref, ka_ref,
                 rk_ref, lg_ref, lb_ref, o_ref,
                 carry_ref, st_ref, al_s, be_s, ka_s, rh_s, bt_s, kt_s, v_s, gc_s, y_s, *, TB):
    @pl.when(pl.program_id(1) == 0)
    def _():
        carry_ref[...] = jnp.zeros_like(carry_ref)
        st_ref[...] = jnp.zeros_like(st_ref)

    xs = p_ref[...]
    prev = pltpu.roll(xs, 1, 0)
    row = lax.broadcasted_iota(jnp.int32, (TB, 1), 0)
    prev = jnp.where(row == 0, carry_ref[...], prev)
    carry_ref[...] = xs[TB - 1:TB, :]
    xm = xs + (prev - xs) * mu_ref[...]
    r = xm[:, 0:R_W]
    k = xm[:, R_W:2 * R_W]
    v = xm[:, 2 * R_W:3 * R_W]
    wa = xm[:, 3 * R_W:3 * R_W + LANES]
    gd = xm[:, 3 * R_W + LANES:3 * R_W + 2 * LANES]
    wz = w0_ref[...] + _dot(jnp.tanh(wa), wup_ref[...])
    lw = -math.exp(-0.5) * _sigmoid(wz)
    a = _sigmoid(a0_ref[...] + _dot(wa, aup_ref[...]))
    g = _dot(_sigmoid(gd), gup_ref[...])
    seg = _seg_ones(R_GW, R_N).astype(BF16)

    def head_sum(x, passes):
        f = _split_dot if passes == 2 else (
            lambda t, o: jnp.dot(t.astype(BF16), o, preferred_element_type=F32))
        return jnp.concatenate(
            [f(x[:, j * R_GW:(j + 1) * R_GW], seg) for j in range(R_W // R_GW)], axis=-1)

    kk = k * kk_ref[...]
    kk = kk * jnp.minimum(lax.rsqrt(head_sum(kk * kk, 1)), 1e12)
    k2 = k * (1.0 + (a - 1.0) * ka_ref[...])
    bonus = head_sum(r * k2 * rk_ref[...], 2) * v
    bv = kk * a

    rr = lax.broadcasted_iota(jnp.int32, (TB, TB), 0)
    cc = lax.broadcasted_iota(jnp.int32, (TB, TB), 1)
    tril_blk = jnp.logical_and(rr // CHUNK == cc // CHUNK, cc <= rr).astype(BF16)
    Lg = _cumsum3(tril_blk, lw)
    inv = jnp.exp(-Lg)
    al_s[...] = (jnp.exp(Lg - lw) * kk).astype(BF16)
    be_s[...] = (bv * inv).astype(BF16)
    ka_s[...] = (k2 * inv).astype(BF16)
    rh_s[...] = (jnp.exp(Lg) * r).astype(BF16)
    v_s[...] = v.astype(BF16)
    for c in range(TB // CHUNK):
        rows = slice(c * CHUNK, (c + 1) * CHUNK)
        gC = Lg[(c + 1) * CHUNK - 1:(c + 1) * CHUNK, :]
        tail = jnp.exp(gC - Lg[rows, :])
        bt_s[rows, :] = (bv[rows, :] * tail).astype(BF16)
        kt_s[rows, :] = (k2[rows, :] * tail).astype(BF16)
        gc_s[c * SUBLANES:(c + 1) * SUBLANES, :] = jnp.broadcast_to(jnp.exp(gC), (SUBLANES, R_W))

    ri = lax.broadcasted_iota(jnp.int32, (R_GW, R_GW), 0)
    ci = lax.broadcasted_iota(jnp.int32, (R_GW, R_GW), 1)
    blk = ri // R_N == ci // R_N
    strict = ci % R_N < ri % R_N
    strict_t = ri % R_N < ci % R_N
    incl = ci % R_N <= ri % R_N
    zero = jnp.zeros((), BF16)

    def expand(x):
        return jnp.where(blk, jnp.concatenate([x] * R_GROUP, axis=0), zero)

    n_groups = R_HEADS // R_GROUP
    pair = 2

    def chunk_pair(ip, _):
        chains = []
        for dc in range(pair):
            c = ip * pair + dc
            rows = pl.ds(pl.multiple_of(c * CHUNK, CHUNK), CHUNK)
            for gi in range(n_groups):
                chains.append(dict(c=c, rows=rows, gi=gi, cols=slice(gi * R_GW, (gi + 1) * R_GW)))
        for ch in chains:
            rows, cols = ch["rows"], ch["cols"]
            ch["A"] = expand(al_s[rows, cols])
            ch["R"] = expand(rh_s[rows, cols])
            ch["B"] = expand(be_s[rows, cols])
            ch["K"] = expand(ka_s[rows, cols])
            vc = v_s[rows, cols]
            ch["V"] = jnp.concatenate(
                [vc[:, h * R_N:(h + 1) * R_N] for h in range(R_GROUP)], axis=0)
        for ch in chains:
            ch["X"] = -jnp.where(strict, _dot_nt(ch["A"], ch["B"]), 0.0)
            ch["m_ak_t"] = jnp.where(strict_t, _dot_nt(ch["K"], ch["A"]), 0.0).astype(BF16)
            ch["m_rb"] = jnp.where(incl, _dot_nt(ch["R"], ch["B"]), 0.0).astype(BF16)
            ch["m_rk"] = jnp.where(incl, _dot_nt(ch["R"], ch["K"]), 0.0).astype(BF16)
        for ch in chains:
            ch["Z"] = ch["A"].astype(F32)
            ch["Wt"] = _dot_tn(ch["V"], ch["m_ak_t"])
        n = 1
        while True:
            last = 2 * n >= CHUNK
            for ch in chains:
                Xb = ch["X"].astype(BF16)
                ch["Z"] = ch["Z"] + jnp.dot(Xb, ch["Z"].astype(BF16), preferred_element_type=F32)
                ch["Wt"] = ch["Wt"] + _dot_nt(ch["Wt"].astype(BF16), Xb)
                if not last:
                    ch["X"] = jnp.dot(Xb, Xb, preferred_element_type=F32)
            n *= 2
            if last:
                break
        for ch in chains:
            rows, cols = ch["rows"], ch["cols"]
            Bt = expand(bt_s[rows, cols])
            Kt = expand(kt_s[rows, cols])
            Zb = ch["Z"].astype(BF16)
            Wtb = ch["Wt"].astype(BF16)
            ch["y_a"] = (ch["R"].astype(F32)
                         - jnp.dot(ch["m_rb"], Zb, preferred_element_type=F32)).astype(BF16)
            ch["y_b"] = (jnp.dot(ch["m_rk"], ch["V"], preferred_element_type=F32)
                         - _dot_nt(ch["m_rb"], Wtb))
            ch["p_neg"] = _dot_tn(Zb, Bt).astype(BF16)
            ch["q"] = _dot_tn(ch["V"], Kt) - jnp.dot(Wtb, Bt, preferred_element_type=F32)
        for ch in chains:
            rows, cols, gi = ch["rows"], ch["cols"], ch["gi"]
            g0 = pl.multiple_of(ch["c"] * SUBLANES, SUBLANES)
            S0 = st_ref[gi]
            S0b = S0.astype(BF16)
            y = _dot_nt(ch["y_a"], S0b) + ch["y_b"]
            st_ref[gi] = (S0 * gc_s[pl.ds(g0, SUBLANES), cols][0:1, :]
                          - jnp.dot(S0b, ch["p_neg"], preferred_element_type=F32) + ch["q"])
            for h in range(R_GROUP):
                hh = gi * R_GROUP + h
                y_s[rows, hh * R_N:(hh + 1) * R_N] = y[h * R_N:(h + 1) * R_N, :]
        return 0

    lax.fori_loop(0, TB // CHUNK // pair, chunk_pair, 0)

    y = y_s[...]
    mean = head_sum(y, 2) * (1.0 / R_N)
    yc = y - mean
    var = head_sum(yc * yc, 1) * (1.0 / R_N)
    yn = yc * lax.rsqrt(var + RWKV_GN_EPS) * lg_ref[...] + lb_ref[...]
    o_ref[...] = ((yn + bonus) * g).astype(o_ref.dtype)


def _rwkv(p_rwkv, mu, w_up, w0, a_up, a0, g_up, k_k, k_a, r_k, lnx_g, lnx_b, B, S):
    TB = 256
    T = B * S
    nt = S // TB
    row = lambda t: t.reshape(1, -1)
    zeros = jnp.zeros((R_N, R_W), F32)
    wup_pad = jnp.concatenate([w_up, zeros], axis=0)
    aup_pad = jnp.concatenate([zeros, a_up], axis=0)
    vec = lambda n: pl.BlockSpec((1, n), lambda b, i: (0, 0))
    mat = lambda m, n: pl.BlockSpec((m, n), lambda b, i: (0, 0))
    kern = functools.partial(_rwkv_kernel, TB=TB)
    return pl.pallas_call(
        kern,
        grid=(B, nt),
        in_specs=[
            pl.BlockSpec((TB, R_COLS), lambda b, i: (b * nt + i, 0)),
            vec(R_COLS), mat(LANES, R_W), vec(R_W), mat(LANES, R_W), vec(R_W), mat(LANES, R_W),
            vec(R_W), vec(R_W), vec(R_W), vec(R_W), vec(R_W),
        ],
        out_specs=pl.BlockSpec((TB, R_W), lambda b, i: (b * nt + i, 0)),
        out_shape=jax.ShapeDtypeStruct((T, R_W), BF16),
        scratch_shapes=[
            pltpu.VMEM((1, R_COLS), F32),
            pltpu.VMEM((R_HEADS // R_GROUP, R_N, R_GW), F32),
        ] + [pltpu.VMEM((TB, R_W), BF16)] * 7 + [
            pltpu.VMEM((TB // CHUNK * SUBLANES, R_W), F32),
            pltpu.VMEM((TB, R_W), F32),
        ],
        compiler_params=pltpu.CompilerParams(
            dimension_semantics=("arbitrary", "arbitrary"), vmem_limit_bytes=VMEM_LIMIT),
        name="rwkv7",
    )(p_rwkv, row(mu), wup_pad, row(w0), aup_pad, row(a0), g_up, row(k_k), row(k_a),
      row(r_k), row(lnx_g), row(lnx_b))


def _gla_kernel(p_ref, aup_ref, ab_ref, ng_ref, o_ref, st_ref, *, TB):
    @pl.when(pl.program_id(1) == 0)
    def _():
        st_ref[...] = jnp.zeros_like(st_ref)

    c_gv = 2 * G_KW
    c_ad = c_gv + G_VW
    c_gate = c_ad + LANES
    nchunk = TB // CHUNK
    rr = lax.broadcasted_iota(jnp.int32, (TB, TB), 0)
    cc = lax.broadcasted_iota(jnp.int32, (TB, TB), 1)
    causal = jnp.logical_and(rr // CHUNK == cc // CHUNK, cc <= rr)

    q = p_ref[:, 0:G_KW] * (G_DK ** -0.5)
    k = p_ref[:, G_KW:2 * G_KW]
    vb = p_ref[:, c_gv:c_gv + G_VW].astype(BF16)
    z = _dot(p_ref[:, c_ad:c_ad + LANES], aup_ref[...]) + ab_ref[...]
    la = -_softplus(-z) * (1.0 / G_TAU)
    b = _cumsum3(causal.astype(BF16), la)
    qe = (q * jnp.exp(b)).astype(BF16)
    ke = (k * jnp.exp(-b)).astype(BF16)
    kts, e_lasts = [], []
    for c in range(nchunk):
        rows = slice(c * CHUNK, (c + 1) * CHUNK)
        b_last = b[(c + 1) * CHUNK - 1:(c + 1) * CHUNK, :]
        kts.append((k[rows, :] * jnp.exp(b_last - b[rows, :])).astype(BF16))
        e_lasts.append(jnp.exp(b_last))
    heads = range(G_HEADS)
    sls = [slice(h * G_DK, (h + 1) * G_DK) for h in heads]
    vss = [slice(h * G_DV, (h + 1) * G_DV) for h in heads]
    scs = [jnp.where(causal, _dot_nt(qe[:, sls[h]], ke[:, sls[h]]), 0.0) for h in heads]
    o_intra = [_dot(scs[h], vb[:, vss[h]]) for h in heads]
    kvs = [[_dot_tn(vb[c * CHUNK:(c + 1) * CHUNK, vss[h]], kts[c][:, sls[h]]) for h in heads]
           for c in range(nchunk)]
    states = [st_ref[h] for h in heads]
    parts = [[] for _ in heads]
    for c in range(nchunk):
        rows = slice(c * CHUNK, (c + 1) * CHUNK)
        for h in heads:
            parts[h].append(o_intra[h][rows, :] + _dot_nt(qe[rows, sls[h]], states[h].astype(BF16)))
            states[h] = states[h] * e_lasts[c][:, sls[h]] + kvs[c][h]
    for h in heads:
        vs = vss[h]
        st_ref[h] = states[h]
        o = jnp.concatenate(parts[h], axis=0)
        ms = jnp.mean(o * o, axis=-1, keepdims=True)
        gt = p_ref[:, c_gate + h * G_DV:c_gate + (h + 1) * G_DV]
        o = o * lax.rsqrt(ms + EPS) * ng_ref[...] * (gt * _sigmoid(gt))
        o_ref[:, vs] = o.astype(o_ref.dtype)


def _gla(p_gla, alpha_up, alpha_b, norm_g, B, S):
    TB = 256
    T = B * S
    nt = S // TB
    aup_pad = jnp.concatenate([alpha_up, jnp.zeros((LANES - G_LORA, G_KW), F32)], axis=0)
    kern = functools.partial(_gla_kernel, TB=TB)
    return pl.pallas_call(
        kern,
        grid=(B, nt),
        in_specs=[
            pl.BlockSpec((TB, G_COLS_PAD), lambda b, i: (b * nt + i, 0)),
            pl.BlockSpec((LANES, G_KW), lambda b, i: (0, 0)),
            pl.BlockSpec((1, G_KW), lambda b, i: (0, 0)),
            pl.BlockSpec((1, G_DV), lambda b, i: (0, 0)),
        ],
        out_specs=pl.BlockSpec((TB, G_VW), lambda b, i: (b * nt + i, 0)),
        out_shape=jax.ShapeDtypeStruct((T, G_VW), BF16),
        scratch_shapes=[pltpu.VMEM((G_HEADS, G_DV, G_DK), F32)],
        compiler_params=pltpu.CompilerParams(
            dimension_semantics=("arbitrary", "arbitrary"), vmem_limit_bytes=VMEM_LIMIT),
        name="gla",
    )(p_gla, aup_pad, alpha_b.reshape(1, G_KW), norm_g.reshape(1, G_DV))


def _merge_kernel(x_ref, oa_ref, or_ref, og_ref, gate_ref, pa_ref, pr_ref, pg_ref, wo_ref,
                  mod_ref, g2_ref, rw_ref, rb_ref, x1_ref, h2_ref, lg_ref):
    D = D_MODEL
    merged = (_sigmoid(gate_ref[:, 0:D].astype(F32))
              * jnp.dot(oa_ref[...], pa_ref[...], preferred_element_type=F32)
              + _sigmoid(gate_ref[:, D:2 * D].astype(F32))
              * jnp.dot(or_ref[...], pr_ref[...], preferred_element_type=F32)
              + _sigmoid(gate_ref[:, 2 * D:3 * D].astype(F32))
              * jnp.dot(og_ref[...], pg_ref[...], preferred_element_type=F32))
    gt1 = mod_ref[:, 2 * D:3 * D]
    sh2 = mod_ref[:, 3 * D:4 * D]
    sc2 = mod_ref[:, 4 * D:5 * D]
    x1 = x_ref[...] + gt1 * jnp.dot(merged.astype(BF16), wo_ref[...], preferred_element_type=F32)
    x1_ref[...] = x1
    ms = jnp.mean(x1 * x1, axis=-1, keepdims=True)
    h2 = x1 * lax.rsqrt(ms + EPS) * g2_ref[...] * (1.0 + sc2) + sh2
    for s in range(NSUB):
        h2_ref[pl.ds(s, h2.shape[0], stride=NSUB), :] = h2[:, s * LANES:(s + 1) * LANES]
    h_hi = h2.astype(BF16)
    h_lo = (h2 - h_hi.astype(F32)).astype(BF16)
    lg_ref[...] = (jnp.dot(h_hi, rw_ref[0], preferred_element_type=F32)
                   + jnp.dot(h_lo, rw_ref[0], preferred_element_type=F32)
                   + jnp.dot(h_hi, rw_ref[1], preferred_element_type=F32) + rb_ref[...])


def _merge(x2, o_a, o_r, o_g, p_gate, proj_a, proj_r, proj_g, w_out, mod_l, norm2_g,
           router_w, router_b, S, tm):
    T, D = x2.shape
    tiles_per_batch = S // tm
    tile = lambda w: pl.BlockSpec((tm, w), lambda i: (i, 0))
    const = lambda m, n: pl.BlockSpec((m, n), lambda i: (0, 0))
    return pl.pallas_call(
        _merge_kernel,
        grid=(T // tm,),
        in_specs=[
            tile(D), tile(A_VW), tile(R_W), tile(G_VW), tile(GATE_COLS),
            const(A_VW, D), const(R_W, D), const(G_VW, D), const(D, D),
            pl.BlockSpec((None, 1, 6 * D), lambda i: (i // tiles_per_batch, 0, 0)),
            const(1, D), pl.BlockSpec((2, D, LANES), lambda i: (0, 0, 0)), const(1, LANES),
        ],
        out_specs=[
            tile(D),
            pl.BlockSpec((tm * NSUB, LANES), lambda i: (i, 0)),
            tile(LANES),
        ],
        out_shape=[
            jax.ShapeDtypeStruct((T, D), F32),
            jax.ShapeDtypeStruct((T * NSUB, LANES), F32),
            jax.ShapeDtypeStruct((T, LANES), F32),
        ],
        compiler_params=pltpu.CompilerParams(
            dimension_semantics=("arbitrary",), vmem_limit_bytes=VMEM_LIMIT),
        name="merge",
    )(x2, o_a, o_r, o_g, p_gate, proj_a, proj_r, proj_g, w_out, mod_l,
      norm2_g.reshape(1, D), router_w, router_b)


MOE_ROWS = 256
ROUTE_ROWS = 512
E_LANE0 = N_GROUPS


def _route_kernel(lg_ref, info_ref, cnt_ref, carry_ref):
    @pl.when(pl.program_id(0) == 0)
    def _():
        carry_ref[...] = jnp.zeros_like(carry_ref)

    lg = lg_ref[...]
    n = lg.shape[0]
    lane = lax.broadcasted_iota(jnp.int32, (n, LANES), 1).astype(F32)
    neg = -jnp.inf
    big = float(LANES)

    def first_max(vals):
        m = jnp.max(vals, axis=-1, keepdims=True)
        idx = jnp.min(jnp.where(vals == m, lane, big), axis=-1, keepdims=True)
        return m, idx

    in_grp = lane < N_GROUPS
    gm, grp = first_max(jnp.where(in_grp, lg, neg))
    g_prob = 1.0 / jnp.sum(jnp.where(in_grp, jnp.exp(lg - gm), 0.0), axis=-1, keepdims=True)
    lo = E_LANE0 + grp * EXP_PER_GROUP
    el = jnp.where(jnp.logical_and(lane >= lo, lane < lo + EXP_PER_GROUP), lg, neg)
    v1, i1 = first_max(el)
    v2, i2 = first_max(jnp.where(lane == i1, neg, el))
    e21 = jnp.exp(v2 - v1)
    w0 = g_prob / (1.0 + e21)
    w1 = g_prob * e21 / (1.0 + e21)
    oh0 = lane == i1
    oh1 = lane == i2
    oh = jnp.logical_or(oh0, oh1).astype(F32)
    before = _tri(n, True).astype(BF16)
    cnt = jnp.dot(before, oh.astype(BF16), preferred_element_type=F32) + carry_ref[...]
    rank0 = jnp.sum(jnp.where(oh0, cnt, 0.0), axis=-1, keepdims=True)
    rank1 = jnp.sum(jnp.where(oh1, cnt, 0.0), axis=-1, keepdims=True)
    carry = carry_ref[...] + jnp.sum(oh, axis=0, keepdims=True)
    carry_ref[...] = carry
    cnt_ref[...] = carry
    cols = (i1 - E_LANE0, i2 - E_LANE0, rank0, rank1, w0, w1)
    info = jnp.zeros((n, LANES), F32)
    for j, col in enumerate(cols):
        info = jnp.where(lane == j, col, info)
    info_ref[...] = info


def _route(logits):
    T = logits.shape[0]
    tr = min(ROUTE_ROWS, T)
    info, cnt = pl.pallas_call(
        _route_kernel,
        grid=(T // tr,),
        in_specs=[pl.BlockSpec((tr, LANES), lambda i: (i, 0))],
        out_specs=[pl.BlockSpec((tr, LANES), lambda i: (i, 0)),
                   pl.BlockSpec((1, LANES), lambda i: (0, 0))],
        out_shape=[jax.ShapeDtypeStruct((T, LANES), F32), jax.ShapeDtypeStruct((1, LANES), F32)],
        scratch_shapes=[pltpu.VMEM((1, LANES), F32)],
        compiler_params=pltpu.CompilerParams(dimension_semantics=("arbitrary",)),
        name="moe_route",
    )(logits)
    A = T * TOP_K
    counts = cnt[0, E_LANE0:E_LANE0 + N_EXPERTS].astype(jnp.int32)
    padded = (counts + MOE_ROWS - 1) // MOE_ROWS * MOE_ROWS
    pad_end = jnp.cumsum(padded)
    pad_start = pad_end - padded
    n_blocks = -(-A // MOE_ROWS) + N_EXPERTS
    eid = info[:, 0:2].astype(jnp.int32)
    is_e = eid[:, :, None] == jnp.arange(N_EXPERTS, dtype=jnp.int32)
    dest = (jnp.sum(jnp.where(is_e, pad_start, 0), axis=-1)
            + info[:, 2:4].astype(jnp.int32))
    blk_start = jnp.arange(n_blocks, dtype=jnp.int32) * MOE_ROWS
    blk_exp = jnp.minimum(jnp.sum(pad_end[None, :] <= blk_start[:, None], axis=1),
                          N_EXPERTS - 1).astype(jnp.int32)
    n_used = (pad_end[-1:] // MOE_ROWS).astype(jnp.int32)
    return info, dest[:, 0], dest[:, 1], blk_exp, n_used, n_blocks * MOE_ROWS


def _dispatch_kernel(d0_ref, d1_ref, h2_ref, xin_in, xin_hbm, sem):
    del xin_in
    n = h2_ref.shape[0] // NSUB
    base = pl.program_id(0) * n

    def slab(ref, row):
        return ref.at[pl.ds(pl.multiple_of(row * NSUB, NSUB), NSUB)]

    def body(r, _):
        pltpu.make_async_copy(slab(h2_ref, r), slab(xin_hbm, d0_ref[base + r]), sem).start(0)
        pltpu.make_async_copy(slab(h2_ref, r), slab(xin_hbm, d1_ref[base + r]), sem).start(1)
        return 0

    lax.fori_loop(0, n, body, 0, unroll=8)
    for _ in range(TOP_K):
        pltpu.make_async_copy(h2_ref, xin_hbm.at[pl.ds(0, n * NSUB)], sem).wait()


def _dispatch(h2_slab, dest0, dest1, xin_init, td):
    T = h2_slab.shape[0] // NSUB
    grid_spec = pltpu.PrefetchScalarGridSpec(
        num_scalar_prefetch=2,
        grid=(T // td,),
        in_specs=[pl.BlockSpec((td * NSUB, LANES), lambda i, d0, d1: (i, 0)),
                  pl.BlockSpec(memory_space=pl.ANY)],
        out_specs=pl.BlockSpec(memory_space=pl.ANY),
        scratch_shapes=[pltpu.SemaphoreType.DMA(())],
    )
    return pl.pallas_call(
        _dispatch_kernel,
        grid_spec=grid_spec,
        out_shape=jax.ShapeDtypeStruct(xin_init.shape, F32),
        input_output_aliases={3: 0},
        compiler_params=pltpu.CompilerParams(dimension_semantics=("arbitrary",)),
        name="moe_dispatch",
    )(dest0, dest1, h2_slab, xin_init)


def _moe_kernel(be_ref, nu_ref, x_ref, wg_ref, wu_ref, wd_ref, y_ref, wgb, wub, wdb):
    i = pl.program_id(0)

    @pl.when(i < nu_ref[0])
    def _():
        changed = jnp.logical_or(i == 0, be_ref[i] != be_ref[jnp.maximum(i - 1, 0)])

        @pl.when(changed)
        def _():
            wgb[...] = wg_ref[...].astype(BF16)
            wub[...] = wu_ref[...].astype(BF16)
            wdb[...] = wd_ref[...].astype(BF16)

        halves = range(2)
        hr = MOE_ROWS // 2
        xs = [jnp.concatenate(
            [x_ref[pl.ds(p * hr * NSUB + s, hr, stride=NSUB), :].astype(BF16)
             for s in range(NSUB)], axis=-1) for p in halves]
        hgs = [jnp.dot(xs[p], wgb[...], preferred_element_type=F32) for p in halves]
        hus = [jnp.dot(xs[p], wub[...], preferred_element_type=F32) for p in halves]
        hids = [(hgs[p] * _sigmoid(hgs[p]) * hus[p]).astype(BF16) for p in halves]
        ys = [jnp.dot(hids[p], wdb[...], preferred_element_type=F32) for p in halves]
        for p in halves:
            for s in range(NSUB):
                y_ref[pl.ds(p * hr * NSUB + s, hr, stride=NSUB), :] = (
                    ys[p][:, s * LANES:(s + 1) * LANES])

    @pl.when(i >= nu_ref[0])
    def _():
        y_ref[...] = jnp.zeros_like(y_ref)


def _moe(xin, blk_exp, n_used, w_gate, w_up, w_down, layer):
    blk_rows = MOE_ROWS * NSUB
    n_blocks = xin.shape[0] // blk_rows
    wspec = lambda m, n: pl.BlockSpec((None, None, m, n), lambda i, be, nu: (layer, be[i], 0, 0))
    last = lambda i, nu: jnp.minimum(i, nu[0] - 1)
    grid_spec = pltpu.PrefetchScalarGridSpec(
        num_scalar_prefetch=2,
        grid=(n_blocks,),
        in_specs=[
            pl.BlockSpec((blk_rows, LANES), lambda i, be, nu: (last(i, nu), 0)),
            wspec(D_MODEL, D_EXPERT), wspec(D_MODEL, D_EXPERT), wspec(D_EXPERT, D_MODEL),
        ],
        out_specs=pl.BlockSpec((blk_rows, LANES), lambda i, be, nu: (i, 0)),
        scratch_shapes=[
            pltpu.VMEM((D_MODEL, D_EXPERT), BF16),
            pltpu.VMEM((D_MODEL, D_EXPERT), BF16),
            pltpu.VMEM((D_EXPERT, D_MODEL), BF16),
        ],
    )
    return pl.pallas_call(
        _moe_kernel,
        grid_spec=grid_spec,
        out_shape=jax.ShapeDtypeStruct(xin.shape, F32),
        compiler_params=pltpu.CompilerParams(
            dimension_semantics=("arbitrary",), vmem_limit_bytes=VMEM_LIMIT),
        name="moe_ffn",
    )(blk_exp, n_used, xin, w_gate, w_up, w_down)


def _combine_kernel(d0_ref, d1_ref, y_hbm, x1_ref, info_ref, mod_ref, o_ref, ybuf, sem):
    i = pl.program_id(0)
    nsteps = pl.num_programs(0)
    n = x1_ref.shape[0]
    slot = i % 2
    D = D_MODEL

    def slab(ref, row):
        return ref.at[pl.ds(pl.multiple_of(row * NSUB, NSUB), NSUB)]

    def start_gather(step, sl):
        base = step * n

        def body(r, _):
            pltpu.make_async_copy(
                slab(y_hbm, d0_ref[base + r]), slab(ybuf.at[sl, 0], r), sem.at[sl]).start(0)
            pltpu.make_async_copy(
                slab(y_hbm, d1_ref[base + r]), slab(ybuf.at[sl, 1], r), sem.at[sl]).start(1)
            return 0
        lax.fori_loop(0, n, body, 0, unroll=8)

    @pl.when(i == 0)
    def _():
        start_gather(0, 0)

    @pl.when(i + 1 < nsteps)
    def _():
        start_gather(i + 1, 1 - slot)

    for k in range(TOP_K):
        pltpu.make_async_copy(
            y_hbm.at[pl.ds(0, n * NSUB)], ybuf.at[slot, k], sem.at[slot]).wait()

    w0 = info_ref[:, 4:5]
    w1 = info_ref[:, 5:6]
    for s in range(NSUB):
        cols = slice(s * LANES, (s + 1) * LANES)
        gt2 = mod_ref[:, 5 * D + s * LANES:5 * D + (s + 1) * LANES]
        piece = pl.ds(s, n, stride=NSUB)
        moe = w0 * ybuf[slot, 0, piece, :] + w1 * ybuf[slot, 1, piece, :]
        o_ref[:, cols] = x1_ref[:, cols] + gt2 * moe


def _combine(x1, y, info, dest0, dest1, mod_l, S, tm):
    T, D = x1.shape
    tiles_per_batch = S // tm
    grid_spec = pltpu.PrefetchScalarGridSpec(
        num_scalar_prefetch=2,
        grid=(T // tm,),
        in_specs=[
            pl.BlockSpec(memory_space=pl.ANY),
            pl.BlockSpec((tm, D), lambda i, d0, d1: (i, 0)),
            pl.BlockSpec((tm, LANES), lambda i, d0, d1: (i, 0)),
            pl.BlockSpec((None, 1, 6 * D), lambda i, d0, d1: (i // tiles_per_batch, 0, 0)),
        ],
        out_specs=pl.BlockSpec((tm, D), lambda i, d0, d1: (i, 0)),
        scratch_shapes=[pltpu.VMEM((2, TOP_K, tm * NSUB, LANES), F32),
                        pltpu.SemaphoreType.DMA((2,))],
    )
    return pl.pallas_call(
        _combine_kernel,
        grid_spec=grid_spec,
        out_shape=jax.ShapeDtypeStruct((T, D), F32),
        compiler_params=pltpu.CompilerParams(
            dimension_semantics=("arbitrary",), vmem_limit_bytes=VMEM_LIMIT),
        name="moe_combine",
    )(dest0, dest1, y, x1, info, mod_l)


def _pad_w_in(w):
    D = w.shape[0]
    c0 = A_COLS + R_COLS
    c_ad = c0 + 2 * G_KW + G_VW
    c_gg = c_ad + G_LORA
    pad = jnp.zeros((D, LANES - G_LORA), w.dtype)
    return jnp.concatenate([w[:, :c_gg], pad, w[:, c_gg:]], axis=1).astype(BF16)


def kernel(x, c, ada_w, ada_b, norm1_g, norm2_g, w_in, attn_qn_g, attn_kn_g, attn_lambda,
           attn_subln_g, rwkv_mu, rwkv_w_up, rwkv_w0, rwkv_a_up, rwkv_a0, rwkv_g_up, rwkv_k_k,
           rwkv_k_a, rwkv_r_k, rwkv_lnx_g, rwkv_lnx_b, gla_alpha_up, gla_alpha_b, gla_norm_g,
           proj_attn, proj_rwkv, proj_gla, w_out, router_grp_w, router_grp_b, router_exp_w,
           router_exp_b, exp_w_gate, exp_w_up, exp_w_down):
    B, S, D = x.shape
    T = B * S
    L = ada_w.shape[0]
    tm = 256
    tm_mm = 512
    mod = _adaln(c, ada_w, ada_b).reshape(L, B, 1, 6 * D)
    x2 = x.reshape(T, D)
    xin = None
    for l in range(L):
        lambda_init = 0.8 - 0.6 * math.exp(-0.3 * l)
        p_attn, p_rwkv, p_gla, p_gate = _inproj(x2, mod[l], norm1_g[l], _pad_w_in(w_in[l]), S, tm_mm)
        o_a = _attention(p_attn, attn_qn_g[l], attn_kn_g[l], attn_lambda[l], attn_subln_g[l],
                         lambda_init, B, S)
        o_r = _rwkv(p_rwkv, rwkv_mu[l], rwkv_w_up[l], rwkv_w0[l], rwkv_a_up[l], rwkv_a0[l],
                    rwkv_g_up[l], rwkv_k_k[l], rwkv_k_a[l], rwkv_r_k[l], rwkv_lnx_g[l],
                    rwkv_lnx_b[l], B, S)
        o_g = _gla(p_gla, gla_alpha_up[l], gla_alpha_b[l], gla_norm_g[l], B, S)
        n_r = N_GROUPS + N_EXPERTS
        router_w = jnp.concatenate(
            [router_grp_w[l], router_exp_w[l], jnp.zeros((D, LANES - n_r), F32)], axis=1)
        rw_hi = router_w.astype(BF16)
        router_w = jnp.stack([rw_hi, (router_w - rw_hi.astype(F32)).astype(BF16)])
        router_b = jnp.concatenate(
            [router_grp_b[l], router_exp_b[l], jnp.zeros((LANES - n_r,), F32)]).reshape(1, LANES)
        x1, h2, logits = _merge(
            x2, o_a, o_r, o_g, p_gate, proj_attn[l].astype(BF16), proj_rwkv[l].astype(BF16),
            proj_gla[l].astype(BF16), w_out[l].astype(BF16), mod[l], norm2_g[l],
            router_w, router_b, S, tm_mm)
        info, dest0, dest1, blk_exp, n_used, n_rows = _route(logits)
        if xin is None:
            xin = jnp.zeros((n_rows * NSUB, LANES), F32)
        xin = _dispatch(h2, dest0, dest1, xin, tm)
        y = _moe(xin, blk_exp, n_used, exp_w_gate, exp_w_up, exp_w_down, l)
        x2 = _combine(x1, y, info, dest0, dest1, mod[l], S, tm)
    return x2.reshape(B, S, D)
```

```python
import functools
import math

import jax
import jax.numpy as jnp
from jax import lax
from jax.experimental import pallas as pl
from jax.experimental.pallas import tpu as pltpu

F32 = jnp.float32
BF16 = jnp.bfloat16
HIGHEST = lax.Precision.HIGHEST

D_MODEL = 1024
A_HEADS, A_DH, A_DV = 4, 64, 128
A_QW, A_VW = 512, 512
A_COLS = 1536
R_HEADS, R_N, R_W = 8, 64, 512
R_COLS = 1792
RWKV_GN_EPS = 64e-5
G_HEADS, G_DK, G_DV = 4, 64, 128
G_KW, G_VW, G_LORA = 256, 512, 16
G_TAU = 16.0
G_COLS = 1552
G_COLS_PAD = 1664
GATE_COLS = 3072
N_GROUPS, EXP_PER_GROUP, N_EXPERTS, TOP_K = 4, 8, 32, 2
D_EXPERT = 512
EPS = 1e-6

LANES = 128
SUBLANES = 8
NSUB = D_MODEL // LANES
CHUNK = 64
VMEM_LIMIT = 56 * 1024 * 1024


def _dot(a, b):
    return jnp.dot(a.astype(BF16), b.astype(BF16), preferred_element_type=F32)


def _dot_hi(a, b):
    return jnp.dot(a, b, precision=HIGHEST, preferred_element_type=F32)


def _dot_nt(a, b, precision=None):
    return lax.dot_general(a, b, (((1,), (1,)), ((), ())), precision=precision,
                           preferred_element_type=F32)


def _dot_tn(a, b, precision=None):
    return lax.dot_general(a, b, (((0,), (0,)), ((), ())), precision=precision,
                           preferred_element_type=F32)


def _cumsum3(tri, x):
    x1 = x.astype(BF16)
    res = x - x1.astype(F32)
    x2 = res.astype(BF16)
    x3 = (res - x2.astype(F32)).astype(BF16)
    return (jnp.dot(tri, x1, preferred_element_type=F32)
            + jnp.dot(tri, x2, preferred_element_type=F32)
            + jnp.dot(tri, x3, preferred_element_type=F32))


def _sigmoid(x):
    return 1.0 / (1.0 + jnp.exp(-x))


def _softplus(x):
    return jnp.maximum(x, 0.0) + jnp.log(1.0 + jnp.exp(-jnp.abs(x)))


def _seg_ones(n, seg):
    r = lax.broadcasted_iota(jnp.int32, (n, n), 0) // seg
    c = lax.broadcasted_iota(jnp.int32, (n, n), 1) // seg
    return (r == c).astype(F32)


def _tri(n, strict):
    r = lax.broadcasted_iota(jnp.int32, (n, n), 0)
    c = lax.broadcasted_iota(jnp.int32, (n, n), 1)
    return (c < r) if strict else (c <= r)


def _adaln_kernel(c_ref, w_ref, b_ref, o_ref):
    c = c_ref[...]
    c_act = c * _sigmoid(c)
    o_ref[...] = _dot_hi(c_act, w_ref[...]) + b_ref[...]


def _adaln(c, ada_w, ada_b):
    L, D, N = ada_w.shape
    B = c.shape[0]
    tn = D
    return pl.pallas_call(
        _adaln_kernel,
        grid=(L, N // tn),
        in_specs=[
            pl.BlockSpec((B, D), lambda l, j: (0, 0)),
            pl.BlockSpec((None, D, tn), lambda l, j: (l, 0, j)),
            pl.BlockSpec((None, 1, tn), lambda l, j: (l, 0, j)),
        ],
        out_specs=pl.BlockSpec((None, B, tn), lambda l, j: (l, 0, j)),
        out_shape=jax.ShapeDtypeStruct((L, B, N), F32),
        name="adaln",
    )(c, ada_w, ada_b.reshape(L, 1, N))


_IN_SEGS = (A_COLS, R_COLS, G_COLS_PAD, GATE_COLS)
_IN_DTYPES = (BF16, F32, F32, BF16)
_IN_CHUNK = 512


def _inproj_kernel(x_ref, mod_ref, g_ref, w_ref, *o_refs):
    x = x_ref[...]
    D = x.shape[-1]
    ms = jnp.mean(x * x, axis=-1, keepdims=True)
    y = x * lax.rsqrt(ms + EPS) * g_ref[...]
    sh = mod_ref[:, 0:D]
    sc = mod_ref[:, D:2 * D]
    h = (y * (1.0 + sc) + sh).astype(BF16)
    base = 0
    for o_ref, width in zip(o_refs, _IN_SEGS):
        for c0 in range(0, width, _IN_CHUNK):
            c1 = min(c0 + _IN_CHUNK, width)
            o_ref[:, c0:c1] = jnp.dot(
                h, w_ref[:, base + c0:base + c1], preferred_element_type=F32
            ).astype(o_ref.dtype)
        base += width


def _inproj(x2, mod_l, norm_g, w_pad, S, tm):
    T, D = x2.shape
    NP = w_pad.shape[1]
    tiles_per_batch = S // tm
    return pl.pallas_call(
        _inproj_kernel,
        grid=(T // tm,),
        in_specs=[
            pl.BlockSpec((tm, D), lambda i: (i, 0)),
            pl.BlockSpec((None, 1, 2 * D), lambda i: (i // tiles_per_batch, 0, 0)),
            pl.BlockSpec((1, D), lambda i: (0, 0)),
            pl.BlockSpec((D, NP), lambda i: (0, 0), pipeline_mode=pl.Buffered(1)),
        ],
        out_specs=[pl.BlockSpec((tm, w), lambda i: (i, 0)) for w in _IN_SEGS],
        out_shape=[jax.ShapeDtypeStruct((T, w), dt) for w, dt in zip(_IN_SEGS, _IN_DTYPES)],
        compiler_params=pltpu.CompilerParams(
            dimension_semantics=("arbitrary",), vmem_limit_bytes=VMEM_LIMIT),
        name="inproj",
    )(x2, mod_l, norm_g.reshape(1, D), w_pad)


A_TILE = 256
A_POS_SPLIT = 64
LOG2E = math.log2(math.e)


def _split_dot(x, ones):
    hi = x.astype(BF16)
    lo = (x - hi.astype(F32)).astype(BF16)
    return (jnp.dot(hi, ones, preferred_element_type=F32)
            + jnp.dot(lo, ones, preferred_element_type=F32))


def _eye(n):
    r = lax.broadcasted_iota(jnp.int32, (n, n), 0)
    c = lax.broadcasted_iota(jnp.int32, (n, n), 1)
    return (r == c).astype(BF16)


def _attn_kernel(q_ref, k_ref, v_ref, qg_ref, kg_ref, lam_ref, sg_ref, slope_ref, o_ref,
                 qt_s, ka_s, vt_s, s_s, *, S, lambda_init):
    t = A_TILE
    seg = _seg_ones(LANES, A_DH).astype(BF16)
    eye_t = _eye(t)
    eye_l = _eye(LANES)
    slope = slope_ref[...]
    lane = lax.broadcasted_iota(jnp.int32, (t, LANES), 1)
    row = lax.broadcasted_iota(jnp.int32, (t, LANES), 0)

    def qknorm(x, g, scale):
        xf = x.astype(F32)
        ms = _split_dot(xf * xf, seg) * (1.0 / A_DH)
        return xf * lax.rsqrt(ms + EPS) * g * scale

    for b in range(S // t):
        rows = slice(b * t, (b + 1) * t)
        pos = row + b * t
        hi = (pos // A_POS_SPLIT).astype(F32)
        lo = (pos % A_POS_SPLIT).astype(F32)
        kl = lane - A_DH
        term = kl // 2
        s2 = slope * LOG2E
        qv = jnp.where(term == 0, A_POS_SPLIT * s2,
             jnp.where(term == 1, s2,
             jnp.where(term == 2, -A_POS_SPLIT * s2 * hi, -s2 * lo)))
        qv_hi = qv.astype(BF16).astype(F32)
        q_aug = jnp.where(kl < 8, jnp.where(kl % 2 == 0, qv_hi, qv - qv_hi), 0.0)
        k_aug = jnp.where(kl >= 8, 0.0,
                jnp.where(term == 0, hi, jnp.where(term == 1, lo, 1.0)))
        qn = qknorm(q_ref[rows, :], qg_ref[...], A_DH ** -0.5 * LOG2E)
        kn = qknorm(k_ref[rows, :], kg_ref[...], 1.0)
        for c in range(2):
            qc = qn if c == 0 else pltpu.roll(qn, A_DH, 1)
            kc = kn if c == 0 else pltpu.roll(kn, A_DH, 1)
            qa = jnp.where(lane < A_DH, qc, q_aug).astype(BF16)
            qt_s[c, :, rows] = _dot_nt(eye_l, qa).astype(BF16)
            ka_s[c, rows, :] = jnp.where(lane < A_DH, kc, k_aug).astype(BF16)
        vt_s[:, rows] = _dot_nt(eye_l, v_ref[rows, :]).astype(BF16)

    lv = lam_ref[...]
    lam = (jnp.exp(jnp.sum(lv[0:1] * lv[1:2], axis=-1, keepdims=True))
           - jnp.exp(jnp.sum(lv[2:3] * lv[3:4], axis=-1, keepdims=True)) + lambda_init)
    causal = (lax.broadcasted_iota(jnp.int32, (t, t), 0)
              <= lax.broadcasted_iota(jnp.int32, (t, t), 1))

    nt = S // t

    def score_block(i, c, j, st):
        s = jnp.dot(ka_s[c, j * t:(j + 1) * t, :], qt_s[c, :, i * t:(i + 1) * t],
                    preferred_element_type=F32)
        if j == i:
            s = jnp.where(causal, s, -jnp.inf)
        s_s[i % 2, c, j * t:(j + 1) * t, :] = s
        mj = jnp.max(s, axis=0, keepdims=True)
        st["m"][c] = mj if st["m"][c] is None else jnp.maximum(st["m"][c], mj)

    def value_block(i, c, j, st):
        p = jnp.exp2(s_s[i % 2, c, j * t:(j + 1) * t, :] - st["m"][c])
        st["l"][c] = st["l"][c] + jnp.sum(p, axis=0, keepdims=True)
        st["acc"][c] = st["acc"][c] + jnp.dot(vt_s[:, j * t:(j + 1) * t], p.astype(BF16),
                                              preferred_element_type=F32)

    def finish(i, st):
        o = st["acc"][0] / st["l"][0] - lam * (st["acc"][1] / st["l"][1])
        ms = jnp.mean(o * o, axis=0, keepdims=True)
        o = o * lax.rsqrt(ms + EPS) * sg_ref[...] * (1.0 - lambda_init)
        o_ref[i * t:(i + 1) * t, :] = _dot_nt(eye_t, o.astype(BF16)).astype(o_ref.dtype)

    prev = None
    for i in range(nt + 1):
        cur = None
        first = []
        if i < nt:
            cur = dict(m=[None, None], l=[jnp.zeros((1, t), F32)] * 2,
                       acc=[jnp.zeros((A_DV, t), F32)] * 2)
            first = [(c, j) for j in range(i + 1) for c in range(2)]
        second = [(c, j) for j in range(i) for c in range(2)] if prev is not None else []
        for n in range(max(len(first), len(second))):
            if n < len(first):
                score_block(i, first[n][0], first[n][1], cur)
            if n < len(second):
                value_block(i - 1, second[n][0], second[n][1], prev)
        if prev is not None:
            finish(i - 1, prev)
        prev = cur


def _attention(p_attn, qn_g, kn_g, lam_vecs, subln_g, lambda_init, B, S):
    pa = p_attn.reshape(B, S, A_COLS)
    dup = lambda g: jnp.concatenate([g, g]).reshape(1, LANES)
    slopes = jnp.asarray(
        [[2.0 ** (-8.0 * (i + 1) / A_HEADS)] * LANES for i in range(A_HEADS)], F32
    ).reshape(A_HEADS, 1, LANES)
    nqb = A_QW // LANES
    kern = functools.partial(_attn_kernel, S=S, lambda_init=lambda_init)
    out = pl.pallas_call(
        kern,
        grid=(B, A_HEADS),
        in_specs=[
            pl.BlockSpec((None, S, LANES), lambda b, h: (b, 0, h)),
            pl.BlockSpec((None, S, LANES), lambda b, h: (b, 0, nqb + h)),
            pl.BlockSpec((None, S, LANES), lambda b, h: (b, 0, 2 * nqb + h)),
            pl.BlockSpec((1, LANES), lambda b, h: (0, 0)),
            pl.BlockSpec((1, LANES), lambda b, h: (0, 0)),
            pl.BlockSpec((4, A_DH), lambda b, h: (0, 0)),
            pl.BlockSpec((A_DV, 1), lambda b, h: (0, 0)),
            pl.BlockSpec((None, 1, LANES), lambda b, h: (h, 0, 0)),
        ],
        out_specs=pl.BlockSpec((None, S, A_DV), lambda b, h: (b, 0, h)),
        out_shape=jax.ShapeDtypeStruct((B, S, A_VW), BF16),
        scratch_shapes=[pltpu.VMEM((2, LANES, S), BF16), pltpu.VMEM((2, S, LANES), BF16),
                        pltpu.VMEM((A_DV, S), BF16), pltpu.VMEM((2, 2, S, A_TILE), F32)],
        compiler_params=pltpu.CompilerParams(
            dimension_semantics=("arbitrary", "arbitrary"), vmem_limit_bytes=VMEM_LIMIT),
        name="diff_attn",
    )(pa, pa, pa, dup(qn_g), dup(kn_g), lam_vecs, subln_g.reshape(A_DV, 1), slopes)
    return out.reshape(B * S, A_VW)


R_GROUP = 4
R_GW = R_GROUP * R_N


def _rwkv_kernel(p_ref, mu_ref, wup_ref, w0_ref, aup_ref, a0_ref, gup_ref, kk_ref, ka_ref,
                 rk_ref, lg_ref, lb_ref, o_ref,
                 carry_ref, st_ref, al_s, be_s, ka_s, rh_s, bt_s, kt_s, v_s, gc_s, y_s, *, TB):
    @pl.when(pl.program_id(1) == 0)
    def _():
        carry_ref[...] = jnp.zeros_like(carry_ref)
        st_ref[...] = jnp.zeros_like(st_ref)

    xs = p_ref[...]
    prev = pltpu.roll(xs, 1, 0)
    row = lax.broadcasted_iota(jnp.int32, (TB, 1), 0)
    prev = jnp.where(row == 0, carry_ref[...], prev)
    carry_ref[...] = xs[TB - 1:TB, :]
    xm = xs + (prev - xs) * mu_ref[...]
    r = xm[:, 0:R_W]
    k = xm[:, R_W:2 * R_W]
    v = xm[:, 2 * R_W:3 * R_W]
    wa = xm[:, 3 * R_W:3 * R_W + LANES]
    gd = xm[:, 3 * R_W + LANES:3 * R_W + 2 * LANES]
    wz = w0_ref[...] + _dot(jnp.tanh(wa), wup_ref[...])
    lw = -math.exp(-0.5) * _sigmoid(wz)
    a = _sigmoid(a0_ref[...] + _dot(wa, aup_ref[...]))
    g = _dot(_sigmoid(gd), gup_ref[...])
    seg = _seg_ones(R_GW, R_N).astype(BF16)

    def head_sum(x, passes):
        f = _split_dot if passes == 2 else (
            lambda t, o: jnp.dot(t.astype(BF16), o, preferred_element_type=F32))
        return jnp.concatenate(
            [f(x[:, j * R_GW:(j + 1) * R_GW], seg) for j in range(R_W // R_GW)], axis=-1)

    kk = k * kk_ref[...]
    kk = kk * jnp.minimum(lax.rsqrt(head_sum(kk * kk, 1)), 1e12)
    k2 = k * (1.0 + (a - 1.0) * ka_ref[...])
    bonus = head_sum(r * k2 * rk_ref[...], 2) * v
    bv = kk * a

    rr = lax.broadcasted_iota(jnp.int32, (TB, TB), 0)
    cc = lax.broadcasted_iota(jnp.int32, (TB, TB), 1)
    tril_blk = jnp.logical_and(rr // CHUNK == cc // CHUNK, cc <= rr).astype(BF16)
    Lg = _cumsum3(tril_blk, lw)
    inv = jnp.exp(-Lg)
    al_s[...] = (jnp.exp(Lg - lw) * kk).astype(BF16)
    be_s[...] = (bv * inv).astype(BF16)
    ka_s[...] = (k2 * inv).astype(BF16)
    rh_s[...] = (jnp.exp(Lg) * r).astype(BF16)
    v_s[...] = v.astype(BF16)
    for c in range(TB // CHUNK):
        rows = slice(c * CHUNK, (c + 1) * CHUNK)
        gC = Lg[(c + 1) * CHUNK - 1:(c + 1) * CHUNK, :]
        tail = jnp.exp(gC - Lg[rows, :])
        bt_s[rows, :] = (bv[rows, :] * tail).astype(BF16)
        kt_s[rows, :] = (k2[rows, :] * tail).astype(BF16)
        gc_s[c * SUBLANES:(c + 1) * SUBLANES, :] = jnp.broadcast_to(jnp.exp(gC), (SUBLANES, R_W))

    ri = lax.broadcasted_iota(jnp.int32, (R_GW, R_GW), 0)
    ci = lax.broadcasted_iota(jnp.int32, (R_GW, R_GW), 1)
    blk = ri // R_N == ci // R_N
    strict = ci % R_N < ri % R_N
    strict_t = ri % R_N < ci % R_N
    incl = ci % R_N <= ri % R_N
    zero = jnp.zeros((), BF16)

    def expand(x):
        return jnp.where(blk, jnp.concatenate([x] * R_GROUP, axis=0), zero)

    n_groups = R_HEADS // R_GROUP
    pair = 2

    def chunk_pair(ip, _):
        chains = []
        for dc in range(pair):
            c = ip * pair + dc
            rows = pl.ds(pl.multiple_of(c * CHUNK, CHUNK), CHUNK)
            for gi in range(n_groups):
                chains.append(dict(c=c, rows=rows, gi=gi, cols=slice(gi * R_GW, (gi + 1) * R_GW)))
        for ch in chains:
            rows, cols = ch["rows"], ch["cols"]
            ch["A"] = expand(al_s[rows, cols])
            ch["R"] = expand(rh_s[rows, cols])
            ch["B"] = expand(be_s[rows, cols])
            ch["K"] = expand(ka_s[rows, cols])
            vc = v_s[rows, cols]
            ch["V"] = jnp.concatenate(
                [vc[:, h * R_N:(h + 1) * R_N] for h in range(R_GROUP)], axis=0)
        for ch in chains:
            ch["X"] = -jnp.where(strict, _dot_nt(ch["A"], ch["B"]), 0.0)
            ch["m_ak_t"] = jnp.where(strict_t, _dot_nt(ch["K"], ch["A"]), 0.0).astype(BF16)
            ch["m_rb"] = jnp.where(incl, _dot_nt(ch["R"], ch["B"]), 0.0).astype(BF16)
            ch["m_rk"] = jnp.where(incl, _dot_nt(ch["R"], ch["K"]), 0.0).astype(BF16)
        for ch in chains:
            ch["Z"] = ch["A"].astype(F32)
            ch["Wt"] = _dot_tn(ch["V"], ch["m_ak_t"])
        n = 1
        while True:
            last = 2 * n >= CHUNK
            for ch in chains:
                Xb = ch["X"].astype(BF16)
                ch["Z"] = ch["Z"] + jnp.dot(Xb, ch["Z"].astype(BF16), preferred_element_type=F32)
                ch["Wt"] = ch["Wt"] + _dot_nt(ch["Wt"].astype(BF16), Xb)
                if not last:
                    ch["X"] = jnp.dot(Xb, Xb, preferred_element_type=F32)
            n *= 2
            if last:
                break
        for ch in chains:
            rows, cols = ch["rows"], ch["cols"]
            Bt = expand(bt_s[rows, cols])
            Kt = expand(kt_s[rows, cols])
            Zb = ch["Z"].astype(BF16)
            Wtb = ch["Wt"].astype(BF16)
            ch["y_a"] = (ch["R"].astype(F32)
                         - jnp.dot(ch["m_rb"], Zb, preferred_element_type=F32)).astype(BF16)
            ch["y_b"] = (jnp.dot(ch["m_rk"], ch["V"], preferred_element_type=F32)
                         - _dot_nt(ch["m_rb"], Wtb))
            ch["p_neg"] = _dot_tn(Zb, Bt).astype(BF16)
            ch["q"] = _dot_tn(ch["V"], Kt) - jnp.dot(Wtb, Bt, preferred_element_type=F32)
        for ch in chains:
            rows, cols, gi = ch["rows"], ch["cols"], ch["gi"]
            g0 = pl.multiple_of(ch["c"] * SUBLANES, SUBLANES)
            S0 = st_ref[gi]
            S0b = S0.astype(BF16)
            y = _dot_nt(ch["y_a"], S0b) + ch["y_b"]
            st_ref[gi] = (S0 * gc_s[pl.ds(g0, SUBLANES), cols][0:1, :]
                          - jnp.dot(S0b, ch["p_neg"], preferred_element_type=F32) + ch["q"])
            for h in range(R_GROUP):
                hh = gi * R_GROUP + h
                y_s[rows, hh * R_N:(hh + 1) * R_N] = y[h * R_N:(h + 1) * R_N, :]
        return 0

    lax.fori_loop(0, TB // CHUNK // pair, chunk_pair, 0)

    y = y_s[...]
    mean = head_sum(y, 2) * (1.0 / R_N)
    yc = y - mean
    var = head_sum(yc * yc, 1) * (1.0 / R_N)
    yn = yc * lax.rsqrt(var + RWKV_GN_EPS) * lg_ref[...] + lb_ref[...]
    o_ref[...] = ((yn + bonus) * g).astype(o_ref.dtype)


def _rwkv(p_rwkv, mu, w_up, w0, a_up, a0, g_up, k_k, k_a, r_k, lnx_g, lnx_b, B, S):
    TB = 256
    T = B * S
    nt = S // TB
    row = lambda t: t.reshape(1, -1)
    zeros = jnp.zeros((R_N, R_W), F32)
    wup_pad = jnp.concatenate([w_up, zeros], axis=0)
    aup_pad = jnp.concatenate([zeros, a_up], axis=0)
    vec = lambda n: pl.BlockSpec((1, n), lambda b, i: (0, 0))
    mat = lambda m, n: pl.BlockSpec((m, n), lambda b, i: (0, 0))
    kern = functools.partial(_rwkv_kernel, TB=TB)
    return pl.pallas_call(
        kern,
        grid=(B, nt),
        in_specs=[
            pl.BlockSpec((TB, R_COLS), lambda b, i: (b * nt + i, 0)),
            vec(R_COLS), mat(LANES, R_W), vec(R_W), mat(LANES, R_W), vec(R_W), mat(LANES, R_W),
            vec(R_W), vec(R_W), vec(R_W), vec(R_W), vec(R_W),
        ],
        out_specs=pl.BlockSpec((TB, R_W), lambda b, i: (b * nt + i, 0)),
        out_shape=jax.ShapeDtypeStruct((T, R_W), BF16),
        scratch_shapes=[
            pltpu.VMEM((1, R_COLS), F32),
            pltpu.VMEM((R_HEADS // R_GROUP, R_N, R_GW), F32),
        ] + [pltpu.VMEM((TB, R_W), BF16)] * 7 + [
            pltpu.VMEM((TB // CHUNK * SUBLANES, R_W), F32),
            pltpu.VMEM((TB, R_W), F32),
        ],
        compiler_params=pltpu.CompilerParams(
            dimension_semantics=("arbitrary", "arbitrary"), vmem_limit_bytes=VMEM_LIMIT),
        name="rwkv7",
    )(p_rwkv, row(mu), wup_pad, row(w0), aup_pad, row(a0), g_up, row(k_k), row(k_a),
      row(r_k), row(lnx_g), row(lnx_b))


def _gla_kernel(p_ref, aup_ref, ab_ref, ng_ref, o_ref, st_ref, *, TB):
    @pl.when(pl.program_id(1) == 0)
    def _():
        st_ref[...] = jnp.zeros_like(st_ref)

    c_gv = 2 * G_KW
    c_ad = c_gv + G_VW
    c_gate = c_ad + LANES
    nchunk = TB // CHUNK
    rr = lax.broadcasted_iota(jnp.int32, (TB, TB), 0)
    cc = lax.broadcasted_iota(jnp.int32, (TB, TB), 1)
    causal = jnp.logical_and(rr // CHUNK == cc // CHUNK, cc <= rr)

    q = p_ref[:, 0:G_KW] * (G_DK ** -0.5)
    k = p_ref[:, G_KW:2 * G_KW]
    vb = p_ref[:, c_gv:c_gv + G_VW].astype(BF16)
    z = _dot(p_ref[:, c_ad:c_ad + LANES], aup_ref[...]) + ab_ref[...]
    la = -_softplus(-z) * (1.0 / G_TAU)
    b = _cumsum3(causal.astype(BF16), la)
    qe = (q * jnp.exp(b)).astype(BF16)
    ke = (k * jnp.exp(-b)).astype(BF16)
    kts, e_lasts = [], []
    for c in range(nchunk):
        rows = slice(c * CHUNK, (c + 1) * CHUNK)
        b_last = b[(c + 1) * CHUNK - 1:(c + 1) * CHUNK, :]
        kts.append((k[rows, :] * jnp.exp(b_last - b[rows, :])).astype(BF16))
        e_lasts.append(jnp.exp(b_last))
    heads = range(G_HEADS)
    sls = [slice(h * G_DK, (h + 1) * G_DK) for h in heads]
    vss = [slice(h * G_DV, (h + 1) * G_DV) for h in heads]
    scs = [jnp.where(causal, _dot_nt(qe[:, sls[h]], ke[:, sls[h]]), 0.0) for h in heads]
    o_intra = [_dot(scs[h], vb[:, vss[h]]) for h in heads]
    kvs = [[_dot_tn(vb[c * CHUNK:(c + 1) * CHUNK, vss[h]], kts[c][:, sls[h]]) for h in heads]
           for c in range(nchunk)]
    states = [st_ref[h] for h in heads]
    parts = [[] for _ in heads]
    for c in range(nchunk):
        rows = slice(c * CHUNK, (c + 1) * CHUNK)
        for h in heads:
            parts[h].append(o_intra[h][rows, :] + _dot_nt(qe[rows, sls[h]], states[h].astype(BF16)))
            states[h] = states[h] * e_lasts[c][:, sls[h]] + kvs[c][h]
    for h in heads:
        vs = vss[h]
        st_ref[h] = states[h]
        o = jnp.concatenate(parts[h], axis=0)
        ms = jnp.mean(o * o, axis=-1, keepdims=True)
        gt = p_ref[:, c_gate + h * G_DV:c_gate + (h + 1) * G_DV]
        o = o * lax.rsqrt(ms + EPS) * ng_ref[...] * (gt * _sigmoid(gt))
        o_ref[:, vs] = o.astype(o_ref.dtype)


def _gla(p_gla, alpha_up, alpha_b, norm_g, B, S):
    TB = 256
    T = B * S
    nt = S // TB
    aup_pad = jnp.concatenate([alpha_up, jnp.zeros((LANES - G_LORA, G_KW), F32)], axis=0)
    kern = functools.partial(_gla_kernel, TB=TB)
    return pl.pallas_call(
        kern,
        grid=(B, nt),
        in_specs=[
            pl.BlockSpec((TB, G_COLS_PAD), lambda b, i: (b * nt + i, 0)),
            pl.BlockSpec((LANES, G_KW), lambda b, i: (0, 0)),
            pl.BlockSpec((1, G_KW), lambda b, i: (0, 0)),
            pl.BlockSpec((1, G_DV), lambda b, i: (0, 0)),
        ],
        out_specs=pl.BlockSpec((TB, G_VW), lambda b, i: (b * nt + i, 0)),
        out_shape=jax.ShapeDtypeStruct((T, G_VW), BF16),
        scratch_shapes=[pltpu.VMEM((G_HEADS, G_DV, G_DK), F32)],
        compiler_params=pltpu.CompilerParams(
            dimension_semantics=("arbitrary", "arbitrary"), vmem_limit_bytes=VMEM_LIMIT),
        name="gla",
    )(p_gla, aup_pad, alpha_b.reshape(1, G_KW), norm_g.reshape(1, G_DV))


def _merge_kernel(x_ref, oa_ref, or_ref, og_ref, gate_ref, pa_ref, pr_ref, pg_ref, wo_ref,
                  mod_ref, g2_ref, rw_ref, rb_ref, x1_ref, h2_ref, lg_ref):
    D = D_MODEL
    merged = (_sigmoid(gate_ref[:, 0:D].astype(F32))
              * jnp.dot(oa_ref[...], pa_ref[...], preferred_element_type=F32)
              + _sigmoid(gate_ref[:, D:2 * D].astype(F32))
              * jnp.dot(or_ref[...], pr_ref[...], preferred_element_type=F32)
              + _sigmoid(gate_ref[:, 2 * D:3 * D].astype(F32))
              * jnp.dot(og_ref[...], pg_ref[...], preferred_element_type=F32))
    gt1 = mod_ref[:, 2 * D:3 * D]
    sh2 = mod_ref[:, 3 * D:4 * D]
    sc2 = mod_ref[:, 4 * D:5 * D]
    x1 = x_ref[...] + gt1 * jnp.dot(merged.astype(BF16), wo_ref[...], preferred_element_type=F32)
    x1_ref[...] = x1
    ms = jnp.mean(x1 * x1, axis=-1, keepdims=True)
    h2 = x1 * lax.rsqrt(ms + EPS) * g2_ref[...] * (1.0 + sc2) + sh2
    for s in range(NSUB):
        h2_ref[pl.ds(s, h2.shape[0], stride=NSUB), :] = h2[:, s * LANES:(s + 1) * LANES]
    h_hi = h2.astype(BF16)
    h_lo = (h2 - h_hi.astype(F32)).astype(BF16)
    lg_ref[...] = (jnp.dot(h_hi, rw_ref[0], preferred_element_type=F32)
                   + jnp.dot(h_lo, rw_ref[0], preferred_element_type=F32)
                   + jnp.dot(h_hi, rw_ref[1], preferred_element_type=F32) + rb_ref[...])


def _merge(x2, o_a, o_r, o_g, p_gate, proj_a, proj_r, proj_g, w_out, mod_l, norm2_g,
           router_w, router_b, S, tm):
    T, D = x2.shape
    tiles_per_batch = S // tm
    tile = lambda w: pl.BlockSpec((tm, w), lambda i: (i, 0))
    const = lambda m, n: pl.BlockSpec((m, n), lambda i: (0, 0))
    return pl.pallas_call(
        _merge_kernel,
        grid=(T // tm,),
        in_specs=[
            tile(D), tile(A_VW), tile(R_W), tile(G_VW), tile(GATE_COLS),
            const(A_VW, D), const(R_W, D), const(G_VW, D), const(D, D),
            pl.BlockSpec((None, 1, 6 * D), lambda i: (i // tiles_per_batch, 0, 0)),
            const(1, D), pl.BlockSpec((2, D, LANES), lambda i: (0, 0, 0)), const(1, LANES),
        ],
        out_specs=[
            tile(D),
            pl.BlockSpec((tm * NSUB, LANES), lambda i: (i, 0)),
            tile(LANES),
        ],
        out_shape=[
            jax.ShapeDtypeStruct((T, D), F32),
            jax.ShapeDtypeStruct((T * NSUB, LANES), F32),
            jax.ShapeDtypeStruct((T, LANES), F32),
        ],
        compiler_params=pltpu.CompilerParams(
            dimension_semantics=("arbitrary",), vmem_limit_bytes=VMEM_LIMIT),
        name="merge",
    )(x2, o_a, o_r, o_g, p_gate, proj_a, proj_r, proj_g, w_out, mod_l,
      norm2_g.reshape(1, D), router_w, router_b)


MOE_ROWS = 256
ROUTE_ROWS = 512
E_LANE0 = N_GROUPS


def _route_kernel(lg_ref, info_ref, cnt_ref, carry_ref):
    @pl.when(pl.program_id(0) == 0)
    def _():
        carry_ref[...] = jnp.zeros_like(carry_ref)

    lg = lg_ref[...]
    n = lg.shape[0]
    lane = lax.broadcasted_iota(jnp.int32, (n, LANES), 1).astype(F32)
    neg = -jnp.inf
    big = float(LANES)

    def first_max(vals):
        m = jnp.max(vals, axis=-1, keepdims=True)
        idx = jnp.min(jnp.where(vals == m, lane, big), axis=-1, keepdims=True)
        return m, idx

    in_grp = lane < N_GROUPS
    gm, grp = first_max(jnp.where(in_grp, lg, neg))
    g_prob = 1.0 / jnp.sum(jnp.where(in_grp, jnp.exp(lg - gm), 0.0), axis=-1, keepdims=True)
    lo = E_LANE0 + grp * EXP_PER_GROUP
    el = jnp.where(jnp.logical_and(lane >= lo, lane < lo + EXP_PER_GROUP), lg, neg)
    v1, i1 = first_max(el)
    v2, i2 = first_max(jnp.where(lane == i1, neg, el))
    e21 = jnp.exp(v2 - v1)
    w0 = g_prob / (1.0 + e21)
    w1 = g_prob * e21 / (1.0 + e21)
    oh0 = lane == i1
    oh1 = lane == i2
    oh = jnp.logical_or(oh0, oh1).astype(F32)
    before = _tri(n, True).astype(BF16)
    cnt = jnp.dot(before, oh.astype(BF16), preferred_element_type=F32) + carry_ref[...]
    rank0 = jnp.sum(jnp.where(oh0, cnt, 0.0), axis=-1, keepdims=True)
    rank1 = jnp.sum(jnp.where(oh1, cnt, 0.0), axis=-1, keepdims=True)
    carry = carry_ref[...] + jnp.sum(oh, axis=0, keepdims=True)
    carry_ref[...] = carry
    cnt_ref[...] = carry
    cols = (i1 - E_LANE0, i2 - E_LANE0, rank0, rank1, w0, w1)
    info = jnp.zeros((n, LANES), F32)
    for j, col in enumerate(cols):
        info = jnp.where(lane == j, col, info)
    info_ref[...] = info


def _route(logits):
    T = logits.shape[0]
    tr = min(ROUTE_ROWS, T)
    info, cnt = pl.pallas_call(
        _route_kernel,
        grid=(T // tr,),
        in_specs=[pl.BlockSpec((tr, LANES), lambda i: (i, 0))],
        out_specs=[pl.BlockSpec((tr, LANES), lambda i: (i, 0)),
                   pl.BlockSpec((1, LANES), lambda i: (0, 0))],
        out_shape=[jax.ShapeDtypeStruct((T, LANES), F32), jax.ShapeDtypeStruct((1, LANES), F32)],
        scratch_shapes=[pltpu.VMEM((1, LANES), F32)],
        compiler_params=pltpu.CompilerParams(dimension_semantics=("arbitrary",)),
        name="moe_route",
    )(logits)
    A = T * TOP_K
    counts = cnt[0, E_LANE0:E_LANE0 + N_EXPERTS].astype(jnp.int32)
    padded = (counts + MOE_ROWS - 1) // MOE_ROWS * MOE_ROWS
    pad_end = jnp.cumsum(padded)
    pad_start = pad_end - padded
    n_blocks = -(-A // MOE_ROWS) + N_EXPERTS
    eid = info[:, 0:2].astype(jnp.int32)
    is_e = eid[:, :, None] == jnp.arange(N_EXPERTS, dtype=jnp.int32)
    dest = (jnp.sum(jnp.where(is_e, pad_start, 0), axis=-1)
            + info[:, 2:4].astype(jnp.int32))
    blk_start = jnp.arange(n_blocks, dtype=jnp.int32) * MOE_ROWS
    blk_exp = jnp.minimum(jnp.sum(pad_end[None, :] <= blk_start[:, None], axis=1),
                          N_EXPERTS - 1).astype(jnp.int32)
    n_used = (pad_end[-1:] // MOE_ROWS).astype(jnp.int32)
    nonempty = counts > 0
    ids = jnp.arange(N_EXPERTS, dtype=jnp.int32)
    grp_of = jnp.cumsum(nonempty.astype(jnp.int32)) - 1
    later = jnp.logical_and(ids[None, :] > ids[:, None], nonempty[None, :])
    nxt_of = jnp.min(jnp.where(later, ids[None, :], N_EXPERTS), axis=1)
    nxt_of = jnp.where(nxt_of == N_EXPERTS, -1, nxt_of)
    is_b = blk_exp[:, None] == ids[None, :]
    blk_grp = jnp.sum(jnp.where(is_b, grp_of[None, :], 0), axis=1).astype(jnp.int32)
    blk_nxt = jnp.sum(jnp.where(is_b, nxt_of[None, :], 0), axis=1).astype(jnp.int32)
    return info, dest[:, 0], dest[:, 1], blk_exp, n_used, blk_grp, blk_nxt, n_blocks * MOE_ROWS


def _dispatch_kernel(d0_ref, d1_ref, h2_ref, xin_in, xin_hbm, sem):
    del xin_in
    n = h2_ref.shape[0] // NSUB
    base = pl.program_id(0) * n

    def slab(ref, row):
        return ref.at[pl.ds(pl.multiple_of(row * NSUB, NSUB), NSUB)]

    def body(r, _):
        pltpu.make_async_copy(slab(h2_ref, r), slab(xin_hbm, d0_ref[base + r]), sem).start(0)
        pltpu.make_async_copy(slab(h2_ref, r), slab(xin_hbm, d1_ref[base + r]), sem).start(1)
        return 0

    lax.fori_loop(0, n, body, 0, unroll=8)
    for _ in range(TOP_K):
        pltpu.make_async_copy(h2_ref, xin_hbm.at[pl.ds(0, n * NSUB)], sem).wait()


def _dispatch(h2_slab, dest0, dest1, xin_init, td):
    T = h2_slab.shape[0] // NSUB
    grid_spec = pltpu.PrefetchScalarGridSpec(
        num_scalar_prefetch=2,
        grid=(T // td,),
        in_specs=[pl.BlockSpec((td * NSUB, LANES), lambda i, d0, d1: (i, 0)),
                  pl.BlockSpec(memory_space=pl.ANY)],
        out_specs=pl.BlockSpec(memory_space=pl.ANY),
        scratch_shapes=[pltpu.SemaphoreType.DMA(())],
    )
    return pl.pallas_call(
        _dispatch_kernel,
        grid_spec=grid_spec,
        out_shape=jax.ShapeDtypeStruct(xin_init.shape, F32),
        input_output_aliases={3: 0},
        compiler_params=pltpu.CompilerParams(dimension_semantics=("arbitrary",)),
        name="moe_dispatch",
    )(dest0, dest1, h2_slab, xin_init)


def _moe_kernel(be_ref, nu_ref, grp_ref, nxt_ref, x_ref, wg_hbm, wu_hbm, wd_hbm, y_ref,
                wgf, wuf, wdf, wgb, wub, wdb, sem, *, layer):
    i = pl.program_id(0)

    def weight_copies(expert, slot):
        return [pltpu.make_async_copy(src.at[layer, expert], dst.at[slot], sem.at[slot, k])
                for k, (src, dst) in enumerate(((wg_hbm, wgf), (wu_hbm, wuf), (wd_hbm, wdf)))]

    @pl.when(i < nu_ref[0])
    def _():
        changed = jnp.logical_or(i == 0, be_ref[i] != be_ref[jnp.maximum(i - 1, 0)])

        @pl.when(changed)
        def _():
            slot = grp_ref[i] % 2

            @pl.when(i == 0)
            def _():
                for cp in weight_copies(be_ref[0], 0):
                    cp.start()

            for cp in weight_copies(be_ref[i], slot):
                cp.wait()
            wgb[...] = wgf[slot].astype(BF16)
            wub[...] = wuf[slot].astype(BF16)
            wdb[...] = wdf[slot].astype(BF16)

            @pl.when(nxt_ref[i] >= 0)
            def _():
                for cp in weight_copies(nxt_ref[i], 1 - slot):
                    cp.start()

        kw = 2 * LANES
        per = kw // LANES
        hg = hu = None
        for j in range(D_MODEL // kw):
            xj = jnp.concatenate(
                [x_ref[pl.ds(j * per + s, MOE_ROWS, stride=NSUB), :].astype(BF16)
                 for s in range(per)], axis=-1)
            dg = jnp.dot(xj, wgb[j * kw:(j + 1) * kw, :], preferred_element_type=F32)
            du = jnp.dot(xj, wub[j * kw:(j + 1) * kw, :], preferred_element_type=F32)
            hg = dg if hg is None else hg + dg
            hu = du if hu is None else hu + du
        hid = (hg * _sigmoid(hg) * hu).astype(BF16)
        for j in range(D_MODEL // kw):
            yj = jnp.dot(hid, wdb[:, j * kw:(j + 1) * kw], preferred_element_type=F32)
            for s in range(per):
                y_ref[pl.ds(j * per + s, MOE_ROWS, stride=NSUB), :] = yj[:, s * LANES:(s + 1) * LANES]

    @pl.when(i >= nu_ref[0])
    def _():
        y_ref[...] = jnp.zeros_like(y_ref)


def _moe(xin, blk_exp, n_used, blk_grp, blk_nxt, w_gate, w_up, w_down, layer):
    blk_rows = MOE_ROWS * NSUB
    n_blocks = xin.shape[0] // blk_rows
    last = lambda i, nu: jnp.minimum(i, nu[0] - 1)
    grid_spec = pltpu.PrefetchScalarGridSpec(
        num_scalar_prefetch=4,
        grid=(n_blocks,),
        in_specs=[
            pl.BlockSpec((blk_rows, LANES), lambda i, be, nu, gr, nx: (last(i, nu), 0)),
            pl.BlockSpec(memory_space=pl.ANY), pl.BlockSpec(memory_space=pl.ANY),
            pl.BlockSpec(memory_space=pl.ANY),
        ],
        out_specs=pl.BlockSpec((blk_rows, LANES), lambda i, be, nu, gr, nx: (i, 0)),
        scratch_shapes=[
            pltpu.VMEM((2, D_MODEL, D_EXPERT), F32),
            pltpu.VMEM((2, D_MODEL, D_EXPERT), F32),
            pltpu.VMEM((2, D_EXPERT, D_MODEL), F32),
            pltpu.VMEM((D_MODEL, D_EXPERT), BF16),
            pltpu.VMEM((D_MODEL, D_EXPERT), BF16),
            pltpu.VMEM((D_EXPERT, D_MODEL), BF16),
            pltpu.SemaphoreType.DMA((2, 3)),
        ],
    )
    return pl.pallas_call(
        functools.partial(_moe_kernel, layer=layer),
        grid_spec=grid_spec,
        out_shape=jax.ShapeDtypeStruct(xin.shape, F32),
        compiler_params=pltpu.CompilerParams(
            dimension_semantics=("arbitrary",), vmem_limit_bytes=VMEM_LIMIT),
        name="moe_ffn",
    )(blk_exp, n_used, blk_grp, blk_nxt, xin, w_gate, w_up, w_down)


def _combine_kernel(d0_ref, d1_ref, y_hbm, x1_ref, info_ref, mod_ref, o_ref, ybuf, sem):
    i = pl.program_id(0)
    nsteps = pl.num_programs(0)
    n = x1_ref.shape[0]
    slot = i % 2
    D = D_MODEL

    def slab(ref, row):
        return ref.at[pl.ds(pl.multiple_of(row * NSUB, NSUB), NSUB)]

    def start_gather(step, sl):
        base = step * n

        def body(r, _):
            pltpu.make_async_copy(
                slab(y_hbm, d0_ref[base + r]), slab(ybuf.at[sl, 0], r), sem.at[sl]).start(0)
            pltpu.make_async_copy(
                slab(y_hbm, d1_ref[base + r]), slab(ybuf.at[sl, 1], r), sem.at[sl]).start(1)
            return 0
        lax.fori_loop(0, n, body, 0, unroll=8)

    @pl.when(i == 0)
    def _():
        start_gather(0, 0)

    @pl.when(i + 1 < nsteps)
    def _():
        start_gather(i + 1, 1 - slot)

    for k in range(TOP_K):
        pltpu.make_async_copy(
            y_hbm.at[pl.ds(0, n * NSUB)], ybuf.at[slot, k], sem.at[slot]).wait()

    w0 = info_ref[:, 4:5]
    w1 = info_ref[:, 5:6]
    for s in range(NSUB):
        cols = slice(s * LANES, (s + 1) * LANES)
        gt2 = mod_ref[:, 5 * D + s * LANES:5 * D + (s + 1) * LANES]
        piece = pl.ds(s, n, stride=NSUB)
        moe = w0 * ybuf[slot, 0, piece, :] + w1 * ybuf[slot, 1, piece, :]
        o_ref[:, cols] = x1_ref[:, cols] + gt2 * moe


def _combine(x1, y, info, dest0, dest1, mod_l, S, tm):
    T, D = x1.shape
    tiles_per_batch = S // tm
    grid_spec = pltpu.PrefetchScalarGridSpec(
        num_scalar_prefetch=2,
        grid=(T // tm,),
        in_specs=[
            pl.BlockSpec(memory_space=pl.ANY),
            pl.BlockSpec((tm, D), lambda i, d0, d1: (i, 0)),
            pl.BlockSpec((tm, LANES), lambda i, d0, d1: (i, 0)),
            pl.BlockSpec((None, 1, 6 * D), lambda i, d0, d1: (i // tiles_per_batch, 0, 0)),
        ],
        out_specs=pl.BlockSpec((tm, D), lambda i, d0, d1: (i, 0)),
        scratch_shapes=[pltpu.VMEM((2, TOP_K, tm * NSUB, LANES), F32),
                        pltpu.SemaphoreType.DMA((2,))],
    )
    return pl.pallas_call(
        _combine_kernel,
        grid_spec=grid_spec,
        out_shape=jax.ShapeDtypeStruct((T, D), F32),
        compiler_params=pltpu.CompilerParams(
            dimension_semantics=("arbitrary",), vmem_limit_bytes=VMEM_LIMIT),
        name="moe_combine",
    )(dest0, dest1, y, x1, info, mod_l)


def _pad_w_in(w):
    D = w.shape[0]
    c0 = A_COLS + R_COLS
    c_ad = c0 + 2 * G_KW + G_VW
    c_gg = c_ad + G_LORA
    pad = jnp.zeros((D, LANES - G_LORA), w.dtype)
    return jnp.concatenate([w[:, :c_gg], pad, w[:, c_gg:]], axis=1).astype(BF16)


def kernel(x, c, ada_w, ada_b, norm1_g, norm2_g, w_in, attn_qn_g, attn_kn_g, attn_lambda,
           attn_subln_g, rwkv_mu, rwkv_w_up, rwkv_w0, rwkv_a_up, rwkv_a0, rwkv_g_up, rwkv_k_k,
           rwkv_k_a, rwkv_r_k, rwkv_lnx_g, rwkv_lnx_b, gla_alpha_up, gla_alpha_b, gla_norm_g,
           proj_attn, proj_rwkv, proj_gla, w_out, router_grp_w, router_grp_b, router_exp_w,
           router_exp_b, exp_w_gate, exp_w_up, exp_w_down):
    B, S, D = x.shape
    T = B * S
    L = ada_w.shape[0]
    tm = 256
    tm_mm = 512
    mod = _adaln(c, ada_w, ada_b).reshape(L, B, 1, 6 * D)
    x2 = x.reshape(T, D)
    xin = None
    for l in range(L):
        lambda_init = 0.8 - 0.6 * math.exp(-0.3 * l)
        p_attn, p_rwkv, p_gla, p_gate = _inproj(x2, mod[l], norm1_g[l], _pad_w_in(w_in[l]), S, tm_mm)
        o_a = _attention(p_attn, attn_qn_g[l], attn_kn_g[l], attn_lambda[l], attn_subln_g[l],
                         lambda_init, B, S)
        o_r = _rwkv(p_rwkv, rwkv_mu[l], rwkv_w_up[l], rwkv_w0[l], rwkv_a_up[l], rwkv_a0[l],
                    rwkv_g_up[l], rwkv_k_k[l], rwkv_k_a[l], rwkv_r_k[l], rwkv_lnx_g[l],
                    rwkv_lnx_b[l], B, S)
        o_g = _gla(p_gla, gla_alpha_up[l], gla_alpha_b[l], gla_norm_g[l], B, S)
        n_r = N_GROUPS + N_EXPERTS
        router_w = jnp.concatenate(
            [router_grp_w[l], router_exp_w[l], jnp.zeros((D, LANES - n_r), F32)], axis=1)
        rw_hi = router_w.astype(BF16)
        router_w = jnp.stack([rw_hi, (router_w - rw_hi.astype(F32)).astype(BF16)])
        router_b = jnp.concatenate(
            [router_grp_b[l], router_exp_b[l], jnp.zeros((LANES - n_r,), F32)]).reshape(1, LANES)
        x1, h2, logits = _merge(
            x2, o_a, o_r, o_g, p_gate, proj_attn[l].astype(BF16), proj_rwkv[l].astype(BF16),
            proj_gla[l].astype(BF16), w_out[l].astype(BF16), mod[l], norm2_g[l],
            router_w, router_b, S, tm_mm)
        info, dest0, dest1, blk_exp, n_used, blk_grp, blk_nxt, n_rows = _route(logits)
        if xin is None:
            xin = jnp.zeros((n_rows * NSUB, LANES), F32)
        xin = _dispatch(h2, dest0, dest1, xin, tm)
        y = _moe(xin, blk_exp, n_used, blk_grp, blk_nxt, exp_w_gate, exp_w_up, exp_w_down, l)
        x2 = _combine(x1, y, info, dest0, dest1, mod[l], S, tm)
    return x2.reshape(B, S, D)
```

```python
import functools
import math

import jax
import jax.numpy as jnp
from jax import lax
from jax.experimental import pallas as pl
from jax.experimental.pallas import tpu as pltpu

F32 = jnp.float32
BF16 = jnp.bfloat16
HIGHEST = lax.Precision.HIGHEST

D_MODEL = 1024
A_HEADS, A_DH, A_DV = 4, 64, 128
A_QW, A_VW = 512, 512
A_COLS = 1536
R_HEADS, R_N, R_W = 8, 64, 512
R_COLS = 1792
RWKV_GN_EPS = 64e-5
G_HEADS, G_DK, G_DV = 4, 64, 128
G_KW, G_VW, G_LORA = 256, 512, 16
G_TAU = 16.0
G_COLS = 1552
G_COLS_PAD = 1664
GATE_COLS = 3072
N_GROUPS, EXP_PER_GROUP, N_EXPERTS, TOP_K = 4, 8, 32, 2
D_EXPERT = 512
EPS = 1e-6

LANES = 128
SUBLANES = 8
NSUB = D_MODEL // LANES
CHUNK = 64
VMEM_LIMIT = 56 * 1024 * 1024


def _dot(a, b):
    return jnp.dot(a.astype(BF16), b.astype(BF16), preferred_element_type=F32)


def _dot_hi(a, b):
    return jnp.dot(a, b, precision=HIGHEST, preferred_element_type=F32)


def _dot_nt(a, b, precision=None):
    return lax.dot_general(a, b, (((1,), (1,)), ((), ())), precision=precision,
                           preferred_element_type=F32)


def _dot_tn(a, b, precision=None):
    return lax.dot_general(a, b, (((0,), (0,)), ((), ())), precision=precision,
                           preferred_element_type=F32)


def _cumsum3(tri, x):
    x1 = x.astype(BF16)
    res = x - x1.astype(F32)
    x2 = res.astype(BF16)
    x3 = (res - x2.astype(F32)).astype(BF16)
    return (jnp.dot(tri, x1, preferred_element_type=F32)
            + jnp.dot(tri, x2, preferred_element_type=F32)
            + jnp.dot(tri, x3, preferred_element_type=F32))


def _sigmoid(x):
    return 1.0 / (1.0 + jnp.exp(-x))


def _softplus(x):
    return jnp.maximum(x, 0.0) + jnp.log(1.0 + jnp.exp(-jnp.abs(x)))


def _seg_ones(n, seg):
    r = lax.broadcasted_iota(jnp.int32, (n, n), 0) // seg
    c = lax.broadcasted_iota(jnp.int32, (n, n), 1) // seg
    return (r == c).astype(F32)


def _tri(n, strict):
    r = lax.broadcasted_iota(jnp.int32, (n, n), 0)
    c = lax.broadcasted_iota(jnp.int32, (n, n), 1)
    return (c < r) if strict else (c <= r)


def _adaln_kernel(c_ref, w_ref, b_ref, o_ref):
    c = c_ref[...]
    c_act = c * _sigmoid(c)
    o_ref[...] = _dot_hi(c_act, w_ref[...]) + b_ref[...]


def _adaln(c, ada_w, ada_b):
    L, D, N = ada_w.shape
    B = c.shape[0]
    tn = D
    return pl.pallas_call(
        _adaln_kernel,
        grid=(L, N // tn),
        in_specs=[
            pl.BlockSpec((B, D), lambda l, j: (0, 0)),
            pl.BlockSpec((None, D, tn), lambda l, j: (l, 0, j)),
            pl.BlockSpec((None, 1, tn), lambda l, j: (l, 0, j)),
        ],
        out_specs=pl.BlockSpec((None, B, tn), lambda l, j: (l, 0, j)),
        out_shape=jax.ShapeDtypeStruct((L, B, N), F32),
        name="adaln",
    )(c, ada_w, ada_b.reshape(L, 1, N))


_IN_SEGS = (A_COLS, R_COLS, G_COLS_PAD, GATE_COLS)
_IN_DTYPES = (BF16, F32, F32, BF16)
_IN_CHUNK = 512


def _inproj_kernel(x_ref, mod_ref, g_ref, w_ref, *o_refs):
    x = x_ref[...]
    D = x.shape[-1]
    ms = jnp.mean(x * x, axis=-1, keepdims=True)
    y = x * lax.rsqrt(ms + EPS) * g_ref[...]
    sh = mod_ref[:, 0:D]
    sc = mod_ref[:, D:2 * D]
    h = (y * (1.0 + sc) + sh).astype(BF16)
    base = 0
    for o_ref, width in zip(o_refs, _IN_SEGS):
        for c0 in range(0, width, _IN_CHUNK):
            c1 = min(c0 + _IN_CHUNK, width)
            o_ref[:, c0:c1] = jnp.dot(
                h, w_ref[:, base + c0:base + c1], preferred_element_type=F32
            ).astype(o_ref.dtype)
        base += width


def _inproj(x2, mod_l, norm_g, w_pad, S, tm):
    T, D = x2.shape
    NP = w_pad.shape[1]
    tiles_per_batch = S // tm
    return pl.pallas_call(
        _inproj_kernel,
        grid=(T // tm,),
        in_specs=[
            pl.BlockSpec((tm, D), lambda i: (i, 0)),
            pl.BlockSpec((None, 1, 2 * D), lambda i: (i // tiles_per_batch, 0, 0)),
            pl.BlockSpec((1, D), lambda i: (0, 0)),
            pl.BlockSpec((D, NP), lambda i: (0, 0), pipeline_mode=pl.Buffered(1)),
        ],
        out_specs=[pl.BlockSpec((tm, w), lambda i: (i, 0)) for w in _IN_SEGS],
        out_shape=[jax.ShapeDtypeStruct((T, w), dt) for w, dt in zip(_IN_SEGS, _IN_DTYPES)],
        compiler_params=pltpu.CompilerParams(
            dimension_semantics=("arbitrary",), vmem_limit_bytes=VMEM_LIMIT),
        name="inproj",
    )(x2, mod_l, norm_g.reshape(1, D), w_pad)


A_TILE = 256
A_POS_SPLIT = 64
LOG2E = math.log2(math.e)


def _split_dot(x, ones):
    hi = x.astype(BF16)
    lo = (x - hi.astype(F32)).astype(BF16)
    return (jnp.dot(hi, ones, preferred_element_type=F32)
            + jnp.dot(lo, ones, preferred_element_type=F32))


def _eye(n):
    r = lax.broadcasted_iota(jnp.int32, (n, n), 0)
    c = lax.broadcasted_iota(jnp.int32, (n, n), 1)
    return (r == c).astype(BF16)


def _attn_kernel(q_ref, k_ref, v_ref, qg_ref, kg_ref, lam_ref, sg_ref, qaug_ref, kaug_ref, o_ref,
                 qt_s, ka_s, vt_s, s_s, *, S, lambda_init):
    t = A_TILE
    seg = _seg_ones(LANES, A_DH).astype(BF16)
    eye_t = _eye(t)
    eye_l = _eye(LANES)
    is_qk = lax.broadcasted_iota(jnp.int32, (t, LANES), 1) < A_DH

    blocks = [slice(b * t, (b + 1) * t) for b in range(S // t)]
    sides = ((q_ref, qg_ref, A_DH ** -0.5 * LOG2E), (k_ref, kg_ref, 1.0))
    xf = [[ref[rows, :].astype(F32) for rows in blocks] for ref, _, _ in sides]
    ms = [[jnp.dot((x * x).astype(BF16), seg, preferred_element_type=F32) for x in xs]
          for xs in xf]
    qn, kn = [[x * lax.rsqrt(m * (1.0 / A_DH) + EPS) * g_ref[...] * scale
               for x, m in zip(xs, mss)] for xs, mss, (_, g_ref, scale) in zip(xf, ms, sides)]
    for b, rows in enumerate(blocks):
        vt_s[:, rows] = _dot_nt(eye_l, v_ref[rows, :]).astype(BF16)
        for c in range(2):
            kc = kn[b] if c == 0 else pltpu.roll(kn[b], A_DH, 1)
            ka_s[c, rows, :] = jnp.where(is_qk, kc.astype(BF16), kaug_ref[rows, :])
    qas = [[jnp.where(is_qk, (qn[b] if c == 0 else pltpu.roll(qn[b], A_DH, 1)).astype(BF16),
                      qaug_ref[rows, :]) for c in range(2)] for b, rows in enumerate(blocks)]
    qts = [[_dot_nt(eye_l, qa) for qa in pair] for pair in qas]
    for b, rows in enumerate(blocks):
        for c in range(2):
            qt_s[c, :, rows] = qts[b][c].astype(BF16)

    lv = lam_ref[...]
    lam = (jnp.exp(jnp.sum(lv[0:1] * lv[1:2], axis=-1, keepdims=True))
           - jnp.exp(jnp.sum(lv[2:3] * lv[3:4], axis=-1, keepdims=True)) + lambda_init)
    causal = (lax.broadcasted_iota(jnp.int32, (t, t), 0)
              <= lax.broadcasted_iota(jnp.int32, (t, t), 1))

    nt = S // t

    def score_block(i, c, j, st):
        s = jnp.dot(ka_s[c, j * t:(j + 1) * t, :], qt_s[c, :, i * t:(i + 1) * t],
                    preferred_element_type=F32)
        if j == i:
            s = jnp.where(causal, s, -jnp.inf)
        s_s[i % 2, c, j * t:(j + 1) * t, :] = s
        mj = jnp.max(s, axis=0, keepdims=True)
        st["m"][c] = mj if st["m"][c] is None else jnp.maximum(st["m"][c], mj)

    def value_block(i, c, j, st):
        p = jnp.exp2(s_s[i % 2, c, j * t:(j + 1) * t, :] - st["m"][c])
        st["l"][c] = st["l"][c] + jnp.sum(p, axis=0, keepdims=True)
        st["acc"][c] = st["acc"][c] + jnp.dot(vt_s[:, j * t:(j + 1) * t], p.astype(BF16),
                                              preferred_element_type=F32)

    def finish(i, st):
        o = st["acc"][0] / st["l"][0] - lam * (st["acc"][1] / st["l"][1])
        ms = jnp.mean(o * o, axis=0, keepdims=True)
        o = o * lax.rsqrt(ms + EPS) * sg_ref[...] * (1.0 - lambda_init)
        o_ref[i * t:(i + 1) * t, :] = _dot_nt(eye_t, o.astype(BF16)).astype(o_ref.dtype)

    prev = None
    for i in range(nt + 1):
        cur = None
        first = []
        if i < nt:
            cur = dict(m=[None, None], l=[jnp.zeros((1, t), F32)] * 2,
                       acc=[jnp.zeros((A_DV, t), F32)] * 2)
            first = [(c, j) for j in range(i + 1) for c in range(2)]
        second = [(c, j) for j in range(i) for c in range(2)] if prev is not None else []
        for n in range(max(len(first), len(second))):
            if n < len(first):
                score_block(i, first[n][0], first[n][1], cur)
            if n < len(second):
                value_block(i - 1, second[n][0], second[n][1], prev)
        if prev is not None:
            finish(i - 1, prev)
        prev = cur


def _alibi_columns(S):
    pos = jnp.arange(S, dtype=jnp.int32)[:, None]
    hi = (pos // A_POS_SPLIT).astype(F32)
    lo = (pos % A_POS_SPLIT).astype(F32)
    kl = jnp.arange(LANES, dtype=jnp.int32)[None, :] - A_DH
    term = kl // 2
    used = jnp.logical_and(kl >= 0, kl < 8)
    k_aug = jnp.where(used, jnp.where(term == 0, hi, jnp.where(term == 1, lo, 1.0)), 0.0)
    q_augs = []
    for h in range(A_HEADS):
        s2 = 2.0 ** (-8.0 * (h + 1) / A_HEADS) * LOG2E
        qv = jnp.where(term == 0, A_POS_SPLIT * s2,
             jnp.where(term == 1, s2,
             jnp.where(term == 2, -A_POS_SPLIT * s2 * hi, -s2 * lo)))
        qv_hi = qv.astype(BF16).astype(F32)
        q_augs.append(jnp.where(used, jnp.where(kl % 2 == 0, qv_hi, qv - qv_hi), 0.0))
    return jnp.stack(q_augs).astype(BF16), k_aug.astype(BF16)


def _attention(p_attn, qn_g, kn_g, lam_vecs, subln_g, lambda_init, B, S):
    pa = p_attn.reshape(B, S, A_COLS)
    dup = lambda g: jnp.concatenate([g, g]).reshape(1, LANES)
    q_aug, k_aug = _alibi_columns(S)
    nqb = A_QW // LANES
    kern = functools.partial(_attn_kernel, S=S, lambda_init=lambda_init)
    out = pl.pallas_call(
        kern,
        grid=(B, A_HEADS),
        in_specs=[
            pl.BlockSpec((None, S, LANES), lambda b, h: (b, 0, h)),
            pl.BlockSpec((None, S, LANES), lambda b, h: (b, 0, nqb + h)),
            pl.BlockSpec((None, S, LANES), lambda b, h: (b, 0, 2 * nqb + h)),
            pl.BlockSpec((1, LANES), lambda b, h: (0, 0)),
            pl.BlockSpec((1, LANES), lambda b, h: (0, 0)),
            pl.BlockSpec((4, A_DH), lambda b, h: (0, 0)),
            pl.BlockSpec((A_DV, 1), lambda b, h: (0, 0)),
            pl.BlockSpec((None, S, LANES), lambda b, h: (h, 0, 0)),
            pl.BlockSpec((S, LANES), lambda b, h: (0, 0)),
        ],
        out_specs=pl.BlockSpec((None, S, A_DV), lambda b, h: (b, 0, h)),
        out_shape=jax.ShapeDtypeStruct((B, S, A_VW), BF16),
        scratch_shapes=[pltpu.VMEM((2, LANES, S), BF16), pltpu.VMEM((2, S, LANES), BF16),
                        pltpu.VMEM((A_DV, S), BF16), pltpu.VMEM((2, 2, S, A_TILE), F32)],
        compiler_params=pltpu.CompilerParams(
            dimension_semantics=("arbitrary", "arbitrary"), vmem_limit_bytes=VMEM_LIMIT),
        name="diff_attn",
    )(pa, pa, pa, dup(qn_g), dup(kn_g), lam_vecs, subln_g.reshape(A_DV, 1), q_aug, k_aug)
    return out.reshape(B * S, A_VW)


R_GROUP = 4
R_GW = R_GROUP * R_N


def _rwkv_kernel(p_ref, mu_ref, wup_ref, w0_ref, aup_ref, a0_ref, gup_ref, kk_ref, ka_ref,
                 rk_ref, lg_ref, lb_ref, o_ref,
                 carry_ref, st_ref, al_s, be_s, ka_s, rh_s, bt_s, kt_s, v_s, gc_s, y_s, *, TB):
    @pl.when(pl.program_id(1) == 0)
    def _():
        carry_ref[...] = jnp.zeros_like(carry_ref)
        st_ref[...] = jnp.zeros_like(st_ref)

    xs = p_ref[...]
    prev = pltpu.roll(xs, 1, 0)
    row = lax.broadcasted_iota(jnp.int32, (TB, 1), 0)
    prev = jnp.where(row == 0, carry_ref[...], prev)
    carry_ref[...] = xs[TB - 1:TB, :]
    xm = xs + (prev - xs) * mu_ref[...]
    r = xm[:, 0:R_W]
    k = xm[:, R_W:2 * R_W]
    v = xm[:, 2 * R_W:3 * R_W]
    wa = xm[:, 3 * R_W:3 * R_W + LANES]
    gd = xm[:, 3 * R_W + LANES:3 * R_W + 2 * LANES]
    wz = w0_ref[...] + _dot(jnp.tanh(wa), wup_ref[...])
    lw = -math.exp(-0.5) * _sigmoid(wz)
    a = _sigmoid(a0_ref[...] + _dot(wa, aup_ref[...]))
    g = _dot(_sigmoid(gd), gup_ref[...])
    seg = _seg_ones(R_GW, R_N).astype(BF16)

    def head_sum(x, passes):
        f = _split_dot if passes == 2 else (
            lambda t, o: jnp.dot(t.astype(BF16), o, preferred_element_type=F32))
        return jnp.concatenate(
            [f(x[:, j * R_GW:(j + 1) * R_GW], seg) for j in range(R_W // R_GW)], axis=-1)

    kk = k * kk_ref[...]
    kk = kk * jnp.minimum(lax.rsqrt(head_sum(kk * kk, 1)), 1e12)
    k2 = k * (1.0 + (a - 1.0) * ka_ref[...])
    bonus = head_sum(r * k2 * rk_ref[...], 2) * v
    bv = kk * a

    rr = lax.broadcasted_iota(jnp.int32, (TB, TB), 0)
    cc = lax.broadcasted_iota(jnp.int32, (TB, TB), 1)
    tril_blk = jnp.logical_and(rr // CHUNK == cc // CHUNK, cc <= rr).astype(BF16)
    Lg = _cumsum3(tril_blk, lw)
    inv = jnp.exp(-Lg)
    al_s[...] = (jnp.exp(Lg - lw) * kk).astype(BF16)
    be_s[...] = (bv * inv).astype(BF16)
    ka_s[...] = (k2 * inv).astype(BF16)
    rh_s[...] = (jnp.exp(Lg) * r).astype(BF16)
    v_s[...] = v.astype(BF16)
    for c in range(TB // CHUNK):
        rows = slice(c * CHUNK, (c + 1) * CHUNK)
        gC = Lg[(c + 1) * CHUNK - 1:(c + 1) * CHUNK, :]
        tail = jnp.exp(gC - Lg[rows, :])
        bt_s[rows, :] = (bv[rows, :] * tail).astype(BF16)
        kt_s[rows, :] = (k2[rows, :] * tail).astype(BF16)
        gc_s[c * SUBLANES:(c + 1) * SUBLANES, :] = jnp.broadcast_to(jnp.exp(gC), (SUBLANES, R_W))

    ri = lax.broadcasted_iota(jnp.int32, (R_GW, R_GW), 0)
    ci = lax.broadcasted_iota(jnp.int32, (R_GW, R_GW), 1)
    blk = ri // R_N == ci // R_N
    strict = ci % R_N < ri % R_N
    strict_t = ri % R_N < ci % R_N
    incl = ci % R_N <= ri % R_N
    zero = jnp.zeros((), BF16)

    def expand(x):
        return jnp.where(blk, jnp.concatenate([x] * R_GROUP, axis=0), zero)

    n_groups = R_HEADS // R_GROUP
    pair = 2

    def chunk_pair(ip, _):
        chains = []
        for dc in range(pair):
            c = ip * pair + dc
            rows = pl.ds(pl.multiple_of(c * CHUNK, CHUNK), CHUNK)
            for gi in range(n_groups):
                chains.append(dict(c=c, rows=rows, gi=gi, cols=slice(gi * R_GW, (gi + 1) * R_GW)))
        for ch in chains:
            rows, cols = ch["rows"], ch["cols"]
            ch["A"] = expand(al_s[rows, cols])
            ch["R"] = expand(rh_s[rows, cols])
            ch["B"] = expand(be_s[rows, cols])
            ch["K"] = expand(ka_s[rows, cols])
            vc = v_s[rows, cols]
            ch["V"] = jnp.concatenate(
                [vc[:, h * R_N:(h + 1) * R_N] for h in range(R_GROUP)], axis=0)
        for ch in chains:
            ch["X"] = -jnp.where(strict, _dot_nt(ch["A"], ch["B"]), 0.0)
            ch["m_ak_t"] = jnp.where(strict_t, _dot_nt(ch["K"], ch["A"]), 0.0).astype(BF16)
            ch["m_rb"] = jnp.where(incl, _dot_nt(ch["R"], ch["B"]), 0.0).astype(BF16)
            ch["m_rk"] = jnp.where(incl, _dot_nt(ch["R"], ch["K"]), 0.0).astype(BF16)
        for ch in chains:
            ch["Z"] = ch["A"].astype(F32)
            ch["Wt"] = _dot_tn(ch["V"], ch["m_ak_t"])
        n = 1
        while True:
            last = 2 * n >= CHUNK
            for ch in chains:
                Xb = ch["X"].astype(BF16)
                ch["Z"] = ch["Z"] + jnp.dot(Xb, ch["Z"].astype(BF16), preferred_element_type=F32)
                ch["Wt"] = ch["Wt"] + _dot_nt(ch["Wt"].astype(BF16), Xb)
                if not last:
                    ch["X"] = jnp.dot(Xb, Xb, preferred_element_type=F32)
            n *= 2
            if last:
                break
        for ch in chains:
            rows, cols = ch["rows"], ch["cols"]
            Bt = expand(bt_s[rows, cols])
            Kt = expand(kt_s[rows, cols])
            Zb = ch["Z"].astype(BF16)
            Wtb = ch["Wt"].astype(BF16)
            ch["y_a"] = (ch["R"].astype(F32)
                         - jnp.dot(ch["m_rb"], Zb, preferred_element_type=F32)).astype(BF16)
            ch["y_b"] = (jnp.dot(ch["m_rk"], ch["V"], preferred_element_type=F32)
                         - _dot_nt(ch["m_rb"], Wtb))
            ch["p_neg"] = _dot_tn(Zb, Bt).astype(BF16)
            ch["q"] = _dot_tn(ch["V"], Kt) - jnp.dot(Wtb, Bt, preferred_element_type=F32)
        for ch in chains:
            rows, cols, gi = ch["rows"], ch["cols"], ch["gi"]
            g0 = pl.multiple_of(ch["c"] * SUBLANES, SUBLANES)
            S0 = st_ref[gi]
            S0b = S0.astype(BF16)
            y = _dot_nt(ch["y_a"], S0b) + ch["y_b"]
            st_ref[gi] = (S0 * gc_s[pl.ds(g0, SUBLANES), cols][0:1, :]
                          - jnp.dot(S0b, ch["p_neg"], preferred_element_type=F32) + ch["q"])
            for h in range(R_GROUP):
                hh = gi * R_GROUP + h
                y_s[rows, hh * R_N:(hh + 1) * R_N] = y[h * R_N:(h + 1) * R_N, :]
        return 0

    lax.fori_loop(0, TB // CHUNK // pair, chunk_pair, 0)

    y = y_s[...]
    mean = head_sum(y, 2) * (1.0 / R_N)
    yc = y - mean
    var = head_sum(yc * yc, 1) * (1.0 / R_N)
    yn = yc * lax.rsqrt(var + RWKV_GN_EPS) * lg_ref[...] + lb_ref[...]
    o_ref[...] = ((yn + bonus) * g).astype(o_ref.dtype)


def _rwkv(p_rwkv, mu, w_up, w0, a_up, a0, g_up, k_k, k_a, r_k, lnx_g, lnx_b, B, S):
    TB = 256
    T = B * S
    nt = S // TB
    row = lambda t: t.reshape(1, -1)
    zeros = jnp.zeros((R_N, R_W), F32)
    wup_pad = jnp.concatenate([w_up, zeros], axis=0)
    aup_pad = jnp.concatenate([zeros, a_up], axis=0)
    vec = lambda n: pl.BlockSpec((1, n), lambda b, i: (0, 0))
    mat = lambda m, n: pl.BlockSpec((m, n), lambda b, i: (0, 0))
    kern = functools.partial(_rwkv_kernel, TB=TB)
    return pl.pallas_call(
        kern,
        grid=(B, nt),
        in_specs=[
            pl.BlockSpec((TB, R_COLS), lambda b, i: (b * nt + i, 0)),
            vec(R_COLS), mat(LANES, R_W), vec(R_W), mat(LANES, R_W), vec(R_W), mat(LANES, R_W),
            vec(R_W), vec(R_W), vec(R_W), vec(R_W), vec(R_W),
        ],
        out_specs=pl.BlockSpec((TB, R_W), lambda b, i: (b * nt + i, 0)),
        out_shape=jax.ShapeDtypeStruct((T, R_W), BF16),
        scratch_shapes=[
            pltpu.VMEM((1, R_COLS), F32),
            pltpu.VMEM((R_HEADS // R_GROUP, R_N, R_GW), F32),
        ] + [pltpu.VMEM((TB, R_W), BF16)] * 7 + [
            pltpu.VMEM((TB // CHUNK * SUBLANES, R_W), F32),
            pltpu.VMEM((TB, R_W), F32),
        ],
        compiler_params=pltpu.CompilerParams(
            dimension_semantics=("arbitrary", "arbitrary"), vmem_limit_bytes=VMEM_LIMIT),
        name="rwkv7",
    )(p_rwkv, row(mu), wup_pad, row(w0), aup_pad, row(a0), g_up, row(k_k), row(k_a),
      row(r_k), row(lnx_g), row(lnx_b))


def _gla_kernel(p_ref, aup_ref, ab_ref, ng_ref, o_ref, st_ref, *, TB):
    @pl.when(pl.program_id(1) == 0)
    def _():
        st_ref[...] = jnp.zeros_like(st_ref)

    c_gv = 2 * G_KW
    c_ad = c_gv + G_VW
    c_gate = c_ad + LANES
    nchunk = TB // CHUNK
    rr = lax.broadcasted_iota(jnp.int32, (TB, TB), 0)
    cc = lax.broadcasted_iota(jnp.int32, (TB, TB), 1)
    causal = jnp.logical_and(rr // CHUNK == cc // CHUNK, cc <= rr)

    q = p_ref[:, 0:G_KW] * (G_DK ** -0.5)
    k = p_ref[:, G_KW:2 * G_KW]
    vb = p_ref[:, c_gv:c_gv + G_VW].astype(BF16)
    z = _dot(p_ref[:, c_ad:c_ad + LANES], aup_ref[...]) + ab_ref[...]
    la = -_softplus(-z) * (1.0 / G_TAU)
    b = _cumsum3(causal.astype(BF16), la)
    qe = (q * jnp.exp(b)).astype(BF16)
    ke = (k * jnp.exp(-b)).astype(BF16)
    kts, e_lasts = [], []
    for c in range(nchunk):
        rows = slice(c * CHUNK, (c + 1) * CHUNK)
        b_last = b[(c + 1) * CHUNK - 1:(c + 1) * CHUNK, :]
        kts.append((k[rows, :] * jnp.exp(b_last - b[rows, :])).astype(BF16))
        e_lasts.append(jnp.exp(b_last))
    heads = range(G_HEADS)
    sls = [slice(h * G_DK, (h + 1) * G_DK) for h in heads]
    vss = [slice(h * G_DV, (h + 1) * G_DV) for h in heads]
    scs = [jnp.where(causal, _dot_nt(qe[:, sls[h]], ke[:, sls[h]]), 0.0) for h in heads]
    o_intra = [_dot(scs[h], vb[:, vss[h]]) for h in heads]
    kvs = [[_dot_tn(vb[c * CHUNK:(c + 1) * CHUNK, vss[h]], kts[c][:, sls[h]]) for h in heads]
           for c in range(nchunk)]
    states = [st_ref[h] for h in heads]
    parts = [[] for _ in heads]
    for c in range(nchunk):
        rows = slice(c * CHUNK, (c + 1) * CHUNK)
        for h in heads:
            parts[h].append(o_intra[h][rows, :] + _dot_nt(qe[rows, sls[h]], states[h].astype(BF16)))
            states[h] = states[h] * e_lasts[c][:, sls[h]] + kvs[c][h]
    for h in heads:
        vs = vss[h]
        st_ref[h] = states[h]
        o = jnp.concatenate(parts[h], axis=0)
        ms = jnp.mean(o * o, axis=-1, keepdims=True)
        gt = p_ref[:, c_gate + h * G_DV:c_gate + (h + 1) * G_DV]
        o = o * lax.rsqrt(ms + EPS) * ng_ref[...] * (gt * _sigmoid(gt))
        o_ref[:, vs] = o.astype(o_ref.dtype)


def _gla(p_gla, alpha_up, alpha_b, norm_g, B, S):
    TB = 256
    T = B * S
    nt = S // TB
    aup_pad = jnp.concatenate([alpha_up, jnp.zeros((LANES - G_LORA, G_KW), F32)], axis=0)
    kern = functools.partial(_gla_kernel, TB=TB)
    return pl.pallas_call(
        kern,
        grid=(B, nt),
        in_specs=[
            pl.BlockSpec((TB, G_COLS_PAD), lambda b, i: (b * nt + i, 0)),
            pl.BlockSpec((LANES, G_KW), lambda b, i: (0, 0)),
            pl.BlockSpec((1, G_KW), lambda b, i: (0, 0)),
            pl.BlockSpec((1, G_DV), lambda b, i: (0, 0)),
        ],
        out_specs=pl.BlockSpec((TB, G_VW), lambda b, i: (b * nt + i, 0)),
        out_shape=jax.ShapeDtypeStruct((T, G_VW), BF16),
        scratch_shapes=[pltpu.VMEM((G_HEADS, G_DV, G_DK), F32)],
        compiler_params=pltpu.CompilerParams(
            dimension_semantics=("arbitrary", "arbitrary"), vmem_limit_bytes=VMEM_LIMIT),
        name="gla",
    )(p_gla, aup_pad, alpha_b.reshape(1, G_KW), norm_g.reshape(1, G_DV))


def _merge_kernel(x_ref, oa_ref, or_ref, og_ref, gate_ref, pa_ref, pr_ref, pg_ref, wo_ref,
                  mod_ref, g2_ref, rw_ref, rb_ref, x1_ref, h2_ref, info_ref, cnt_ref, carry_ref):
    D = D_MODEL

    @pl.when(pl.program_id(0) == 0)
    def _():
        carry_ref[...] = jnp.zeros_like(carry_ref)

    merged = (_sigmoid(gate_ref[:, 0:D].astype(F32))
              * jnp.dot(oa_ref[...], pa_ref[...], preferred_element_type=F32)
              + _sigmoid(gate_ref[:, D:2 * D].astype(F32))
              * jnp.dot(or_ref[...], pr_ref[...], preferred_element_type=F32)
              + _sigmoid(gate_ref[:, 2 * D:3 * D].astype(F32))
              * jnp.dot(og_ref[...], pg_ref[...], preferred_element_type=F32))
    gt1 = mod_ref[:, 2 * D:3 * D]
    sh2 = mod_ref[:, 3 * D:4 * D]
    sc2 = mod_ref[:, 4 * D:5 * D]
    x1 = x_ref[...] + gt1 * jnp.dot(merged.astype(BF16), wo_ref[...], preferred_element_type=F32)
    x1_ref[...] = x1
    ms = jnp.mean(x1 * x1, axis=-1, keepdims=True)
    h2 = x1 * lax.rsqrt(ms + EPS) * g2_ref[...] * (1.0 + sc2) + sh2
    for s in range(NSUB):
        h2_ref[pl.ds(s, h2.shape[0], stride=NSUB), :] = h2[:, s * LANES:(s + 1) * LANES]
    h_hi = h2.astype(BF16)
    h_lo = (h2 - h_hi.astype(F32)).astype(BF16)
    logits = (jnp.dot(h_hi, rw_ref[0], preferred_element_type=F32)
              + jnp.dot(h_lo, rw_ref[0], preferred_element_type=F32)
              + jnp.dot(h_hi, rw_ref[1], preferred_element_type=F32) + rb_ref[...])
    info_ref[...], cnt_ref[...] = _route_tile(logits, carry_ref)


def _merge(x2, o_a, o_r, o_g, p_gate, proj_a, proj_r, proj_g, w_out, mod_l, norm2_g,
           router_w, router_b, S, tm):
    T, D = x2.shape
    tiles_per_batch = S // tm
    tile = lambda w: pl.BlockSpec((tm, w), lambda i: (i, 0))
    const = lambda m, n: pl.BlockSpec((m, n), lambda i: (0, 0))
    return pl.pallas_call(
        _merge_kernel,
        grid=(T // tm,),
        in_specs=[
            tile(D), tile(A_VW), tile(R_W), tile(G_VW), tile(GATE_COLS),
            const(A_VW, D), const(R_W, D), const(G_VW, D), const(D, D),
            pl.BlockSpec((None, 1, 6 * D), lambda i: (i // tiles_per_batch, 0, 0)),
            const(1, D), pl.BlockSpec((2, D, LANES), lambda i: (0, 0, 0)), const(1, LANES),
        ],
        out_specs=[
            tile(D),
            pl.BlockSpec((tm * NSUB, LANES), lambda i: (i, 0)),
            tile(LANES),
            const(1, LANES),
        ],
        out_shape=[
            jax.ShapeDtypeStruct((T, D), F32),
            jax.ShapeDtypeStruct((T * NSUB, LANES), F32),
            jax.ShapeDtypeStruct((T, LANES), F32),
            jax.ShapeDtypeStruct((1, LANES), F32),
        ],
        scratch_shapes=[pltpu.VMEM((1, LANES), F32)],
        compiler_params=pltpu.CompilerParams(
            dimension_semantics=("arbitrary",), vmem_limit_bytes=VMEM_LIMIT),
        name="merge",
    )(x2, o_a, o_r, o_g, p_gate, proj_a, proj_r, proj_g, w_out, mod_l,
      norm2_g.reshape(1, D), router_w, router_b)


MOE_ROWS = 256
E_LANE0 = N_GROUPS


def _route_tile(lg, carry_ref):
    n = lg.shape[0]
    lane = lax.broadcasted_iota(jnp.int32, (n, LANES), 1).astype(F32)
    neg = -jnp.inf
    big = float(LANES)

    def first_max(vals):
        m = jnp.max(vals, axis=-1, keepdims=True)
        idx = jnp.min(jnp.where(vals == m, lane, big), axis=-1, keepdims=True)
        return m, idx

    in_grp = lane < N_GROUPS
    gm, grp = first_max(jnp.where(in_grp, lg, neg))
    g_prob = 1.0 / jnp.sum(jnp.where(in_grp, jnp.exp(lg - gm), 0.0), axis=-1, keepdims=True)
    lo = E_LANE0 + grp * EXP_PER_GROUP
    el = jnp.where(jnp.logical_and(lane >= lo, lane < lo + EXP_PER_GROUP), lg, neg)
    v1, i1 = first_max(el)
    v2, i2 = first_max(jnp.where(lane == i1, neg, el))
    e21 = jnp.exp(v2 - v1)
    w0 = g_prob / (1.0 + e21)
    w1 = g_prob * e21 / (1.0 + e21)
    oh0 = lane == i1
    oh1 = lane == i2
    oh = jnp.logical_or(oh0, oh1).astype(F32)
    before = _tri(n, True).astype(BF16)
    cnt = jnp.dot(before, oh.astype(BF16), preferred_element_type=F32) + carry_ref[...]
    rank0 = jnp.sum(jnp.where(oh0, cnt, 0.0), axis=-1, keepdims=True)
    rank1 = jnp.sum(jnp.where(oh1, cnt, 0.0), axis=-1, keepdims=True)
    carry = carry_ref[...] + jnp.sum(oh, axis=0, keepdims=True)
    carry_ref[...] = carry
    cols = (i1 - E_LANE0, i2 - E_LANE0, rank0, rank1, w0, w1)
    info = jnp.zeros((n, LANES), F32)
    for j, col in enumerate(cols):
        info = jnp.where(lane == j, col, info)
    return info, carry


def _route_tables(info, cnt):
    T = info.shape[0]
    A = T * TOP_K
    counts = cnt[0, E_LANE0:E_LANE0 + N_EXPERTS].astype(jnp.int32)
    padded = (counts + MOE_ROWS - 1) // MOE_ROWS * MOE_ROWS
    pad_end = jnp.cumsum(padded)
    pad_start = pad_end - padded
    n_blocks = -(-A // MOE_ROWS) + N_EXPERTS
    eid = info[:, 0:2].astype(jnp.int32)
    is_e = eid[:, :, None] == jnp.arange(N_EXPERTS, dtype=jnp.int32)
    dest = (jnp.sum(jnp.where(is_e, pad_start, 0), axis=-1)
            + info[:, 2:4].astype(jnp.int32))
    blk_start = jnp.arange(n_blocks, dtype=jnp.int32) * MOE_ROWS
    blk_exp = jnp.minimum(jnp.sum(pad_end[None, :] <= blk_start[:, None], axis=1),
                          N_EXPERTS - 1).astype(jnp.int32)
    n_used = (pad_end[-1:] // MOE_ROWS).astype(jnp.int32)
    nonempty = counts > 0
    ids = jnp.arange(N_EXPERTS, dtype=jnp.int32)
    grp_of = jnp.cumsum(nonempty.astype(jnp.int32)) - 1
    later = jnp.logical_and(ids[None, :] > ids[:, None], nonempty[None, :])
    nxt_of = jnp.min(jnp.where(later, ids[None, :], N_EXPERTS), axis=1)
    nxt_of = jnp.where(nxt_of == N_EXPERTS, -1, nxt_of)
    is_b = blk_exp[:, None] == ids[None, :]
    blk_grp = jnp.sum(jnp.where(is_b, grp_of[None, :], 0), axis=1).astype(jnp.int32)
    blk_nxt = jnp.sum(jnp.where(is_b, nxt_of[None, :], 0), axis=1).astype(jnp.int32)
    return dest[:, 0], dest[:, 1], blk_exp, n_used, blk_grp, blk_nxt, n_blocks * MOE_ROWS


def _dispatch_kernel(d0_ref, d1_ref, h2_ref, xin_in, xin_hbm, sem):
    del xin_in
    n = h2_ref.shape[0] // NSUB
    base = pl.program_id(0) * n

    def slab(ref, row):
        return ref.at[pl.ds(pl.multiple_of(row * NSUB, NSUB), NSUB)]

    def body(r, _):
        pltpu.make_async_copy(slab(h2_ref, r), slab(xin_hbm, d0_ref[base + r]), sem).start(0)
        pltpu.make_async_copy(slab(h2_ref, r), slab(xin_hbm, d1_ref[base + r]), sem).start(1)
        return 0

    lax.fori_loop(0, n, body, 0, unroll=8)
    for _ in range(TOP_K):
        pltpu.make_async_copy(h2_ref, xin_hbm.at[pl.ds(0, n * NSUB)], sem).wait()


def _dispatch(h2_slab, dest0, dest1, xin_init, td):
    T = h2_slab.shape[0] // NSUB
    grid_spec = pltpu.PrefetchScalarGridSpec(
        num_scalar_prefetch=2,
        grid=(T // td,),
        in_specs=[pl.BlockSpec((td * NSUB, LANES), lambda i, d0, d1: (i, 0)),
                  pl.BlockSpec(memory_space=pl.ANY)],
        out_specs=pl.BlockSpec(memory_space=pl.ANY),
        scratch_shapes=[pltpu.SemaphoreType.DMA(())],
    )
    return pl.pallas_call(
        _dispatch_kernel,
        grid_spec=grid_spec,
        out_shape=jax.ShapeDtypeStruct(xin_init.shape, F32),
        input_output_aliases={3: 0},
        compiler_params=pltpu.CompilerParams(dimension_semantics=("arbitrary",)),
        name="moe_dispatch",
    )(dest0, dest1, h2_slab, xin_init)


def _moe_kernel(be_ref, nu_ref, grp_ref, nxt_ref, x_ref, wg_hbm, wu_hbm, wd_hbm, y_ref,
                wgf, wuf, wdf, wgb, wub, wdb, sem, *, layer):
    i = pl.program_id(0)

    def weight_copies(expert, slot):
        return [pltpu.make_async_copy(src.at[layer, expert], dst.at[slot], sem.at[slot, k])
                for k, (src, dst) in enumerate(((wg_hbm, wgf), (wu_hbm, wuf), (wd_hbm, wdf)))]

    @pl.when(i < nu_ref[0])
    def _():
        changed = jnp.logical_or(i == 0, be_ref[i] != be_ref[jnp.maximum(i - 1, 0)])

        @pl.when(changed)
        def _():
            slot = grp_ref[i] % 2

            @pl.when(i == 0)
            def _():
                for cp in weight_copies(be_ref[0], 0):
                    cp.start()

            for cp in weight_copies(be_ref[i], slot):
                cp.wait()
            wgb[...] = wgf[slot].astype(BF16)
            wub[...] = wuf[slot].astype(BF16)
            wdb[...] = wdf[slot].astype(BF16)

            @pl.when(nxt_ref[i] >= 0)
            def _():
                for cp in weight_copies(nxt_ref[i], 1 - slot):
                    cp.start()

        kw = 2 * LANES
        per = kw // LANES
        hg = hu = None
        for j in range(D_MODEL // kw):
            xj = jnp.concatenate(
                [x_ref[pl.ds(j * per + s, MOE_ROWS, stride=NSUB), :].astype(BF16)
                 for s in range(per)], axis=-1)
            dg = jnp.dot(xj, wgb[j * kw:(j + 1) * kw, :], preferred_element_type=F32)
            du = jnp.dot(xj, wub[j * kw:(j + 1) * kw, :], preferred_element_type=F32)
            hg = dg if hg is None else hg + dg
            hu = du if hu is None else hu + du
        hid = (hg * _sigmoid(hg) * hu).astype(BF16)
        for j in range(D_MODEL // kw):
            yj = jnp.dot(hid, wdb[:, j * kw:(j + 1) * kw], preferred_element_type=F32)
            for s in range(per):
                y_ref[pl.ds(j * per + s, MOE_ROWS, stride=NSUB), :] = yj[:, s * LANES:(s + 1) * LANES]

    @pl.when(i >= nu_ref[0])
    def _():
        y_ref[...] = jnp.zeros_like(y_ref)


def _moe(xin, blk_exp, n_used, blk_grp, blk_nxt, w_gate, w_up, w_down, layer):
    blk_rows = MOE_ROWS * NSUB
    n_blocks = xin.shape[0] // blk_rows
    last = lambda i, nu: jnp.minimum(i, nu[0] - 1)
    grid_spec = pltpu.PrefetchScalarGridSpec(
        num_scalar_prefetch=4,
        grid=(n_blocks,),
        in_specs=[
            pl.BlockSpec((blk_rows, LANES), lambda i, be, nu, gr, nx: (last(i, nu), 0)),
            pl.BlockSpec(memory_space=pl.ANY), pl.BlockSpec(memory_space=pl.ANY),
            pl.BlockSpec(memory_space=pl.ANY),
        ],
        out_specs=pl.BlockSpec((blk_rows, LANES), lambda i, be, nu, gr, nx: (i, 0)),
        scratch_shapes=[
            pltpu.VMEM((2, D_MODEL, D_EXPERT), F32),
            pltpu.VMEM((2, D_MODEL, D_EXPERT), F32),
            pltpu.VMEM((2, D_EXPERT, D_MODEL), F32),
            pltpu.VMEM((D_MODEL, D_EXPERT), BF16),
            pltpu.VMEM((D_MODEL, D_EXPERT), BF16),
            pltpu.VMEM((D_EXPERT, D_MODEL), BF16),
            pltpu.SemaphoreType.DMA((2, 3)),
        ],
    )
    return pl.pallas_call(
        functools.partial(_moe_kernel, layer=layer),
        grid_spec=grid_spec,
        out_shape=jax.ShapeDtypeStruct(xin.shape, F32),
        compiler_params=pltpu.CompilerParams(
            dimension_semantics=("arbitrary",), vmem_limit_bytes=VMEM_LIMIT),
        name="moe_ffn",
    )(blk_exp, n_used, blk_grp, blk_nxt, xin, w_gate, w_up, w_down)


def _combine_kernel(d0_ref, d1_ref, y_hbm, x1_ref, info_ref, mod_ref, o_ref, ybuf, sem):
    i = pl.program_id(0)
    nsteps = pl.num_programs(0)
    n = x1_ref.shape[0]
    slot = i % 2
    D = D_MODEL

    def slab(ref, row):
        return ref.at[pl.ds(pl.multiple_of(row * NSUB, NSUB), NSUB)]

    def start_gather(step, sl):
        base = step * n

        def body(r, _):
            pltpu.make_async_copy(
                slab(y_hbm, d0_ref[base + r]), slab(ybuf.at[sl, 0], r), sem.at[sl]).start(0)
            pltpu.make_async_copy(
                slab(y_hbm, d1_ref[base + r]), slab(ybuf.at[sl, 1], r), sem.at[sl]).start(1)
            return 0
        lax.fori_loop(0, n, body, 0, unroll=8)

    @pl.when(i == 0)
    def _():
        start_gather(0, 0)

    @pl.when(i + 1 < nsteps)
    def _():
        start_gather(i + 1, 1 - slot)

    for k in range(TOP_K):
        pltpu.make_async_copy(
            y_hbm.at[pl.ds(0, n * NSUB)], ybuf.at[slot, k], sem.at[slot]).wait()

    w0 = info_ref[:, 4:5]
    w1 = info_ref[:, 5:6]
    for s in range(NSUB):
        cols = slice(s * LANES, (s + 1) * LANES)
        gt2 = mod_ref[:, 5 * D + s * LANES:5 * D + (s + 1) * LANES]
        piece = pl.ds(s, n, stride=NSUB)
        moe = w0 * ybuf[slot, 0, piece, :] + w1 * ybuf[slot, 1, piece, :]
        o_ref[:, cols] = x1_ref[:, cols] + gt2 * moe


def _combine(x1, y, info, dest0, dest1, mod_l, S, tm):
    T, D = x1.shape
    tiles_per_batch = S // tm
    grid_spec = pltpu.PrefetchScalarGridSpec(
        num_scalar_prefetch=2,
        grid=(T // tm,),
        in_specs=[
            pl.BlockSpec(memory_space=pl.ANY),
            pl.BlockSpec((tm, D), lambda i, d0, d1: (i, 0)),
            pl.BlockSpec((tm, LANES), lambda i, d0, d1: (i, 0)),
            pl.BlockSpec((None, 1, 6 * D), lambda i, d0, d1: (i // tiles_per_batch, 0, 0)),
        ],
        out_specs=pl.BlockSpec((tm, D), lambda i, d0, d1: (i, 0)),
        scratch_shapes=[pltpu.VMEM((2, TOP_K, tm * NSUB, LANES), F32),
                        pltpu.SemaphoreType.DMA((2,))],
    )
    return pl.pallas_call(
        _combine_kernel,
        grid_spec=grid_spec,
        out_shape=jax.ShapeDtypeStruct((T, D), F32),
        compiler_params=pltpu.CompilerParams(
            dimension_semantics=("arbitrary",), vmem_limit_bytes=VMEM_LIMIT),
        name="moe_combine",
    )(dest0, dest1, y, x1, info, mod_l)


def _pad_w_in(w):
    D = w.shape[0]
    c0 = A_COLS + R_COLS
    c_ad = c0 + 2 * G_KW + G_VW
    c_gg = c_ad + G_LORA
    pad = jnp.zeros((D, LANES - G_LORA), w.dtype)
    return jnp.concatenate([w[:, :c_gg], pad, w[:, c_gg:]], axis=1).astype(BF16)


def kernel(x, c, ada_w, ada_b, norm1_g, norm2_g, w_in, attn_qn_g, attn_kn_g, attn_lambda,
           attn_subln_g, rwkv_mu, rwkv_w_up, rwkv_w0, rwkv_a_up, rwkv_a0, rwkv_g_up, rwkv_k_k,
           rwkv_k_a, rwkv_r_k, rwkv_lnx_g, rwkv_lnx_b, gla_alpha_up, gla_alpha_b, gla_norm_g,
           proj_attn, proj_rwkv, proj_gla, w_out, router_grp_w, router_grp_b, router_exp_w,
           router_exp_b, exp_w_gate, exp_w_up, exp_w_down):
    B, S, D = x.shape
    T = B * S
    L = ada_w.shape[0]
    tm = 256
    tm_mm = 512
    mod = _adaln(c, ada_w, ada_b).reshape(L, B, 1, 6 * D)
    x2 = x.reshape(T, D)
    xin = None
    for l in range(L):
        lambda_init = 0.8 - 0.6 * math.exp(-0.3 * l)
        p_attn, p_rwkv, p_gla, p_gate = _inproj(x2, mod[l], norm1_g[l], _pad_w_in(w_in[l]), S, tm_mm)
        o_a = _attention(p_attn, attn_qn_g[l], attn_kn_g[l], attn_lambda[l], attn_subln_g[l],
                         lambda_init, B, S)
        o_r = _rwkv(p_rwkv, rwkv_mu[l], rwkv_w_up[l], rwkv_w0[l], rwkv_a_up[l], rwkv_a0[l],
                    rwkv_g_up[l], rwkv_k_k[l], rwkv_k_a[l], rwkv_r_k[l], rwkv_lnx_g[l],
                    rwkv_lnx_b[l], B, S)
        o_g = _gla(p_gla, gla_alpha_up[l], gla_alpha_b[l], gla_norm_g[l], B, S)
        n_r = N_GROUPS + N_EXPERTS
        router_w = jnp.concatenate(
            [router_grp_w[l], router_exp_w[l], jnp.zeros((D, LANES - n_r), F32)], axis=1)
        rw_hi = router_w.astype(BF16)
        router_w = jnp.stack([rw_hi, (router_w - rw_hi.astype(F32)).astype(BF16)])
        router_b = jnp.concatenate(
            [router_grp_b[l], router_exp_b[l], jnp.zeros((LANES - n_r,), F32)]).reshape(1, LANES)
        x1, h2, info, cnt = _merge(
            x2, o_a, o_r, o_g, p_gate, proj_attn[l].astype(BF16), proj_rwkv[l].astype(BF16),
            proj_gla[l].astype(BF16), w_out[l].astype(BF16), mod[l], norm2_g[l],
            router_w, router_b, S, tm_mm)
        dest0, dest1, blk_exp, n_used, blk_grp, blk_nxt, n_rows = _route_tables(info, cnt)
        if xin is None:
            xin = jnp.zeros((n_rows * NSUB, LANES), F32)
        xin = _dispatch(h2, dest0, dest1, xin, tm)
        y = _moe(xin, blk_exp, n_used, blk_grp, blk_nxt, exp_w_gate, exp_w_up, exp_w_down, l)
        x2 = _combine(x1, y, info, dest0, dest1, mod[l], S, tm)
    return x2.reshape(B, S, D)
```

```python
import functools
import math

import jax
import jax.numpy as jnp
from jax import lax
from jax.experimental import pallas as pl
from jax.experimental.pallas import tpu as pltpu

F32 = jnp.float32
BF16 = jnp.bfloat16
HIGHEST = lax.Precision.HIGHEST

D_MODEL = 1024
A_HEADS, A_DH, A_DV = 4, 64, 128
A_QW, A_VW = 512, 512
A_COLS = 1536
R_HEADS, R_N, R_W = 8, 64, 512
R_COLS = 1792
RWKV_GN_EPS = 64e-5
G_HEADS, G_DK, G_DV = 4, 64, 128
G_KW, G_VW, G_LORA = 256, 512, 16
G_TAU = 16.0
G_COLS = 1552
G_COLS_PAD = 1664
GATE_COLS = 3072
N_GROUPS, EXP_PER_GROUP, N_EXPERTS, TOP_K = 4, 8, 32, 2
D_EXPERT = 512
EPS = 1e-6

LANES = 128
SUBLANES = 8
NSUB = D_MODEL // LANES
CHUNK = 64
VMEM_LIMIT = 56 * 1024 * 1024


def _dot(a, b):
    return jnp.dot(a.astype(BF16), b.astype(BF16), preferred_element_type=F32)


def _dot_hi(a, b):
    return jnp.dot(a, b, precision=HIGHEST, preferred_element_type=F32)


def _dot_nt(a, b, precision=None):
    return lax.dot_general(a, b, (((1,), (1,)), ((), ())), precision=precision,
                           preferred_element_type=F32)


def _dot_tn(a, b, precision=None):
    return lax.dot_general(a, b, (((0,), (0,)), ((), ())), precision=precision,
                           preferred_element_type=F32)


def _cumsum3(tri, x):
    x1 = x.astype(BF16)
    res = x - x1.astype(F32)
    x2 = res.astype(BF16)
    x3 = (res - x2.astype(F32)).astype(BF16)
    return (jnp.dot(tri, x1, preferred_element_type=F32)
            + jnp.dot(tri, x2, preferred_element_type=F32)
            + jnp.dot(tri, x3, preferred_element_type=F32))


def _sigmoid(x):
    return 1.0 / (1.0 + jnp.exp(-x))


def _softplus(x):
    return jnp.maximum(x, 0.0) + jnp.log(1.0 + jnp.exp(-jnp.abs(x)))


def _seg_ones(n, seg):
    r = lax.broadcasted_iota(jnp.int32, (n, n), 0) // seg
    c = lax.broadcasted_iota(jnp.int32, (n, n), 1) // seg
    return (r == c).astype(F32)


def _tri(n, strict):
    r = lax.broadcasted_iota(jnp.int32, (n, n), 0)
    c = lax.broadcasted_iota(jnp.int32, (n, n), 1)
    return (c < r) if strict else (c <= r)


def _adaln_kernel(c_ref, w_ref, b_ref, o_ref):
    c = c_ref[...]
    c_act = c * _sigmoid(c)
    o_ref[...] = _dot_hi(c_act, w_ref[...]) + b_ref[...]


def _adaln(c, ada_w, ada_b):
    L, D, N = ada_w.shape
    B = c.shape[0]
    tn = D
    return pl.pallas_call(
        _adaln_kernel,
        grid=(L, N // tn),
        in_specs=[
            pl.BlockSpec((B, D), lambda l, j: (0, 0)),
            pl.BlockSpec((None, D, tn), lambda l, j: (l, 0, j)),
            pl.BlockSpec((None, 1, tn), lambda l, j: (l, 0, j)),
        ],
        out_specs=pl.BlockSpec((None, B, tn), lambda l, j: (l, 0, j)),
        out_shape=jax.ShapeDtypeStruct((L, B, N), F32),
        name="adaln",
    )(c, ada_w, ada_b.reshape(L, 1, N))


_IN_SEGS = (A_COLS, R_COLS, G_COLS_PAD, GATE_COLS)
_IN_DTYPES = (BF16, F32, F32, BF16)
_IN_CHUNK = 512


def _inproj_kernel(x_ref, mod_ref, g_ref, w_ref, *o_refs):
    x = x_ref[...]
    D = x.shape[-1]
    ms = jnp.mean(x * x, axis=-1, keepdims=True)
    y = x * lax.rsqrt(ms + EPS) * g_ref[...]
    sh = mod_ref[:, 0:D]
    sc = mod_ref[:, D:2 * D]
    h = (y * (1.0 + sc) + sh).astype(BF16)
    base = 0
    for o_ref, width in zip(o_refs, _IN_SEGS):
        for c0 in range(0, width, _IN_CHUNK):
            c1 = min(c0 + _IN_CHUNK, width)
            o_ref[:, c0:c1] = jnp.dot(
                h, w_ref[:, base + c0:base + c1], preferred_element_type=F32
            ).astype(o_ref.dtype)
        base += width


def _inproj(x2, mod_l, norm_g, w_pad, S, tm):
    T, D = x2.shape
    NP = w_pad.shape[1]
    tiles_per_batch = S // tm
    return pl.pallas_call(
        _inproj_kernel,
        grid=(T // tm,),
        in_specs=[
            pl.BlockSpec((tm, D), lambda i: (i, 0)),
            pl.BlockSpec((None, 1, 2 * D), lambda i: (i // tiles_per_batch, 0, 0)),
            pl.BlockSpec((1, D), lambda i: (0, 0)),
            pl.BlockSpec((D, NP), lambda i: (0, 0), pipeline_mode=pl.Buffered(1)),
        ],
        out_specs=[pl.BlockSpec((tm, w), lambda i: (i, 0)) for w in _IN_SEGS],
        out_shape=[jax.ShapeDtypeStruct((T, w), dt) for w, dt in zip(_IN_SEGS, _IN_DTYPES)],
        compiler_params=pltpu.CompilerParams(
            dimension_semantics=("arbitrary",), vmem_limit_bytes=VMEM_LIMIT),
        name="inproj",
    )(x2, mod_l, norm_g.reshape(1, D), w_pad)


A_TILE = 256
A_POS_SPLIT = 64
LOG2E = math.log2(math.e)


def _split_dot(x, ones):
    hi = x.astype(BF16)
    lo = (x - hi.astype(F32)).astype(BF16)
    return (jnp.dot(hi, ones, preferred_element_type=F32)
            + jnp.dot(lo, ones, preferred_element_type=F32))


def _eye(n):
    r = lax.broadcasted_iota(jnp.int32, (n, n), 0)
    c = lax.broadcasted_iota(jnp.int32, (n, n), 1)
    return (r == c).astype(BF16)


def _attn_kernel(q_ref, k_ref, v_ref, qg_ref, kg_ref, lam_ref, sg_ref, qaug_ref, kaug_ref, o_ref,
                 qt_s, ka_s, vt_s, s_s, *, S, lambda_init):
    t = A_TILE
    seg = _seg_ones(LANES, A_DH).astype(BF16)
    eye_t = _eye(t)
    eye_l = _eye(LANES)
    is_qk = lax.broadcasted_iota(jnp.int32, (t, LANES), 1) < A_DH

    blocks = [slice(b * t, (b + 1) * t) for b in range(S // t)]
    sides = ((q_ref, qg_ref, A_DH ** -0.5 * LOG2E), (k_ref, kg_ref, 1.0))
    xf = [[ref[rows, :].astype(F32) for rows in blocks] for ref, _, _ in sides]
    ms = [[jnp.dot((x * x).astype(BF16), seg, preferred_element_type=F32) for x in xs]
          for xs in xf]
    qn, kn = [[x * lax.rsqrt(m * (1.0 / A_DH) + EPS) * g_ref[...] * scale
               for x, m in zip(xs, mss)] for xs, mss, (_, g_ref, scale) in zip(xf, ms, sides)]
    for b, rows in enumerate(blocks):
        vt_s[:, rows] = _dot_nt(eye_l, v_ref[rows, :]).astype(BF16)
        for c in range(2):
            kc = kn[b] if c == 0 else pltpu.roll(kn[b], A_DH, 1)
            ka_s[c, rows, :] = jnp.where(is_qk, kc.astype(BF16), kaug_ref[rows, :])
    qas = [[jnp.where(is_qk, (qn[b] if c == 0 else pltpu.roll(qn[b], A_DH, 1)).astype(BF16),
                      qaug_ref[rows, :]) for c in range(2)] for b, rows in enumerate(blocks)]
    qts = [[_dot_nt(eye_l, qa) for qa in pair] for pair in qas]
    for b, rows in enumerate(blocks):
        for c in range(2):
            qt_s[c, :, rows] = qts[b][c].astype(BF16)

    lv = lam_ref[...]
    lam = (jnp.exp(jnp.sum(lv[0:1] * lv[1:2], axis=-1, keepdims=True))
           - jnp.exp(jnp.sum(lv[2:3] * lv[3:4], axis=-1, keepdims=True)) + lambda_init)
    causal = (lax.broadcasted_iota(jnp.int32, (t, t), 0)
              <= lax.broadcasted_iota(jnp.int32, (t, t), 1))

    nt = S // t

    def score_block(i, c, j, st):
        s = jnp.dot(ka_s[c, j * t:(j + 1) * t, :], qt_s[c, :, i * t:(i + 1) * t],
                    preferred_element_type=F32)
        if j == i:
            s = jnp.where(causal, s, -jnp.inf)
        s_s[i % 2, c, j * t:(j + 1) * t, :] = s
        mj = jnp.max(s, axis=0, keepdims=True)
        st["m"][c] = mj if st["m"][c] is None else jnp.maximum(st["m"][c], mj)

    def value_block(i, c, j, st):
        p = jnp.exp2(s_s[i % 2, c, j * t:(j + 1) * t, :] - st["m"][c])
        st["l"][c] = st["l"][c] + jnp.sum(p, axis=0, keepdims=True)
        st["acc"][c] = st["acc"][c] + jnp.dot(vt_s[:, j * t:(j + 1) * t], p.astype(BF16),
                                              preferred_element_type=F32)

    def finish(i, st):
        o = st["acc"][0] / st["l"][0] - lam * (st["acc"][1] / st["l"][1])
        ms = jnp.mean(o * o, axis=0, keepdims=True)
        o = o * lax.rsqrt(ms + EPS) * sg_ref[...] * (1.0 - lambda_init)
        o_ref[i * t:(i + 1) * t, :] = _dot_nt(eye_t, o.astype(BF16)).astype(o_ref.dtype)

    prev = None
    for i in range(nt + 1):
        cur = None
        first = []
        if i < nt:
            cur = dict(m=[None, None], l=[jnp.zeros((1, t), F32)] * 2,
                       acc=[jnp.zeros((A_DV, t), F32)] * 2)
            first = [(c, j) for j in range(i + 1) for c in range(2)]
        second = [(c, j) for j in range(i) for c in range(2)] if prev is not None else []
        for n in range(max(len(first), len(second))):
            if n < len(first):
                score_block(i, first[n][0], first[n][1], cur)
            if n < len(second):
                value_block(i - 1, second[n][0], second[n][1], prev)
        if prev is not None:
            finish(i - 1, prev)
        prev = cur


def _alibi_columns(S):
    pos = jnp.arange(S, dtype=jnp.int32)[:, None]
    hi = (pos // A_POS_SPLIT).astype(F32)
    lo = (pos % A_POS_SPLIT).astype(F32)
    kl = jnp.arange(LANES, dtype=jnp.int32)[None, :] - A_DH
    term = kl // 2
    used = jnp.logical_and(kl >= 0, kl < 8)
    k_aug = jnp.where(used, jnp.where(term == 0, hi, jnp.where(term == 1, lo, 1.0)), 0.0)
    q_augs = []
    for h in range(A_HEADS):
        s2 = 2.0 ** (-8.0 * (h + 1) / A_HEADS) * LOG2E
        qv = jnp.where(term == 0, A_POS_SPLIT * s2,
             jnp.where(term == 1, s2,
             jnp.where(term == 2, -A_POS_SPLIT * s2 * hi, -s2 * lo)))
        qv_hi = qv.astype(BF16).astype(F32)
        q_augs.append(jnp.where(used, jnp.where(kl % 2 == 0, qv_hi, qv - qv_hi), 0.0))
    return jnp.stack(q_augs).astype(BF16), k_aug.astype(BF16)


def _attention(p_attn, qn_g, kn_g, lam_vecs, subln_g, lambda_init, B, S):
    pa = p_attn.reshape(B, S, A_COLS)
    dup = lambda g: jnp.concatenate([g, g]).reshape(1, LANES)
    q_aug, k_aug = _alibi_columns(S)
    nqb = A_QW // LANES
    kern = functools.partial(_attn_kernel, S=S, lambda_init=lambda_init)
    out = pl.pallas_call(
        kern,
        grid=(B, A_HEADS),
        in_specs=[
            pl.BlockSpec((None, S, LANES), lambda b, h: (b, 0, h)),
            pl.BlockSpec((None, S, LANES), lambda b, h: (b, 0, nqb + h)),
            pl.BlockSpec((None, S, LANES), lambda b, h: (b, 0, 2 * nqb + h)),
            pl.BlockSpec((1, LANES), lambda b, h: (0, 0)),
            pl.BlockSpec((1, LANES), lambda b, h: (0, 0)),
            pl.BlockSpec((4, A_DH), lambda b, h: (0, 0)),
            pl.BlockSpec((A_DV, 1), lambda b, h: (0, 0)),
            pl.BlockSpec((None, S, LANES), lambda b, h: (h, 0, 0)),
            pl.BlockSpec((S, LANES), lambda b, h: (0, 0)),
        ],
        out_specs=pl.BlockSpec((None, S, A_DV), lambda b, h: (b, 0, h)),
        out_shape=jax.ShapeDtypeStruct((B, S, A_VW), BF16),
        scratch_shapes=[pltpu.VMEM((2, LANES, S), BF16), pltpu.VMEM((2, S, LANES), BF16),
                        pltpu.VMEM((A_DV, S), BF16), pltpu.VMEM((2, 2, S, A_TILE), F32)],
        compiler_params=pltpu.CompilerParams(
            dimension_semantics=("arbitrary", "arbitrary"), vmem_limit_bytes=VMEM_LIMIT),
        name="diff_attn",
    )(pa, pa, pa, dup(qn_g), dup(kn_g), lam_vecs, subln_g.reshape(A_DV, 1), q_aug, k_aug)
    return out.reshape(B * S, A_VW)


R_GROUP = 4
R_GW = R_GROUP * R_N


def _rwkv_kernel(p_ref, mu_ref, wup_ref, w0_ref, aup_ref, a0_ref, gup_ref, kk_ref, ka_ref,
                 rk_ref, lg_ref, lb_ref, o_ref,
                 carry_ref, st_ref, al_s, be_s, ka_s, rh_s, bt_s, kt_s, v_s, gc_s, y_s, *, TB):
    @pl.when(pl.program_id(1) == 0)
    def _():
        carry_ref[...] = jnp.zeros_like(carry_ref)
        st_ref[...] = jnp.zeros_like(st_ref)

    xs = p_ref[...]
    prev = pltpu.roll(xs, 1, 0)
    row = lax.broadcasted_iota(jnp.int32, (TB, 1), 0)
    prev = jnp.where(row == 0, carry_ref[...], prev)
    carry_ref[...] = xs[TB - 1:TB, :]
    xm = xs + (prev - xs) * mu_ref[...]
    r = xm[:, 0:R_W]
    k = xm[:, R_W:2 * R_W]
    v = xm[:, 2 * R_W:3 * R_W]
    wa = xm[:, 3 * R_W:3 * R_W + LANES]
    gd = xm[:, 3 * R_W + LANES:3 * R_W + 2 * LANES]
    wz = w0_ref[...] + _dot(jnp.tanh(wa), wup_ref[...])
    lw = -math.exp(-0.5) * _sigmoid(wz)
    a = _sigmoid(a0_ref[...] + _dot(wa, aup_ref[...]))
    g = _dot(_sigmoid(gd), gup_ref[...])
    seg = _seg_ones(R_GW, R_N).astype(BF16)

    def head_sum(x, passes):
        f = _split_dot if passes == 2 else (
            lambda t, o: jnp.dot(t.astype(BF16), o, preferred_element_type=F32))
        return jnp.concatenate(
            [f(x[:, j * R_GW:(j + 1) * R_GW], seg) for j in range(R_W // R_GW)], axis=-1)

    kk = k * kk_ref[...]
    kk = kk * jnp.minimum(lax.rsqrt(head_sum(kk * kk, 1)), 1e12)
    k2 = k * (1.0 + (a - 1.0) * ka_ref[...])
    bonus = head_sum(r * k2 * rk_ref[...], 2) * v
    bv = kk * a

    rr = lax.broadcasted_iota(jnp.int32, (TB, TB), 0)
    cc = lax.broadcasted_iota(jnp.int32, (TB, TB), 1)
    tril_blk = jnp.logical_and(rr // CHUNK == cc // CHUNK, cc <= rr).astype(BF16)
    Lg = _cumsum3(tril_blk, lw)
    inv = jnp.exp(-Lg)
    al_s[...] = (jnp.exp(Lg - lw) * kk).astype(BF16)
    be_s[...] = (bv * inv).astype(BF16)
    ka_s[...] = (k2 * inv).astype(BF16)
    rh_s[...] = (jnp.exp(Lg) * r).astype(BF16)
    v_s[...] = v.astype(BF16)
    for c in range(TB // CHUNK):
        rows = slice(c * CHUNK, (c + 1) * CHUNK)
        gC = Lg[(c + 1) * CHUNK - 1:(c + 1) * CHUNK, :]
        tail = jnp.exp(gC - Lg[rows, :])
        bt_s[rows, :] = (bv[rows, :] * tail).astype(BF16)
        kt_s[rows, :] = (k2[rows, :] * tail).astype(BF16)
        gc_s[c * SUBLANES:(c + 1) * SUBLANES, :] = jnp.broadcast_to(jnp.exp(gC), (SUBLANES, R_W))

    ri = lax.broadcasted_iota(jnp.int32, (R_GW, R_GW), 0)
    ci = lax.broadcasted_iota(jnp.int32, (R_GW, R_GW), 1)
    blk = ri // R_N == ci // R_N
    strict = ci % R_N < ri % R_N
    strict_t = ri % R_N < ci % R_N
    incl = ci % R_N <= ri % R_N
    zero = jnp.zeros((), BF16)

    def expand(x):
        return jnp.where(blk, jnp.concatenate([x] * R_GROUP, axis=0), zero)

    n_groups = R_HEADS // R_GROUP
    pair = 2

    def chunk_pair(ip, _):
        chains = []
        for dc in range(pair):
            c = ip * pair + dc
            rows = pl.ds(pl.multiple_of(c * CHUNK, CHUNK), CHUNK)
            for gi in range(n_groups):
                chains.append(dict(c=c, rows=rows, gi=gi, cols=slice(gi * R_GW, (gi + 1) * R_GW)))
        for ch in chains:
            rows, cols = ch["rows"], ch["cols"]
            ch["A"] = expand(al_s[rows, cols])
            ch["R"] = expand(rh_s[rows, cols])
            ch["B"] = expand(be_s[rows, cols])
            ch["K"] = expand(ka_s[rows, cols])
            vc = v_s[rows, cols]
            ch["V"] = jnp.concatenate(
                [vc[:, h * R_N:(h + 1) * R_N] for h in range(R_GROUP)], axis=0)
        for ch in chains:
            ch["X"] = -jnp.where(strict, _dot_nt(ch["A"], ch["B"]), 0.0)
            ch["m_ak_t"] = jnp.where(strict_t, _dot_nt(ch["K"], ch["A"]), 0.0).astype(BF16)
            ch["m_rb"] = jnp.where(incl, _dot_nt(ch["R"], ch["B"]), 0.0).astype(BF16)
            ch["m_rk"] = jnp.where(incl, _dot_nt(ch["R"], ch["K"]), 0.0).astype(BF16)
        for ch in chains:
            ch["Z"] = ch["A"].astype(F32)
            ch["Wt"] = _dot_tn(ch["V"], ch["m_ak_t"])
        n = 1
        while True:
            last = 2 * n >= CHUNK
            for ch in chains:
                Xb = ch["X"].astype(BF16)
                ch["Z"] = ch["Z"] + jnp.dot(Xb, ch["Z"].astype(BF16), preferred_element_type=F32)
                ch["Wt"] = ch["Wt"] + _dot_nt(ch["Wt"].astype(BF16), Xb)
                if not last:
                    ch["X"] = jnp.dot(Xb, Xb, preferred_element_type=F32)
            n *= 2
            if last:
                break
        for ch in chains:
            rows, cols = ch["rows"], ch["cols"]
            Bt = expand(bt_s[rows, cols])
            Kt = expand(kt_s[rows, cols])
            Zb = ch["Z"].astype(BF16)
            Wtb = ch["Wt"].astype(BF16)
            ch["y_a"] = (ch["R"].astype(F32)
                         - jnp.dot(ch["m_rb"], Zb, preferred_element_type=F32)).astype(BF16)
            ch["y_b"] = (jnp.dot(ch["m_rk"], ch["V"], preferred_element_type=F32)
                         - _dot_nt(ch["m_rb"], Wtb))
            ch["p_neg"] = _dot_tn(Zb, Bt).astype(BF16)
            ch["q"] = _dot_tn(ch["V"], Kt) - jnp.dot(Wtb, Bt, preferred_element_type=F32)
        for ch in chains:
            rows, cols, gi = ch["rows"], ch["cols"], ch["gi"]
            g0 = pl.multiple_of(ch["c"] * SUBLANES, SUBLANES)
            S0 = st_ref[gi]
            S0b = S0.astype(BF16)
            y = _dot_nt(ch["y_a"], S0b) + ch["y_b"]
            st_ref[gi] = (S0 * gc_s[pl.ds(g0, SUBLANES), cols][0:1, :]
                          - jnp.dot(S0b, ch["p_neg"], preferred_element_type=F32) + ch["q"])
            for h in range(R_GROUP):
                hh = gi * R_GROUP + h
                y_s[rows, hh * R_N:(hh + 1) * R_N] = y[h * R_N:(h + 1) * R_N, :]
        return 0

    lax.fori_loop(0, TB // CHUNK // pair, chunk_pair, 0)

    y = y_s[...]
    mean = head_sum(y, 2) * (1.0 / R_N)
    yc = y - mean
    var = head_sum(yc * yc, 1) * (1.0 / R_N)
    yn = yc * lax.rsqrt(var + RWKV_GN_EPS) * lg_ref[...] + lb_ref[...]
    o_ref[...] = ((yn + bonus) * g).astype(o_ref.dtype)


def _rwkv(p_rwkv, mu, w_up, w0, a_up, a0, g_up, k_k, k_a, r_k, lnx_g, lnx_b, B, S):
    TB = 256
    T = B * S
    nt = S // TB
    row = lambda t: t.reshape(1, -1)
    zeros = jnp.zeros((R_N, R_W), F32)
    wup_pad = jnp.concatenate([w_up, zeros], axis=0)
    aup_pad = jnp.concatenate([zeros, a_up], axis=0)
    vec = lambda n: pl.BlockSpec((1, n), lambda b, i: (0, 0))
    mat = lambda m, n: pl.BlockSpec((m, n), lambda b, i: (0, 0))
    kern = functools.partial(_rwkv_kernel, TB=TB)
    return pl.pallas_call(
        kern,
        grid=(B, nt),
        in_specs=[
            pl.BlockSpec((TB, R_COLS), lambda b, i: (b * nt + i, 0)),
            vec(R_COLS), mat(LANES, R_W), vec(R_W), mat(LANES, R_W), vec(R_W), mat(LANES, R_W),
            vec(R_W), vec(R_W), vec(R_W), vec(R_W), vec(R_W),
        ],
        out_specs=pl.BlockSpec((TB, R_W), lambda b, i: (b * nt + i, 0)),
        out_shape=jax.ShapeDtypeStruct((T, R_W), BF16),
        scratch_shapes=[
            pltpu.VMEM((1, R_COLS), F32),
            pltpu.VMEM((R_HEADS // R_GROUP, R_N, R_GW), F32),
        ] + [pltpu.VMEM((TB, R_W), BF16)] * 7 + [
            pltpu.VMEM((TB // CHUNK * SUBLANES, R_W), F32),
            pltpu.VMEM((TB, R_W), F32),
        ],
        compiler_params=pltpu.CompilerParams(
            dimension_semantics=("arbitrary", "arbitrary"), vmem_limit_bytes=VMEM_LIMIT),
        name="rwkv7",
    )(p_rwkv, row(mu), wup_pad, row(w0), aup_pad, row(a0), g_up, row(k_k), row(k_a),
      row(r_k), row(lnx_g), row(lnx_b))


G_SUBTILE = 256


def _gla_kernel(p_ref, aup_ref, ab_ref, ng_ref, o_ref, st_ref, *, TB):
    @pl.when(pl.program_id(1) == 0)
    def _():
        st_ref[...] = jnp.zeros_like(st_ref)

    c_gv = 2 * G_KW
    c_ad = c_gv + G_VW
    c_gate = c_ad + LANES
    sub = G_SUBTILE
    subs = range(TB // sub)
    nchunk = sub // CHUNK
    rr = lax.broadcasted_iota(jnp.int32, (sub, sub), 0)
    cc = lax.broadcasted_iota(jnp.int32, (sub, sub), 1)
    causal = jnp.logical_and(rr // CHUNK == cc // CHUNK, cc <= rr)
    causal_b = causal.astype(BF16)
    heads = range(G_HEADS)
    sls = [slice(h * G_DK, (h + 1) * G_DK) for h in heads]
    vss = [slice(h * G_DV, (h + 1) * G_DV) for h in heads]

    rws = [slice(u * sub, (u + 1) * sub) for u in subs]
    zs = [_dot(p_ref[r, c_ad:c_ad + LANES], aup_ref[...]) + ab_ref[...] for r in rws]
    las = [-_softplus(-z) * (1.0 / G_TAU) for z in zs]
    bs = [_cumsum3(causal_b, la) for la in las]
    ks = [p_ref[r, G_KW:2 * G_KW] for r in rws]
    vbs = [p_ref[r, c_gv:c_gv + G_VW].astype(BF16) for r in rws]
    qes = [(p_ref[r, 0:G_KW] * (G_DK ** -0.5) * jnp.exp(b)).astype(BF16) for r, b in zip(rws, bs)]
    kes = [(k * jnp.exp(-b)).astype(BF16) for k, b in zip(ks, bs)]
    scs = [[jnp.where(causal, _dot_nt(qes[u][:, sls[h]], kes[u][:, sls[h]]), 0.0) for h in heads]
           for u in subs]
    o_intra = [[_dot(scs[u][h], vbs[u][:, vss[h]]) for h in heads] for u in subs]
    kts, e_lasts, kvs = {}, {}, {}
    for u in subs:
        for c in range(nchunk):
            rows = slice(c * CHUNK, (c + 1) * CHUNK)
            b_last = bs[u][(c + 1) * CHUNK - 1:(c + 1) * CHUNK, :]
            kt = (ks[u][rows, :] * jnp.exp(b_last - bs[u][rows, :])).astype(BF16)
            e_lasts[u, c] = jnp.exp(b_last)
            for h in heads:
                kvs[u, c, h] = _dot_tn(vbs[u][rows, vss[h]], kt[:, sls[h]])
    states = [st_ref[h] for h in heads]
    parts = [[] for _ in heads]
    for u in subs:
        for c in range(nchunk):
            rows = slice(c * CHUNK, (c + 1) * CHUNK)
            for h in heads:
                parts[h].append(o_intra[u][h][rows, :]
                                + _dot_nt(qes[u][rows, sls[h]], states[h].astype(BF16)))
                states[h] = states[h] * e_lasts[u, c][:, sls[h]] + kvs[u, c, h]
    for h in heads:
        vs = vss[h]
        st_ref[h] = states[h]
        o = jnp.concatenate(parts[h], axis=0)
        ms = jnp.mean(o * o, axis=-1, keepdims=True)
        gt = p_ref[:, c_gate + h * G_DV:c_gate + (h + 1) * G_DV]
        o = o * lax.rsqrt(ms + EPS) * ng_ref[...] * (gt * _sigmoid(gt))
        o_ref[:, vs] = o.astype(o_ref.dtype)


def _gla(p_gla, alpha_up, alpha_b, norm_g, B, S):
    TB = 512
    T = B * S
    nt = S // TB
    aup_pad = jnp.concatenate([alpha_up, jnp.zeros((LANES - G_LORA, G_KW), F32)], axis=0)
    kern = functools.partial(_gla_kernel, TB=TB)
    return pl.pallas_call(
        kern,
        grid=(B, nt),
        in_specs=[
            pl.BlockSpec((TB, G_COLS_PAD), lambda b, i: (b * nt + i, 0)),
            pl.BlockSpec((LANES, G_KW), lambda b, i: (0, 0)),
            pl.BlockSpec((1, G_KW), lambda b, i: (0, 0)),
            pl.BlockSpec((1, G_DV), lambda b, i: (0, 0)),
        ],
        out_specs=pl.BlockSpec((TB, G_VW), lambda b, i: (b * nt + i, 0)),
        out_shape=jax.ShapeDtypeStruct((T, G_VW), BF16),
        scratch_shapes=[pltpu.VMEM((G_HEADS, G_DV, G_DK), F32)],
        compiler_params=pltpu.CompilerParams(
            dimension_semantics=("arbitrary", "arbitrary"), vmem_limit_bytes=VMEM_LIMIT),
        name="gla",
    )(p_gla, aup_pad, alpha_b.reshape(1, G_KW), norm_g.reshape(1, G_DV))


def _merge_kernel(x_ref, oa_ref, or_ref, og_ref, gate_ref, pa_ref, pr_ref, pg_ref, wo_ref,
                  mod_ref, g2_ref, rw_ref, rb_ref, x1_ref, h2_ref, info_ref, cnt_ref, carry_ref):
    D = D_MODEL

    @pl.when(pl.program_id(0) == 0)
    def _():
        carry_ref[...] = jnp.zeros_like(carry_ref)

    merged = (_sigmoid(gate_ref[:, 0:D].astype(F32))
              * jnp.dot(oa_ref[...], pa_ref[...], preferred_element_type=F32)
              + _sigmoid(gate_ref[:, D:2 * D].astype(F32))
              * jnp.dot(or_ref[...], pr_ref[...], preferred_element_type=F32)
              + _sigmoid(gate_ref[:, 2 * D:3 * D].astype(F32))
              * jnp.dot(og_ref[...], pg_ref[...], preferred_element_type=F32))
    gt1 = mod_ref[:, 2 * D:3 * D]
    sh2 = mod_ref[:, 3 * D:4 * D]
    sc2 = mod_ref[:, 4 * D:5 * D]
    x1 = x_ref[...] + gt1 * jnp.dot(merged.astype(BF16), wo_ref[...], preferred_element_type=F32)
    x1_ref[...] = x1
    ms = jnp.mean(x1 * x1, axis=-1, keepdims=True)
    h2 = x1 * lax.rsqrt(ms + EPS) * g2_ref[...] * (1.0 + sc2) + sh2
    for s in range(NSUB):
        h2_ref[pl.ds(s, h2.shape[0], stride=NSUB), :] = h2[:, s * LANES:(s + 1) * LANES]
    h_hi = h2.astype(BF16)
    h_lo = (h2 - h_hi.astype(F32)).astype(BF16)
    logits = (jnp.dot(h_hi, rw_ref[0], preferred_element_type=F32)
              + jnp.dot(h_lo, rw_ref[0], preferred_element_type=F32)
              + jnp.dot(h_hi, rw_ref[1], preferred_element_type=F32) + rb_ref[...])
    info_ref[...], cnt_ref[...] = _route_tile(logits, carry_ref)


def _merge(x2, o_a, o_r, o_g, p_gate, proj_a, proj_r, proj_g, w_out, mod_l, norm2_g,
           router_w, router_b, S, tm):
    T, D = x2.shape
    tiles_per_batch = S // tm
    tile = lambda w: pl.BlockSpec((tm, w), lambda i: (i, 0))
    const = lambda m, n: pl.BlockSpec((m, n), lambda i: (0, 0))
    return pl.pallas_call(
        _merge_kernel,
        grid=(T // tm,),
        in_specs=[
            tile(D), tile(A_VW), tile(R_W), tile(G_VW), tile(GATE_COLS),
            const(A_VW, D), const(R_W, D), const(G_VW, D), const(D, D),
            pl.BlockSpec((None, 1, 6 * D), lambda i: (i // tiles_per_batch, 0, 0)),
            const(1, D), pl.BlockSpec((2, D, LANES), lambda i: (0, 0, 0)), const(1, LANES),
        ],
        out_specs=[
            tile(D),
            pl.BlockSpec((tm * NSUB, LANES), lambda i: (i, 0)),
            tile(LANES),
            const(1, LANES),
        ],
        out_shape=[
            jax.ShapeDtypeStruct((T, D), F32),
            jax.ShapeDtypeStruct((T * NSUB, LANES), F32),
            jax.ShapeDtypeStruct((T, LANES), F32),
            jax.ShapeDtypeStruct((1, LANES), F32),
        ],
        scratch_shapes=[pltpu.VMEM((1, LANES), F32)],
        compiler_params=pltpu.CompilerParams(
            dimension_semantics=("arbitrary",), vmem_limit_bytes=VMEM_LIMIT),
        name="merge",
    )(x2, o_a, o_r, o_g, p_gate, proj_a, proj_r, proj_g, w_out, mod_l,
      norm2_g.reshape(1, D), router_w, router_b)


MOE_ROWS = 256
E_LANE0 = N_GROUPS


def _route_tile(lg, carry_ref):
    n = lg.shape[0]
    lane = lax.broadcasted_iota(jnp.int32, (n, LANES), 1).astype(F32)
    neg = -jnp.inf
    big = float(LANES)

    def first_max(vals):
        m = jnp.max(vals, axis=-1, keepdims=True)
        idx = jnp.min(jnp.where(vals == m, lane, big), axis=-1, keepdims=True)
        return m, idx

    in_grp = lane < N_GROUPS
    gm, grp = first_max(jnp.where(in_grp, lg, neg))
    g_prob = 1.0 / jnp.sum(jnp.where(in_grp, jnp.exp(lg - gm), 0.0), axis=-1, keepdims=True)
    lo = E_LANE0 + grp * EXP_PER_GROUP
    el = jnp.where(jnp.logical_and(lane >= lo, lane < lo + EXP_PER_GROUP), lg, neg)
    v1, i1 = first_max(el)
    v2, i2 = first_max(jnp.where(lane == i1, neg, el))
    e21 = jnp.exp(v2 - v1)
    w0 = g_prob / (1.0 + e21)
    w1 = g_prob * e21 / (1.0 + e21)
    oh0 = lane == i1
    oh1 = lane == i2
    oh = jnp.logical_or(oh0, oh1).astype(F32)
    before = _tri(n, True).astype(BF16)
    cnt = jnp.dot(before, oh.astype(BF16), preferred_element_type=F32) + carry_ref[...]
    rank0 = jnp.sum(jnp.where(oh0, cnt, 0.0), axis=-1, keepdims=True)
    rank1 = jnp.sum(jnp.where(oh1, cnt, 0.0), axis=-1, keepdims=True)
    carry = carry_ref[...] + jnp.sum(oh, axis=0, keepdims=True)
    carry_ref[...] = carry
    cols = (i1 - E_LANE0, i2 - E_LANE0, rank0, rank1, w0, w1)
    info = jnp.zeros((n, LANES), F32)
    for j, col in enumerate(cols):
        info = jnp.where(lane == j, col, info)
    return info, carry


def _route_tables(info, cnt):
    T = info.shape[0]
    A = T * TOP_K
    counts = cnt[0, E_LANE0:E_LANE0 + N_EXPERTS].astype(jnp.int32)
    padded = (counts + MOE_ROWS - 1) // MOE_ROWS * MOE_ROWS
    pad_end = jnp.cumsum(padded)
    pad_start = pad_end - padded
    n_blocks = -(-A // MOE_ROWS) + N_EXPERTS
    eid = info[:, 0:2].astype(jnp.int32)
    is_e = eid[:, :, None] == jnp.arange(N_EXPERTS, dtype=jnp.int32)
    dest = (jnp.sum(jnp.where(is_e, pad_start, 0), axis=-1)
            + info[:, 2:4].astype(jnp.int32))
    blk_start = jnp.arange(n_blocks, dtype=jnp.int32) * MOE_ROWS
    blk_exp = jnp.minimum(jnp.sum(pad_end[None, :] <= blk_start[:, None], axis=1),
                          N_EXPERTS - 1).astype(jnp.int32)
    n_used = (pad_end[-1:] // MOE_ROWS).astype(jnp.int32)
    nonempty = counts > 0
    ids = jnp.arange(N_EXPERTS, dtype=jnp.int32)
    grp_of = jnp.cumsum(nonempty.astype(jnp.int32)) - 1
    later = jnp.logical_and(ids[None, :] > ids[:, None], nonempty[None, :])
    nxt_of = jnp.min(jnp.where(later, ids[None, :], N_EXPERTS), axis=1)
    nxt_of = jnp.where(nxt_of == N_EXPERTS, -1, nxt_of)
    is_b = blk_exp[:, None] == ids[None, :]
    blk_grp = jnp.sum(jnp.where(is_b, grp_of[None, :], 0), axis=1).astype(jnp.int32)
    blk_nxt = jnp.sum(jnp.where(is_b, nxt_of[None, :], 0), axis=1).astype(jnp.int32)
    return dest[:, 0], dest[:, 1], blk_exp, n_used, blk_grp, blk_nxt, n_blocks * MOE_ROWS


def _dispatch_kernel(d0_ref, d1_ref, h2_ref, xin_in, xin_hbm, sem):
    del xin_in
    n = h2_ref.shape[0] // NSUB
    base = pl.program_id(0) * n

    def slab(ref, row):
        return ref.at[pl.ds(pl.multiple_of(row * NSUB, NSUB), NSUB)]

    def body(r, _):
        pltpu.make_async_copy(slab(h2_ref, r), slab(xin_hbm, d0_ref[base + r]), sem).start(0)
        pltpu.make_async_copy(slab(h2_ref, r), slab(xin_hbm, d1_ref[base + r]), sem).start(1)
        return 0

    lax.fori_loop(0, n, body, 0, unroll=8)
    for _ in range(TOP_K):
        pltpu.make_async_copy(h2_ref, xin_hbm.at[pl.ds(0, n * NSUB)], sem).wait()


def _dispatch(h2_slab, dest0, dest1, xin_init, td):
    T = h2_slab.shape[0] // NSUB
    grid_spec = pltpu.PrefetchScalarGridSpec(
        num_scalar_prefetch=2,
        grid=(T // td,),
        in_specs=[pl.BlockSpec((td * NSUB, LANES), lambda i, d0, d1: (i, 0)),
                  pl.BlockSpec(memory_space=pl.ANY)],
        out_specs=pl.BlockSpec(memory_space=pl.ANY),
        scratch_shapes=[pltpu.SemaphoreType.DMA(())],
    )
    return pl.pallas_call(
        _dispatch_kernel,
        grid_spec=grid_spec,
        out_shape=jax.ShapeDtypeStruct(xin_init.shape, F32),
        input_output_aliases={3: 0},
        compiler_params=pltpu.CompilerParams(dimension_semantics=("arbitrary",)),
        name="moe_dispatch",
    )(dest0, dest1, h2_slab, xin_init)


def _moe_kernel(be_ref, nu_ref, grp_ref, nxt_ref, x_ref, wg_hbm, wu_hbm, wd_hbm, y_ref,
                wgf, wuf, wdf, wgb, wub, wdb, sem, *, layer):
    i = pl.program_id(0)

    def weight_copies(expert, slot):
        return [pltpu.make_async_copy(src.at[layer, expert], dst.at[slot], sem.at[slot, k])
                for k, (src, dst) in enumerate(((wg_hbm, wgf), (wu_hbm, wuf), (wd_hbm, wdf)))]

    @pl.when(i < nu_ref[0])
    def _():
        changed = jnp.logical_or(i == 0, be_ref[i] != be_ref[jnp.maximum(i - 1, 0)])

        @pl.when(changed)
        def _():
            slot = grp_ref[i] % 2

            @pl.when(i == 0)
            def _():
                for cp in weight_copies(be_ref[0], 0):
                    cp.start()

            for cp in weight_copies(be_ref[i], slot):
                cp.wait()
            wgb[...] = wgf[slot].astype(BF16)
            wub[...] = wuf[slot].astype(BF16)
            wdb[...] = wdf[slot].astype(BF16)

            @pl.when(nxt_ref[i] >= 0)
            def _():
                for cp in weight_copies(nxt_ref[i], 1 - slot):
                    cp.start()

        kw = 2 * LANES
        per = kw // LANES
        hg = hu = None
        for j in range(D_MODEL // kw):
            xj = jnp.concatenate(
                [x_ref[pl.ds(j * per + s, MOE_ROWS, stride=NSUB), :].astype(BF16)
                 for s in range(per)], axis=-1)
            dg = jnp.dot(xj, wgb[j * kw:(j + 1) * kw, :], preferred_element_type=F32)
            du = jnp.dot(xj, wub[j * kw:(j + 1) * kw, :], preferred_element_type=F32)
            hg = dg if hg is None else hg + dg
            hu = du if hu is None else hu + du
        hid = (hg * _sigmoid(hg) * hu).astype(BF16)
        for j in range(D_MODEL // kw):
            yj = jnp.dot(hid, wdb[:, j * kw:(j + 1) * kw], preferred_element_type=F32)
            for s in range(per):
                y_ref[pl.ds(j * per + s, MOE_ROWS, stride=NSUB), :] = yj[:, s * LANES:(s + 1) * LANES]

    @pl.when(i >= nu_ref[0])
    def _():
        y_ref[...] = jnp.zeros_like(y_ref)


def _moe(xin, blk_exp, n_used, blk_grp, blk_nxt, w_gate, w_up, w_down, layer):
    blk_rows = MOE_ROWS * NSUB
    n_blocks = xin.shape[0] // blk_rows
    last = lambda i, nu: jnp.minimum(i, nu[0] - 1)
    grid_spec = pltpu.PrefetchScalarGridSpec(
        num_scalar_prefetch=4,
        grid=(n_blocks,),
        in_specs=[
            pl.BlockSpec((blk_rows, LANES), lambda i, be, nu, gr, nx: (last(i, nu), 0)),
            pl.BlockSpec(memory_space=pl.ANY), pl.BlockSpec(memory_space=pl.ANY),
            pl.BlockSpec(memory_space=pl.ANY),
        ],
        out_specs=pl.BlockSpec((blk_rows, LANES), lambda i, be, nu, gr, nx: (i, 0)),
        scratch_shapes=[
            pltpu.VMEM((2, D_MODEL, D_EXPERT), F32),
            pltpu.VMEM((2, D_MODEL, D_EXPERT), F32),
            pltpu.VMEM((2, D_EXPERT, D_MODEL), F32),
            pltpu.VMEM((D_MODEL, D_EXPERT), BF16),
            pltpu.VMEM((D_MODEL, D_EXPERT), BF16),
            pltpu.VMEM((D_EXPERT, D_MODEL), BF16),
            pltpu.SemaphoreType.DMA((2, 3)),
        ],
    )
    return pl.pallas_call(
        functools.partial(_moe_kernel, layer=layer),
        grid_spec=grid_spec,
        out_shape=jax.ShapeDtypeStruct(xin.shape, F32),
        compiler_params=pltpu.CompilerParams(
            dimension_semantics=("arbitrary",), vmem_limit_bytes=VMEM_LIMIT),
        name="moe_ffn",
    )(blk_exp, n_used, blk_grp, blk_nxt, xin, w_gate, w_up, w_down)


def _combine_kernel(d0_ref, d1_ref, y_hbm, x1_ref, info_ref, mod_ref, o_ref, ybuf, sem):
    i = pl.program_id(0)
    nsteps = pl.num_programs(0)
    n = x1_ref.shape[0]
    slot = i % 2
    D = D_MODEL

    def slab(ref, row):
        return ref.at[pl.ds(pl.multiple_of(row * NSUB, NSUB), NSUB)]

    def start_gather(step, sl):
        base = step * n

        def body(r, _):
            pltpu.make_async_copy(
                slab(y_hbm, d0_ref[base + r]), slab(ybuf.at[sl, 0], r), sem.at[sl]).start(0)
            pltpu.make_async_copy(
                slab(y_hbm, d1_ref[base + r]), slab(ybuf.at[sl, 1], r), sem.at[sl]).start(1)
            return 0
        lax.fori_loop(0, n, body, 0, unroll=8)

    @pl.when(i == 0)
    def _():
        start_gather(0, 0)

    @pl.when(i + 1 < nsteps)
    def _():
        start_gather(i + 1, 1 - slot)

    for k in range(TOP_K):
        pltpu.make_async_copy(
            y_hbm.at[pl.ds(0, n * NSUB)], ybuf.at[slot, k], sem.at[slot]).wait()

    w0 = info_ref[:, 4:5]
    w1 = info_ref[:, 5:6]
    for s in range(NSUB):
        cols = slice(s * LANES, (s + 1) * LANES)
        gt2 = mod_ref[:, 5 * D + s * LANES:5 * D + (s + 1) * LANES]
        piece = pl.ds(s, n, stride=NSUB)
        moe = w0 * ybuf[slot, 0, piece, :] + w1 * ybuf[slot, 1, piece, :]
        o_ref[:, cols] = x1_ref[:, cols] + gt2 * moe


def _combine(x1, y, info, dest0, dest1, mod_l, S, tm):
    T, D = x1.shape
    tiles_per_batch = S // tm
    grid_spec = pltpu.PrefetchScalarGridSpec(
        num_scalar_prefetch=2,
        grid=(T // tm,),
        in_specs=[
            pl.BlockSpec(memory_space=pl.ANY),
            pl.BlockSpec((tm, D), lambda i, d0, d1: (i, 0)),
            pl.BlockSpec((tm, LANES), lambda i, d0, d1: (i, 0)),
            pl.BlockSpec((None, 1, 6 * D), lambda i, d0, d1: (i // tiles_per_batch, 0, 0)),
        ],
        out_specs=pl.BlockSpec((tm, D), lambda i, d0, d1: (i, 0)),
        scratch_shapes=[pltpu.VMEM((2, TOP_K, tm * NSUB, LANES), F32),
                        pltpu.SemaphoreType.DMA((2,))],
    )
    return pl.pallas_call(
        _combine_kernel,
        grid_spec=grid_spec,
        out_shape=jax.ShapeDtypeStruct((T, D), F32),
        compiler_params=pltpu.CompilerParams(
            dimension_semantics=("arbitrary",), vmem_limit_bytes=VMEM_LIMIT),
        name="moe_combine",
    )(dest0, dest1, y, x1, info, mod_l)


def _pad_w_in(w):
    D = w.shape[0]
    c0 = A_COLS + R_COLS
    c_ad = c0 + 2 * G_KW + G_VW
    c_gg = c_ad + G_LORA
    wb = w.astype(BF16)
    pad = jnp.zeros((D, LANES - G_LORA), BF16)
    return jnp.concatenate([wb[:, :c_gg], pad, wb[:, c_gg:]], axis=1)


def kernel(x, c, ada_w, ada_b, norm1_g, norm2_g, w_in, attn_qn_g, attn_kn_g, attn_lambda,
           attn_subln_g, rwkv_mu, rwkv_w_up, rwkv_w0, rwkv_a_up, rwkv_a0, rwkv_g_up, rwkv_k_k,
           rwkv_k_a, rwkv_r_k, rwkv_lnx_g, rwkv_lnx_b, gla_alpha_up, gla_alpha_b, gla_norm_g,
           proj_attn, proj_rwkv, proj_gla, w_out, router_grp_w, router_grp_b, router_exp_w,
           router_exp_b, exp_w_gate, exp_w_up, exp_w_down):
    B, S, D = x.shape
    T = B * S
    L = ada_w.shape[0]
    tm = 256
    tm_mm = 512
    mod = _adaln(c, ada_w, ada_b).reshape(L, B, 1, 6 * D)
    x2 = x.reshape(T, D)
    xin = None
    for l in range(L):
        lambda_init = 0.8 - 0.6 * math.exp(-0.3 * l)
        p_attn, p_rwkv, p_gla, p_gate = _inproj(x2, mod[l], norm1_g[l], _pad_w_in(w_in[l]), S, tm_mm)
        o_a = _attention(p_attn, attn_qn_g[l], attn_kn_g[l], attn_lambda[l], attn_subln_g[l],
                         lambda_init, B, S)
        o_r = _rwkv(p_rwkv, rwkv_mu[l], rwkv_w_up[l], rwkv_w0[l], rwkv_a_up[l], rwkv_a0[l],
                    rwkv_g_up[l], rwkv_k_k[l], rwkv_k_a[l], rwkv_r_k[l], rwkv_lnx_g[l],
                    rwkv_lnx_b[l], B, S)
        o_g = _gla(p_gla, gla_alpha_up[l], gla_alpha_b[l], gla_norm_g[l], B, S)
        n_r = N_GROUPS + N_EXPERTS
        router_w = jnp.concatenate(
            [router_grp_w[l], router_exp_w[l], jnp.zeros((D, LANES - n_r), F32)], axis=1)
        rw_hi = router_w.astype(BF16)
        router_w = jnp.stack([rw_hi, (router_w - rw_hi.astype(F32)).astype(BF16)])
        router_b = jnp.concatenate(
            [router_grp_b[l], router_exp_b[l], jnp.zeros((LANES - n_r,), F32)]).reshape(1, LANES)
        x1, h2, info, cnt = _merge(
            x2, o_a, o_r, o_g, p_gate, proj_attn[l].astype(BF16), proj_rwkv[l].astype(BF16),
            proj_gla[l].astype(BF16), w_out[l].astype(BF16), mod[l], norm2_g[l],
            router_w, router_b, S, tm_mm)
        dest0, dest1, blk_exp, n_used, blk_grp, blk_nxt, n_rows = _route_tables(info, cnt)
        if xin is None:
            xin = jnp.zeros((n_rows * NSUB, LANES), F32)
        xin = _dispatch(h2, dest0, dest1, xin, tm)
        y = _moe(xin, blk_exp, n_used, blk_grp, blk_nxt, exp_w_gate, exp_w_up, exp_w_down, l)
        x2 = _combine(x1, y, info, dest0, dest1, mod[l], S, tm)
    return x2.reshape(B, S, D)
```

```python
import functools
import math

import jax
import jax.numpy as jnp
from jax import lax
from jax.experimental import pallas as pl
from jax.experimental.pallas import tpu as pltpu

F32 = jnp.float32
BF16 = jnp.bfloat16
U32 = jnp.uint32
HIGHEST = lax.Precision.HIGHEST

D_MODEL = 1024
A_HEADS, A_DH, A_DV = 4, 64, 128
A_QW, A_VW = 512, 512
A_COLS = 1536
R_HEADS, R_N, R_W = 8, 64, 512
R_COLS = 1792
RWKV_GN_EPS = 64e-5
G_HEADS, G_DK, G_DV = 4, 64, 128
G_KW, G_VW, G_LORA = 256, 512, 16
G_TAU = 16.0
G_COLS = 1552
G_COLS_PAD = 1664
GATE_COLS = 3072
N_GROUPS, EXP_PER_GROUP, N_EXPERTS, TOP_K = 4, 8, 32, 2
D_EXPERT = 512
EPS = 1e-6

LANES = 128
SUBLANES = 8
NSUB = D_MODEL // (2 * LANES)
CHUNK = 64
VMEM_LIMIT = 56 * 1024 * 1024


def _dot(a, b):
    return jnp.dot(a.astype(BF16), b.astype(BF16), preferred_element_type=F32)


def _dot_hi(a, b):
    return jnp.dot(a, b, precision=HIGHEST, preferred_element_type=F32)


def _dot_nt(a, b, precision=None):
    return lax.dot_general(a, b, (((1,), (1,)), ((), ())), precision=precision,
                           preferred_element_type=F32)


def _dot_tn(a, b, precision=None):
    return lax.dot_general(a, b, (((0,), (0,)), ((), ())), precision=precision,
                           preferred_element_type=F32)


def _cumsum3(tri, x):
    x1 = x.astype(BF16)
    res = x - x1.astype(F32)
    x2 = res.astype(BF16)
    x3 = (res - x2.astype(F32)).astype(BF16)
    return (jnp.dot(tri, x1, preferred_element_type=F32)
            + jnp.dot(tri, x2, preferred_element_type=F32)
            + jnp.dot(tri, x3, preferred_element_type=F32))


def _pack_pair(a, b):
    ua = pltpu.bitcast(a.astype(BF16).astype(F32), U32)
    ub = pltpu.bitcast(b.astype(BF16).astype(F32), U32)
    return jnp.bitwise_or(ub, jnp.right_shift(ua, jnp.uint32(16)))


def _unpack_pair(w):
    a = pltpu.bitcast(jnp.left_shift(w, jnp.uint32(16)), F32)
    b = pltpu.bitcast(jnp.bitwise_and(w, jnp.uint32(0xFFFF0000)), F32)
    return a, b


def _sigmoid(x):
    return 1.0 / (1.0 + jnp.exp(-x))


def _softplus(x):
    return jnp.maximum(x, 0.0) + jnp.log(1.0 + jnp.exp(-jnp.abs(x)))


def _seg_ones(n, seg):
    r = lax.broadcasted_iota(jnp.int32, (n, n), 0) // seg
    c = lax.broadcasted_iota(jnp.int32, (n, n), 1) // seg
    return (r == c).astype(F32)


def _tri(n, strict):
    r = lax.broadcasted_iota(jnp.int32, (n, n), 0)
    c = lax.broadcasted_iota(jnp.int32, (n, n), 1)
    return (c < r) if strict else (c <= r)


def _adaln_kernel(c_ref, w_ref, b_ref, o_ref):
    c = c_ref[...]
    c_act = c * _sigmoid(c)
    o_ref[...] = _dot_hi(c_act, w_ref[...]) + b_ref[...]


def _adaln(c, ada_w, ada_b):
    L, D, N = ada_w.shape
    B = c.shape[0]
    tn = D
    return pl.pallas_call(
        _adaln_kernel,
        grid=(L, N // tn),
        in_specs=[
            pl.BlockSpec((B, D), lambda l, j: (0, 0)),
            pl.BlockSpec((None, D, tn), lambda l, j: (l, 0, j)),
            pl.BlockSpec((None, 1, tn), lambda l, j: (l, 0, j)),
        ],
        out_specs=pl.BlockSpec((None, B, tn), lambda l, j: (l, 0, j)),
        out_shape=jax.ShapeDtypeStruct((L, B, N), F32),
        name="adaln",
    )(c, ada_w, ada_b.reshape(L, 1, N))


_IN_SEGS = (A_COLS, R_COLS, G_COLS_PAD, GATE_COLS)
_IN_DTYPES = (BF16, F32, F32, BF16)
_IN_CHUNK = 512


def _inproj_kernel(x_ref, mod_ref, g_ref, w_ref, *o_refs):
    x = x_ref[...]
    D = x.shape[-1]
    ms = jnp.mean(x * x, axis=-1, keepdims=True)
    y = x * lax.rsqrt(ms + EPS) * g_ref[...]
    sh = mod_ref[:, 0:D]
    sc = mod_ref[:, D:2 * D]
    h = (y * (1.0 + sc) + sh).astype(BF16)
    base = 0
    for o_ref, width in zip(o_refs, _IN_SEGS):
        for c0 in range(0, width, _IN_CHUNK):
            c1 = min(c0 + _IN_CHUNK, width)
            o_ref[:, c0:c1] = jnp.dot(
                h, w_ref[:, base + c0:base + c1], preferred_element_type=F32
            ).astype(o_ref.dtype)
        base += width


def _inproj(x2, mod_l, norm_g, w_pad_all, layer, S, tm):
    T, D = x2.shape
    NP = w_pad_all.shape[2]
    tiles_per_batch = S // tm
    return pl.pallas_call(
        _inproj_kernel,
        grid=(T // tm,),
        in_specs=[
            pl.BlockSpec((tm, D), lambda i: (i, 0)),
            pl.BlockSpec((None, 1, 2 * D), lambda i: (i // tiles_per_batch, 0, 0)),
            pl.BlockSpec((1, D), lambda i: (0, 0)),
            pl.BlockSpec((None, D, NP), lambda i: (layer, 0, 0), pipeline_mode=pl.Buffered(1)),
        ],
        out_specs=[pl.BlockSpec((tm, w), lambda i: (i, 0)) for w in _IN_SEGS],
        out_shape=[jax.ShapeDtypeStruct((T, w), dt) for w, dt in zip(_IN_SEGS, _IN_DTYPES)],
        compiler_params=pltpu.CompilerParams(
            dimension_semantics=("arbitrary",), vmem_limit_bytes=VMEM_LIMIT),
        name="inproj",
    )(x2, mod_l, norm_g.reshape(1, D), w_pad_all)


A_TILE = 256
A_POS_SPLIT = 64
LOG2E = math.log2(math.e)


def _split_dot(x, ones):
    hi = x.astype(BF16)
    lo = (x - hi.astype(F32)).astype(BF16)
    return (jnp.dot(hi, ones, preferred_element_type=F32)
            + jnp.dot(lo, ones, preferred_element_type=F32))


def _eye(n):
    r = lax.broadcasted_iota(jnp.int32, (n, n), 0)
    c = lax.broadcasted_iota(jnp.int32, (n, n), 1)
    return (r == c).astype(BF16)


def _attn_kernel(q_ref, k_ref, v_ref, qg_ref, kg_ref, lam_ref, sg_ref, qaug_ref, kaug_ref, o_ref,
                 qt_s, ka_s, vt_s, s_s, *, S, lambda_init):
    t = A_TILE
    seg = _seg_ones(LANES, A_DH).astype(BF16)
    eye_t = _eye(t)
    eye_l = _eye(LANES)
    is_qk = lax.broadcasted_iota(jnp.int32, (t, LANES), 1) < A_DH

    blocks = [slice(b * t, (b + 1) * t) for b in range(S // t)]
    sides = ((q_ref, qg_ref, A_DH ** -0.5 * LOG2E), (k_ref, kg_ref, 1.0))
    xf = [[ref[rows, :].astype(F32) for rows in blocks] for ref, _, _ in sides]
    ms = [[jnp.dot((x * x).astype(BF16), seg, preferred_element_type=F32) for x in xs]
          for xs in xf]
    qn, kn = [[x * lax.rsqrt(m * (1.0 / A_DH) + EPS) * g_ref[...] * scale
               for x, m in zip(xs, mss)] for xs, mss, (_, g_ref, scale) in zip(xf, ms, sides)]
    for b, rows in enumerate(blocks):
        vt_s[:, rows] = _dot_nt(eye_l, v_ref[rows, :]).astype(BF16)
        for c in range(2):
            kc = kn[b] if c == 0 else pltpu.roll(kn[b], A_DH, 1)
            ka_s[c, rows, :] = jnp.where(is_qk, kc.astype(BF16), kaug_ref[rows, :])
    qas = [[jnp.where(is_qk, (qn[b] if c == 0 else pltpu.roll(qn[b], A_DH, 1)).astype(BF16),
                      qaug_ref[rows, :]) for c in range(2)] for b, rows in enumerate(blocks)]
    qts = [[_dot_nt(eye_l, qa) for qa in pair] for pair in qas]
    for b, rows in enumerate(blocks):
        for c in range(2):
            qt_s[c, :, rows] = qts[b][c].astype(BF16)

    lv = lam_ref[...]
    lam = (jnp.exp(jnp.sum(lv[0:1] * lv[1:2], axis=-1, keepdims=True))
           - jnp.exp(jnp.sum(lv[2:3] * lv[3:4], axis=-1, keepdims=True)) + lambda_init)
    causal = (lax.broadcasted_iota(jnp.int32, (t, t), 0)
              <= lax.broadcasted_iota(jnp.int32, (t, t), 1))

    nt = S // t

    def score_block(i, c, j, st):
        s = jnp.dot(ka_s[c, j * t:(j + 1) * t, :], qt_s[c, :, i * t:(i + 1) * t],
                    preferred_element_type=F32)
        if j == i:
            s = jnp.where(causal, s, -jnp.inf)
        s_s[i % 2, c, j * t:(j + 1) * t, :] = s
        mj = jnp.max(s, axis=0, keepdims=True)
        st["m"][c] = mj if st["m"][c] is None else jnp.maximum(st["m"][c], mj)

    def value_block(i, c, j, st):
        p = jnp.exp2(s_s[i % 2, c, j * t:(j + 1) * t, :] - st["m"][c])
        st["l"][c] = st["l"][c] + jnp.sum(p, axis=0, keepdims=True)
        st["acc"][c] = st["acc"][c] + jnp.dot(vt_s[:, j * t:(j + 1) * t], p.astype(BF16),
                                              preferred_element_type=F32)

    def finish(i, st):
        o = st["acc"][0] / st["l"][0] - lam * (st["acc"][1] / st["l"][1])
        ms = jnp.mean(o * o, axis=0, keepdims=True)
        o = o * lax.rsqrt(ms + EPS) * sg_ref[...] * (1.0 - lambda_init)
        o_ref[i * t:(i + 1) * t, :] = _dot_nt(eye_t, o.astype(BF16)).astype(o_ref.dtype)

    prev = None
    for i in range(nt + 1):
        cur = None
        first = []
        if i < nt:
            cur = dict(m=[None, None], l=[jnp.zeros((1, t), F32)] * 2,
                       acc=[jnp.zeros((A_DV, t), F32)] * 2)
            first = [(c, j) for j in range(i + 1) for c in range(2)]
        second = [(c, j) for j in range(i) for c in range(2)] if prev is not None else []
        for n in range(max(len(first), len(second))):
            if n < len(first):
                score_block(i, first[n][0], first[n][1], cur)
            if n < len(second):
                value_block(i - 1, second[n][0], second[n][1], prev)
        if prev is not None:
            finish(i - 1, prev)
        prev = cur


def _alibi_columns(S):
    pos = jnp.arange(S, dtype=jnp.int32)[:, None]
    hi = (pos // A_POS_SPLIT).astype(F32)
    lo = (pos % A_POS_SPLIT).astype(F32)
    kl = jnp.arange(LANES, dtype=jnp.int32)[None, :] - A_DH
    term = kl // 2
    used = jnp.logical_and(kl >= 0, kl < 8)
    k_aug = jnp.where(used, jnp.where(term == 0, hi, jnp.where(term == 1, lo, 1.0)), 0.0)
    q_augs = []
    for h in range(A_HEADS):
        s2 = 2.0 ** (-8.0 * (h + 1) / A_HEADS) * LOG2E
        qv = jnp.where(term == 0, A_POS_SPLIT * s2,
             jnp.where(term == 1, s2,
             jnp.where(term == 2, -A_POS_SPLIT * s2 * hi, -s2 * lo)))
        qv_hi = qv.astype(BF16).astype(F32)
        q_augs.append(jnp.where(used, jnp.where(kl % 2 == 0, qv_hi, qv - qv_hi), 0.0))
    return jnp.stack(q_augs).astype(BF16), k_aug.astype(BF16)


def _attention(p_attn, qn_g, kn_g, lam_vecs, subln_g, lambda_init, B, S):
    pa = p_attn.reshape(B, S, A_COLS)
    dup = lambda g: jnp.concatenate([g, g]).reshape(1, LANES)
    q_aug, k_aug = _alibi_columns(S)
    nqb = A_QW // LANES
    kern = functools.partial(_attn_kernel, S=S, lambda_init=lambda_init)
    out = pl.pallas_call(
        kern,
        grid=(B, A_HEADS),
        in_specs=[
            pl.BlockSpec((None, S, LANES), lambda b, h: (b, 0, h)),
            pl.BlockSpec((None, S, LANES), lambda b, h: (b, 0, nqb + h)),
            pl.BlockSpec((None, S, LANES), lambda b, h: (b, 0, 2 * nqb + h)),
            pl.BlockSpec((1, LANES), lambda b, h: (0, 0)),
            pl.BlockSpec((1, LANES), lambda b, h: (0, 0)),
            pl.BlockSpec((4, A_DH), lambda b, h: (0, 0)),
            pl.BlockSpec((A_DV, 1), lambda b, h: (0, 0)),
            pl.BlockSpec((None, S, LANES), lambda b, h: (h, 0, 0)),
            pl.BlockSpec((S, LANES), lambda b, h: (0, 0)),
        ],
        out_specs=pl.BlockSpec((None, S, A_DV), lambda b, h: (b, 0, h)),
        out_shape=jax.ShapeDtypeStruct((B, S, A_VW), BF16),
        scratch_shapes=[pltpu.VMEM((2, LANES, S), BF16), pltpu.VMEM((2, S, LANES), BF16),
                        pltpu.VMEM((A_DV, S), BF16), pltpu.VMEM((2, 2, S, A_TILE), F32)],
        compiler_params=pltpu.CompilerParams(
            dimension_semantics=("arbitrary", "arbitrary"), vmem_limit_bytes=VMEM_LIMIT),
        name="diff_attn",
    )(pa, pa, pa, dup(qn_g), dup(kn_g), lam_vecs, subln_g.reshape(A_DV, 1), q_aug, k_aug)
    return out.reshape(B * S, A_VW)


R_GROUP = 4
R_GW = R_GROUP * R_N


def _rwkv_kernel(p_ref, mu_ref, wup_ref, w0_ref, aup_ref, a0_ref, gup_ref, kk_ref, ka_ref,
                 rk_ref, lg_ref, lb_ref, o_ref,
                 carry_ref, st_ref, al_s, be_s, ka_s, rh_s, bt_s, kt_s, v_s, gc_s, y_s, *, TB):
    @pl.when(pl.program_id(1) == 0)
    def _():
        carry_ref[...] = jnp.zeros_like(carry_ref)
        st_ref[...] = jnp.zeros_like(st_ref)

    xs = p_ref[...]
    prev = pltpu.roll(xs, 1, 0)
    row = lax.broadcasted_iota(jnp.int32, (TB, 1), 0)
    prev = jnp.where(row == 0, carry_ref[...], prev)
    carry_ref[...] = xs[TB - 1:TB, :]
    xm = xs + (prev - xs) * mu_ref[...]
    r = xm[:, 0:R_W]
    k = xm[:, R_W:2 * R_W]
    v = xm[:, 2 * R_W:3 * R_W]
    wa = xm[:, 3 * R_W:3 * R_W + LANES]
    gd = xm[:, 3 * R_W + LANES:3 * R_W + 2 * LANES]
    wz = w0_ref[...] + _dot(jnp.tanh(wa), wup_ref[...])
    lw = -math.exp(-0.5) * _sigmoid(wz)
    a = _sigmoid(a0_ref[...] + _dot(wa, aup_ref[...]))
    g = _dot(_sigmoid(gd), gup_ref[...])
    seg = _seg_ones(R_GW, R_N).astype(BF16)

    def head_sum(x, passes):
        f = _split_dot if passes == 2 else (
            lambda t, o: jnp.dot(t.astype(BF16), o, preferred_element_type=F32))
        return jnp.concatenate(
            [f(x[:, j * R_GW:(j + 1) * R_GW], seg) for j in range(R_W // R_GW)], axis=-1)

    kk = k * kk_ref[...]
    kk = kk * jnp.minimum(lax.rsqrt(head_sum(kk * kk, 1)), 1e12)
    k2 = k * (1.0 + (a - 1.0) * ka_ref[...])
    bonus = head_sum(r * k2 * rk_ref[...], 2) * v
    bv = kk * a

    rr = lax.broadcasted_iota(jnp.int32, (TB, TB), 0)
    cc = lax.broadcasted_iota(jnp.int32, (TB, TB), 1)
    tril_blk = jnp.logical_and(rr // CHUNK == cc // CHUNK, cc <= rr).astype(BF16)
    Lg = _cumsum3(tril_blk, lw)
    inv = jnp.exp(-Lg)
    al_s[...] = (jnp.exp(Lg - lw) * kk).astype(BF16)
    be_s[...] = (bv * inv).astype(BF16)
    ka_s[...] = (k2 * inv).astype(BF16)
    rh_s[...] = (jnp.exp(Lg) * r).astype(BF16)
    v_s[...] = v.astype(BF16)
    for c in range(TB // CHUNK):
        rows = slice(c * CHUNK, (c + 1) * CHUNK)
        gC = Lg[(c + 1) * CHUNK - 1:(c + 1) * CHUNK, :]
        tail = jnp.exp(gC - Lg[rows, :])
        bt_s[rows, :] = (bv[rows, :] * tail).astype(BF16)
        kt_s[rows, :] = (k2[rows, :] * tail).astype(BF16)
        gc_s[c * SUBLANES:(c + 1) * SUBLANES, :] = jnp.broadcast_to(jnp.exp(gC), (SUBLANES, R_W))

    ri = lax.broadcasted_iota(jnp.int32, (R_GW, R_GW), 0)
    ci = lax.broadcasted_iota(jnp.int32, (R_GW, R_GW), 1)
    blk = ri // R_N == ci // R_N
    strict = ci % R_N < ri % R_N
    strict_t = ri % R_N < ci % R_N
    incl = ci % R_N <= ri % R_N
    zero = jnp.zeros((), BF16)

    def expand(x):
        return jnp.where(blk, jnp.concatenate([x] * R_GROUP, axis=0), zero)

    n_groups = R_HEADS // R_GROUP
    pair = 2

    def chunk_pair(ip, _):
        chains = []
        for dc in range(pair):
            c = ip * pair + dc
            rows = pl.ds(pl.multiple_of(c * CHUNK, CHUNK), CHUNK)
            for gi in range(n_groups):
                chains.append(dict(c=c, rows=rows, gi=gi, cols=slice(gi * R_GW, (gi + 1) * R_GW)))
        for ch in chains:
            rows, cols = ch["rows"], ch["cols"]
            ch["A"] = expand(al_s[rows, cols])
            ch["R"] = expand(rh_s[rows, cols])
            ch["B"] = expand(be_s[rows, cols])
            ch["K"] = expand(ka_s[rows, cols])
            vc = v_s[rows, cols]
            ch["V"] = jnp.concatenate(
                [vc[:, h * R_N:(h + 1) * R_N] for h in range(R_GROUP)], axis=0)
        for ch in chains:
            ch["X"] = -jnp.where(strict, _dot_nt(ch["A"], ch["B"]), 0.0)
            ch["m_ak_t"] = jnp.where(strict_t, _dot_nt(ch["K"], ch["A"]), 0.0).astype(BF16)
            ch["m_rb"] = jnp.where(incl, _dot_nt(ch["R"], ch["B"]), 0.0).astype(BF16)
            ch["m_rk"] = jnp.where(incl, _dot_nt(ch["R"], ch["K"]), 0.0).astype(BF16)
        for ch in chains:
            ch["Z"] = ch["A"].astype(F32)
            ch["Wt"] = _dot_tn(ch["V"], ch["m_ak_t"])
        n = 1
        while True:
            last = 2 * n >= CHUNK
            for ch in chains:
                Xb = ch["X"].astype(BF16)
                ch["Z"] = ch["Z"] + jnp.dot(Xb, ch["Z"].astype(BF16), preferred_element_type=F32)
                ch["Wt"] = ch["Wt"] + _dot_nt(ch["Wt"].astype(BF16), Xb)
                if not last:
                    ch["X"] = jnp.dot(Xb, Xb, preferred_element_type=F32)
            n *= 2
            if last:
                break
        for ch in chains:
            rows, cols = ch["rows"], ch["cols"]
            Bt = expand(bt_s[rows, cols])
            Kt = expand(kt_s[rows, cols])
            Zb = ch["Z"].astype(BF16)
            Wtb = ch["Wt"].astype(BF16)
            ch["y_a"] = (ch["R"].astype(F32)
                         - jnp.dot(ch["m_rb"], Zb, preferred_element_type=F32)).astype(BF16)
            ch["y_b"] = (jnp.dot(ch["m_rk"], ch["V"], preferred_element_type=F32)
                         - _dot_nt(ch["m_rb"], Wtb))
            ch["p_neg"] = _dot_tn(Zb, Bt).astype(BF16)
            ch["q"] = _dot_tn(ch["V"], Kt) - jnp.dot(Wtb, Bt, preferred_element_type=F32)
        for ch in chains:
            rows, cols, gi = ch["rows"], ch["cols"], ch["gi"]
            g0 = pl.multiple_of(ch["c"] * SUBLANES, SUBLANES)
            S0 = st_ref[gi]
            S0b = S0.astype(BF16)
            y = _dot_nt(ch["y_a"], S0b) + ch["y_b"]
            st_ref[gi] = (S0 * gc_s[pl.ds(g0, SUBLANES), cols][0:1, :]
                          - jnp.dot(S0b, ch["p_neg"], preferred_element_type=F32) + ch["q"])
            for h in range(R_GROUP):
                hh = gi * R_GROUP + h
                y_s[rows, hh * R_N:(hh + 1) * R_N] = y[h * R_N:(h + 1) * R_N, :]
        return 0

    lax.fori_loop(0, TB // CHUNK // pair, chunk_pair, 0)

    y = y_s[...]
    mean = head_sum(y, 2) * (1.0 / R_N)
    yc = y - mean
    var = head_sum(yc * yc, 1) * (1.0 / R_N)
    yn = yc * lax.rsqrt(var + RWKV_GN_EPS) * lg_ref[...] + lb_ref[...]
    o_ref[...] = ((yn + bonus) * g).astype(o_ref.dtype)


def _rwkv(p_rwkv, mu, w_up, w0, a_up, a0, g_up, k_k, k_a, r_k, lnx_g, lnx_b, B, S):
    TB = 256
    T = B * S
    nt = S // TB
    row = lambda t: t.reshape(1, -1)
    zeros = jnp.zeros((R_N, R_W), F32)
    wup_pad = jnp.concatenate([w_up, zeros], axis=0)
    aup_pad = jnp.concatenate([zeros, a_up], axis=0)
    vec = lambda n: pl.BlockSpec((1, n), lambda b, i: (0, 0))
    mat = lambda m, n: pl.BlockSpec((m, n), lambda b, i: (0, 0))
    kern = functools.partial(_rwkv_kernel, TB=TB)
    return pl.pallas_call(
        kern,
        grid=(B, nt),
        in_specs=[
            pl.BlockSpec((TB, R_COLS), lambda b, i: (b * nt + i, 0)),
            vec(R_COLS), mat(LANES, R_W), vec(R_W), mat(LANES, R_W), vec(R_W), mat(LANES, R_W),
            vec(R_W), vec(R_W), vec(R_W), vec(R_W), vec(R_W),
        ],
        out_specs=pl.BlockSpec((TB, R_W), lambda b, i: (b * nt + i, 0)),
        out_shape=jax.ShapeDtypeStruct((T, R_W), BF16),
        scratch_shapes=[
            pltpu.VMEM((1, R_COLS), F32),
            pltpu.VMEM((R_HEADS // R_GROUP, R_N, R_GW), F32),
        ] + [pltpu.VMEM((TB, R_W), BF16)] * 7 + [
            pltpu.VMEM((TB // CHUNK * SUBLANES, R_W), F32),
            pltpu.VMEM((TB, R_W), F32),
        ],
        compiler_params=pltpu.CompilerParams(
            dimension_semantics=("arbitrary", "arbitrary"), vmem_limit_bytes=VMEM_LIMIT),
        name="rwkv7",
    )(p_rwkv, row(mu), wup_pad, row(w0), aup_pad, row(a0), g_up, row(k_k), row(k_a),
      row(r_k), row(lnx_g), row(lnx_b))


G_SUBTILE = 256


def _gla_kernel(p_ref, aup_ref, ab_ref, ng_ref, o_ref, st_ref, *, TB):
    @pl.when(pl.program_id(1) == 0)
    def _():
        st_ref[...] = jnp.zeros_like(st_ref)

    c_gv = 2 * G_KW
    c_ad = c_gv + G_VW
    c_gate = c_ad + LANES
    sub = G_SUBTILE
    subs = range(TB // sub)
    nchunk = sub // CHUNK
    rr = lax.broadcasted_iota(jnp.int32, (sub, sub), 0)
    cc = lax.broadcasted_iota(jnp.int32, (sub, sub), 1)
    causal = jnp.logical_and(rr // CHUNK == cc // CHUNK, cc <= rr)
    causal_b = causal.astype(BF16)
    heads = range(G_HEADS)
    sls = [slice(h * G_DK, (h + 1) * G_DK) for h in heads]
    vss = [slice(h * G_DV, (h + 1) * G_DV) for h in heads]

    rws = [slice(u * sub, (u + 1) * sub) for u in subs]
    zs = [_dot(p_ref[r, c_ad:c_ad + LANES], aup_ref[...]) + ab_ref[...] for r in rws]
    las = [-_softplus(-z) * (1.0 / G_TAU) for z in zs]
    bs = [_cumsum3(causal_b, la) for la in las]
    ks = [p_ref[r, G_KW:2 * G_KW] for r in rws]
    vbs = [p_ref[r, c_gv:c_gv + G_VW].astype(BF16) for r in rws]
    qes = [(p_ref[r, 0:G_KW] * (G_DK ** -0.5) * jnp.exp(b)).astype(BF16) for r, b in zip(rws, bs)]
    kes = [(k * jnp.exp(-b)).astype(BF16) for k, b in zip(ks, bs)]
    scs = [[jnp.where(causal, _dot_nt(qes[u][:, sls[h]], kes[u][:, sls[h]]), 0.0) for h in heads]
           for u in subs]
    o_intra = [[_dot(scs[u][h], vbs[u][:, vss[h]]) for h in heads] for u in subs]
    kts, e_lasts, kvs = {}, {}, {}
    for u in subs:
        for c in range(nchunk):
            rows = slice(c * CHUNK, (c + 1) * CHUNK)
            b_last = bs[u][(c + 1) * CHUNK - 1:(c + 1) * CHUNK, :]
            kt = (ks[u][rows, :] * jnp.exp(b_last - bs[u][rows, :])).astype(BF16)
            e_lasts[u, c] = jnp.exp(b_last)
            for h in heads:
                kvs[u, c, h] = _dot_tn(vbs[u][rows, vss[h]], kt[:, sls[h]])
    states = [st_ref[h] for h in heads]
    parts = [[] for _ in heads]
    for u in subs:
        for c in range(nchunk):
            rows = slice(c * CHUNK, (c + 1) * CHUNK)
            for h in heads:
                parts[h].append(o_intra[u][h][rows, :]
                                + _dot_nt(qes[u][rows, sls[h]], states[h].astype(BF16)))
                states[h] = states[h] * e_lasts[u, c][:, sls[h]] + kvs[u, c, h]
    for h in heads:
        vs = vss[h]
        st_ref[h] = states[h]
        o = jnp.concatenate(parts[h], axis=0)
        ms = jnp.mean(o * o, axis=-1, keepdims=True)
        gt = p_ref[:, c_gate + h * G_DV:c_gate + (h + 1) * G_DV]
        o = o * lax.rsqrt(ms + EPS) * ng_ref[...] * (gt * _sigmoid(gt))
        o_ref[:, vs] = o.astype(o_ref.dtype)


def _gla(p_gla, alpha_up, alpha_b, norm_g, B, S):
    TB = 512
    T = B * S
    nt = S // TB
    aup_pad = jnp.concatenate([alpha_up, jnp.zeros((LANES - G_LORA, G_KW), F32)], axis=0)
    kern = functools.partial(_gla_kernel, TB=TB)
    return pl.pallas_call(
        kern,
        grid=(B, nt),
        in_specs=[
            pl.BlockSpec((TB, G_COLS_PAD), lambda b, i: (b * nt + i, 0)),
            pl.BlockSpec((LANES, G_KW), lambda b, i: (0, 0)),
            pl.BlockSpec((1, G_KW), lambda b, i: (0, 0)),
            pl.BlockSpec((1, G_DV), lambda b, i: (0, 0)),
        ],
        out_specs=pl.BlockSpec((TB, G_VW), lambda b, i: (b * nt + i, 0)),
        out_shape=jax.ShapeDtypeStruct((T, G_VW), BF16),
        scratch_shapes=[pltpu.VMEM((G_HEADS, G_DV, G_DK), F32)],
        compiler_params=pltpu.CompilerParams(
            dimension_semantics=("arbitrary", "arbitrary"), vmem_limit_bytes=VMEM_LIMIT),
        name="gla",
    )(p_gla, aup_pad, alpha_b.reshape(1, G_KW), norm_g.reshape(1, G_DV))


def _merge_kernel(x_ref, oa_ref, or_ref, og_ref, gate_ref, pa_ref, pr_ref, pg_ref, wo_ref,
                  mod_ref, g2_ref, rw_ref, rb_ref, x1_ref, h2_ref, info_ref, cnt_ref, carry_ref):
    D = D_MODEL

    @pl.when(pl.program_id(0) == 0)
    def _():
        carry_ref[...] = jnp.zeros_like(carry_ref)

    merged = (_sigmoid(gate_ref[:, 0:D].astype(F32))
              * jnp.dot(oa_ref[...], pa_ref[...], preferred_element_type=F32)
              + _sigmoid(gate_ref[:, D:2 * D].astype(F32))
              * jnp.dot(or_ref[...], pr_ref[...], preferred_element_type=F32)
              + _sigmoid(gate_ref[:, 2 * D:3 * D].astype(F32))
              * jnp.dot(og_ref[...], pg_ref[...], preferred_element_type=F32))
    gt1 = mod_ref[:, 2 * D:3 * D]
    sh2 = mod_ref[:, 3 * D:4 * D]
    sc2 = mod_ref[:, 4 * D:5 * D]
    x1 = x_ref[...] + gt1 * jnp.dot(merged.astype(BF16), wo_ref[...], preferred_element_type=F32)
    x1_ref[...] = x1
    ms = jnp.mean(x1 * x1, axis=-1, keepdims=True)
    h2 = x1 * lax.rsqrt(ms + EPS) * g2_ref[...] * (1.0 + sc2) + sh2
    for s in range(NSUB):
        h2_ref[pl.ds(s, h2.shape[0], stride=NSUB), :] = _pack_pair(
            h2[:, 2 * s * LANES:(2 * s + 1) * LANES], h2[:, (2 * s + 1) * LANES:(2 * s + 2) * LANES])
    h_hi = h2.astype(BF16)
    h_lo = (h2 - h_hi.astype(F32)).astype(BF16)
    logits = (jnp.dot(h_hi, rw_ref[0], preferred_element_type=F32)
              + jnp.dot(h_lo, rw_ref[0], preferred_element_type=F32)
              + jnp.dot(h_hi, rw_ref[1], preferred_element_type=F32) + rb_ref[...])
    info_ref[...], cnt_ref[...] = _route_tile(logits, carry_ref)


def _merge(x2, o_a, o_r, o_g, p_gate, proj_a, proj_r, proj_g, w_out, mod_l, norm2_g,
           router_w, router_b, S, tm):
    T, D = x2.shape
    tiles_per_batch = S // tm
    tile = lambda w: pl.BlockSpec((tm, w), lambda i: (i, 0))
    const = lambda m, n: pl.BlockSpec((m, n), lambda i: (0, 0))
    return pl.pallas_call(
        _merge_kernel,
        grid=(T // tm,),
        in_specs=[
            tile(D), tile(A_VW), tile(R_W), tile(G_VW), tile(GATE_COLS),
            const(A_VW, D), const(R_W, D), const(G_VW, D), const(D, D),
            pl.BlockSpec((None, 1, 6 * D), lambda i: (i // tiles_per_batch, 0, 0)),
            const(1, D), pl.BlockSpec((2, D, LANES), lambda i: (0, 0, 0)), const(1, LANES),
        ],
        out_specs=[
            tile(D),
            pl.BlockSpec((tm * NSUB, LANES), lambda i: (i, 0)),
            tile(LANES),
            const(1, LANES),
        ],
        out_shape=[
            jax.ShapeDtypeStruct((T, D), F32),
            jax.ShapeDtypeStruct((T * NSUB, LANES), U32),
            jax.ShapeDtypeStruct((T, LANES), F32),
            jax.ShapeDtypeStruct((1, LANES), F32),
        ],
        scratch_shapes=[pltpu.VMEM((1, LANES), F32)],
        compiler_params=pltpu.CompilerParams(
            dimension_semantics=("arbitrary",), vmem_limit_bytes=VMEM_LIMIT),
        name="merge",
    )(x2, o_a, o_r, o_g, p_gate, proj_a, proj_r, proj_g, w_out, mod_l,
      norm2_g.reshape(1, D), router_w, router_b)


MOE_ROWS = 256
E_LANE0 = N_GROUPS


def _route_tile(lg, carry_ref):
    n = lg.shape[0]
    lane = lax.broadcasted_iota(jnp.int32, (n, LANES), 1).astype(F32)
    neg = -jnp.inf
    big = float(LANES)

    def first_max(vals):
        m = jnp.max(vals, axis=-1, keepdims=True)
        idx = jnp.min(jnp.where(vals == m, lane, big), axis=-1, keepdims=True)
        return m, idx

    in_grp = lane < N_GROUPS
    gm, grp = first_max(jnp.where(in_grp, lg, neg))
    g_prob = 1.0 / jnp.sum(jnp.where(in_grp, jnp.exp(lg - gm), 0.0), axis=-1, keepdims=True)
    lo = E_LANE0 + grp * EXP_PER_GROUP
    el = jnp.where(jnp.logical_and(lane >= lo, lane < lo + EXP_PER_GROUP), lg, neg)
    v1, i1 = first_max(el)
    v2, i2 = first_max(jnp.where(lane == i1, neg, el))
    e21 = jnp.exp(v2 - v1)
    w0 = g_prob / (1.0 + e21)
    w1 = g_prob * e21 / (1.0 + e21)
    oh0 = lane == i1
    oh1 = lane == i2
    oh = jnp.logical_or(oh0, oh1).astype(F32)
    before = _tri(n, True).astype(BF16)
    cnt = jnp.dot(before, oh.astype(BF16), preferred_element_type=F32) + carry_ref[...]
    rank0 = jnp.sum(jnp.where(oh0, cnt, 0.0), axis=-1, keepdims=True)
    rank1 = jnp.sum(jnp.where(oh1, cnt, 0.0), axis=-1, keepdims=True)
    carry = carry_ref[...] + jnp.sum(oh, axis=0, keepdims=True)
    carry_ref[...] = carry
    cols = (i1 - E_LANE0, i2 - E_LANE0, rank0, rank1, w0, w1)
    info = jnp.zeros((n, LANES), F32)
    for j, col in enumerate(cols):
        info = jnp.where(lane == j, col, info)
    return info, carry


def _route_tables(info, cnt):
    T = info.shape[0]
    A = T * TOP_K
    counts = cnt[0, E_LANE0:E_LANE0 + N_EXPERTS].astype(jnp.int32)
    padded = (counts + MOE_ROWS - 1) // MOE_ROWS * MOE_ROWS
    pad_end = jnp.cumsum(padded)
    pad_start = pad_end - padded
    n_blocks = -(-A // MOE_ROWS) + N_EXPERTS
    eid = info[:, 0:2].astype(jnp.int32)
    is_e = eid[:, :, None] == jnp.arange(N_EXPERTS, dtype=jnp.int32)
    dest = (jnp.sum(jnp.where(is_e, pad_start, 0), axis=-1)
            + info[:, 2:4].astype(jnp.int32))
    blk_start = jnp.arange(n_blocks, dtype=jnp.int32) * MOE_ROWS
    blk_exp = jnp.minimum(jnp.sum(pad_end[None, :] <= blk_start[:, None], axis=1),
                          N_EXPERTS - 1).astype(jnp.int32)
    n_used = (pad_end[-1:] // MOE_ROWS).astype(jnp.int32)
    nonempty = counts > 0
    ids = jnp.arange(N_EXPERTS, dtype=jnp.int32)
    grp_of = jnp.cumsum(nonempty.astype(jnp.int32)) - 1
    later = jnp.logical_and(ids[None, :] > ids[:, None], nonempty[None, :])
    nxt_of = jnp.min(jnp.where(later, ids[None, :], N_EXPERTS), axis=1)
    nxt_of = jnp.where(nxt_of == N_EXPERTS, -1, nxt_of)
    is_b = blk_exp[:, None] == ids[None, :]
    blk_grp = jnp.sum(jnp.where(is_b, grp_of[None, :], 0), axis=1).astype(jnp.int32)
    blk_nxt = jnp.sum(jnp.where(is_b, nxt_of[None, :], 0), axis=1).astype(jnp.int32)
    return dest[:, 0], dest[:, 1], blk_exp, n_used, blk_grp, blk_nxt, n_blocks * MOE_ROWS


def _dispatch_kernel(d0_ref, d1_ref, h2_ref, xin_in, xin_hbm, sem):
    del xin_in
    n = h2_ref.shape[0] // NSUB
    base = pl.program_id(0) * n

    def slab(ref, row):
        return ref.at[pl.ds(pl.multiple_of(row * NSUB, NSUB), NSUB)]

    def body(r, _):
        pltpu.make_async_copy(slab(h2_ref, r), slab(xin_hbm, d0_ref[base + r]), sem).start(0)
        pltpu.make_async_copy(slab(h2_ref, r), slab(xin_hbm, d1_ref[base + r]), sem).start(1)
        return 0

    lax.fori_loop(0, n, body, 0, unroll=8)
    for _ in range(TOP_K):
        pltpu.make_async_copy(h2_ref, xin_hbm.at[pl.ds(0, n * NSUB)], sem).wait()


def _dispatch(h2_slab, dest0, dest1, xin_init, td):
    T = h2_slab.shape[0] // NSUB
    grid_spec = pltpu.PrefetchScalarGridSpec(
        num_scalar_prefetch=2,
        grid=(T // td,),
        in_specs=[pl.BlockSpec((td * NSUB, LANES), lambda i, d0, d1: (i, 0)),
                  pl.BlockSpec(memory_space=pl.ANY)],
        out_specs=pl.BlockSpec(memory_space=pl.ANY),
        scratch_shapes=[pltpu.SemaphoreType.DMA(())],
    )
    return pl.pallas_call(
        _dispatch_kernel,
        grid_spec=grid_spec,
        out_shape=jax.ShapeDtypeStruct(xin_init.shape, xin_init.dtype),
        input_output_aliases={3: 0},
        compiler_params=pltpu.CompilerParams(dimension_semantics=("arbitrary",)),
        name="moe_dispatch",
    )(dest0, dest1, h2_slab, xin_init)


def _moe_kernel(be_ref, nu_ref, grp_ref, nxt_ref, x_ref, wg_hbm, wu_hbm, wd_hbm, y_ref,
                wgf, wuf, wdf, wgb, wub, wdb, sem, *, layer):
    i = pl.program_id(0)

    def weight_copies(expert, slot):
        return [pltpu.make_async_copy(src.at[layer, expert], dst.at[slot], sem.at[slot, k])
                for k, (src, dst) in enumerate(((wg_hbm, wgf), (wu_hbm, wuf), (wd_hbm, wdf)))]

    @pl.when(i < nu_ref[0])
    def _():
        changed = jnp.logical_or(i == 0, be_ref[i] != be_ref[jnp.maximum(i - 1, 0)])

        @pl.when(changed)
        def _():
            slot = grp_ref[i] % 2

            @pl.when(i == 0)
            def _():
                for cp in weight_copies(be_ref[0], 0):
                    cp.start()

            for cp in weight_copies(be_ref[i], slot):
                cp.wait()
            wgb[...] = wgf[slot].astype(BF16)
            wub[...] = wuf[slot].astype(BF16)
            wdb[...] = wdf[slot].astype(BF16)

            @pl.when(nxt_ref[i] >= 0)
            def _():
                for cp in weight_copies(nxt_ref[i], 1 - slot):
                    cp.start()

        kw = 2 * LANES
        hg = hu = None
        for j in range(NSUB):
            xa, xb = _unpack_pair(x_ref[pl.ds(j, MOE_ROWS, stride=NSUB), :])
            xj = jnp.concatenate([xa.astype(BF16), xb.astype(BF16)], axis=-1)
            dg = jnp.dot(xj, wgb[j * kw:(j + 1) * kw, :], preferred_element_type=F32)
            du = jnp.dot(xj, wub[j * kw:(j + 1) * kw, :], preferred_element_type=F32)
            hg = dg if hg is None else hg + dg
            hu = du if hu is None else hu + du
        hid = (hg * _sigmoid(hg) * hu).astype(BF16)
        for j in range(NSUB):
            yj = jnp.dot(hid, wdb[:, j * kw:(j + 1) * kw], preferred_element_type=F32)
            y_ref[pl.ds(j, MOE_ROWS, stride=NSUB), :] = _pack_pair(yj[:, :LANES], yj[:, LANES:])

    @pl.when(i >= nu_ref[0])
    def _():
        y_ref[...] = jnp.zeros_like(y_ref)


def _moe(xin, blk_exp, n_used, blk_grp, blk_nxt, w_gate, w_up, w_down, layer):
    blk_rows = MOE_ROWS * NSUB
    n_blocks = xin.shape[0] // blk_rows
    last = lambda i, nu: jnp.minimum(i, nu[0] - 1)
    grid_spec = pltpu.PrefetchScalarGridSpec(
        num_scalar_prefetch=4,
        grid=(n_blocks,),
        in_specs=[
            pl.BlockSpec((blk_rows, LANES), lambda i, be, nu, gr, nx: (last(i, nu), 0)),
            pl.BlockSpec(memory_space=pl.ANY), pl.BlockSpec(memory_space=pl.ANY),
            pl.BlockSpec(memory_space=pl.ANY),
        ],
        out_specs=pl.BlockSpec((blk_rows, LANES), lambda i, be, nu, gr, nx: (i, 0)),
        scratch_shapes=[
            pltpu.VMEM((2, D_MODEL, D_EXPERT), F32),
            pltpu.VMEM((2, D_MODEL, D_EXPERT), F32),
            pltpu.VMEM((2, D_EXPERT, D_MODEL), F32),
            pltpu.VMEM((D_MODEL, D_EXPERT), BF16),
            pltpu.VMEM((D_MODEL, D_EXPERT), BF16),
            pltpu.VMEM((D_EXPERT, D_MODEL), BF16),
            pltpu.SemaphoreType.DMA((2, 3)),
        ],
    )
    return pl.pallas_call(
        functools.partial(_moe_kernel, layer=layer),
        grid_spec=grid_spec,
        out_shape=jax.ShapeDtypeStruct(xin.shape, xin.dtype),
        compiler_params=pltpu.CompilerParams(
            dimension_semantics=("arbitrary",), vmem_limit_bytes=VMEM_LIMIT),
        name="moe_ffn",
    )(blk_exp, n_used, blk_grp, blk_nxt, xin, w_gate, w_up, w_down)


def _combine_kernel(d0_ref, d1_ref, y_hbm, x1_ref, info_ref, mod_ref, o_ref, ybuf, sem):
    i = pl.program_id(0)
    nsteps = pl.num_programs(0)
    n = x1_ref.shape[0]
    slot = i % 2
    D = D_MODEL

    def slab(ref, row):
        return ref.at[pl.ds(pl.multiple_of(row * NSUB, NSUB), NSUB)]

    def start_gather(step, sl):
        base = step * n

        def body(r, _):
            pltpu.make_async_copy(
                slab(y_hbm, d0_ref[base + r]), slab(ybuf.at[sl, 0], r), sem.at[sl]).start(0)
            pltpu.make_async_copy(
                slab(y_hbm, d1_ref[base + r]), slab(ybuf.at[sl, 1], r), sem.at[sl]).start(1)
            return 0
        lax.fori_loop(0, n, body, 0, unroll=8)

    @pl.when(i == 0)
    def _():
        start_gather(0, 0)

    @pl.when(i + 1 < nsteps)
    def _():
        start_gather(i + 1, 1 - slot)

    for k in range(TOP_K):
        pltpu.make_async_copy(
            y_hbm.at[pl.ds(0, n * NSUB)], ybuf.at[slot, k], sem.at[slot]).wait()

    w0 = info_ref[:, 4:5]
    w1 = info_ref[:, 5:6]
    for s in range(NSUB):
        piece = pl.ds(s, n, stride=NSUB)
        halves0 = _unpack_pair(ybuf[slot, 0, piece, :])
        halves1 = _unpack_pair(ybuf[slot, 1, piece, :])
        for half in range(2):
            c0 = (2 * s + half) * LANES
            cols = slice(c0, c0 + LANES)
            gt2 = mod_ref[:, 5 * D + c0:5 * D + c0 + LANES]
            moe = w0 * halves0[half] + w1 * halves1[half]
            o_ref[:, cols] = x1_ref[:, cols] + gt2 * moe


def _combine(x1, y, info, dest0, dest1, mod_l, S, tm):
    T, D = x1.shape
    tiles_per_batch = S // tm
    grid_spec = pltpu.PrefetchScalarGridSpec(
        num_scalar_prefetch=2,
        grid=(T // tm,),
        in_specs=[
            pl.BlockSpec(memory_space=pl.ANY),
            pl.BlockSpec((tm, D), lambda i, d0, d1: (i, 0)),
            pl.BlockSpec((tm, LANES), lambda i, d0, d1: (i, 0)),
            pl.BlockSpec((None, 1, 6 * D), lambda i, d0, d1: (i // tiles_per_batch, 0, 0)),
        ],
        out_specs=pl.BlockSpec((tm, D), lambda i, d0, d1: (i, 0)),
        scratch_shapes=[pltpu.VMEM((2, TOP_K, tm * NSUB, LANES), U32),
                        pltpu.SemaphoreType.DMA((2,))],
    )
    return pl.pallas_call(
        _combine_kernel,
        grid_spec=grid_spec,
        out_shape=jax.ShapeDtypeStruct((T, D), F32),
        compiler_params=pltpu.CompilerParams(
            dimension_semantics=("arbitrary",), vmem_limit_bytes=VMEM_LIMIT),
        name="moe_combine",
    )(dest0, dest1, y, x1, info, mod_l)


def _pad_w_in(w):
    L, D, _ = w.shape
    c0 = A_COLS + R_COLS
    c_ad = c0 + 2 * G_KW + G_VW
    c_gg = c_ad + G_LORA
    pad = jnp.zeros((L, D, LANES - G_LORA), BF16)
    return jnp.concatenate(
        [w[:, :, :c_gg].astype(BF16), pad, w[:, :, c_gg:].astype(BF16)], axis=2)


def kernel(x, c, ada_w, ada_b, norm1_g, norm2_g, w_in, attn_qn_g, attn_kn_g, attn_lambda,
           attn_subln_g, rwkv_mu, rwkv_w_up, rwkv_w0, rwkv_a_up, rwkv_a0, rwkv_g_up, rwkv_k_k,
           rwkv_k_a, rwkv_r_k, rwkv_lnx_g, rwkv_lnx_b, gla_alpha_up, gla_alpha_b, gla_norm_g,
           proj_attn, proj_rwkv, proj_gla, w_out, router_grp_w, router_grp_b, router_exp_w,
           router_exp_b, exp_w_gate, exp_w_up, exp_w_down):
    B, S, D = x.shape
    T = B * S
    L = ada_w.shape[0]
    tm = 256
    tm_mm = 512
    mod = _adaln(c, ada_w, ada_b).reshape(L, B, 1, 6 * D)
    x2 = x.reshape(T, D)
    xin = None
    w_in_pad = _pad_w_in(w_in)
    for l in range(L):
        lambda_init = 0.8 - 0.6 * math.exp(-0.3 * l)
        p_attn, p_rwkv, p_gla, p_gate = _inproj(x2, mod[l], norm1_g[l], w_in_pad, l, S, tm_mm)
        o_a = _attention(p_attn, attn_qn_g[l], attn_kn_g[l], attn_lambda[l], attn_subln_g[l],
                         lambda_init, B, S)
        o_r = _rwkv(p_rwkv, rwkv_mu[l], rwkv_w_up[l], rwkv_w0[l], rwkv_a_up[l], rwkv_a0[l],
                    rwkv_g_up[l], rwkv_k_k[l], rwkv_k_a[l], rwkv_r_k[l], rwkv_lnx_g[l],
                    rwkv_lnx_b[l], B, S)
        o_g = _gla(p_gla, gla_alpha_up[l], gla_alpha_b[l], gla_norm_g[l], B, S)
        n_r = N_GROUPS + N_EXPERTS
        router_w = jnp.concatenate(
            [router_grp_w[l], router_exp_w[l], jnp.zeros((D, LANES - n_r), F32)], axis=1)
        rw_hi = router_w.astype(BF16)
        router_w = jnp.stack([rw_hi, (router_w - rw_hi.astype(F32)).astype(BF16)])
        router_b = jnp.concatenate(
            [router_grp_b[l], router_exp_b[l], jnp.zeros((LANES - n_r,), F32)]).reshape(1, LANES)
        x1, h2, info, cnt = _merge(
            x2, o_a, o_r, o_g, p_gate, proj_attn[l].astype(BF16), proj_rwkv[l].astype(BF16),
            proj_gla[l].astype(BF16), w_out[l].astype(BF16), mod[l], norm2_g[l],
            router_w, router_b, S, tm_mm)
        dest0, dest1, blk_exp, n_used, blk_grp, blk_nxt, n_rows = _route_tables(info, cnt)
        if xin is None:
            xin = jnp.zeros((n_rows * NSUB, LANES), U32)
        xin = _dispatch(h2, dest0, dest1, xin, tm)
        y = _moe(xin, blk_exp, n_used, blk_grp, blk_nxt, exp_w_gate, exp_w_up, exp_w_down, l)
        x2 = _combine(x1, y, info, dest0, dest1, mod[l], S, tm)
    return x2.reshape(B, S, D)
```

```python
import functools
import math

import jax
import jax.numpy as jnp
from jax import lax
from jax.experimental import pallas as pl
from jax.experimental.pallas import tpu as pltpu

F32 = jnp.float32
BF16 = jnp.bfloat16
U32 = jnp.uint32
HIGHEST = lax.Precision.HIGHEST

D_MODEL = 1024
A_HEADS, A_DH, A_DV = 4, 64, 128
A_QW, A_VW = 512, 512
A_COLS = 1536
R_HEADS, R_N, R_W = 8, 64, 512
R_COLS = 1792
RWKV_GN_EPS = 64e-5
G_HEADS, G_DK, G_DV = 4, 64, 128
G_KW, G_VW, G_LORA = 256, 512, 16
G_TAU = 16.0
G_COLS = 1552
G_COLS_PAD = 1664
GATE_COLS = 3072
N_GROUPS, EXP_PER_GROUP, N_EXPERTS, TOP_K = 4, 8, 32, 2
D_EXPERT = 512
EPS = 1e-6

LANES = 128
SUBLANES = 8
NSUB = D_MODEL // (2 * LANES)
CHUNK = 64
VMEM_LIMIT = 56 * 1024 * 1024


def _dot(a, b):
    return jnp.dot(a.astype(BF16), b.astype(BF16), preferred_element_type=F32)


def _dot_hi(a, b):
    return jnp.dot(a, b, precision=HIGHEST, preferred_element_type=F32)


def _dot_nt(a, b, precision=None):
    return lax.dot_general(a, b, (((1,), (1,)), ((), ())), precision=precision,
                           preferred_element_type=F32)


def _dot_tn(a, b, precision=None):
    return lax.dot_general(a, b, (((0,), (0,)), ((), ())), precision=precision,
                           preferred_element_type=F32)


def _cumsum3(tri, x):
    x1 = x.astype(BF16)
    res = x - x1.astype(F32)
    x2 = res.astype(BF16)
    x3 = (res - x2.astype(F32)).astype(BF16)
    return (jnp.dot(tri, x1, preferred_element_type=F32)
            + jnp.dot(tri, x2, preferred_element_type=F32)
            + jnp.dot(tri, x3, preferred_element_type=F32))


def _pack_pair(a, b):
    ua = pltpu.bitcast(a.astype(BF16).astype(F32), U32)
    ub = pltpu.bitcast(b.astype(BF16).astype(F32), U32)
    return jnp.bitwise_or(ub, jnp.right_shift(ua, jnp.uint32(16)))


def _unpack_pair(w):
    a = pltpu.bitcast(jnp.left_shift(w, jnp.uint32(16)), F32)
    b = pltpu.bitcast(jnp.bitwise_and(w, jnp.uint32(0xFFFF0000)), F32)
    return a, b


def _sigmoid(x):
    return 1.0 / (1.0 + jnp.exp(-x))


def _softplus(x):
    return jnp.maximum(x, 0.0) + jnp.log(1.0 + jnp.exp(-jnp.abs(x)))


def _seg_ones(n, seg):
    r = lax.broadcasted_iota(jnp.int32, (n, n), 0) // seg
    c = lax.broadcasted_iota(jnp.int32, (n, n), 1) // seg
    return (r == c).astype(F32)


def _tri(n, strict):
    r = lax.broadcasted_iota(jnp.int32, (n, n), 0)
    c = lax.broadcasted_iota(jnp.int32, (n, n), 1)
    return (c < r) if strict else (c <= r)


def _adaln_kernel(c_ref, w_ref, b_ref, o_ref):
    c = c_ref[...]
    c_act = c * _sigmoid(c)
    o_ref[...] = _dot_hi(c_act, w_ref[...]) + b_ref[...]


def _adaln(c, ada_w, ada_b):
    L, D, N = ada_w.shape
    B = c.shape[0]
    tn = D
    return pl.pallas_call(
        _adaln_kernel,
        grid=(L, N // tn),
        in_specs=[
            pl.BlockSpec((B, D), lambda l, j: (0, 0)),
            pl.BlockSpec((None, D, tn), lambda l, j: (l, 0, j)),
            pl.BlockSpec((None, 1, tn), lambda l, j: (l, 0, j)),
        ],
        out_specs=pl.BlockSpec((None, B, tn), lambda l, j: (l, 0, j)),
        out_shape=jax.ShapeDtypeStruct((L, B, N), F32),
        name="adaln",
    )(c, ada_w, ada_b.reshape(L, 1, N))


_IN_SEGS = (A_COLS, R_COLS, G_COLS_PAD, GATE_COLS)
_IN_DTYPES = (BF16, F32, F32, BF16)
_IN_CHUNK = 512


_C_GLA = A_COLS + R_COLS
_C_GAD = _C_GLA + 2 * G_KW + G_VW
_C_SPLIT = _C_GAD + G_LORA
_IN_PLAN = (
    (0, 0, 0, 0, A_COLS),
    (1, 0, 0, A_COLS, R_COLS),
    (2, 0, 0, _C_GLA, 2 * G_KW + G_VW),
    (2, 2 * G_KW + G_VW + LANES, 1, 0, G_VW),
    (3, 0, 1, G_VW, GATE_COLS),
)


def _inproj_kernel(x_ref, mod_ref, g_ref, wa_ref, wb_ref, *o_refs):
    x = x_ref[...]
    D = x.shape[-1]
    ms = jnp.mean(x * x, axis=-1, keepdims=True)
    y = x * lax.rsqrt(ms + EPS) * g_ref[...]
    sh = mod_ref[:, 0:D]
    sc = mod_ref[:, D:2 * D]
    h = (y * (1.0 + sc) + sh).astype(BF16)
    w_refs = (wa_ref, wb_ref)
    for out, oc, part, wc, width in _IN_PLAN:
        o_ref, w_ref = o_refs[out], w_refs[part]
        for c0 in range(0, width, _IN_CHUNK):
            n = min(_IN_CHUNK, width - c0)
            o_ref[:, oc + c0:oc + c0 + n] = jnp.dot(
                h, w_ref[:, wc + c0:wc + c0 + n], preferred_element_type=F32).astype(o_ref.dtype)
    lora = jnp.dot(h, wa_ref[:, _C_GAD:_C_SPLIT], preferred_element_type=F32)
    pad = jnp.zeros((lora.shape[0], LANES - G_LORA), F32)
    c_lora = 2 * G_KW + G_VW
    o_refs[2][:, c_lora:c_lora + LANES] = jnp.concatenate([lora, pad], axis=-1)


def _inproj(x2, mod_l, norm_g, w_parts, layer, S, tm):
    T, D = x2.shape
    wspec = lambda w: pl.BlockSpec((None, D, w.shape[2]), lambda i: (layer, 0, 0),
                                   pipeline_mode=pl.Buffered(1))
    tiles_per_batch = S // tm
    return pl.pallas_call(
        _inproj_kernel,
        grid=(T // tm,),
        in_specs=[
            pl.BlockSpec((tm, D), lambda i: (i, 0)),
            pl.BlockSpec((None, 1, 2 * D), lambda i: (i // tiles_per_batch, 0, 0)),
            pl.BlockSpec((1, D), lambda i: (0, 0)),
            wspec(w_parts[0]), wspec(w_parts[1]),
        ],
        out_specs=[pl.BlockSpec((tm, w), lambda i: (i, 0)) for w in _IN_SEGS],
        out_shape=[jax.ShapeDtypeStruct((T, w), dt) for w, dt in zip(_IN_SEGS, _IN_DTYPES)],
        compiler_params=pltpu.CompilerParams(
            dimension_semantics=("arbitrary",), vmem_limit_bytes=VMEM_LIMIT),
        name="inproj",
    )(x2, mod_l, norm_g.reshape(1, D), *w_parts)


A_TILE = 256
A_POS_SPLIT = 64
LOG2E = math.log2(math.e)


def _split_dot(x, ones):
    hi = x.astype(BF16)
    lo = (x - hi.astype(F32)).astype(BF16)
    return (jnp.dot(hi, ones, preferred_element_type=F32)
            + jnp.dot(lo, ones, preferred_element_type=F32))


def _eye(n):
    r = lax.broadcasted_iota(jnp.int32, (n, n), 0)
    c = lax.broadcasted_iota(jnp.int32, (n, n), 1)
    return (r == c).astype(BF16)


def _attn_kernel(q_ref, k_ref, v_ref, qg_ref, kg_ref, lam_ref, sg_ref, qaug_ref, kaug_ref, o_ref,
                 qt_s, ka_s, vt_s, s_s, *, S, lambda_init):
    t = A_TILE
    seg = _seg_ones(LANES, A_DH).astype(BF16)
    eye_t = _eye(t)
    eye_l = _eye(LANES)
    is_qk = lax.broadcasted_iota(jnp.int32, (t, LANES), 1) < A_DH

    blocks = [slice(b * t, (b + 1) * t) for b in range(S // t)]
    sides = ((q_ref, qg_ref, A_DH ** -0.5 * LOG2E), (k_ref, kg_ref, 1.0))
    xf = [[ref[rows, :].astype(F32) for rows in blocks] for ref, _, _ in sides]
    ms = [[jnp.dot((x * x).astype(BF16), seg, preferred_element_type=F32) for x in xs]
          for xs in xf]
    qn, kn = [[x * lax.rsqrt(m * (1.0 / A_DH) + EPS) * g_ref[...] * scale
               for x, m in zip(xs, mss)] for xs, mss, (_, g_ref, scale) in zip(xf, ms, sides)]
    for b, rows in enumerate(blocks):
        vt_s[:, rows] = _dot_nt(eye_l, v_ref[rows, :]).astype(BF16)
        for c in range(2):
            kc = kn[b] if c == 0 else pltpu.roll(kn[b], A_DH, 1)
            ka_s[c, rows, :] = jnp.where(is_qk, kc.astype(BF16), kaug_ref[rows, :])
    qas = [[jnp.where(is_qk, (qn[b] if c == 0 else pltpu.roll(qn[b], A_DH, 1)).astype(BF16),
                      qaug_ref[rows, :]) for c in range(2)] for b, rows in enumerate(blocks)]
    qts = [[_dot_nt(eye_l, qa) for qa in pair] for pair in qas]
    for b, rows in enumerate(blocks):
        for c in range(2):
            qt_s[c, :, rows] = qts[b][c].astype(BF16)

    lv = lam_ref[...]
    lam = (jnp.exp(jnp.sum(lv[0:1] * lv[1:2], axis=-1, keepdims=True))
           - jnp.exp(jnp.sum(lv[2:3] * lv[3:4], axis=-1, keepdims=True)) + lambda_init)
    causal = (lax.broadcasted_iota(jnp.int32, (t, t), 0)
              <= lax.broadcasted_iota(jnp.int32, (t, t), 1))

    nt = S // t

    def score_block(i, c, j, st):
        s = jnp.dot(ka_s[c, j * t:(j + 1) * t, :], qt_s[c, :, i * t:(i + 1) * t],
                    preferred_element_type=F32)
        if j == i:
            s = jnp.where(causal, s, -jnp.inf)
        s_s[i % 2, c, j * t:(j + 1) * t, :] = s
        mj = jnp.max(s, axis=0, keepdims=True)
        st["m"][c] = mj if st["m"][c] is None else jnp.maximum(st["m"][c], mj)

    def value_block(i, c, j, st):
        p = jnp.exp2(s_s[i % 2, c, j * t:(j + 1) * t, :] - st["m"][c])
        st["l"][c] = st["l"][c] + jnp.sum(p, axis=0, keepdims=True)
        st["acc"][c] = st["acc"][c] + jnp.dot(vt_s[:, j * t:(j + 1) * t], p.astype(BF16),
                                              preferred_element_type=F32)

    def finish(i, st):
        o = st["acc"][0] / st["l"][0] - lam * (st["acc"][1] / st["l"][1])
        ms = jnp.mean(o * o, axis=0, keepdims=True)
        o = o * lax.rsqrt(ms + EPS) * sg_ref[...] * (1.0 - lambda_init)
        o_ref[i * t:(i + 1) * t, :] = _dot_nt(eye_t, o.astype(BF16)).astype(o_ref.dtype)

    prev = None
    for i in range(nt + 1):
        cur = None
        first = []
        if i < nt:
            cur = dict(m=[None, None], l=[jnp.zeros((1, t), F32)] * 2,
                       acc=[jnp.zeros((A_DV, t), F32)] * 2)
            first = [(c, j) for j in range(i + 1) for c in range(2)]
        second = [(c, j) for j in range(i) for c in range(2)] if prev is not None else []
        for n in range(max(len(first), len(second))):
            if n < len(first):
                score_block(i, first[n][0], first[n][1], cur)
            if n < len(second):
                value_block(i - 1, second[n][0], second[n][1], prev)
        if prev is not None:
            finish(i - 1, prev)
        prev = cur


def _alibi_columns(S):
    pos = jnp.arange(S, dtype=jnp.int32)[:, None]
    hi = (pos // A_POS_SPLIT).astype(F32)
    lo = (pos % A_POS_SPLIT).astype(F32)
    kl = jnp.arange(LANES, dtype=jnp.int32)[None, :] - A_DH
    term = kl // 2
    used = jnp.logical_and(kl >= 0, kl < 8)
    k_aug = jnp.where(used, jnp.where(term == 0, hi, jnp.where(term == 1, lo, 1.0)), 0.0)
    q_augs = []
    for h in range(A_HEADS):
        s2 = 2.0 ** (-8.0 * (h + 1) / A_HEADS) * LOG2E
        qv = jnp.where(term == 0, A_POS_SPLIT * s2,
             jnp.where(term == 1, s2,
             jnp.where(term == 2, -A_POS_SPLIT * s2 * hi, -s2 * lo)))
        qv_hi = qv.astype(BF16).astype(F32)
        q_augs.append(jnp.where(used, jnp.where(kl % 2 == 0, qv_hi, qv - qv_hi), 0.0))
    return jnp.stack(q_augs).astype(BF16), k_aug.astype(BF16)


def _attention(p_attn, qn_g, kn_g, lam_vecs, subln_g, lambda_init, B, S):
    pa = p_attn.reshape(B, S, A_COLS)
    dup = lambda g: jnp.concatenate([g, g]).reshape(1, LANES)
    q_aug, k_aug = _alibi_columns(S)
    nqb = A_QW // LANES
    kern = functools.partial(_attn_kernel, S=S, lambda_init=lambda_init)
    out = pl.pallas_call(
        kern,
        grid=(B, A_HEADS),
        in_specs=[
            pl.BlockSpec((None, S, LANES), lambda b, h: (b, 0, h)),
            pl.BlockSpec((None, S, LANES), lambda b, h: (b, 0, nqb + h)),
            pl.BlockSpec((None, S, LANES), lambda b, h: (b, 0, 2 * nqb + h)),
            pl.BlockSpec((1, LANES), lambda b, h: (0, 0)),
            pl.BlockSpec((1, LANES), lambda b, h: (0, 0)),
            pl.BlockSpec((4, A_DH), lambda b, h: (0, 0)),
            pl.BlockSpec((A_DV, 1), lambda b, h: (0, 0)),
            pl.BlockSpec((None, S, LANES), lambda b, h: (h, 0, 0)),
            pl.BlockSpec((S, LANES), lambda b, h: (0, 0)),
        ],
        out_specs=pl.BlockSpec((None, S, A_DV), lambda b, h: (b, 0, h)),
        out_shape=jax.ShapeDtypeStruct((B, S, A_VW), BF16),
        scratch_shapes=[pltpu.VMEM((2, LANES, S), BF16), pltpu.VMEM((2, S, LANES), BF16),
                        pltpu.VMEM((A_DV, S), BF16), pltpu.VMEM((2, 2, S, A_TILE), F32)],
        compiler_params=pltpu.CompilerParams(
            dimension_semantics=("arbitrary", "arbitrary"), vmem_limit_bytes=VMEM_LIMIT),
        name="diff_attn",
    )(pa, pa, pa, dup(qn_g), dup(kn_g), lam_vecs, subln_g.reshape(A_DV, 1), q_aug, k_aug)
    return out.reshape(B * S, A_VW)


R_GROUP = 4
R_GW = R_GROUP * R_N


def _rwkv_kernel(p_ref, mu_ref, wup_ref, w0_ref, aup_ref, a0_ref, gup_ref, kk_ref, ka_ref,
                 rk_ref, lg_ref, lb_ref, o_ref,
                 carry_ref, st_ref, al_s, be_s, ka_s, rh_s, bt_s, kt_s, v_s, gc_s, y_s, *, TB):
    @pl.when(pl.program_id(1) == 0)
    def _():
        carry_ref[...] = jnp.zeros_like(carry_ref)
        st_ref[...] = jnp.zeros_like(st_ref)

    xs = p_ref[...]
    prev = pltpu.roll(xs, 1, 0)
    row = lax.broadcasted_iota(jnp.int32, (TB, 1), 0)
    prev = jnp.where(row == 0, carry_ref[...], prev)
    carry_ref[...] = xs[TB - 1:TB, :]
    xm = xs + (prev - xs) * mu_ref[...]
    r = xm[:, 0:R_W]
    k = xm[:, R_W:2 * R_W]
    v = xm[:, 2 * R_W:3 * R_W]
    wa = xm[:, 3 * R_W:3 * R_W + LANES]
    gd = xm[:, 3 * R_W + LANES:3 * R_W + 2 * LANES]
    wz = w0_ref[...] + _dot(jnp.tanh(wa), wup_ref[...])
    lw = -math.exp(-0.5) * _sigmoid(wz)
    a = _sigmoid(a0_ref[...] + _dot(wa, aup_ref[...]))
    g = _dot(_sigmoid(gd), gup_ref[...])
    seg = _seg_ones(R_GW, R_N).astype(BF16)

    def head_sum(x, passes):
        f = _split_dot if passes == 2 else (
            lambda t, o: jnp.dot(t.astype(BF16), o, preferred_element_type=F32))
        return jnp.concatenate(
            [f(x[:, j * R_GW:(j + 1) * R_GW], seg) for j in range(R_W // R_GW)], axis=-1)

    kk = k * kk_ref[...]
    kk = kk * jnp.minimum(lax.rsqrt(head_sum(kk * kk, 1)), 1e12)
    k2 = k * (1.0 + (a - 1.0) * ka_ref[...])
    bonus = head_sum(r * k2 * rk_ref[...], 2) * v
    bv = kk * a

    rr = lax.broadcasted_iota(jnp.int32, (TB, TB), 0)
    cc = lax.broadcasted_iota(jnp.int32, (TB, TB), 1)
    tril_blk = jnp.logical_and(rr // CHUNK == cc // CHUNK, cc <= rr).astype(BF16)
    Lg = _cumsum3(tril_blk, lw)
    inv = jnp.exp(-Lg)
    al_s[...] = (jnp.exp(Lg - lw) * kk).astype(BF16)
    be_s[...] = (bv * inv).astype(BF16)
    ka_s[...] = (k2 * inv).astype(BF16)
    rh_s[...] = (jnp.exp(Lg) * r).astype(BF16)
    v_s[...] = v.astype(BF16)
    for c in range(TB // CHUNK):
        rows = slice(c * CHUNK, (c + 1) * CHUNK)
        gC = Lg[(c + 1) * CHUNK - 1:(c + 1) * CHUNK, :]
        tail = jnp.exp(gC - Lg[rows, :])
        bt_s[rows, :] = (bv[rows, :] * tail).astype(BF16)
        kt_s[rows, :] = (k2[rows, :] * tail).astype(BF16)
        gc_s[c * SUBLANES:(c + 1) * SUBLANES, :] = jnp.broadcast_to(jnp.exp(gC), (SUBLANES, R_W))

    ri = lax.broadcasted_iota(jnp.int32, (R_GW, R_GW), 0)
    ci = lax.broadcasted_iota(jnp.int32, (R_GW, R_GW), 1)
    blk = ri // R_N == ci // R_N
    strict = ci % R_N < ri % R_N
    strict_t = ri % R_N < ci % R_N
    strict_st = (lax.broadcasted_iota(jnp.int32, (R_N, R_GW), 1) % R_N
                 < lax.broadcasted_iota(jnp.int32, (R_N, R_GW), 0))
    incl = ci % R_N <= ri % R_N
    zero = jnp.zeros((), BF16)

    def expand(x):
        return jnp.where(blk, jnp.concatenate([x] * R_GROUP, axis=0), zero)

    n_groups = R_HEADS // R_GROUP
    pair = 4

    def chunk_pair(ip, _):
        chains = []
        for dc in range(pair):
            c = ip * pair + dc
            rows = pl.ds(pl.multiple_of(c * CHUNK, CHUNK), CHUNK)
            for gi in range(n_groups):
                chains.append(dict(c=c, rows=rows, gi=gi, cols=slice(gi * R_GW, (gi + 1) * R_GW)))
        for ch in chains:
            rows, cols = ch["rows"], ch["cols"]
            ch["al"] = al_s[rows, cols]
            ch["A"] = expand(ch["al"])
            ch["R"] = expand(rh_s[rows, cols])
            ch["B"] = expand(be_s[rows, cols])
            ch["K"] = expand(ka_s[rows, cols])
            vc = v_s[rows, cols]
            ch["V"] = jnp.concatenate(
                [vc[:, h * R_N:(h + 1) * R_N] for h in range(R_GROUP)], axis=0)
        for ch in chains:
            ch["P"] = -jnp.where(strict_st, _dot_nt(ch["al"], ch["B"]), 0.0)
            ch["m_ak_t"] = jnp.where(strict_t, _dot_nt(ch["K"], ch["A"]), 0.0).astype(BF16)
            ch["m_rb"] = jnp.where(incl, _dot_nt(ch["R"], ch["B"]), 0.0).astype(BF16)
            ch["m_rk"] = jnp.where(incl, _dot_nt(ch["R"], ch["K"]), 0.0).astype(BF16)
        for ch in chains:
            ch["Z"] = ch["A"].astype(F32)
            ch["Wt"] = _dot_tn(ch["V"], ch["m_ak_t"])
        n = 1
        while True:
            last = 2 * n >= CHUNK
            for ch in chains:
                Pb = ch["P"].astype(BF16)
                Xb = expand(Pb)
                ch["Z"] = ch["Z"] + jnp.dot(Xb, ch["Z"].astype(BF16), preferred_element_type=F32)
                ch["Wt"] = ch["Wt"] + _dot_nt(ch["Wt"].astype(BF16), Xb)
                if not last:
                    ch["P"] = jnp.dot(Pb, Xb, preferred_element_type=F32)
            n *= 2
            if last:
                break
        for ch in chains:
            rows, cols = ch["rows"], ch["cols"]
            Bt = expand(bt_s[rows, cols])
            Kt = expand(kt_s[rows, cols])
            Zb = ch["Z"].astype(BF16)
            Wtb = ch["Wt"].astype(BF16)
            ch["y_a"] = (ch["R"].astype(F32)
                         - jnp.dot(ch["m_rb"], Zb, preferred_element_type=F32)).astype(BF16)
            ch["y_b"] = (jnp.dot(ch["m_rk"], ch["V"], preferred_element_type=F32)
                         - _dot_nt(ch["m_rb"], Wtb))
            ch["p_neg"] = _dot_tn(Zb, Bt).astype(BF16)
            ch["q"] = _dot_tn(ch["V"], Kt) - jnp.dot(Wtb, Bt, preferred_element_type=F32)
        for ch in chains:
            rows, cols, gi = ch["rows"], ch["cols"], ch["gi"]
            g0 = pl.multiple_of(ch["c"] * SUBLANES, SUBLANES)
            S0 = st_ref[gi]
            S0b = S0.astype(BF16)
            y = _dot_nt(ch["y_a"], S0b) + ch["y_b"]
            st_ref[gi] = (S0 * gc_s[pl.ds(g0, SUBLANES), cols][0:1, :]
                          - jnp.dot(S0b, ch["p_neg"], preferred_element_type=F32) + ch["q"])
            for h in range(R_GROUP):
                hh = gi * R_GROUP + h
                y_s[rows, hh * R_N:(hh + 1) * R_N] = y[h * R_N:(h + 1) * R_N, :]
        return 0

    lax.fori_loop(0, TB // CHUNK // pair, chunk_pair, 0)

    y = y_s[...]
    mean = head_sum(y, 2) * (1.0 / R_N)
    yc = y - mean
    var = head_sum(yc * yc, 1) * (1.0 / R_N)
    yn = yc * lax.rsqrt(var + RWKV_GN_EPS) * lg_ref[...] + lb_ref[...]
    o_ref[...] = ((yn + bonus) * g).astype(o_ref.dtype)


def _rwkv(p_rwkv, mu, w_up, w0, a_up, a0, g_up, k_k, k_a, r_k, lnx_g, lnx_b, B, S):
    TB = 256
    T = B * S
    nt = S // TB
    row = lambda t: t.reshape(1, -1)
    zeros = jnp.zeros((R_N, R_W), F32)
    wup_pad = jnp.concatenate([w_up, zeros], axis=0)
    aup_pad = jnp.concatenate([zeros, a_up], axis=0)
    vec = lambda n: pl.BlockSpec((1, n), lambda b, i: (0, 0))
    mat = lambda m, n: pl.BlockSpec((m, n), lambda b, i: (0, 0))
    kern = functools.partial(_rwkv_kernel, TB=TB)
    return pl.pallas_call(
        kern,
        grid=(B, nt),
        in_specs=[
            pl.BlockSpec((TB, R_COLS), lambda b, i: (b * nt + i, 0)),
            vec(R_COLS), mat(LANES, R_W), vec(R_W), mat(LANES, R_W), vec(R_W), mat(LANES, R_W),
            vec(R_W), vec(R_W), vec(R_W), vec(R_W), vec(R_W),
        ],
        out_specs=pl.BlockSpec((TB, R_W), lambda b, i: (b * nt + i, 0)),
        out_shape=jax.ShapeDtypeStruct((T, R_W), BF16),
        scratch_shapes=[
            pltpu.VMEM((1, R_COLS), F32),
            pltpu.VMEM((R_HEADS // R_GROUP, R_N, R_GW), F32),
        ] + [pltpu.VMEM((TB, R_W), BF16)] * 7 + [
            pltpu.VMEM((TB // CHUNK * SUBLANES, R_W), F32),
            pltpu.VMEM((TB, R_W), F32),
        ],
        compiler_params=pltpu.CompilerParams(
            dimension_semantics=("arbitrary", "arbitrary"), vmem_limit_bytes=VMEM_LIMIT),
        name="rwkv7",
    )(p_rwkv, row(mu), wup_pad, row(w0), aup_pad, row(a0), g_up, row(k_k), row(k_a),
      row(r_k), row(lnx_g), row(lnx_b))


G_SUBTILE = 256


def _gla_kernel(p_ref, aup_ref, ab_ref, ng_ref, o_ref, st_ref, *, TB):
    @pl.when(pl.program_id(1) == 0)
    def _():
        st_ref[...] = jnp.zeros_like(st_ref)

    c_gv = 2 * G_KW
    c_ad = c_gv + G_VW
    c_gate = c_ad + LANES
    sub = G_SUBTILE
    subs = range(TB // sub)
    nchunk = sub // CHUNK
    rr = lax.broadcasted_iota(jnp.int32, (sub, sub), 0)
    cc = lax.broadcasted_iota(jnp.int32, (sub, sub), 1)
    causal = jnp.logical_and(rr // CHUNK == cc // CHUNK, cc <= rr)
    causal_b = causal.astype(BF16)
    heads = range(G_HEADS)
    sls = [slice(h * G_DK, (h + 1) * G_DK) for h in heads]
    vss = [slice(h * G_DV, (h + 1) * G_DV) for h in heads]

    rws = [slice(u * sub, (u + 1) * sub) for u in subs]
    zs = [_dot(p_ref[r, c_ad:c_ad + LANES], aup_ref[...]) + ab_ref[...] for r in rws]
    las = [-_softplus(-z) * (1.0 / G_TAU) for z in zs]
    bs = [_cumsum3(causal_b, la) for la in las]
    ks = [p_ref[r, G_KW:2 * G_KW] for r in rws]
    vbs = [p_ref[r, c_gv:c_gv + G_VW].astype(BF16) for r in rws]
    qes = [(p_ref[r, 0:G_KW] * (G_DK ** -0.5) * jnp.exp(b)).astype(BF16) for r, b in zip(rws, bs)]
    kes = [(k * jnp.exp(-b)).astype(BF16) for k, b in zip(ks, bs)]
    scs = [[jnp.where(causal, _dot_nt(qes[u][:, sls[h]], kes[u][:, sls[h]]), 0.0) for h in heads]
           for u in subs]
    o_intra = [[_dot(scs[u][h], vbs[u][:, vss[h]]) for h in heads] for u in subs]
    kts, e_lasts, kvs = {}, {}, {}
    for u in subs:
        for c in range(nchunk):
            rows = slice(c * CHUNK, (c + 1) * CHUNK)
            b_last = bs[u][(c + 1) * CHUNK - 1:(c + 1) * CHUNK, :]
            kt = (ks[u][rows, :] * jnp.exp(b_last - bs[u][rows, :])).astype(BF16)
            e_lasts[u, c] = jnp.exp(b_last)
            for h in heads:
                kvs[u, c, h] = _dot_tn(vbs[u][rows, vss[h]], kt[:, sls[h]])
    states = [st_ref[h] for h in heads]
    parts = [[] for _ in heads]
    for u in subs:
        for c in range(nchunk):
            rows = slice(c * CHUNK, (c + 1) * CHUNK)
            for h in heads:
                parts[h].append(o_intra[u][h][rows, :]
                                + _dot_nt(qes[u][rows, sls[h]], states[h].astype(BF16)))
                states[h] = states[h] * e_lasts[u, c][:, sls[h]] + kvs[u, c, h]
    for h in heads:
        vs = vss[h]
        st_ref[h] = states[h]
        o = jnp.concatenate(parts[h], axis=0)
        ms = jnp.mean(o * o, axis=-1, keepdims=True)
        gt = p_ref[:, c_gate + h * G_DV:c_gate + (h + 1) * G_DV]
        o = o * lax.rsqrt(ms + EPS) * ng_ref[...] * (gt * _sigmoid(gt))
        o_ref[:, vs] = o.astype(o_ref.dtype)


def _gla(p_gla, alpha_up, alpha_b, norm_g, B, S):
    TB = 512
    T = B * S
    nt = S // TB
    aup_pad = jnp.concatenate([alpha_up, jnp.zeros((LANES - G_LORA, G_KW), F32)], axis=0)
    kern = functools.partial(_gla_kernel, TB=TB)
    return pl.pallas_call(
        kern,
        grid=(B, nt),
        in_specs=[
            pl.BlockSpec((TB, G_COLS_PAD), lambda b, i: (b * nt + i, 0)),
            pl.BlockSpec((LANES, G_KW), lambda b, i: (0, 0)),
            pl.BlockSpec((1, G_KW), lambda b, i: (0, 0)),
            pl.BlockSpec((1, G_DV), lambda b, i: (0, 0)),
        ],
        out_specs=pl.BlockSpec((TB, G_VW), lambda b, i: (b * nt + i, 0)),
        out_shape=jax.ShapeDtypeStruct((T, G_VW), BF16),
        scratch_shapes=[pltpu.VMEM((G_HEADS, G_DV, G_DK), F32)],
        compiler_params=pltpu.CompilerParams(
            dimension_semantics=("arbitrary", "arbitrary"), vmem_limit_bytes=VMEM_LIMIT),
        name="gla",
    )(p_gla, aup_pad, alpha_b.reshape(1, G_KW), norm_g.reshape(1, G_DV))


def _merge_kernel(x_ref, oa_ref, or_ref, og_ref, gate_ref, pa_ref, pr_ref, pg_ref, wo_ref,
                  mod_ref, g2_ref, rw_ref, rb_ref, x1_ref, h2_ref, info_ref, cnt_ref, carry_ref):
    D = D_MODEL

    @pl.when(pl.program_id(0) == 0)
    def _():
        carry_ref[...] = jnp.zeros_like(carry_ref)

    merged = (_sigmoid(gate_ref[:, 0:D].astype(F32))
              * jnp.dot(oa_ref[...], pa_ref[...], preferred_element_type=F32)
              + _sigmoid(gate_ref[:, D:2 * D].astype(F32))
              * jnp.dot(or_ref[...], pr_ref[...], preferred_element_type=F32)
              + _sigmoid(gate_ref[:, 2 * D:3 * D].astype(F32))
              * jnp.dot(og_ref[...], pg_ref[...], preferred_element_type=F32))
    gt1 = mod_ref[:, 2 * D:3 * D]
    sh2 = mod_ref[:, 3 * D:4 * D]
    sc2 = mod_ref[:, 4 * D:5 * D]
    x1 = x_ref[...] + gt1 * jnp.dot(merged.astype(BF16), wo_ref[...], preferred_element_type=F32)
    x1_ref[...] = x1
    ms = jnp.mean(x1 * x1, axis=-1, keepdims=True)
    h2 = x1 * lax.rsqrt(ms + EPS) * g2_ref[...] * (1.0 + sc2) + sh2
    for s in range(NSUB):
        h2_ref[pl.ds(s, h2.shape[0], stride=NSUB), :] = _pack_pair(
            h2[:, 2 * s * LANES:(2 * s + 1) * LANES], h2[:, (2 * s + 1) * LANES:(2 * s + 2) * LANES])
    h_hi = h2.astype(BF16)
    h_lo = (h2 - h_hi.astype(F32)).astype(BF16)
    logits = (jnp.dot(h_hi, rw_ref[0], preferred_element_type=F32)
              + jnp.dot(h_lo, rw_ref[0], preferred_element_type=F32)
              + jnp.dot(h_hi, rw_ref[1], preferred_element_type=F32) + rb_ref[...])
    info_ref[...], cnt_ref[...] = _route_tile(logits, carry_ref)


def _merge(x2, o_a, o_r, o_g, p_gate, proj_a, proj_r, proj_g, w_out, mod_l, norm2_g,
           router_w, router_b, S, tm):
    T, D = x2.shape
    tiles_per_batch = S // tm
    tile = lambda w: pl.BlockSpec((tm, w), lambda i: (i, 0))
    const = lambda m, n: pl.BlockSpec((m, n), lambda i: (0, 0))
    return pl.pallas_call(
        _merge_kernel,
        grid=(T // tm,),
        in_specs=[
            tile(D), tile(A_VW), tile(R_W), tile(G_VW), tile(GATE_COLS),
            const(A_VW, D), const(R_W, D), const(G_VW, D), const(D, D),
            pl.BlockSpec((None, 1, 6 * D), lambda i: (i // tiles_per_batch, 0, 0)),
            const(1, D), pl.BlockSpec((2, D, LANES), lambda i: (0, 0, 0)), const(1, LANES),
        ],
        out_specs=[
            tile(D),
            pl.BlockSpec((tm * NSUB, LANES), lambda i: (i, 0)),
            tile(LANES),
            const(1, LANES),
        ],
        out_shape=[
            jax.ShapeDtypeStruct((T, D), F32),
            jax.ShapeDtypeStruct((T * NSUB, LANES), U32),
            jax.ShapeDtypeStruct((T, LANES), F32),
            jax.ShapeDtypeStruct((1, LANES), F32),
        ],
        scratch_shapes=[pltpu.VMEM((1, LANES), F32)],
        compiler_params=pltpu.CompilerParams(
            dimension_semantics=("arbitrary",), vmem_limit_bytes=VMEM_LIMIT),
        name="merge",
    )(x2, o_a, o_r, o_g, p_gate, proj_a, proj_r, proj_g, w_out, mod_l,
      norm2_g.reshape(1, D), router_w, router_b)


MOE_ROWS = 256
E_LANE0 = N_GROUPS


def _route_tile(lg, carry_ref):
    n = lg.shape[0]
    lane = lax.broadcasted_iota(jnp.int32, (n, LANES), 1).astype(F32)
    neg = -jnp.inf
    big = float(LANES)

    def first_max(vals):
        m = jnp.max(vals, axis=-1, keepdims=True)
        idx = jnp.min(jnp.where(vals == m, lane, big), axis=-1, keepdims=True)
        return m, idx

    in_grp = lane < N_GROUPS
    gm, grp = first_max(jnp.where(in_grp, lg, neg))
    g_prob = 1.0 / jnp.sum(jnp.where(in_grp, jnp.exp(lg - gm), 0.0), axis=-1, keepdims=True)
    lo = E_LANE0 + grp * EXP_PER_GROUP
    el = jnp.where(jnp.logical_and(lane >= lo, lane < lo + EXP_PER_GROUP), lg, neg)
    v1, i1 = first_max(el)
    v2, i2 = first_max(jnp.where(lane == i1, neg, el))
    e21 = jnp.exp(v2 - v1)
    w0 = g_prob / (1.0 + e21)
    w1 = g_prob * e21 / (1.0 + e21)
    oh0 = lane == i1
    oh1 = lane == i2
    oh = jnp.logical_or(oh0, oh1).astype(F32)
    before = _tri(n, True).astype(BF16)
    cnt = jnp.dot(before, oh.astype(BF16), preferred_element_type=F32) + carry_ref[...]
    rank0 = jnp.sum(jnp.where(oh0, cnt, 0.0), axis=-1, keepdims=True)
    rank1 = jnp.sum(jnp.where(oh1, cnt, 0.0), axis=-1, keepdims=True)
    carry = carry_ref[...] + jnp.sum(oh, axis=0, keepdims=True)
    carry_ref[...] = carry
    cols = (i1 - E_LANE0, i2 - E_LANE0, rank0, rank1, w0, w1)
    info = jnp.zeros((n, LANES), F32)
    for j, col in enumerate(cols):
        info = jnp.where(lane == j, col, info)
    return info, carry


def _route_tables(info, cnt):
    T = info.shape[0]
    A = T * TOP_K
    counts = cnt[0, E_LANE0:E_LANE0 + N_EXPERTS].astype(jnp.int32)
    padded = (counts + MOE_ROWS - 1) // MOE_ROWS * MOE_ROWS
    pad_end = jnp.cumsum(padded)
    pad_start = pad_end - padded
    n_blocks = -(-A // MOE_ROWS) + N_EXPERTS
    eid = info[:, 0:2].astype(jnp.int32)
    is_e = eid[:, :, None] == jnp.arange(N_EXPERTS, dtype=jnp.int32)
    dest = (jnp.sum(jnp.where(is_e, pad_start, 0), axis=-1)
            + info[:, 2:4].astype(jnp.int32))
    blk_start = jnp.arange(n_blocks, dtype=jnp.int32) * MOE_ROWS
    blk_exp = jnp.minimum(jnp.sum(pad_end[None, :] <= blk_start[:, None], axis=1),
                          N_EXPERTS - 1).astype(jnp.int32)
    n_used = (pad_end[-1:] // MOE_ROWS).astype(jnp.int32)
    nonempty = counts > 0
    ids = jnp.arange(N_EXPERTS, dtype=jnp.int32)
    grp_of = jnp.cumsum(nonempty.astype(jnp.int32)) - 1
    later = jnp.logical_and(ids[None, :] > ids[:, None], nonempty[None, :])
    nxt_of = jnp.min(jnp.where(later, ids[None, :], N_EXPERTS), axis=1)
    nxt_of = jnp.where(nxt_of == N_EXPERTS, -1, nxt_of)
    is_b = blk_exp[:, None] == ids[None, :]
    blk_grp = jnp.sum(jnp.where(is_b, grp_of[None, :], 0), axis=1).astype(jnp.int32)
    blk_nxt = jnp.sum(jnp.where(is_b, nxt_of[None, :], 0), axis=1).astype(jnp.int32)
    return dest[:, 0], dest[:, 1], blk_exp, n_used, blk_grp, blk_nxt, n_blocks * MOE_ROWS


def _dispatch_kernel(d0_ref, d1_ref, h2_ref, xin_in, xin_hbm, sem):
    del xin_in
    n = h2_ref.shape[0] // NSUB
    base = pl.program_id(0) * n

    def slab(ref, row):
        return ref.at[pl.ds(pl.multiple_of(row * NSUB, NSUB), NSUB)]

    def body(r, _):
        pltpu.make_async_copy(slab(h2_ref, r), slab(xin_hbm, d0_ref[base + r]), sem).start(0)
        pltpu.make_async_copy(slab(h2_ref, r), slab(xin_hbm, d1_ref[base + r]), sem).start(1)
        return 0

    lax.fori_loop(0, n, body, 0, unroll=8)
    for _ in range(TOP_K):
        pltpu.make_async_copy(h2_ref, xin_hbm.at[pl.ds(0, n * NSUB)], sem).wait()


def _dispatch(h2_slab, dest0, dest1, xin_init, td):
    T = h2_slab.shape[0] // NSUB
    grid_spec = pltpu.PrefetchScalarGridSpec(
        num_scalar_prefetch=2,
        grid=(T // td,),
        in_specs=[pl.BlockSpec((td * NSUB, LANES), lambda i, d0, d1: (i, 0)),
                  pl.BlockSpec(memory_space=pl.ANY)],
        out_specs=pl.BlockSpec(memory_space=pl.ANY),
        scratch_shapes=[pltpu.SemaphoreType.DMA(())],
    )
    return pl.pallas_call(
        _dispatch_kernel,
        grid_spec=grid_spec,
        out_shape=jax.ShapeDtypeStruct(xin_init.shape, xin_init.dtype),
        input_output_aliases={3: 0},
        compiler_params=pltpu.CompilerParams(dimension_semantics=("arbitrary",)),
        name="moe_dispatch",
    )(dest0, dest1, h2_slab, xin_init)


def _moe_kernel(be_ref, nu_ref, grp_ref, nxt_ref, x_ref, wg_hbm, wu_hbm, wd_hbm, y_ref,
                wgf, wuf, wdf, wgb, wub, wdb, sem, *, layer):
    i = pl.program_id(0)

    def weight_copies(expert, slot):
        return [pltpu.make_async_copy(src.at[layer, expert], dst.at[slot], sem.at[slot, k])
                for k, (src, dst) in enumerate(((wg_hbm, wgf), (wu_hbm, wuf), (wd_hbm, wdf)))]

    @pl.when(i < nu_ref[0])
    def _():
        changed = jnp.logical_or(i == 0, be_ref[i] != be_ref[jnp.maximum(i - 1, 0)])

        @pl.when(changed)
        def _():
            slot = grp_ref[i] % 2

            @pl.when(i == 0)
            def _():
                for cp in weight_copies(be_ref[0], 0):
                    cp.start()

            for cp in weight_copies(be_ref[i], slot):
                cp.wait()
            wgb[...] = wgf[slot].astype(BF16)
            wub[...] = wuf[slot].astype(BF16)
            wdb[...] = wdf[slot].astype(BF16)

            @pl.when(nxt_ref[i] >= 0)
            def _():
                for cp in weight_copies(nxt_ref[i], 1 - slot):
                    cp.start()

        kw = 2 * LANES
        hg = hu = None
        for j in range(NSUB):
            xa, xb = _unpack_pair(x_ref[pl.ds(j, MOE_ROWS, stride=NSUB), :])
            xj = jnp.concatenate([xa.astype(BF16), xb.astype(BF16)], axis=-1)
            dg = jnp.dot(xj, wgb[j * kw:(j + 1) * kw, :], preferred_element_type=F32)
            du = jnp.dot(xj, wub[j * kw:(j + 1) * kw, :], preferred_element_type=F32)
            hg = dg if hg is None else hg + dg
            hu = du if hu is None else hu + du
        hid = (hg * _sigmoid(hg) * hu).astype(BF16)
        for j in range(NSUB):
            yj = jnp.dot(hid, wdb[:, j * kw:(j + 1) * kw], preferred_element_type=F32)
            y_ref[pl.ds(j, MOE_ROWS, stride=NSUB), :] = _pack_pair(yj[:, :LANES], yj[:, LANES:])

    @pl.when(i >= nu_ref[0])
    def _():
        y_ref[...] = jnp.zeros_like(y_ref)


def _moe(xin, blk_exp, n_used, blk_grp, blk_nxt, w_gate, w_up, w_down, layer):
    blk_rows = MOE_ROWS * NSUB
    n_blocks = xin.shape[0] // blk_rows
    last = lambda i, nu: jnp.minimum(i, nu[0] - 1)
    grid_spec = pltpu.PrefetchScalarGridSpec(
        num_scalar_prefetch=4,
        grid=(n_blocks,),
        in_specs=[
            pl.BlockSpec((blk_rows, LANES), lambda i, be, nu, gr, nx: (last(i, nu), 0)),
            pl.BlockSpec(memory_space=pl.ANY), pl.BlockSpec(memory_space=pl.ANY),
            pl.BlockSpec(memory_space=pl.ANY),
        ],
        out_specs=pl.BlockSpec((blk_rows, LANES), lambda i, be, nu, gr, nx: (i, 0)),
        scratch_shapes=[
            pltpu.VMEM((2, D_MODEL, D_EXPERT), F32),
            pltpu.VMEM((2, D_MODEL, D_EXPERT), F32),
            pltpu.VMEM((2, D_EXPERT, D_MODEL), F32),
            pltpu.VMEM((D_MODEL, D_EXPERT), BF16),
            pltpu.VMEM((D_MODEL, D_EXPERT), BF16),
            pltpu.VMEM((D_EXPERT, D_MODEL), BF16),
            pltpu.SemaphoreType.DMA((2, 3)),
        ],
    )
    return pl.pallas_call(
        functools.partial(_moe_kernel, layer=layer),
        grid_spec=grid_spec,
        out_shape=jax.ShapeDtypeStruct(xin.shape, xin.dtype),
        compiler_params=pltpu.CompilerParams(
            dimension_semantics=("arbitrary",), vmem_limit_bytes=VMEM_LIMIT),
        name="moe_ffn",
    )(blk_exp, n_used, blk_grp, blk_nxt, xin, w_gate, w_up, w_down)


def _combine_kernel(d0_ref, d1_ref, y_hbm, x1_ref, info_ref, mod_ref, o_ref, ybuf, sem):
    i = pl.program_id(0)
    nsteps = pl.num_programs(0)
    n = x1_ref.shape[0]
    slot = i % 2
    D = D_MODEL

    def slab(ref, row):
        return ref.at[pl.ds(pl.multiple_of(row * NSUB, NSUB), NSUB)]

    def start_gather(step, sl):
        base = step * n

        def body(r, _):
            pltpu.make_async_copy(
                slab(y_hbm, d0_ref[base + r]), slab(ybuf.at[sl, 0], r), sem.at[sl]).start(0)
            pltpu.make_async_copy(
                slab(y_hbm, d1_ref[base + r]), slab(ybuf.at[sl, 1], r), sem.at[sl]).start(1)
            return 0
        lax.fori_loop(0, n, body, 0, unroll=8)

    @pl.when(i == 0)
    def _():
        start_gather(0, 0)

    @pl.when(i + 1 < nsteps)
    def _():
        start_gather(i + 1, 1 - slot)

    for k in range(TOP_K):
        pltpu.make_async_copy(
            y_hbm.at[pl.ds(0, n * NSUB)], ybuf.at[slot, k], sem.at[slot]).wait()

    w0 = info_ref[:, 4:5]
    w1 = info_ref[:, 5:6]
    for s in range(NSUB):
        piece = pl.ds(s, n, stride=NSUB)
        halves0 = _unpack_pair(ybuf[slot, 0, piece, :])
        halves1 = _unpack_pair(ybuf[slot, 1, piece, :])
        for half in range(2):
            c0 = (2 * s + half) * LANES
            cols = slice(c0, c0 + LANES)
            gt2 = mod_ref[:, 5 * D + c0:5 * D + c0 + LANES]
            moe = w0 * halves0[half] + w1 * halves1[half]
            o_ref[:, cols] = x1_ref[:, cols] + gt2 * moe


def _combine(x1, y, info, dest0, dest1, mod_l, S, tm):
    T, D = x1.shape
    tiles_per_batch = S // tm
    grid_spec = pltpu.PrefetchScalarGridSpec(
        num_scalar_prefetch=2,
        grid=(T // tm,),
        in_specs=[
            pl.BlockSpec(memory_space=pl.ANY),
            pl.BlockSpec((tm, D), lambda i, d0, d1: (i, 0)),
            pl.BlockSpec((tm, LANES), lambda i, d0, d1: (i, 0)),
            pl.BlockSpec((None, 1, 6 * D), lambda i, d0, d1: (i // tiles_per_batch, 0, 0)),
        ],
        out_specs=pl.BlockSpec((tm, D), lambda i, d0, d1: (i, 0)),
        scratch_shapes=[pltpu.VMEM((2, TOP_K, tm * NSUB, LANES), U32),
                        pltpu.SemaphoreType.DMA((2,))],
    )
    return pl.pallas_call(
        _combine_kernel,
        grid_spec=grid_spec,
        out_shape=jax.ShapeDtypeStruct((T, D), F32),
        compiler_params=pltpu.CompilerParams(
            dimension_semantics=("arbitrary",), vmem_limit_bytes=VMEM_LIMIT),
        name="moe_combine",
    )(dest0, dest1, y, x1, info, mod_l)


def _split_w_in(w):
    return w[:, :, :_C_SPLIT].astype(BF16), w[:, :, _C_SPLIT:].astype(BF16)


def kernel(x, c, ada_w, ada_b, norm1_g, norm2_g, w_in, attn_qn_g, attn_kn_g, attn_lambda,
           attn_subln_g, rwkv_mu, rwkv_w_up, rwkv_w0, rwkv_a_up, rwkv_a0, rwkv_g_up, rwkv_k_k,
           rwkv_k_a, rwkv_r_k, rwkv_lnx_g, rwkv_lnx_b, gla_alpha_up, gla_alpha_b, gla_norm_g,
           proj_attn, proj_rwkv, proj_gla, w_out, router_grp_w, router_grp_b, router_exp_w,
           router_exp_b, exp_w_gate, exp_w_up, exp_w_down):
    B, S, D = x.shape
    T = B * S
    L = ada_w.shape[0]
    tm = 256
    tm_mm = 512
    mod = _adaln(c, ada_w, ada_b).reshape(L, B, 1, 6 * D)
    x2 = x.reshape(T, D)
    xin = None
    w_in_parts = _split_w_in(w_in)
    for l in range(L):
        lambda_init = 0.8 - 0.6 * math.exp(-0.3 * l)
        p_attn, p_rwkv, p_gla, p_gate = _inproj(x2, mod[l], norm1_g[l], w_in_parts, l, S, tm_mm)
        o_a = _attention(p_attn, attn_qn_g[l], attn_kn_g[l], attn_lambda[l], attn_subln_g[l],
                         lambda_init, B, S)
        o_r = _rwkv(p_rwkv, rwkv_mu[l], rwkv_w_up[l], rwkv_w0[l], rwkv_a_up[l], rwkv_a0[l],
                    rwkv_g_up[l], rwkv_k_k[l], rwkv_k_a[l], rwkv_r_k[l], rwkv_lnx_g[l],
                    rwkv_lnx_b[l], B, S)
        o_g = _gla(p_gla, gla_alpha_up[l], gla_alpha_b[l], gla_norm_g[l], B, S)
        n_r = N_GROUPS + N_EXPERTS
        router_w = jnp.concatenate(
            [router_grp_w[l], router_exp_w[l], jnp.zeros((D, LANES - n_r), F32)], axis=1)
        rw_hi = router_w.astype(BF16)
        router_w = jnp.stack([rw_hi, (router_w - rw_hi.astype(F32)).astype(BF16)])
        router_b = jnp.concatenate(
            [router_grp_b[l], router_exp_b[l], jnp.zeros((LANES - n_r,), F32)]).reshape(1, LANES)
        x1, h2, info, cnt = _merge(
            x2, o_a, o_r, o_g, p_gate, proj_attn[l].astype(BF16), proj_rwkv[l].astype(BF16),
            proj_gla[l].astype(BF16), w_out[l].astype(BF16), mod[l], norm2_g[l],
            router_w, router_b, S, tm_mm)
        dest0, dest1, blk_exp, n_used, blk_grp, blk_nxt, n_rows = _route_tables(info, cnt)
        if xin is None:
            xin = jnp.zeros((n_rows * NSUB, LANES), U32)
        xin = _dispatch(h2, dest0, dest1, xin, tm)
        y = _moe(xin, blk_exp, n_used, blk_grp, blk_nxt, exp_w_gate, exp_w_up, exp_w_down, l)
        x2 = _combine(x1, y, info, dest0, dest1, mod[l], S, tm)
    return x2.reshape(B, S, D)
```

```python
import functools
import math

import jax
import jax.numpy as jnp
from jax import lax
from jax.experimental import pallas as pl
from jax.experimental.pallas import tpu as pltpu

F32 = jnp.float32
BF16 = jnp.bfloat16
U32 = jnp.uint32
HIGHEST = lax.Precision.HIGHEST

D_MODEL = 1024
A_HEADS, A_DH, A_DV = 4, 64, 128
A_QW, A_VW = 512, 512
A_COLS = 1536
R_HEADS, R_N, R_W = 8, 64, 512
R_COLS = 1792
RWKV_GN_EPS = 64e-5
G_HEADS, G_DK, G_DV = 4, 64, 128
G_KW, G_VW, G_LORA = 256, 512, 16
G_TAU = 16.0
G_COLS = 1552
G_COLS_PAD = 1664
GATE_COLS = 3072
N_GROUPS, EXP_PER_GROUP, N_EXPERTS, TOP_K = 4, 8, 32, 2
D_EXPERT = 512
EPS = 1e-6

LANES = 128
SUBLANES = 8
NSUB = D_MODEL // (2 * LANES)
CHUNK = 64
VMEM_LIMIT = 56 * 1024 * 1024


def _dot(a, b):
    return jnp.dot(a.astype(BF16), b.astype(BF16), preferred_element_type=F32)


def _dot_hi(a, b):
    return jnp.dot(a, b, precision=HIGHEST, preferred_element_type=F32)


def _dot_nt(a, b, precision=None):
    return lax.dot_general(a, b, (((1,), (1,)), ((), ())), precision=precision,
                           preferred_element_type=F32)


def _dot_tn(a, b, precision=None):
    return lax.dot_general(a, b, (((0,), (0,)), ((), ())), precision=precision,
                           preferred_element_type=F32)


def _cumsum3(tri, x):
    x1 = x.astype(BF16)
    res = x - x1.astype(F32)
    x2 = res.astype(BF16)
    x3 = (res - x2.astype(F32)).astype(BF16)
    return (jnp.dot(tri, x1, preferred_element_type=F32)
            + jnp.dot(tri, x2, preferred_element_type=F32)
            + jnp.dot(tri, x3, preferred_element_type=F32))


def _pack_pair(a, b):
    ua = pltpu.bitcast(a.astype(BF16).astype(F32), U32)
    ub = pltpu.bitcast(b.astype(BF16).astype(F32), U32)
    return jnp.bitwise_or(ub, jnp.right_shift(ua, jnp.uint32(16)))


def _unpack_pair(w):
    a = pltpu.bitcast(jnp.left_shift(w, jnp.uint32(16)), F32)
    b = pltpu.bitcast(jnp.bitwise_and(w, jnp.uint32(0xFFFF0000)), F32)
    return a, b


def _sigmoid(x):
    return 1.0 / (1.0 + jnp.exp(-x))


def _softplus(x):
    return jnp.maximum(x, 0.0) + jnp.log(1.0 + jnp.exp(-jnp.abs(x)))


def _seg_ones(n, seg):
    r = lax.broadcasted_iota(jnp.int32, (n, n), 0) // seg
    c = lax.broadcasted_iota(jnp.int32, (n, n), 1) // seg
    return (r == c).astype(F32)


def _tri(n, strict):
    r = lax.broadcasted_iota(jnp.int32, (n, n), 0)
    c = lax.broadcasted_iota(jnp.int32, (n, n), 1)
    return (c < r) if strict else (c <= r)


def _adaln_kernel(c_ref, w_ref, b_ref, o_ref):
    c = c_ref[...]
    c_act = c * _sigmoid(c)
    o_ref[...] = _dot_hi(c_act, w_ref[...]) + b_ref[...]


def _adaln(c, ada_w, ada_b):
    L, D, N = ada_w.shape
    B = c.shape[0]
    tn = D
    return pl.pallas_call(
        _adaln_kernel,
        grid=(L, N // tn),
        in_specs=[
            pl.BlockSpec((B, D), lambda l, j: (0, 0)),
            pl.BlockSpec((None, D, tn), lambda l, j: (l, 0, j)),
            pl.BlockSpec((None, 1, tn), lambda l, j: (l, 0, j)),
        ],
        out_specs=pl.BlockSpec((None, B, tn), lambda l, j: (l, 0, j)),
        out_shape=jax.ShapeDtypeStruct((L, B, N), F32),
        name="adaln",
    )(c, ada_w, ada_b.reshape(L, 1, N))


_IN_SEGS = (A_COLS, R_COLS, G_COLS_PAD, GATE_COLS)
_IN_DTYPES = (BF16, F32, F32, BF16)
_IN_CHUNK = 512


_C_GLA = A_COLS + R_COLS
_C_GAD = _C_GLA + 2 * G_KW + G_VW
_C_SPLIT = _C_GAD + G_LORA
_IN_PLAN = (
    (0, 0, 0, 0, A_COLS),
    (1, 0, 0, A_COLS, R_COLS),
    (2, 0, 0, _C_GLA, 2 * G_KW + G_VW),
    (2, 2 * G_KW + G_VW + LANES, 1, 0, G_VW),
    (3, 0, 1, G_VW, GATE_COLS),
)


def _inproj_kernel(x_ref, mod_ref, g_ref, wa_ref, wb_ref, *o_refs):
    x = x_ref[...]
    D = x.shape[-1]
    ms = jnp.mean(x * x, axis=-1, keepdims=True)
    y = x * lax.rsqrt(ms + EPS) * g_ref[...]
    sh = mod_ref[:, 0:D]
    sc = mod_ref[:, D:2 * D]
    h = (y * (1.0 + sc) + sh).astype(BF16)
    w_refs = (wa_ref, wb_ref)
    for out, oc, part, wc, width in _IN_PLAN:
        o_ref, w_ref = o_refs[out], w_refs[part]
        for c0 in range(0, width, _IN_CHUNK):
            n = min(_IN_CHUNK, width - c0)
            o_ref[:, oc + c0:oc + c0 + n] = jnp.dot(
                h, w_ref[:, wc + c0:wc + c0 + n], preferred_element_type=F32).astype(o_ref.dtype)
    lora = jnp.dot(h, wa_ref[:, _C_GAD:_C_SPLIT], preferred_element_type=F32)
    pad = jnp.zeros((lora.shape[0], LANES - G_LORA), F32)
    c_lora = 2 * G_KW + G_VW
    o_refs[2][:, c_lora:c_lora + LANES] = jnp.concatenate([lora, pad], axis=-1)


def _inproj(x2, mod_l, norm_g, w_parts, layer, S, tm):
    T, D = x2.shape
    wspec = lambda w: pl.BlockSpec((None, D, w.shape[2]), lambda i: (layer, 0, 0),
                                   pipeline_mode=pl.Buffered(1))
    tiles_per_batch = S // tm
    return pl.pallas_call(
        _inproj_kernel,
        grid=(T // tm,),
        in_specs=[
            pl.BlockSpec((tm, D), lambda i: (i, 0)),
            pl.BlockSpec((None, 1, 2 * D), lambda i: (i // tiles_per_batch, 0, 0)),
            pl.BlockSpec((1, D), lambda i: (0, 0)),
            wspec(w_parts[0]), wspec(w_parts[1]),
        ],
        out_specs=[pl.BlockSpec((tm, w), lambda i: (i, 0)) for w in _IN_SEGS],
        out_shape=[jax.ShapeDtypeStruct((T, w), dt) for w, dt in zip(_IN_SEGS, _IN_DTYPES)],
        compiler_params=pltpu.CompilerParams(
            dimension_semantics=("arbitrary",), vmem_limit_bytes=VMEM_LIMIT),
        name="inproj",
    )(x2, mod_l, norm_g.reshape(1, D), *w_parts)


A_TILE = 256
A_POS_SPLIT = 64
LOG2E = math.log2(math.e)


def _split_dot(x, ones):
    hi = x.astype(BF16)
    lo = (x - hi.astype(F32)).astype(BF16)
    return (jnp.dot(hi, ones, preferred_element_type=F32)
            + jnp.dot(lo, ones, preferred_element_type=F32))


def _eye(n):
    r = lax.broadcasted_iota(jnp.int32, (n, n), 0)
    c = lax.broadcasted_iota(jnp.int32, (n, n), 1)
    return (r == c).astype(BF16)


def _attn_kernel(q_ref, k_ref, v_ref, qg_ref, kg_ref, lam_ref, sg_ref, qaug_ref, kaug_ref, o_ref,
                 qt_s, ka_s, vt_s, s_s, *, S, lambda_init):
    t = A_TILE
    seg = _seg_ones(LANES, A_DH).astype(BF16)
    eye_t = _eye(t)
    eye_l = _eye(LANES)
    is_qk = lax.broadcasted_iota(jnp.int32, (t, LANES), 1) < A_DH

    blocks = [slice(b * t, (b + 1) * t) for b in range(S // t)]
    sides = ((q_ref, qg_ref, A_DH ** -0.5 * LOG2E), (k_ref, kg_ref, 1.0))
    xf = [[ref[rows, :].astype(F32) for rows in blocks] for ref, _, _ in sides]
    ms = [[jnp.dot((x * x).astype(BF16), seg, preferred_element_type=F32) for x in xs]
          for xs in xf]
    qn, kn = [[x * lax.rsqrt(m * (1.0 / A_DH) + EPS) * g_ref[...] * scale
               for x, m in zip(xs, mss)] for xs, mss, (_, g_ref, scale) in zip(xf, ms, sides)]
    for b, rows in enumerate(blocks):
        vt_s[:, rows] = _dot_nt(eye_l, v_ref[rows, :]).astype(BF16)
        for c in range(2):
            kc = kn[b] if c == 0 else pltpu.roll(kn[b], A_DH, 1)
            ka_s[c, rows, :] = jnp.where(is_qk, kc.astype(BF16), kaug_ref[rows, :])
    qas = [[jnp.where(is_qk, (qn[b] if c == 0 else pltpu.roll(qn[b], A_DH, 1)).astype(BF16),
                      qaug_ref[rows, :]) for c in range(2)] for b, rows in enumerate(blocks)]
    qts = [[_dot_nt(eye_l, qa) for qa in pair] for pair in qas]
    for b, rows in enumerate(blocks):
        for c in range(2):
            qt_s[c, :, rows] = qts[b][c].astype(BF16)

    lv = lam_ref[...]
    lam = (jnp.exp(jnp.sum(lv[0:1] * lv[1:2], axis=-1, keepdims=True))
           - jnp.exp(jnp.sum(lv[2:3] * lv[3:4], axis=-1, keepdims=True)) + lambda_init)
    causal = (lax.broadcasted_iota(jnp.int32, (t, t), 0)
              <= lax.broadcasted_iota(jnp.int32, (t, t), 1))

    nt = S // t

    def score_block(i, c, j, st):
        s = jnp.dot(ka_s[c, j * t:(j + 1) * t, :], qt_s[c, :, i * t:(i + 1) * t],
                    preferred_element_type=F32)
        if j == i:
            s = jnp.where(causal, s, -jnp.inf)
        s_s[i % 2, c, j * t:(j + 1) * t, :] = s
        mj = jnp.max(s, axis=0, keepdims=True)
        st["m"][c] = mj if st["m"][c] is None else jnp.maximum(st["m"][c], mj)

    def value_block(i, c, j, st):
        p = jnp.exp2(s_s[i % 2, c, j * t:(j + 1) * t, :] - st["m"][c])
        st["l"][c] = st["l"][c] + jnp.sum(p, axis=0, keepdims=True)
        st["acc"][c] = st["acc"][c] + jnp.dot(vt_s[:, j * t:(j + 1) * t], p.astype(BF16),
                                              preferred_element_type=F32)

    def finish(i, st):
        o = st["acc"][0] / st["l"][0] - lam * (st["acc"][1] / st["l"][1])
        ms = jnp.mean(o * o, axis=0, keepdims=True)
        o = o * lax.rsqrt(ms + EPS) * sg_ref[...] * (1.0 - lambda_init)
        o_ref[i * t:(i + 1) * t, :] = _dot_nt(eye_t, o.astype(BF16)).astype(o_ref.dtype)

    prev = None
    for i in range(nt + 1):
        cur = None
        first = []
        if i < nt:
            cur = dict(m=[None, None], l=[jnp.zeros((1, t), F32)] * 2,
                       acc=[jnp.zeros((A_DV, t), F32)] * 2)
            first = [(c, j) for j in range(i + 1) for c in range(2)]
        second = [(c, j) for j in range(i) for c in range(2)] if prev is not None else []
        for n in range(max(len(first), len(second))):
            if n < len(first):
                score_block(i, first[n][0], first[n][1], cur)
            if n < len(second):
                value_block(i - 1, second[n][0], second[n][1], prev)
        if prev is not None:
            finish(i - 1, prev)
        prev = cur


def _alibi_columns(S):
    pos = jnp.arange(S, dtype=jnp.int32)[:, None]
    hi = (pos // A_POS_SPLIT).astype(F32)
    lo = (pos % A_POS_SPLIT).astype(F32)
    kl = jnp.arange(LANES, dtype=jnp.int32)[None, :] - A_DH
    term = kl // 2
    used = jnp.logical_and(kl >= 0, kl < 8)
    k_aug = jnp.where(used, jnp.where(term == 0, hi, jnp.where(term == 1, lo, 1.0)), 0.0)
    q_augs = []
    for h in range(A_HEADS):
        s2 = 2.0 ** (-8.0 * (h + 1) / A_HEADS) * LOG2E
        qv = jnp.where(term == 0, A_POS_SPLIT * s2,
             jnp.where(term == 1, s2,
             jnp.where(term == 2, -A_POS_SPLIT * s2 * hi, -s2 * lo)))
        qv_hi = qv.astype(BF16).astype(F32)
        q_augs.append(jnp.where(used, jnp.where(kl % 2 == 0, qv_hi, qv - qv_hi), 0.0))
    return jnp.stack(q_augs).astype(BF16), k_aug.astype(BF16)


def _attention(p_attn, qn_g, kn_g, lam_vecs, subln_g, lambda_init, B, S):
    pa = p_attn.reshape(B, S, A_COLS)
    dup = lambda g: jnp.concatenate([g, g]).reshape(1, LANES)
    q_aug, k_aug = _alibi_columns(S)
    nqb = A_QW // LANES
    kern = functools.partial(_attn_kernel, S=S, lambda_init=lambda_init)
    out = pl.pallas_call(
        kern,
        grid=(B, A_HEADS),
        in_specs=[
            pl.BlockSpec((None, S, LANES), lambda b, h: (b, 0, h)),
            pl.BlockSpec((None, S, LANES), lambda b, h: (b, 0, nqb + h)),
            pl.BlockSpec((None, S, LANES), lambda b, h: (b, 0, 2 * nqb + h)),
            pl.BlockSpec((1, LANES), lambda b, h: (0, 0)),
            pl.BlockSpec((1, LANES), lambda b, h: (0, 0)),
            pl.BlockSpec((4, A_DH), lambda b, h: (0, 0)),
            pl.BlockSpec((A_DV, 1), lambda b, h: (0, 0)),
            pl.BlockSpec((None, S, LANES), lambda b, h: (h, 0, 0)),
            pl.BlockSpec((S, LANES), lambda b, h: (0, 0)),
        ],
        out_specs=pl.BlockSpec((None, S, A_DV), lambda b, h: (b, 0, h)),
        out_shape=jax.ShapeDtypeStruct((B, S, A_VW), BF16),
        scratch_shapes=[pltpu.VMEM((2, LANES, S), BF16), pltpu.VMEM((2, S, LANES), BF16),
                        pltpu.VMEM((A_DV, S), BF16), pltpu.VMEM((2, 2, S, A_TILE), F32)],
        compiler_params=pltpu.CompilerParams(
            dimension_semantics=("arbitrary", "arbitrary"), vmem_limit_bytes=VMEM_LIMIT),
        name="diff_attn",
    )(pa, pa, pa, dup(qn_g), dup(kn_g), lam_vecs, subln_g.reshape(A_DV, 1), q_aug, k_aug)
    return out.reshape(B * S, A_VW)


R_GROUP = 4
R_GW = R_GROUP * R_N


def _rwkv_kernel(p_ref, mu_ref, wup_ref, w0_ref, aup_ref, a0_ref, gup_ref, kk_ref, ka_ref,
                 rk_ref, lg_ref, lb_ref, o_ref,
                 carry_ref, st_ref, al_s, be_s, ka_s, rh_s, bt_s, kt_s, v_s, gc_s, y_s, *, TB):
    @pl.when(pl.program_id(1) == 0)
    def _():
        carry_ref[...] = jnp.zeros_like(carry_ref)
        st_ref[...] = jnp.zeros_like(st_ref)

    xs = p_ref[...]
    prev = pltpu.roll(xs, 1, 0)
    row = lax.broadcasted_iota(jnp.int32, (TB, 1), 0)
    prev = jnp.where(row == 0, carry_ref[...], prev)
    carry_ref[...] = xs[TB - 1:TB, :]
    xm = xs + (prev - xs) * mu_ref[...]
    r = xm[:, 0:R_W]
    k = xm[:, R_W:2 * R_W]
    v = xm[:, 2 * R_W:3 * R_W]
    wa = xm[:, 3 * R_W:3 * R_W + LANES]
    gd = xm[:, 3 * R_W + LANES:3 * R_W + 2 * LANES]
    wz = w0_ref[...] + _dot(jnp.tanh(wa), wup_ref[...])
    lw = -math.exp(-0.5) * _sigmoid(wz)
    a = _sigmoid(a0_ref[...] + _dot(wa, aup_ref[...]))
    g = _dot(_sigmoid(gd), gup_ref[...])
    seg = _seg_ones(R_GW, R_N).astype(BF16)

    def head_sum(x, passes):
        f = _split_dot if passes == 2 else (
            lambda t, o: jnp.dot(t.astype(BF16), o, preferred_element_type=F32))
        return jnp.concatenate(
            [f(x[:, j * R_GW:(j + 1) * R_GW], seg) for j in range(R_W // R_GW)], axis=-1)

    kk = k * kk_ref[...]
    kk = kk * jnp.minimum(lax.rsqrt(head_sum(kk * kk, 1)), 1e12)
    k2 = k * (1.0 + (a - 1.0) * ka_ref[...])
    bonus = head_sum(r * k2 * rk_ref[...], 2) * v
    bv = kk * a

    rr = lax.broadcasted_iota(jnp.int32, (TB, TB), 0)
    cc = lax.broadcasted_iota(jnp.int32, (TB, TB), 1)
    tril_blk = jnp.logical_and(rr // CHUNK == cc // CHUNK, cc <= rr).astype(BF16)
    Lg = _cumsum3(tril_blk, lw)
    inv = jnp.exp(-Lg)
    al_s[...] = (jnp.exp(Lg - lw) * kk).astype(BF16)
    be_s[...] = (bv * inv).astype(BF16)
    ka_s[...] = (k2 * inv).astype(BF16)
    rh_s[...] = (jnp.exp(Lg) * r).astype(BF16)
    v_s[...] = v.astype(BF16)
    for c in range(TB // CHUNK):
        rows = slice(c * CHUNK, (c + 1) * CHUNK)
        gC = Lg[(c + 1) * CHUNK - 1:(c + 1) * CHUNK, :]
        tail = jnp.exp(gC - Lg[rows, :])
        bt_s[rows, :] = (bv[rows, :] * tail).astype(BF16)
        kt_s[rows, :] = (k2[rows, :] * tail).astype(BF16)
        gc_s[c * SUBLANES:(c + 1) * SUBLANES, :] = jnp.broadcast_to(jnp.exp(gC), (SUBLANES, R_W))

    ri = lax.broadcasted_iota(jnp.int32, (R_GW, R_GW), 0)
    ci = lax.broadcasted_iota(jnp.int32, (R_GW, R_GW), 1)
    blk = ri // R_N == ci // R_N
    strict_t = ri % R_N < ci % R_N
    strict_st = (lax.broadcasted_iota(jnp.int32, (R_N, R_GW), 1) % R_N
                 < lax.broadcasted_iota(jnp.int32, (R_N, R_GW), 0))
    incl = ci % R_N <= ri % R_N
    zero = jnp.zeros((), BF16)

    def expand(x):
        return jnp.where(blk, jnp.concatenate([x] * R_GROUP, axis=0), zero)

    n_groups = R_HEADS // R_GROUP
    pair = 4

    def chunk_pair(ip, _):
        chains = []
        for dc in range(pair):
            c = ip * pair + dc
            rows = pl.ds(pl.multiple_of(c * CHUNK, CHUNK), CHUNK)
            for gi in range(n_groups):
                chains.append(dict(c=c, rows=rows, gi=gi, cols=slice(gi * R_GW, (gi + 1) * R_GW)))
        for ch in chains:
            rows, cols = ch["rows"], ch["cols"]
            ch["al"] = al_s[rows, cols]
            ch["A"] = expand(ch["al"])
            ch["R"] = expand(rh_s[rows, cols])
            ch["B"] = expand(be_s[rows, cols])
            ch["K"] = expand(ka_s[rows, cols])
            vc = v_s[rows, cols]
            ch["V"] = jnp.concatenate(
                [vc[:, h * R_N:(h + 1) * R_N] for h in range(R_GROUP)], axis=0)
        for ch in chains:
            ch["P"] = -jnp.where(strict_st, _dot_nt(ch["al"], ch["B"]), 0.0)
            ch["m_ak_t"] = jnp.where(strict_t, _dot_nt(ch["K"], ch["A"]), 0.0).astype(BF16)
            ch["m_rb"] = jnp.where(incl, _dot_nt(ch["R"], ch["B"]), 0.0).astype(BF16)
            ch["m_rk"] = jnp.where(incl, _dot_nt(ch["R"], ch["K"]), 0.0).astype(BF16)
        for ch in chains:
            ch["Z"] = ch["A"].astype(F32)
            ch["Wt"] = _dot_tn(ch["V"], ch["m_ak_t"])
        n = 1
        while True:
            last = 2 * n >= CHUNK
            for ch in chains:
                Pb = ch["P"].astype(BF16)
                Xb = expand(Pb)
                ch["Z"] = ch["Z"] + jnp.dot(Xb, ch["Z"].astype(BF16), preferred_element_type=F32)
                ch["Wt"] = ch["Wt"] + _dot_nt(ch["Wt"].astype(BF16), Xb)
                if not last:
                    ch["P"] = jnp.dot(Pb, Xb, preferred_element_type=F32)
            n *= 2
            if last:
                break
        for ch in chains:
            rows, cols = ch["rows"], ch["cols"]
            Bt = expand(bt_s[rows, cols])
            Kt = expand(kt_s[rows, cols])
            Zb = ch["Z"].astype(BF16)
            Wtb = ch["Wt"].astype(BF16)
            ch["y_a"] = (ch["R"].astype(F32)
                         - jnp.dot(ch["m_rb"], Zb, preferred_element_type=F32)).astype(BF16)
            ch["y_b"] = (jnp.dot(ch["m_rk"], ch["V"], preferred_element_type=F32)
                         - _dot_nt(ch["m_rb"], Wtb))
            ch["p_neg"] = _dot_tn(Zb, Bt).astype(BF16)
            ch["q"] = _dot_tn(ch["V"], Kt) - jnp.dot(Wtb, Bt, preferred_element_type=F32)
        for ch in chains:
            rows, cols, gi = ch["rows"], ch["cols"], ch["gi"]
            g0 = pl.multiple_of(ch["c"] * SUBLANES, SUBLANES)
            S0 = st_ref[gi]
            S0b = S0.astype(BF16)
            y = _dot_nt(ch["y_a"], S0b) + ch["y_b"]
            st_ref[gi] = (S0 * gc_s[pl.ds(g0, SUBLANES), cols][0:1, :]
                          - jnp.dot(S0b, ch["p_neg"], preferred_element_type=F32) + ch["q"])
            for h in range(R_GROUP):
                hh = gi * R_GROUP + h
                y_s[rows, hh * R_N:(hh + 1) * R_N] = y[h * R_N:(h + 1) * R_N, :]
        return 0

    lax.fori_loop(0, TB // CHUNK // pair, chunk_pair, 0)

    y = y_s[...]
    mean = head_sum(y, 2) * (1.0 / R_N)
    yc = y - mean
    var = head_sum(yc * yc, 1) * (1.0 / R_N)
    yn = yc * lax.rsqrt(var + RWKV_GN_EPS) * lg_ref[...] + lb_ref[...]
    o_ref[...] = ((yn + bonus) * g).astype(o_ref.dtype)


def _rwkv(p_rwkv, mu, w_up, w0, a_up, a0, g_up, k_k, k_a, r_k, lnx_g, lnx_b, B, S):
    TB = 256
    T = B * S
    nt = S // TB
    row = lambda t: t.reshape(1, -1)
    zeros = jnp.zeros((R_N, R_W), F32)
    wup_pad = jnp.concatenate([w_up, zeros], axis=0)
    aup_pad = jnp.concatenate([zeros, a_up], axis=0)
    vec = lambda n: pl.BlockSpec((1, n), lambda b, i: (0, 0))
    mat = lambda m, n: pl.BlockSpec((m, n), lambda b, i: (0, 0))
    kern = functools.partial(_rwkv_kernel, TB=TB)
    return pl.pallas_call(
        kern,
        grid=(B, nt),
        in_specs=[
            pl.BlockSpec((TB, R_COLS), lambda b, i: (b * nt + i, 0)),
            vec(R_COLS), mat(LANES, R_W), vec(R_W), mat(LANES, R_W), vec(R_W), mat(LANES, R_W),
            vec(R_W), vec(R_W), vec(R_W), vec(R_W), vec(R_W),
        ],
        out_specs=pl.BlockSpec((TB, R_W), lambda b, i: (b * nt + i, 0)),
        out_shape=jax.ShapeDtypeStruct((T, R_W), BF16),
        scratch_shapes=[
            pltpu.VMEM((1, R_COLS), F32),
            pltpu.VMEM((R_HEADS // R_GROUP, R_N, R_GW), F32),
        ] + [pltpu.VMEM((TB, R_W), BF16)] * 7 + [
            pltpu.VMEM((TB // CHUNK * SUBLANES, R_W), F32),
            pltpu.VMEM((TB, R_W), F32),
        ],
        compiler_params=pltpu.CompilerParams(
            dimension_semantics=("arbitrary", "arbitrary"), vmem_limit_bytes=VMEM_LIMIT),
        name="rwkv7",
    )(p_rwkv, row(mu), wup_pad, row(w0), aup_pad, row(a0), g_up, row(k_k), row(k_a),
      row(r_k), row(lnx_g), row(lnx_b))


G_SUBTILE = 256


def _gla_kernel(p_ref, aup_ref, ab_ref, ng_ref, o_ref, st_ref, *, TB):
    @pl.when(pl.program_id(1) == 0)
    def _():
        st_ref[...] = jnp.zeros_like(st_ref)

    c_gv = 2 * G_KW
    c_ad = c_gv + G_VW
    c_gate = c_ad + LANES
    sub = G_SUBTILE
    subs = range(TB // sub)
    nchunk = sub // CHUNK
    rr = lax.broadcasted_iota(jnp.int32, (sub, sub), 0)
    cc = lax.broadcasted_iota(jnp.int32, (sub, sub), 1)
    causal = jnp.logical_and(rr // CHUNK == cc // CHUNK, cc <= rr)
    causal_b = causal.astype(BF16)
    heads = range(G_HEADS)
    sls = [slice(h * G_DK, (h + 1) * G_DK) for h in heads]
    vss = [slice(h * G_DV, (h + 1) * G_DV) for h in heads]

    rws = [slice(u * sub, (u + 1) * sub) for u in subs]
    zs = [_dot(p_ref[r, c_ad:c_ad + LANES], aup_ref[...]) + ab_ref[...] for r in rws]
    las = [-_softplus(-z) * (1.0 / G_TAU) for z in zs]
    bs = [_cumsum3(causal_b, la) for la in las]
    ks = [p_ref[r, G_KW:2 * G_KW] for r in rws]
    vbs = [p_ref[r, c_gv:c_gv + G_VW].astype(BF16) for r in rws]
    qes = [(p_ref[r, 0:G_KW] * (G_DK ** -0.5) * jnp.exp(b)).astype(BF16) for r, b in zip(rws, bs)]
    kes = [(k * jnp.exp(-b)).astype(BF16) for k, b in zip(ks, bs)]
    scs = [[jnp.where(causal, _dot_nt(qes[u][:, sls[h]], kes[u][:, sls[h]]), 0.0) for h in heads]
           for u in subs]
    o_intra = [[_dot(scs[u][h], vbs[u][:, vss[h]]) for h in heads] for u in subs]
    kts, e_lasts, kvs = {}, {}, {}
    for u in subs:
        for c in range(nchunk):
            rows = slice(c * CHUNK, (c + 1) * CHUNK)
            b_last = bs[u][(c + 1) * CHUNK - 1:(c + 1) * CHUNK, :]
            kt = (ks[u][rows, :] * jnp.exp(b_last - bs[u][rows, :])).astype(BF16)
            e_lasts[u, c] = jnp.exp(b_last)
            for h in heads:
                kvs[u, c, h] = _dot_tn(vbs[u][rows, vss[h]], kt[:, sls[h]])
    states = [st_ref[h] for h in heads]
    parts = [[] for _ in heads]
    for u in subs:
        for c in range(nchunk):
            rows = slice(c * CHUNK, (c + 1) * CHUNK)
            for h in heads:
                parts[h].append(o_intra[u][h][rows, :]
                                + _dot_nt(qes[u][rows, sls[h]], states[h].astype(BF16)))
                states[h] = states[h] * e_lasts[u, c][:, sls[h]] + kvs[u, c, h]
    for h in heads:
        vs = vss[h]
        st_ref[h] = states[h]
        o = jnp.concatenate(parts[h], axis=0)
        ms = jnp.mean(o * o, axis=-1, keepdims=True)
        gt = p_ref[:, c_gate + h * G_DV:c_gate + (h + 1) * G_DV]
        o = o * lax.rsqrt(ms + EPS) * ng_ref[...] * (gt * _sigmoid(gt))
        o_ref[:, vs] = o.astype(o_ref.dtype)


def _gla(p_gla, alpha_up, alpha_b, norm_g, B, S):
    TB = 512
    T = B * S
    nt = S // TB
    aup_pad = jnp.concatenate([alpha_up, jnp.zeros((LANES - G_LORA, G_KW), F32)], axis=0)
    kern = functools.partial(_gla_kernel, TB=TB)
    return pl.pallas_call(
        kern,
        grid=(B, nt),
        in_specs=[
            pl.BlockSpec((TB, G_COLS_PAD), lambda b, i: (b * nt + i, 0)),
            pl.BlockSpec((LANES, G_KW), lambda b, i: (0, 0)),
            pl.BlockSpec((1, G_KW), lambda b, i: (0, 0)),
            pl.BlockSpec((1, G_DV), lambda b, i: (0, 0)),
        ],
        out_specs=pl.BlockSpec((TB, G_VW), lambda b, i: (b * nt + i, 0)),
        out_shape=jax.ShapeDtypeStruct((T, G_VW), BF16),
        scratch_shapes=[pltpu.VMEM((G_HEADS, G_DV, G_DK), F32)],
        compiler_params=pltpu.CompilerParams(
            dimension_semantics=("arbitrary", "arbitrary"), vmem_limit_bytes=VMEM_LIMIT),
        name="gla",
    )(p_gla, aup_pad, alpha_b.reshape(1, G_KW), norm_g.reshape(1, G_DV))


def _merge_kernel(x_ref, oa_ref, or_ref, og_ref, gate_ref, pa_ref, pr_ref, pg_ref, wo_ref,
                  mod_ref, g2_ref, rw_ref, rb_ref, x1_ref, h2_ref, info_ref, cnt_ref, carry_ref,
                  lg_s):
    D = D_MODEL
    i = pl.program_id(0)

    @pl.when(i == 0)
    def _():
        carry_ref[...] = jnp.zeros_like(carry_ref)
        lg_s[...] = jnp.zeros_like(lg_s)

    merged = (_sigmoid(gate_ref[:, 0:D].astype(F32))
              * jnp.dot(oa_ref[...], pa_ref[...], preferred_element_type=F32)
              + _sigmoid(gate_ref[:, D:2 * D].astype(F32))
              * jnp.dot(or_ref[...], pr_ref[...], preferred_element_type=F32)
              + _sigmoid(gate_ref[:, 2 * D:3 * D].astype(F32))
              * jnp.dot(og_ref[...], pg_ref[...], preferred_element_type=F32))
    gt1 = mod_ref[:, 2 * D:3 * D]
    sh2 = mod_ref[:, 3 * D:4 * D]
    sc2 = mod_ref[:, 4 * D:5 * D]
    x1 = x_ref[...] + gt1 * jnp.dot(merged.astype(BF16), wo_ref[...], preferred_element_type=F32)
    x1_ref[...] = x1
    info_ref[...], cnt_ref[...] = _route_tile(lg_s[...], carry_ref, jnp.where(i > 0, 1.0, 0.0))
    ms = jnp.mean(x1 * x1, axis=-1, keepdims=True)
    h2 = x1 * lax.rsqrt(ms + EPS) * g2_ref[...] * (1.0 + sc2) + sh2
    for s in range(NSUB):
        h2_ref[pl.ds(s, h2.shape[0], stride=NSUB), :] = _pack_pair(
            h2[:, 2 * s * LANES:(2 * s + 1) * LANES], h2[:, (2 * s + 1) * LANES:(2 * s + 2) * LANES])
    h_hi = h2.astype(BF16)
    h_lo = (h2 - h_hi.astype(F32)).astype(BF16)
    lg_s[...] = (jnp.dot(h_hi, rw_ref[0], preferred_element_type=F32)
                 + jnp.dot(h_lo, rw_ref[0], preferred_element_type=F32)
                 + jnp.dot(h_hi, rw_ref[1], preferred_element_type=F32) + rb_ref[...])


def _merge(x2, o_a, o_r, o_g, p_gate, proj_a, proj_r, proj_g, w_out, mod_l, norm2_g,
           router_w, router_b, S, tm):
    T, D = x2.shape
    tiles_per_batch = S // tm
    n = T // tm
    cur = lambda i: jnp.minimum(i, n - 1)
    tile = lambda w: pl.BlockSpec((tm, w), lambda i: (cur(i), 0))
    const = lambda m, n: pl.BlockSpec((m, n), lambda i: (0, 0))
    return pl.pallas_call(
        _merge_kernel,
        grid=(n + 1,),
        in_specs=[
            tile(D), tile(A_VW), tile(R_W), tile(G_VW), tile(GATE_COLS),
            const(A_VW, D), const(R_W, D), const(G_VW, D), const(D, D),
            pl.BlockSpec((None, 1, 6 * D), lambda i: (cur(i) // tiles_per_batch, 0, 0)),
            const(1, D), pl.BlockSpec((2, D, LANES), lambda i: (0, 0, 0)), const(1, LANES),
        ],
        out_specs=[
            tile(D),
            pl.BlockSpec((tm * NSUB, LANES), lambda i: (cur(i), 0)),
            pl.BlockSpec((tm, LANES), lambda i: (jnp.maximum(i - 1, 0), 0)),
            const(1, LANES),
        ],
        out_shape=[
            jax.ShapeDtypeStruct((T, D), F32),
            jax.ShapeDtypeStruct((T * NSUB, LANES), U32),
            jax.ShapeDtypeStruct((T, LANES), F32),
            jax.ShapeDtypeStruct((1, LANES), F32),
        ],
        scratch_shapes=[pltpu.VMEM((1, LANES), F32), pltpu.VMEM((tm, LANES), F32)],
        compiler_params=pltpu.CompilerParams(
            dimension_semantics=("arbitrary",), vmem_limit_bytes=VMEM_LIMIT),
        name="merge",
    )(x2, o_a, o_r, o_g, p_gate, proj_a, proj_r, proj_g, w_out, mod_l,
      norm2_g.reshape(1, D), router_w, router_b)


MOE_ROWS = 256
E_LANE0 = N_GROUPS


def _route_tile(lg, carry_ref, live):
    n = lg.shape[0]
    lane = lax.broadcasted_iota(jnp.int32, (n, LANES), 1).astype(F32)
    neg = -jnp.inf
    big = float(LANES)

    def first_max(vals):
        m = jnp.max(vals, axis=-1, keepdims=True)
        idx = jnp.min(jnp.where(vals == m, lane, big), axis=-1, keepdims=True)
        return m, idx

    in_grp = lane < N_GROUPS
    gm, grp = first_max(jnp.where(in_grp, lg, neg))
    g_prob = 1.0 / jnp.sum(jnp.where(in_grp, jnp.exp(lg - gm), 0.0), axis=-1, keepdims=True)
    lo = E_LANE0 + grp * EXP_PER_GROUP
    el = jnp.where(jnp.logical_and(lane >= lo, lane < lo + EXP_PER_GROUP), lg, neg)
    v1, i1 = first_max(el)
    v2, i2 = first_max(jnp.where(lane == i1, neg, el))
    e21 = jnp.exp(v2 - v1)
    w0 = g_prob / (1.0 + e21)
    w1 = g_prob * e21 / (1.0 + e21)
    oh0 = lane == i1
    oh1 = lane == i2
    oh = jnp.logical_or(oh0, oh1).astype(F32) * live
    before = _tri(n, True).astype(BF16)
    cnt = jnp.dot(before, oh.astype(BF16), preferred_element_type=F32) + carry_ref[...]
    rank0 = jnp.sum(jnp.where(oh0, cnt, 0.0), axis=-1, keepdims=True)
    rank1 = jnp.sum(jnp.where(oh1, cnt, 0.0), axis=-1, keepdims=True)
    carry = carry_ref[...] + jnp.sum(oh, axis=0, keepdims=True)
    carry_ref[...] = carry
    cols = (i1 - E_LANE0, i2 - E_LANE0, rank0, rank1, w0, w1)
    info = jnp.zeros((n, LANES), F32)
    for j, col in enumerate(cols):
        info = jnp.where(lane == j, col, info)
    return info, carry


def _route_tables(info, cnt):
    T = info.shape[0]
    A = T * TOP_K
    counts = cnt[0, E_LANE0:E_LANE0 + N_EXPERTS].astype(jnp.int32)
    padded = (counts + MOE_ROWS - 1) // MOE_ROWS * MOE_ROWS
    pad_end = jnp.cumsum(padded)
    pad_start = pad_end - padded
    n_blocks = -(-A // MOE_ROWS) + N_EXPERTS
    eid = info[:, 0:2].astype(jnp.int32)
    is_e = eid[:, :, None] == jnp.arange(N_EXPERTS, dtype=jnp.int32)
    dest = (jnp.sum(jnp.where(is_e, pad_start, 0), axis=-1)
            + info[:, 2:4].astype(jnp.int32))
    blk_start = jnp.arange(n_blocks, dtype=jnp.int32) * MOE_ROWS
    blk_exp = jnp.minimum(jnp.sum(pad_end[None, :] <= blk_start[:, None], axis=1),
                          N_EXPERTS - 1).astype(jnp.int32)
    n_used = (pad_end[-1:] // MOE_ROWS).astype(jnp.int32)
    nonempty = counts > 0
    ids = jnp.arange(N_EXPERTS, dtype=jnp.int32)
    grp_of = jnp.cumsum(nonempty.astype(jnp.int32)) - 1
    later = jnp.logical_and(ids[None, :] > ids[:, None], nonempty[None, :])
    nxt_of = jnp.min(jnp.where(later, ids[None, :], N_EXPERTS), axis=1)
    nxt_of = jnp.where(nxt_of == N_EXPERTS, -1, nxt_of)
    is_b = blk_exp[:, None] == ids[None, :]
    blk_grp = jnp.sum(jnp.where(is_b, grp_of[None, :], 0), axis=1).astype(jnp.int32)
    blk_nxt = jnp.sum(jnp.where(is_b, nxt_of[None, :], 0), axis=1).astype(jnp.int32)
    return dest[:, 0], dest[:, 1], blk_exp, n_used, blk_grp, blk_nxt, n_blocks * MOE_ROWS


def _dispatch_kernel(d0_ref, d1_ref, h2_ref, xin_in, xin_hbm, sem):
    del xin_in
    n = h2_ref.shape[0] // NSUB
    base = pl.program_id(0) * n

    def slab(ref, row):
        return ref.at[pl.ds(pl.multiple_of(row * NSUB, NSUB), NSUB)]

    def body(r, _):
        pltpu.make_async_copy(slab(h2_ref, r), slab(xin_hbm, d0_ref[base + r]), sem).start(0)
        pltpu.make_async_copy(slab(h2_ref, r), slab(xin_hbm, d1_ref[base + r]), sem).start(1)
        return 0

    lax.fori_loop(0, n, body, 0, unroll=8)
    for _ in range(TOP_K):
        pltpu.make_async_copy(h2_ref, xin_hbm.at[pl.ds(0, n * NSUB)], sem).wait()


def _dispatch(h2_slab, dest0, dest1, xin_init, td):
    T = h2_slab.shape[0] // NSUB
    grid_spec = pltpu.PrefetchScalarGridSpec(
        num_scalar_prefetch=2,
        grid=(T // td,),
        in_specs=[pl.BlockSpec((td * NSUB, LANES), lambda i, d0, d1: (i, 0)),
                  pl.BlockSpec(memory_space=pl.ANY)],
        out_specs=pl.BlockSpec(memory_space=pl.ANY),
        scratch_shapes=[pltpu.SemaphoreType.DMA(())],
    )
    return pl.pallas_call(
        _dispatch_kernel,
        grid_spec=grid_spec,
        out_shape=jax.ShapeDtypeStruct(xin_init.shape, xin_init.dtype),
        input_output_aliases={3: 0},
        compiler_params=pltpu.CompilerParams(dimension_semantics=("arbitrary",)),
        name="moe_dispatch",
    )(dest0, dest1, h2_slab, xin_init)


def _moe_kernel(be_ref, nu_ref, grp_ref, nxt_ref, x_ref, wg_hbm, wu_hbm, wd_hbm, y_ref,
                wgf, wuf, wdf, wgb, wub, wdb, sem, *, layer):
    i = pl.program_id(0)

    def weight_copies(expert, slot):
        return [pltpu.make_async_copy(src.at[layer, expert], dst.at[slot], sem.at[slot, k])
                for k, (src, dst) in enumerate(((wg_hbm, wgf), (wu_hbm, wuf), (wd_hbm, wdf)))]

    @pl.when(i < nu_ref[0])
    def _():
        changed = jnp.logical_or(i == 0, be_ref[i] != be_ref[jnp.maximum(i - 1, 0)])

        @pl.when(changed)
        def _():
            slot = grp_ref[i] % 2

            @pl.when(i == 0)
            def _():
                for cp in weight_copies(be_ref[0], 0):
                    cp.start()

            for cp in weight_copies(be_ref[i], slot):
                cp.wait()
            wgb[...] = wgf[slot].astype(BF16)
            wub[...] = wuf[slot].astype(BF16)
            wdb[...] = wdf[slot].astype(BF16)

            @pl.when(nxt_ref[i] >= 0)
            def _():
                for cp in weight_copies(nxt_ref[i], 1 - slot):
                    cp.start()

        kw = 2 * LANES
        hg = hu = None
        for j in range(NSUB):
            xa, xb = _unpack_pair(x_ref[pl.ds(j, MOE_ROWS, stride=NSUB), :])
            xj = jnp.concatenate([xa.astype(BF16), xb.astype(BF16)], axis=-1)
            dg = jnp.dot(xj, wgb[j * kw:(j + 1) * kw, :], preferred_element_type=F32)
            du = jnp.dot(xj, wub[j * kw:(j + 1) * kw, :], preferred_element_type=F32)
            hg = dg if hg is None else hg + dg
            hu = du if hu is None else hu + du
        hid = (hg * _sigmoid(hg) * hu).astype(BF16)
        for j in range(NSUB):
            yj = jnp.dot(hid, wdb[:, j * kw:(j + 1) * kw], preferred_element_type=F32)
            y_ref[pl.ds(j, MOE_ROWS, stride=NSUB), :] = _pack_pair(yj[:, :LANES], yj[:, LANES:])

    @pl.when(i >= nu_ref[0])
    def _():
        y_ref[...] = jnp.zeros_like(y_ref)


def _moe(xin, blk_exp, n_used, blk_grp, blk_nxt, w_gate, w_up, w_down, layer):
    blk_rows = MOE_ROWS * NSUB
    n_blocks = xin.shape[0] // blk_rows
    last = lambda i, nu: jnp.minimum(i, nu[0] - 1)
    grid_spec = pltpu.PrefetchScalarGridSpec(
        num_scalar_prefetch=4,
        grid=(n_blocks,),
        in_specs=[
            pl.BlockSpec((blk_rows, LANES), lambda i, be, nu, gr, nx: (last(i, nu), 0)),
            pl.BlockSpec(memory_space=pl.ANY), pl.BlockSpec(memory_space=pl.ANY),
            pl.BlockSpec(memory_space=pl.ANY),
        ],
        out_specs=pl.BlockSpec((blk_rows, LANES), lambda i, be, nu, gr, nx: (i, 0)),
        scratch_shapes=[
            pltpu.VMEM((2, D_MODEL, D_EXPERT), F32),
            pltpu.VMEM((2, D_MODEL, D_EXPERT), F32),
            pltpu.VMEM((2, D_EXPERT, D_MODEL), F32),
            pltpu.VMEM((D_MODEL, D_EXPERT), BF16),
            pltpu.VMEM((D_MODEL, D_EXPERT), BF16),
            pltpu.VMEM((D_EXPERT, D_MODEL), BF16),
            pltpu.SemaphoreType.DMA((2, 3)),
        ],
    )
    return pl.pallas_call(
        functools.partial(_moe_kernel, layer=layer),
        grid_spec=grid_spec,
        out_shape=jax.ShapeDtypeStruct(xin.shape, xin.dtype),
        compiler_params=pltpu.CompilerParams(
            dimension_semantics=("arbitrary",), vmem_limit_bytes=VMEM_LIMIT),
        name="moe_ffn",
    )(blk_exp, n_used, blk_grp, blk_nxt, xin, w_gate, w_up, w_down)


def _combine_kernel(d0_ref, d1_ref, y_hbm, x1_ref, info_ref, mod_ref, o_ref, ybuf, sem):
    i = pl.program_id(0)
    nsteps = pl.num_programs(0)
    n = x1_ref.shape[0]
    slot = i % 2
    D = D_MODEL

    def slab(ref, row):
        return ref.at[pl.ds(pl.multiple_of(row * NSUB, NSUB), NSUB)]

    def start_gather(step, sl):
        base = step * n

        def body(r, _):
            pltpu.make_async_copy(
                slab(y_hbm, d0_ref[base + r]), slab(ybuf.at[sl, 0], r), sem.at[sl]).start(0)
            pltpu.make_async_copy(
                slab(y_hbm, d1_ref[base + r]), slab(ybuf.at[sl, 1], r), sem.at[sl]).start(1)
            return 0
        lax.fori_loop(0, n, body, 0, unroll=8)

    @pl.when(i == 0)
    def _():
        start_gather(0, 0)

    @pl.when(i + 1 < nsteps)
    def _():
        start_gather(i + 1, 1 - slot)

    for k in range(TOP_K):
        pltpu.make_async_copy(
            y_hbm.at[pl.ds(0, n * NSUB)], ybuf.at[slot, k], sem.at[slot]).wait()

    w0 = info_ref[:, 4:5]
    w1 = info_ref[:, 5:6]
    for s in range(NSUB):
        piece = pl.ds(s, n, stride=NSUB)
        halves0 = _unpack_pair(ybuf[slot, 0, piece, :])
        halves1 = _unpack_pair(ybuf[slot, 1, piece, :])
        for half in range(2):
            c0 = (2 * s + half) * LANES
            cols = slice(c0, c0 + LANES)
            gt2 = mod_ref[:, 5 * D + c0:5 * D + c0 + LANES]
            moe = w0 * halves0[half] + w1 * halves1[half]
            o_ref[:, cols] = x1_ref[:, cols] + gt2 * moe


def _combine(x1, y, info, dest0, dest1, mod_l, S, tm):
    T, D = x1.shape
    tiles_per_batch = S // tm
    grid_spec = pltpu.PrefetchScalarGridSpec(
        num_scalar_prefetch=2,
        grid=(T // tm,),
        in_specs=[
            pl.BlockSpec(memory_space=pl.ANY),
            pl.BlockSpec((tm, D), lambda i, d0, d1: (i, 0)),
            pl.BlockSpec((tm, LANES), lambda i, d0, d1: (i, 0)),
            pl.BlockSpec((None, 1, 6 * D), lambda i, d0, d1: (i // tiles_per_batch, 0, 0)),
        ],
        out_specs=pl.BlockSpec((tm, D), lambda i, d0, d1: (i, 0)),
        scratch_shapes=[pltpu.VMEM((2, TOP_K, tm * NSUB, LANES), U32),
                        pltpu.SemaphoreType.DMA((2,))],
    )
    return pl.pallas_call(
        _combine_kernel,
        grid_spec=grid_spec,
        out_shape=jax.ShapeDtypeStruct((T, D), F32),
        compiler_params=pltpu.CompilerParams(
            dimension_semantics=("arbitrary",), vmem_limit_bytes=VMEM_LIMIT),
        name="moe_combine",
    )(dest0, dest1, y, x1, info, mod_l)


def _split_w_in(w):
    return w[:, :, :_C_SPLIT].astype(BF16), w[:, :, _C_SPLIT:].astype(BF16)


def kernel(x, c, ada_w, ada_b, norm1_g, norm2_g, w_in, attn_qn_g, attn_kn_g, attn_lambda,
           attn_subln_g, rwkv_mu, rwkv_w_up, rwkv_w0, rwkv_a_up, rwkv_a0, rwkv_g_up, rwkv_k_k,
           rwkv_k_a, rwkv_r_k, rwkv_lnx_g, rwkv_lnx_b, gla_alpha_up, gla_alpha_b, gla_norm_g,
           proj_attn, proj_rwkv, proj_gla, w_out, router_grp_w, router_grp_b, router_exp_w,
           router_exp_b, exp_w_gate, exp_w_up, exp_w_down):
    B, S, D = x.shape
    T = B * S
    L = ada_w.shape[0]
    tm = 256
    tm_mm = 512
    mod = _adaln(c, ada_w, ada_b).reshape(L, B, 1, 6 * D)
    x2 = x.reshape(T, D)
    xin = None
    w_in_parts = _split_w_in(w_in)
    for l in range(L):
        lambda_init = 0.8 - 0.6 * math.exp(-0.3 * l)
        p_attn, p_rwkv, p_gla, p_gate = _inproj(x2, mod[l], norm1_g[l], w_in_parts, l, S, tm_mm)
        o_a = _attention(p_attn, attn_qn_g[l], attn_kn_g[l], attn_lambda[l], attn_subln_g[l],
                         lambda_init, B, S)
        o_r = _rwkv(p_rwkv, rwkv_mu[l], rwkv_w_up[l], rwkv_w0[l], rwkv_a_up[l], rwkv_a0[l],
                    rwkv_g_up[l], rwkv_k_k[l], rwkv_k_a[l], rwkv_r_k[l], rwkv_lnx_g[l],
                    rwkv_lnx_b[l], B, S)
        o_g = _gla(p_gla, gla_alpha_up[l], gla_alpha_b[l], gla_norm_g[l], B, S)
        n_r = N_GROUPS + N_EXPERTS
        router_w = jnp.concatenate(
            [router_grp_w[l], router_exp_w[l], jnp.zeros((D, LANES - n_r), F32)], axis=1)
        rw_hi = router_w.astype(BF16)
        router_w = jnp.stack([rw_hi, (router_w - rw_hi.astype(F32)).astype(BF16)])
        router_b = jnp.concatenate(
            [router_grp_b[l], router_exp_b[l], jnp.zeros((LANES - n_r,), F32)]).reshape(1, LANES)
        x1, h2, info, cnt = _merge(
            x2, o_a, o_r, o_g, p_gate, proj_attn[l].astype(BF16), proj_rwkv[l].astype(BF16),
            proj_gla[l].astype(BF16), w_out[l].astype(BF16), mod[l], norm2_g[l],
            router_w, router_b, S, tm_mm)
        dest0, dest1, blk_exp, n_used, blk_grp, blk_nxt, n_rows = _route_tables(info, cnt)
        if xin is None:
            xin = jnp.zeros((n_rows * NSUB, LANES), U32)
        xin = _dispatch(h2, dest0, dest1, xin, tm)
        y = _moe(xin, blk_exp, n_used, blk_grp, blk_nxt, exp_w_gate, exp_w_up, exp_w_down, l)
        x2 = _combine(x1, y, info, dest0, dest1, mod[l], S, tm)
    return x2.reshape(B, S, D)
```

```python
import functools
import math

import jax
import jax.numpy as jnp
from jax import lax
from jax.experimental import pallas as pl
from jax.experimental.pallas import tpu as pltpu

F32 = jnp.float32
BF16 = jnp.bfloat16
U32 = jnp.uint32

D_MODEL = 1024
A_HEADS, A_DH, A_DV = 4, 64, 128
A_QW, A_VW = 512, 512
A_COLS = 1536
R_HEADS, R_N, R_W = 8, 64, 512
R_COLS = 1792
RWKV_GN_EPS = 64e-5
G_HEADS, G_DK, G_DV = 4, 64, 128
G_KW, G_VW, G_LORA = 256, 512, 16
G_TAU = 16.0
G_COLS = 1552
G_COLS_PAD = 1664
GATE_COLS = 3072
N_GROUPS, EXP_PER_GROUP, N_EXPERTS, TOP_K = 4, 8, 32, 2
D_EXPERT = 512
EPS = 1e-6

LANES = 128
SUBLANES = 8
NSUB = D_MODEL // (2 * LANES)
CHUNK = 64
VMEM_LIMIT = 56 * 1024 * 1024


def _dot(a, b):
    return jnp.dot(a.astype(BF16), b.astype(BF16), preferred_element_type=F32)


def _dot_nt(a, b, precision=None):
    return lax.dot_general(a, b, (((1,), (1,)), ((), ())), precision=precision,
                           preferred_element_type=F32)


def _dot_tn(a, b, precision=None):
    return lax.dot_general(a, b, (((0,), (0,)), ((), ())), precision=precision,
                           preferred_element_type=F32)


def _cumsum3(tri, x):
    x1 = x.astype(BF16)
    res = x - x1.astype(F32)
    x2 = res.astype(BF16)
    x3 = (res - x2.astype(F32)).astype(BF16)
    return (jnp.dot(tri, x1, preferred_element_type=F32)
            + jnp.dot(tri, x2, preferred_element_type=F32)
            + jnp.dot(tri, x3, preferred_element_type=F32))


def _pack_pair(a, b):
    ua = pltpu.bitcast(a.astype(BF16).astype(F32), U32)
    ub = pltpu.bitcast(b.astype(BF16).astype(F32), U32)
    return jnp.bitwise_or(ub, jnp.right_shift(ua, jnp.uint32(16)))


def _unpack_pair(w):
    a = pltpu.bitcast(jnp.left_shift(w, jnp.uint32(16)), F32)
    b = pltpu.bitcast(jnp.bitwise_and(w, jnp.uint32(0xFFFF0000)), F32)
    return a, b


def _sigmoid(x):
    return 1.0 / (1.0 + jnp.exp(-x))


def _softplus(x):
    return jnp.maximum(x, 0.0) + jnp.log(1.0 + jnp.exp(-jnp.abs(x)))


def _seg_ones(n, seg):
    r = lax.broadcasted_iota(jnp.int32, (n, n), 0) // seg
    c = lax.broadcasted_iota(jnp.int32, (n, n), 1) // seg
    return (r == c).astype(F32)


def _tri(n, strict):
    r = lax.broadcasted_iota(jnp.int32, (n, n), 0)
    c = lax.broadcasted_iota(jnp.int32, (n, n), 1)
    return (c < r) if strict else (c <= r)


def _adaln_kernel(c_ref, w_ref, b_ref, o_ref):
    c = c_ref[...]
    c_act = c * _sigmoid(c)
    B = c.shape[0]
    c_hi = c_act.astype(BF16)
    c_lo = (c_act - c_hi.astype(F32)).astype(BF16)
    w = w_ref[...]
    w_hi = w.astype(BF16)
    w_lo = (w - w_hi.astype(F32)).astype(BF16)
    both = jnp.dot(jnp.concatenate([c_hi, c_lo], axis=0), w_hi, preferred_element_type=F32)
    o_ref[...] = (both[0:B] + both[B:2 * B]
                  + jnp.dot(c_hi, w_lo, preferred_element_type=F32) + b_ref[...])


def _adaln(c, ada_w, ada_b):
    L, D, N = ada_w.shape
    B = c.shape[0]
    tn = D
    return pl.pallas_call(
        _adaln_kernel,
        grid=(L, N // tn),
        in_specs=[
            pl.BlockSpec((B, D), lambda l, j: (0, 0)),
            pl.BlockSpec((None, D, tn), lambda l, j: (l, 0, j)),
            pl.BlockSpec((None, 1, tn), lambda l, j: (l, 0, j)),
        ],
        out_specs=pl.BlockSpec((None, B, tn), lambda l, j: (l, 0, j)),
        out_shape=jax.ShapeDtypeStruct((L, B, N), F32),
        name="adaln",
    )(c, ada_w, ada_b.reshape(L, 1, N))


_IN_SEGS = (A_COLS, R_COLS, G_COLS_PAD, GATE_COLS)
_IN_DTYPES = (BF16, F32, F32, BF16)
_IN_CHUNK = 512


_C_GLA = A_COLS + R_COLS
_C_GAD = _C_GLA + 2 * G_KW + G_VW
_C_SPLIT = _C_GAD + G_LORA
_IN_PLAN = (
    (0, 0, 0, 0, A_COLS),
    (1, 0, 0, A_COLS, R_COLS),
    (2, 0, 0, _C_GLA, 2 * G_KW + G_VW),
    (2, 2 * G_KW + G_VW + LANES, 1, 0, G_VW),
    (3, 0, 1, G_VW, GATE_COLS),
)


def _inproj_kernel(x_ref, mod_ref, g_ref, wa_ref, wb_ref, *o_refs):
    x = x_ref[...]
    D = x.shape[-1]
    ms = jnp.mean(x * x, axis=-1, keepdims=True)
    y = x * lax.rsqrt(ms + EPS) * g_ref[...]
    sh = mod_ref[:, 0:D]
    sc = mod_ref[:, D:2 * D]
    h = (y * (1.0 + sc) + sh).astype(BF16)
    w_refs = (wa_ref, wb_ref)
    for out, oc, part, wc, width in _IN_PLAN:
        o_ref, w_ref = o_refs[out], w_refs[part]
        for c0 in range(0, width, _IN_CHUNK):
            n = min(_IN_CHUNK, width - c0)
            o_ref[:, oc + c0:oc + c0 + n] = jnp.dot(
                h, w_ref[:, wc + c0:wc + c0 + n], preferred_element_type=F32).astype(o_ref.dtype)
    lora = jnp.dot(h, wa_ref[:, _C_GAD:_C_SPLIT], preferred_element_type=F32)
    pad = jnp.zeros((lora.shape[0], LANES - G_LORA), F32)
    c_lora = 2 * G_KW + G_VW
    o_refs[2][:, c_lora:c_lora + LANES] = jnp.concatenate([lora, pad], axis=-1)


def _inproj(x2, mod_l, norm_g, w_parts, layer, S, tm):
    T, D = x2.shape
    wspec = lambda w: pl.BlockSpec((None, D, w.shape[2]), lambda i: (layer, 0, 0),
                                   pipeline_mode=pl.Buffered(1))
    tiles_per_batch = S // tm
    return pl.pallas_call(
        _inproj_kernel,
        grid=(T // tm,),
        in_specs=[
            pl.BlockSpec((tm, D), lambda i: (i, 0)),
            pl.BlockSpec((None, 1, 2 * D), lambda i: (i // tiles_per_batch, 0, 0)),
            pl.BlockSpec((1, D), lambda i: (0, 0)),
            wspec(w_parts[0]), wspec(w_parts[1]),
        ],
        out_specs=[pl.BlockSpec((tm, w), lambda i: (i, 0)) for w in _IN_SEGS],
        out_shape=[jax.ShapeDtypeStruct((T, w), dt) for w, dt in zip(_IN_SEGS, _IN_DTYPES)],
        compiler_params=pltpu.CompilerParams(
            dimension_semantics=("arbitrary",), vmem_limit_bytes=VMEM_LIMIT),
        name="inproj",
    )(x2, mod_l, norm_g.reshape(1, D), *w_parts)


A_TILE = 256
A_POS_SPLIT = 64
LOG2E = math.log2(math.e)


def _split_dot(x, ones):
    hi = x.astype(BF16)
    lo = (x - hi.astype(F32)).astype(BF16)
    return (jnp.dot(hi, ones, preferred_element_type=F32)
            + jnp.dot(lo, ones, preferred_element_type=F32))


def _eye(n):
    r = lax.broadcasted_iota(jnp.int32, (n, n), 0)
    c = lax.broadcasted_iota(jnp.int32, (n, n), 1)
    return (r == c).astype(BF16)


def _attn_kernel(q_ref, k_ref, v_ref, qg_ref, kg_ref, lam_ref, sg_ref, qaug_ref, kaug_ref, o_ref,
                 qt_s, ka_s, vt_s, s_s, *, S, lambda_init):
    t = A_TILE
    seg = _seg_ones(LANES, A_DH).astype(BF16)
    eye_t = _eye(t)
    eye_l = _eye(LANES)
    is_qk = lax.broadcasted_iota(jnp.int32, (t, LANES), 1) < A_DH

    blocks = [slice(b * t, (b + 1) * t) for b in range(S // t)]
    sides = ((q_ref, qg_ref, A_DH ** -0.5 * LOG2E), (k_ref, kg_ref, 1.0))
    xf = [[ref[rows, :].astype(F32) for rows in blocks] for ref, _, _ in sides]
    ms = [[jnp.dot((x * x).astype(BF16), seg, preferred_element_type=F32) for x in xs]
          for xs in xf]
    qn, kn = [[x * lax.rsqrt(m * (1.0 / A_DH) + EPS) * g_ref[...] * scale
               for x, m in zip(xs, mss)] for xs, mss, (_, g_ref, scale) in zip(xf, ms, sides)]
    for b, rows in enumerate(blocks):
        vt_s[:, rows] = _dot_nt(eye_l, v_ref[rows, :]).astype(BF16)
        for c in range(2):
            kc = kn[b] if c == 0 else pltpu.roll(kn[b], A_DH, 1)
            ka_s[c, rows, :] = jnp.where(is_qk, kc.astype(BF16), kaug_ref[rows, :])
    qas = [[jnp.where(is_qk, (qn[b] if c == 0 else pltpu.roll(qn[b], A_DH, 1)).astype(BF16),
                      qaug_ref[rows, :]) for c in range(2)] for b, rows in enumerate(blocks)]
    qts = [[_dot_nt(eye_l, qa) for qa in pair] for pair in qas]
    for b, rows in enumerate(blocks):
        for c in range(2):
            qt_s[c, :, rows] = qts[b][c].astype(BF16)

    lv = lam_ref[...]
    lam = (jnp.exp(jnp.sum(lv[0:1] * lv[1:2], axis=-1, keepdims=True))
           - jnp.exp(jnp.sum(lv[2:3] * lv[3:4], axis=-1, keepdims=True)) + lambda_init)
    causal = (lax.broadcasted_iota(jnp.int32, (t, t), 0)
              <= lax.broadcasted_iota(jnp.int32, (t, t), 1))

    nt = S // t

    def score_block(i, c, j, st):
        s = jnp.dot(ka_s[c, j * t:(j + 1) * t, :], qt_s[c, :, i * t:(i + 1) * t],
                    preferred_element_type=F32)
        if j == i:
            s = jnp.where(causal, s, -jnp.inf)
        s_s[i % 2, c, j * t:(j + 1) * t, :] = s
        mj = jnp.max(s, axis=0, keepdims=True)
        st["m"][c] = mj if st["m"][c] is None else jnp.maximum(st["m"][c], mj)

    def value_block(i, c, j, st):
        p = jnp.exp2(s_s[i % 2, c, j * t:(j + 1) * t, :] - st["m"][c])
        st["l"][c] = st["l"][c] + jnp.sum(p, axis=0, keepdims=True)
        st["acc"][c] = st["acc"][c] + jnp.dot(vt_s[:, j * t:(j + 1) * t], p.astype(BF16),
                                              preferred_element_type=F32)

    def finish(i, st):
        o = st["acc"][0] / st["l"][0] - lam * (st["acc"][1] / st["l"][1])
        ms = jnp.mean(o * o, axis=0, keepdims=True)
        o = o * lax.rsqrt(ms + EPS) * sg_ref[...] * (1.0 - lambda_init)
        o_ref[i * t:(i + 1) * t, :] = _dot_nt(eye_t, o.astype(BF16)).astype(o_ref.dtype)

    prev = None
    for i in range(nt + 1):
        cur = None
        first = []
        if i < nt:
            cur = dict(m=[None, None], l=[jnp.zeros((1, t), F32)] * 2,
                       acc=[jnp.zeros((A_DV, t), F32)] * 2)
            first = [(c, j) for j in range(i + 1) for c in range(2)]
        second = [(c, j) for j in range(i) for c in range(2)] if prev is not None else []
        for n in range(max(len(first), len(second))):
            if n < len(first):
                score_block(i, first[n][0], first[n][1], cur)
            if n < len(second):
                value_block(i - 1, second[n][0], second[n][1], prev)
        if prev is not None:
            finish(i - 1, prev)
        prev = cur


def _alibi_columns(S):
    pos = jnp.arange(S, dtype=jnp.int32)[:, None]
    hi = (pos // A_POS_SPLIT).astype(F32)
    lo = (pos % A_POS_SPLIT).astype(F32)
    kl = jnp.arange(LANES, dtype=jnp.int32)[None, :] - A_DH
    term = kl // 2
    used = jnp.logical_and(kl >= 0, kl < 8)
    k_aug = jnp.where(used, jnp.where(term == 0, hi, jnp.where(term == 1, lo, 1.0)), 0.0)
    q_augs = []
    for h in range(A_HEADS):
        s2 = 2.0 ** (-8.0 * (h + 1) / A_HEADS) * LOG2E
        qv = jnp.where(term == 0, A_POS_SPLIT * s2,
             jnp.where(term == 1, s2,
             jnp.where(term == 2, -A_POS_SPLIT * s2 * hi, -s2 * lo)))
        qv_hi = qv.astype(BF16).astype(F32)
        q_augs.append(jnp.where(used, jnp.where(kl % 2 == 0, qv_hi, qv - qv_hi), 0.0))
    return jnp.stack(q_augs).astype(BF16), k_aug.astype(BF16)


def _attention(p_attn, qn_g, kn_g, lam_vecs, subln_g, lambda_init, B, S):
    pa = p_attn.reshape(B, S, A_COLS)
    dup = lambda g: jnp.concatenate([g, g]).reshape(1, LANES)
    q_aug, k_aug = _alibi_columns(S)
    nqb = A_QW // LANES
    kern = functools.partial(_attn_kernel, S=S, lambda_init=lambda_init)
    out = pl.pallas_call(
        kern,
        grid=(B, A_HEADS),
        in_specs=[
            pl.BlockSpec((None, S, LANES), lambda b, h: (b, 0, h)),
            pl.BlockSpec((None, S, LANES), lambda b, h: (b, 0, nqb + h)),
            pl.BlockSpec((None, S, LANES), lambda b, h: (b, 0, 2 * nqb + h)),
            pl.BlockSpec((1, LANES), lambda b, h: (0, 0)),
            pl.BlockSpec((1, LANES), lambda b, h: (0, 0)),
            pl.BlockSpec((4, A_DH), lambda b, h: (0, 0)),
            pl.BlockSpec((A_DV, 1), lambda b, h: (0, 0)),
            pl.BlockSpec((None, S, LANES), lambda b, h: (h, 0, 0)),
            pl.BlockSpec((S, LANES), lambda b, h: (0, 0)),
        ],
        out_specs=pl.BlockSpec((None, S, A_DV), lambda b, h: (b, 0, h)),
        out_shape=jax.ShapeDtypeStruct((B, S, A_VW), BF16),
        scratch_shapes=[pltpu.VMEM((2, LANES, S), BF16), pltpu.VMEM((2, S, LANES), BF16),
                        pltpu.VMEM((A_DV, S), BF16), pltpu.VMEM((2, 2, S, A_TILE), F32)],
        compiler_params=pltpu.CompilerParams(
            dimension_semantics=("arbitrary", "arbitrary"), vmem_limit_bytes=VMEM_LIMIT),
        name="diff_attn",
    )(pa, pa, pa, dup(qn_g), dup(kn_g), lam_vecs, subln_g.reshape(A_DV, 1), q_aug, k_aug)
    return out.reshape(B * S, A_VW)


R_GROUP = 4
R_GW = R_GROUP * R_N


def _rwkv_kernel(p_ref, mu_ref, wup_ref, w0_ref, aup_ref, a0_ref, gup_ref, kk_ref, ka_ref,
                 rk_ref, lg_ref, lb_ref, o_ref,
                 carry_ref, st_ref, al_s, be_s, ka_s, rh_s, bt_s, kt_s, v_s, gc_s, y_s, *, TB):
    @pl.when(pl.program_id(1) == 0)
    def _():
        carry_ref[...] = jnp.zeros_like(carry_ref)
        st_ref[...] = jnp.zeros_like(st_ref)

    xs = p_ref[...]
    prev = pltpu.roll(xs, 1, 0)
    row = lax.broadcasted_iota(jnp.int32, (TB, 1), 0)
    prev = jnp.where(row == 0, carry_ref[...], prev)
    carry_ref[...] = xs[TB - 1:TB, :]
    xm = xs + (prev - xs) * mu_ref[...]
    r = xm[:, 0:R_W]
    k = xm[:, R_W:2 * R_W]
    v = xm[:, 2 * R_W:3 * R_W]
    wa = xm[:, 3 * R_W:3 * R_W + LANES]
    gd = xm[:, 3 * R_W + LANES:3 * R_W + 2 * LANES]
    wz = w0_ref[...] + _dot(jnp.tanh(wa), wup_ref[...])
    lw = -math.exp(-0.5) * _sigmoid(wz)
    a = _sigmoid(a0_ref[...] + _dot(wa, aup_ref[...]))
    g = _dot(_sigmoid(gd), gup_ref[...])
    seg = _seg_ones(R_GW, R_N).astype(BF16)

    def head_sum(x, passes):
        f = _split_dot if passes == 2 else (
            lambda t, o: jnp.dot(t.astype(BF16), o, preferred_element_type=F32))
        return jnp.concatenate(
            [f(x[:, j * R_GW:(j + 1) * R_GW], seg) for j in range(R_W // R_GW)], axis=-1)

    kk = k * kk_ref[...]
    kk = kk * jnp.minimum(lax.rsqrt(head_sum(kk * kk, 1)), 1e12)
    k2 = k * (1.0 + (a - 1.0) * ka_ref[...])
    bonus = head_sum(r * k2 * rk_ref[...], 2) * v
    bv = kk * a

    rr = lax.broadcasted_iota(jnp.int32, (TB, TB), 0)
    cc = lax.broadcasted_iota(jnp.int32, (TB, TB), 1)
    tril_blk = jnp.logical_and(rr // CHUNK == cc // CHUNK, cc <= rr).astype(BF16)
    Lg = _cumsum3(tril_blk, lw)
    inv = jnp.exp(-Lg)
    al_s[...] = (jnp.exp(Lg - lw) * kk).astype(BF16)
    be_s[...] = (bv * inv).astype(BF16)
    ka_s[...] = (k2 * inv).astype(BF16)
    rh_s[...] = (jnp.exp(Lg) * r).astype(BF16)
    v_s[...] = v.astype(BF16)
    for c in range(TB // CHUNK):
        rows = slice(c * CHUNK, (c + 1) * CHUNK)
        gC = Lg[(c + 1) * CHUNK - 1:(c + 1) * CHUNK, :]
        tail = jnp.exp(gC - Lg[rows, :])
        bt_s[rows, :] = (bv[rows, :] * tail).astype(BF16)
        kt_s[rows, :] = (k2[rows, :] * tail).astype(BF16)
        gc_s[c * SUBLANES:(c + 1) * SUBLANES, :] = jnp.broadcast_to(jnp.exp(gC), (SUBLANES, R_W))

    ri = lax.broadcasted_iota(jnp.int32, (R_GW, R_GW), 0)
    ci = lax.broadcasted_iota(jnp.int32, (R_GW, R_GW), 1)
    blk = ri // R_N == ci // R_N
    strict_t = ri % R_N < ci % R_N
    strict_st = (lax.broadcasted_iota(jnp.int32, (R_N, R_GW), 1) % R_N
                 < lax.broadcasted_iota(jnp.int32, (R_N, R_GW), 0))
    incl = ci % R_N <= ri % R_N
    zero = jnp.zeros((), BF16)

    def expand(x):
        return jnp.where(blk, jnp.concatenate([x] * R_GROUP, axis=0), zero)

    n_groups = R_HEADS // R_GROUP
    pair = 4

    def chunk_pair(ip, _):
        chains = []
        for dc in range(pair):
            c = ip * pair + dc
            rows = pl.ds(pl.multiple_of(c * CHUNK, CHUNK), CHUNK)
            for gi in range(n_groups):
                chains.append(dict(c=c, rows=rows, gi=gi, cols=slice(gi * R_GW, (gi + 1) * R_GW)))
        for ch in chains:
            rows, cols = ch["rows"], ch["cols"]
            ch["al"] = al_s[rows, cols]
            ch["A"] = expand(ch["al"])
            ch["R"] = expand(rh_s[rows, cols])
            ch["B"] = expand(be_s[rows, cols])
            ch["K"] = expand(ka_s[rows, cols])
            vc = v_s[rows, cols]
            ch["V"] = jnp.concatenate(
                [vc[:, h * R_N:(h + 1) * R_N] for h in range(R_GROUP)], axis=0)
        for ch in chains:
            ch["P"] = -jnp.where(strict_st, _dot_nt(ch["al"], ch["B"]), 0.0)
            ch["m_ak_t"] = jnp.where(strict_t, _dot_nt(ch["K"], ch["A"]), 0.0).astype(BF16)
            ch["m_rb"] = jnp.where(incl, _dot_nt(ch["R"], ch["B"]), 0.0).astype(BF16)
            ch["m_rk"] = jnp.where(incl, _dot_nt(ch["R"], ch["K"]), 0.0).astype(BF16)
        for ch in chains:
            ch["Z"] = ch["A"].astype(F32)
            ch["Wt"] = _dot_tn(ch["V"], ch["m_ak_t"])
        n = 1
        while True:
            last = 2 * n >= CHUNK
            for ch in chains:
                Pb = ch["P"].astype(BF16)
                Xb = expand(Pb)
                ch["Z"] = ch["Z"] + jnp.dot(Xb, ch["Z"].astype(BF16), preferred_element_type=F32)
                ch["Wt"] = ch["Wt"] + _dot_nt(ch["Wt"].astype(BF16), Xb)
                if not last:
                    ch["P"] = jnp.dot(Pb, Xb, preferred_element_type=F32)
            n *= 2
            if last:
                break
        for ch in chains:
            rows, cols = ch["rows"], ch["cols"]
            Bt = expand(bt_s[rows, cols])
            Kt = expand(kt_s[rows, cols])
            Zb = ch["Z"].astype(BF16)
            Wtb = ch["Wt"].astype(BF16)
            ch["y_a"] = (ch["R"].astype(F32)
                         - jnp.dot(ch["m_rb"], Zb, preferred_element_type=F32)).astype(BF16)
            ch["y_b"] = (jnp.dot(ch["m_rk"], ch["V"], preferred_element_type=F32)
                         - _dot_nt(ch["m_rb"], Wtb))
            ch["p_neg"] = _dot_tn(Zb, Bt).astype(BF16)
            ch["q"] = _dot_tn(ch["V"], Kt) - jnp.dot(Wtb, Bt, preferred_element_type=F32)
        for ch in chains:
            rows, cols, gi = ch["rows"], ch["cols"], ch["gi"]
            g0 = pl.multiple_of(ch["c"] * SUBLANES, SUBLANES)
            S0 = st_ref[gi]
            S0b = S0.astype(BF16)
            y = _dot_nt(ch["y_a"], S0b) + ch["y_b"]
            st_ref[gi] = (S0 * gc_s[pl.ds(g0, SUBLANES), cols][0:1, :]
                          - jnp.dot(S0b, ch["p_neg"], preferred_element_type=F32) + ch["q"])
            for h in range(R_GROUP):
                hh = gi * R_GROUP + h
                y_s[rows, hh * R_N:(hh + 1) * R_N] = y[h * R_N:(h + 1) * R_N, :]
        return 0

    lax.fori_loop(0, TB // CHUNK // pair, chunk_pair, 0)

    y = y_s[...]
    mean = head_sum(y, 2) * (1.0 / R_N)
    yc = y - mean
    var = head_sum(yc * yc, 1) * (1.0 / R_N)
    yn = yc * lax.rsqrt(var + RWKV_GN_EPS) * lg_ref[...] + lb_ref[...]
    o_ref[...] = ((yn + bonus) * g).astype(o_ref.dtype)


def _rwkv(p_rwkv, mu, w_up, w0, a_up, a0, g_up, k_k, k_a, r_k, lnx_g, lnx_b, B, S):
    TB = 256
    T = B * S
    nt = S // TB
    row = lambda t: t.reshape(1, -1)
    zeros = jnp.zeros((R_N, R_W), F32)
    wup_pad = jnp.concatenate([w_up, zeros], axis=0)
    aup_pad = jnp.concatenate([zeros, a_up], axis=0)
    vec = lambda n: pl.BlockSpec((1, n), lambda b, i: (0, 0))
    mat = lambda m, n: pl.BlockSpec((m, n), lambda b, i: (0, 0))
    kern = functools.partial(_rwkv_kernel, TB=TB)
    return pl.pallas_call(
        kern,
        grid=(B, nt),
        in_specs=[
            pl.BlockSpec((TB, R_COLS), lambda b, i: (b * nt + i, 0)),
            vec(R_COLS), mat(LANES, R_W), vec(R_W), mat(LANES, R_W), vec(R_W), mat(LANES, R_W),
            vec(R_W), vec(R_W), vec(R_W), vec(R_W), vec(R_W),
        ],
        out_specs=pl.BlockSpec((TB, R_W), lambda b, i: (b * nt + i, 0)),
        out_shape=jax.ShapeDtypeStruct((T, R_W), BF16),
        scratch_shapes=[
            pltpu.VMEM((1, R_COLS), F32),
            pltpu.VMEM((R_HEADS // R_GROUP, R_N, R_GW), F32),
        ] + [pltpu.VMEM((TB, R_W), BF16)] * 7 + [
            pltpu.VMEM((TB // CHUNK * SUBLANES, R_W), F32),
            pltpu.VMEM((TB, R_W), F32),
        ],
        compiler_params=pltpu.CompilerParams(
            dimension_semantics=("arbitrary", "arbitrary"), vmem_limit_bytes=VMEM_LIMIT),
        name="rwkv7",
    )(p_rwkv, row(mu), wup_pad, row(w0), aup_pad, row(a0), g_up, row(k_k), row(k_a),
      row(r_k), row(lnx_g), row(lnx_b))


G_SUBTILE = 256


def _gla_kernel(p_ref, aup_ref, ab_ref, ng_ref, o_ref, st_ref, *, TB):
    @pl.when(pl.program_id(1) == 0)
    def _():
        st_ref[...] = jnp.zeros_like(st_ref)

    c_gv = 2 * G_KW
    c_ad = c_gv + G_VW
    c_gate = c_ad + LANES
    sub = G_SUBTILE
    subs = range(TB // sub)
    nchunk = sub // CHUNK
    rr = lax.broadcasted_iota(jnp.int32, (sub, sub), 0)
    cc = lax.broadcasted_iota(jnp.int32, (sub, sub), 1)
    causal = jnp.logical_and(rr // CHUNK == cc // CHUNK, cc <= rr)
    causal_b = causal.astype(BF16)
    heads = range(G_HEADS)
    sls = [slice(h * G_DK, (h + 1) * G_DK) for h in heads]
    vss = [slice(h * G_DV, (h + 1) * G_DV) for h in heads]

    rws = [slice(u * sub, (u + 1) * sub) for u in subs]
    zs = [_dot(p_ref[r, c_ad:c_ad + LANES], aup_ref[...]) + ab_ref[...] for r in rws]
    las = [-_softplus(-z) * (1.0 / G_TAU) for z in zs]
    bs = [_cumsum3(causal_b, la) for la in las]
    ks = [p_ref[r, G_KW:2 * G_KW] for r in rws]
    vbs = [p_ref[r, c_gv:c_gv + G_VW].astype(BF16) for r in rws]
    qes = [(p_ref[r, 0:G_KW] * (G_DK ** -0.5) * jnp.exp(b)).astype(BF16) for r, b in zip(rws, bs)]
    kes = [(k * jnp.exp(-b)).astype(BF16) for k, b in zip(ks, bs)]
    scs = [[jnp.where(causal, _dot_nt(qes[u][:, sls[h]], kes[u][:, sls[h]]), 0.0) for h in heads]
           for u in subs]
    o_intra = [[_dot(scs[u][h], vbs[u][:, vss[h]]) for h in heads] for u in subs]
    kts, e_lasts, kvs = {}, {}, {}
    for u in subs:
        for c in range(nchunk):
            rows = slice(c * CHUNK, (c + 1) * CHUNK)
            b_last = bs[u][(c + 1) * CHUNK - 1:(c + 1) * CHUNK, :]
            kt = (ks[u][rows, :] * jnp.exp(b_last - bs[u][rows, :])).astype(BF16)
            e_lasts[u, c] = jnp.exp(b_last)
            for h in heads:
                kvs[u, c, h] = _dot_tn(vbs[u][rows, vss[h]], kt[:, sls[h]])
    states = [st_ref[h] for h in heads]
    parts = [[] for _ in heads]
    for u in subs:
        for c in range(nchunk):
            rows = slice(c * CHUNK, (c + 1) * CHUNK)
            for h in heads:
                parts[h].append(o_intra[u][h][rows, :]
                                + _dot_nt(qes[u][rows, sls[h]], states[h].astype(BF16)))
                states[h] = states[h] * e_lasts[u, c][:, sls[h]] + kvs[u, c, h]
    for h in heads:
        vs = vss[h]
        st_ref[h] = states[h]
        o = jnp.concatenate(parts[h], axis=0)
        ms = jnp.mean(o * o, axis=-1, keepdims=True)
        gt = p_ref[:, c_gate + h * G_DV:c_gate + (h + 1) * G_DV]
        o = o * lax.rsqrt(ms + EPS) * ng_ref[...] * (gt * _sigmoid(gt))
        o_ref[:, vs] = o.astype(o_ref.dtype)


def _gla(p_gla, alpha_up, alpha_b, norm_g, B, S):
    TB = 512
    T = B * S
    nt = S // TB
    aup_pad = jnp.concatenate([alpha_up, jnp.zeros((LANES - G_LORA, G_KW), F32)], axis=0)
    kern = functools.partial(_gla_kernel, TB=TB)
    return pl.pallas_call(
        kern,
        grid=(B, nt),
        in_specs=[
            pl.BlockSpec((TB, G_COLS_PAD), lambda b, i: (b * nt + i, 0)),
            pl.BlockSpec((LANES, G_KW), lambda b, i: (0, 0)),
            pl.BlockSpec((1, G_KW), lambda b, i: (0, 0)),
            pl.BlockSpec((1, G_DV), lambda b, i: (0, 0)),
        ],
        out_specs=pl.BlockSpec((TB, G_VW), lambda b, i: (b * nt + i, 0)),
        out_shape=jax.ShapeDtypeStruct((T, G_VW), BF16),
        scratch_shapes=[pltpu.VMEM((G_HEADS, G_DV, G_DK), F32)],
        compiler_params=pltpu.CompilerParams(
            dimension_semantics=("arbitrary", "arbitrary"), vmem_limit_bytes=VMEM_LIMIT),
        name="gla",
    )(p_gla, aup_pad, alpha_b.reshape(1, G_KW), norm_g.reshape(1, G_DV))


def _merge_kernel(x_ref, oa_ref, or_ref, og_ref, gate_ref, pa_ref, pr_ref, pg_ref, wo_ref,
                  mod_ref, g2_ref, rw_ref, rb_ref, x1_ref, h2_ref, info_ref, cnt_ref, carry_ref,
                  lg_s):
    D = D_MODEL
    i = pl.program_id(0)

    @pl.when(i == 0)
    def _():
        carry_ref[...] = jnp.zeros_like(carry_ref)
        lg_s[...] = jnp.zeros_like(lg_s)

    merged = (_sigmoid(gate_ref[:, 0:D].astype(F32))
              * jnp.dot(oa_ref[...], pa_ref[...], preferred_element_type=F32)
              + _sigmoid(gate_ref[:, D:2 * D].astype(F32))
              * jnp.dot(or_ref[...], pr_ref[...], preferred_element_type=F32)
              + _sigmoid(gate_ref[:, 2 * D:3 * D].astype(F32))
              * jnp.dot(og_ref[...], pg_ref[...], preferred_element_type=F32))
    gt1 = mod_ref[:, 2 * D:3 * D]
    sh2 = mod_ref[:, 3 * D:4 * D]
    sc2 = mod_ref[:, 4 * D:5 * D]
    x1 = x_ref[...] + gt1 * jnp.dot(merged.astype(BF16), wo_ref[...], preferred_element_type=F32)
    x1_ref[...] = x1
    info_ref[...], cnt_ref[...] = _route_tile(lg_s[...], carry_ref, jnp.where(i > 0, 1.0, 0.0))
    ms = jnp.mean(x1 * x1, axis=-1, keepdims=True)
    h2 = x1 * lax.rsqrt(ms + EPS) * g2_ref[...] * (1.0 + sc2) + sh2
    for s in range(NSUB):
        h2_ref[pl.ds(s, h2.shape[0], stride=NSUB), :] = _pack_pair(
            h2[:, 2 * s * LANES:(2 * s + 1) * LANES], h2[:, (2 * s + 1) * LANES:(2 * s + 2) * LANES])
    h_hi = h2.astype(BF16)
    h_lo = (h2 - h_hi.astype(F32)).astype(BF16)
    lg_s[...] = (jnp.dot(h_hi, rw_ref[0], preferred_element_type=F32)
                 + jnp.dot(h_lo, rw_ref[0], preferred_element_type=F32)
                 + jnp.dot(h_hi, rw_ref[1], preferred_element_type=F32) + rb_ref[...])


def _merge(x2, o_a, o_r, o_g, p_gate, proj_a, proj_r, proj_g, w_out, mod_l, norm2_g,
           router_w, router_b, S, tm):
    T, D = x2.shape
    tiles_per_batch = S // tm
    n = T // tm
    cur = lambda i: jnp.minimum(i, n - 1)
    tile = lambda w: pl.BlockSpec((tm, w), lambda i: (cur(i), 0))
    const = lambda m, n: pl.BlockSpec((m, n), lambda i: (0, 0))
    return pl.pallas_call(
        _merge_kernel,
        grid=(n + 1,),
        in_specs=[
            tile(D), tile(A_VW), tile(R_W), tile(G_VW), tile(GATE_COLS),
            const(A_VW, D), const(R_W, D), const(G_VW, D), const(D, D),
            pl.BlockSpec((None, 1, 6 * D), lambda i: (cur(i) // tiles_per_batch, 0, 0)),
            const(1, D), pl.BlockSpec((2, D, LANES), lambda i: (0, 0, 0)), const(1, LANES),
        ],
        out_specs=[
            tile(D),
            pl.BlockSpec((tm * NSUB, LANES), lambda i: (cur(i), 0)),
            pl.BlockSpec((tm, LANES), lambda i: (jnp.maximum(i - 1, 0), 0)),
            const(1, LANES),
        ],
        out_shape=[
            jax.ShapeDtypeStruct((T, D), F32),
            jax.ShapeDtypeStruct((T * NSUB, LANES), U32),
            jax.ShapeDtypeStruct((T, LANES), F32),
            jax.ShapeDtypeStruct((1, LANES), F32),
        ],
        scratch_shapes=[pltpu.VMEM((1, LANES), F32), pltpu.VMEM((tm, LANES), F32)],
        compiler_params=pltpu.CompilerParams(
            dimension_semantics=("arbitrary",), vmem_limit_bytes=VMEM_LIMIT),
        name="merge",
    )(x2, o_a, o_r, o_g, p_gate, proj_a, proj_r, proj_g, w_out, mod_l,
      norm2_g.reshape(1, D), router_w, router_b)


MOE_ROWS = 256
E_LANE0 = N_GROUPS


def _route_tile(lg, carry_ref, live):
    n = lg.shape[0]
    lane = lax.broadcasted_iota(jnp.int32, (n, LANES), 1).astype(F32)
    neg = -jnp.inf
    big = float(LANES)

    def first_max(vals):
        m = jnp.max(vals, axis=-1, keepdims=True)
        idx = jnp.min(jnp.where(vals == m, lane, big), axis=-1, keepdims=True)
        return m, idx

    in_grp = lane < N_GROUPS
    gm, grp = first_max(jnp.where(in_grp, lg, neg))
    g_prob = 1.0 / jnp.sum(jnp.where(in_grp, jnp.exp(lg - gm), 0.0), axis=-1, keepdims=True)
    lo = E_LANE0 + grp * EXP_PER_GROUP
    el = jnp.where(jnp.logical_and(lane >= lo, lane < lo + EXP_PER_GROUP), lg, neg)
    v1, i1 = first_max(el)
    v2, i2 = first_max(jnp.where(lane == i1, neg, el))
    e21 = jnp.exp(v2 - v1)
    w0 = g_prob / (1.0 + e21)
    w1 = g_prob * e21 / (1.0 + e21)
    oh0 = lane == i1
    oh1 = lane == i2
    oh = jnp.logical_or(oh0, oh1).astype(F32) * live
    before = _tri(n, True).astype(BF16)
    cnt = jnp.dot(before, oh.astype(BF16), preferred_element_type=F32) + carry_ref[...]
    rank0 = jnp.sum(jnp.where(oh0, cnt, 0.0), axis=-1, keepdims=True)
    rank1 = jnp.sum(jnp.where(oh1, cnt, 0.0), axis=-1, keepdims=True)
    carry = carry_ref[...] + jnp.sum(oh, axis=0, keepdims=True)
    carry_ref[...] = carry
    cols = (i1 - E_LANE0, i2 - E_LANE0, rank0, rank1, w0, w1)
    info = jnp.zeros((n, LANES), F32)
    for j, col in enumerate(cols):
        info = jnp.where(lane == j, col, info)
    return info, carry


def _route_tables(info, cnt):
    T = info.shape[0]
    A = T * TOP_K
    counts = cnt[0, E_LANE0:E_LANE0 + N_EXPERTS].astype(jnp.int32)
    padded = (counts + MOE_ROWS - 1) // MOE_ROWS * MOE_ROWS
    pad_end = jnp.cumsum(padded)
    pad_start = pad_end - padded
    n_blocks = -(-A // MOE_ROWS) + N_EXPERTS
    eid = info[:, 0:2].astype(jnp.int32)
    is_e = eid[:, :, None] == jnp.arange(N_EXPERTS, dtype=jnp.int32)
    dest = (jnp.sum(jnp.where(is_e, pad_start, 0), axis=-1)
            + info[:, 2:4].astype(jnp.int32))
    blk_start = jnp.arange(n_blocks, dtype=jnp.int32) * MOE_ROWS
    blk_exp = jnp.minimum(jnp.sum(pad_end[None, :] <= blk_start[:, None], axis=1),
                          N_EXPERTS - 1).astype(jnp.int32)
    n_used = (pad_end[-1:] // MOE_ROWS).astype(jnp.int32)
    nonempty = counts > 0
    ids = jnp.arange(N_EXPERTS, dtype=jnp.int32)
    grp_of = jnp.cumsum(nonempty.astype(jnp.int32)) - 1
    later = jnp.logical_and(ids[None, :] > ids[:, None], nonempty[None, :])
    nxt_of = jnp.min(jnp.where(later, ids[None, :], N_EXPERTS), axis=1)
    nxt_of = jnp.where(nxt_of == N_EXPERTS, -1, nxt_of)
    is_b = blk_exp[:, None] == ids[None, :]
    blk_grp = jnp.sum(jnp.where(is_b, grp_of[None, :], 0), axis=1).astype(jnp.int32)
    blk_nxt = jnp.sum(jnp.where(is_b, nxt_of[None, :], 0), axis=1).astype(jnp.int32)
    return dest[:, 0], dest[:, 1], blk_exp, n_used, blk_grp, blk_nxt, n_blocks * MOE_ROWS


def _dispatch_kernel(d0_ref, d1_ref, h2_ref, xin_in, xin_hbm, sem):
    del xin_in
    n = h2_ref.shape[0] // NSUB
    base = pl.program_id(0) * n

    def slab(ref, row):
        return ref.at[pl.ds(pl.multiple_of(row * NSUB, NSUB), NSUB)]

    def body(r, _):
        pltpu.make_async_copy(slab(h2_ref, r), slab(xin_hbm, d0_ref[base + r]), sem).start(0)
        pltpu.make_async_copy(slab(h2_ref, r), slab(xin_hbm, d1_ref[base + r]), sem).start(1)
        return 0

    lax.fori_loop(0, n, body, 0, unroll=8)
    for _ in range(TOP_K):
        pltpu.make_async_copy(h2_ref, xin_hbm.at[pl.ds(0, n * NSUB)], sem).wait()


def _dispatch(h2_slab, dest0, dest1, xin_init, td):
    T = h2_slab.shape[0] // NSUB
    grid_spec = pltpu.PrefetchScalarGridSpec(
        num_scalar_prefetch=2,
        grid=(T // td,),
        in_specs=[pl.BlockSpec((td * NSUB, LANES), lambda i, d0, d1: (i, 0)),
                  pl.BlockSpec(memory_space=pl.ANY)],
        out_specs=pl.BlockSpec(memory_space=pl.ANY),
        scratch_shapes=[pltpu.SemaphoreType.DMA(())],
    )
    return pl.pallas_call(
        _dispatch_kernel,
        grid_spec=grid_spec,
        out_shape=jax.ShapeDtypeStruct(xin_init.shape, xin_init.dtype),
        input_output_aliases={3: 0},
        compiler_params=pltpu.CompilerParams(dimension_semantics=("arbitrary",)),
        name="moe_dispatch",
    )(dest0, dest1, h2_slab, xin_init)


def _moe_kernel(be_ref, nu_ref, grp_ref, nxt_ref, x_ref, wg_hbm, wu_hbm, wd_hbm, y_ref,
                wgf, wuf, wdf, wgb, wub, wdb, sem, *, layer):
    i = pl.program_id(0)

    def weight_copies(expert, slot):
        return [pltpu.make_async_copy(src.at[layer, expert], dst.at[slot], sem.at[slot, k])
                for k, (src, dst) in enumerate(((wg_hbm, wgf), (wu_hbm, wuf), (wd_hbm, wdf)))]

    @pl.when(i < nu_ref[0])
    def _():
        changed = jnp.logical_or(i == 0, be_ref[i] != be_ref[jnp.maximum(i - 1, 0)])

        @pl.when(changed)
        def _():
            slot = grp_ref[i] % 2

            @pl.when(i == 0)
            def _():
                for cp in weight_copies(be_ref[0], 0):
                    cp.start()

            for cp in weight_copies(be_ref[i], slot):
                cp.wait()
            wgb[...] = wgf[slot].astype(BF16)
            wub[...] = wuf[slot].astype(BF16)
            wdb[...] = wdf[slot].astype(BF16)

            @pl.when(nxt_ref[i] >= 0)
            def _():
                for cp in weight_copies(nxt_ref[i], 1 - slot):
                    cp.start()

        kw = 2 * LANES
        hg = hu = None
        for j in range(NSUB):
            xa, xb = _unpack_pair(x_ref[pl.ds(j, MOE_ROWS, stride=NSUB), :])
            xj = jnp.concatenate([xa.astype(BF16), xb.astype(BF16)], axis=-1)
            dg = jnp.dot(xj, wgb[j * kw:(j + 1) * kw, :], preferred_element_type=F32)
            du = jnp.dot(xj, wub[j * kw:(j + 1) * kw, :], preferred_element_type=F32)
            hg = dg if hg is None else hg + dg
            hu = du if hu is None else hu + du
        hid = (hg * _sigmoid(hg) * hu).astype(BF16)
        for j in range(NSUB):
            yj = jnp.dot(hid, wdb[:, j * kw:(j + 1) * kw], preferred_element_type=F32)
            y_ref[pl.ds(j, MOE_ROWS, stride=NSUB), :] = _pack_pair(yj[:, :LANES], yj[:, LANES:])

    @pl.when(i >= nu_ref[0])
    def _():
        y_ref[...] = jnp.zeros_like(y_ref)


def _moe(xin, blk_exp, n_used, blk_grp, blk_nxt, w_gate, w_up, w_down, layer):
    blk_rows = MOE_ROWS * NSUB
    n_blocks = xin.shape[0] // blk_rows
    last = lambda i, nu: jnp.minimum(i, nu[0] - 1)
    grid_spec = pltpu.PrefetchScalarGridSpec(
        num_scalar_prefetch=4,
        grid=(n_blocks,),
        in_specs=[
            pl.BlockSpec((blk_rows, LANES), lambda i, be, nu, gr, nx: (last(i, nu), 0)),
            pl.BlockSpec(memory_space=pl.ANY), pl.BlockSpec(memory_space=pl.ANY),
            pl.BlockSpec(memory_space=pl.ANY),
        ],
        out_specs=pl.BlockSpec((blk_rows, LANES), lambda i, be, nu, gr, nx: (i, 0)),
        scratch_shapes=[
            pltpu.VMEM((2, D_MODEL, D_EXPERT), F32),
            pltpu.VMEM((2, D_MODEL, D_EXPERT), F32),
            pltpu.VMEM((2, D_EXPERT, D_MODEL), F32),
            pltpu.VMEM((D_MODEL, D_EXPERT), BF16),
            pltpu.VMEM((D_MODEL, D_EXPERT), BF16),
            pltpu.VMEM((D_EXPERT, D_MODEL), BF16),
            pltpu.SemaphoreType.DMA((2, 3)),
        ],
    )
    return pl.pallas_call(
        functools.partial(_moe_kernel, layer=layer),
        grid_spec=grid_spec,
        out_shape=jax.ShapeDtypeStruct(xin.shape, xin.dtype),
        compiler_params=pltpu.CompilerParams(
            dimension_semantics=("arbitrary",), vmem_limit_bytes=VMEM_LIMIT),
        name="moe_ffn",
    )(blk_exp, n_used, blk_grp, blk_nxt, xin, w_gate, w_up, w_down)


def _combine_kernel(d0_ref, d1_ref, y_hbm, x1_ref, info_ref, mod_ref, o_ref, ybuf, sem):
    i = pl.program_id(0)
    nsteps = pl.num_programs(0)
    n = x1_ref.shape[0]
    slot = i % 2
    D = D_MODEL

    def slab(ref, row):
        return ref.at[pl.ds(pl.multiple_of(row * NSUB, NSUB), NSUB)]

    def start_gather(step, sl):
        base = step * n

        def body(r, _):
            pltpu.make_async_copy(
                slab(y_hbm, d0_ref[base + r]), slab(ybuf.at[sl, 0], r), sem.at[sl]).start(0)
            pltpu.make_async_copy(
                slab(y_hbm, d1_ref[base + r]), slab(ybuf.at[sl, 1], r), sem.at[sl]).start(1)
            return 0
        lax.fori_loop(0, n, body, 0, unroll=8)

    @pl.when(i == 0)
    def _():
        start_gather(0, 0)

    @pl.when(i + 1 < nsteps)
    def _():
        start_gather(i + 1, 1 - slot)

    for k in range(TOP_K):
        pltpu.make_async_copy(
            y_hbm.at[pl.ds(0, n * NSUB)], ybuf.at[slot, k], sem.at[slot]).wait()

    w0 = info_ref[:, 4:5]
    w1 = info_ref[:, 5:6]
    for s in range(NSUB):
        piece = pl.ds(s, n, stride=NSUB)
        halves0 = _unpack_pair(ybuf[slot, 0, piece, :])
        halves1 = _unpack_pair(ybuf[slot, 1, piece, :])
        for half in range(2):
            c0 = (2 * s + half) * LANES
            cols = slice(c0, c0 + LANES)
            gt2 = mod_ref[:, 5 * D + c0:5 * D + c0 + LANES]
            moe = w0 * halves0[half] + w1 * halves1[half]
            o_ref[:, cols] = x1_ref[:, cols] + gt2 * moe


def _combine(x1, y, info, dest0, dest1, mod_l, S, tm):
    T, D = x1.shape
    tiles_per_batch = S // tm
    grid_spec = pltpu.PrefetchScalarGridSpec(
        num_scalar_prefetch=2,
        grid=(T // tm,),
        in_specs=[
            pl.BlockSpec(memory_space=pl.ANY),
            pl.BlockSpec((tm, D), lambda i, d0, d1: (i, 0)),
            pl.BlockSpec((tm, LANES), lambda i, d0, d1: (i, 0)),
            pl.BlockSpec((None, 1, 6 * D), lambda i, d0, d1: (i // tiles_per_batch, 0, 0)),
        ],
        out_specs=pl.BlockSpec((tm, D), lambda i, d0, d1: (i, 0)),
        scratch_shapes=[pltpu.VMEM((2, TOP_K, tm * NSUB, LANES), U32),
                        pltpu.SemaphoreType.DMA((2,))],
    )
    return pl.pallas_call(
        _combine_kernel,
        grid_spec=grid_spec,
        out_shape=jax.ShapeDtypeStruct((T, D), F32),
        compiler_params=pltpu.CompilerParams(
            dimension_semantics=("arbitrary",), vmem_limit_bytes=VMEM_LIMIT),
        name="moe_combine",
    )(dest0, dest1, y, x1, info, mod_l)


def _split_w_in(w):
    return w[:, :, :_C_SPLIT].astype(BF16), w[:, :, _C_SPLIT:].astype(BF16)


def kernel(x, c, ada_w, ada_b, norm1_g, norm2_g, w_in, attn_qn_g, attn_kn_g, attn_lambda,
           attn_subln_g, rwkv_mu, rwkv_w_up, rwkv_w0, rwkv_a_up, rwkv_a0, rwkv_g_up, rwkv_k_k,
           rwkv_k_a, rwkv_r_k, rwkv_lnx_g, rwkv_lnx_b, gla_alpha_up, gla_alpha_b, gla_norm_g,
           proj_attn, proj_rwkv, proj_gla, w_out, router_grp_w, router_grp_b, router_exp_w,
           router_exp_b, exp_w_gate, exp_w_up, exp_w_down):
    B, S, D = x.shape
    T = B * S
    L = ada_w.shape[0]
    tm = 256
    tm_mm = 512
    mod = _adaln(c, ada_w, ada_b).reshape(L, B, 1, 6 * D)
    x2 = x.reshape(T, D)
    xin = None
    w_in_parts = _split_w_in(w_in)
    for l in range(L):
        lambda_init = 0.8 - 0.6 * math.exp(-0.3 * l)
        p_attn, p_rwkv, p_gla, p_gate = _inproj(x2, mod[l], norm1_g[l], w_in_parts, l, S, tm_mm)
        o_a = _attention(p_attn, attn_qn_g[l], attn_kn_g[l], attn_lambda[l], attn_subln_g[l],
                         lambda_init, B, S)
        o_r = _rwkv(p_rwkv, rwkv_mu[l], rwkv_w_up[l], rwkv_w0[l], rwkv_a_up[l], rwkv_a0[l],
                    rwkv_g_up[l], rwkv_k_k[l], rwkv_k_a[l], rwkv_r_k[l], rwkv_lnx_g[l],
                    rwkv_lnx_b[l], B, S)
        o_g = _gla(p_gla, gla_alpha_up[l], gla_alpha_b[l], gla_norm_g[l], B, S)
        n_r = N_GROUPS + N_EXPERTS
        router_w = jnp.concatenate(
            [router_grp_w[l], router_exp_w[l], jnp.zeros((D, LANES - n_r), F32)], axis=1)
        rw_hi = router_w.astype(BF16)
        router_w = jnp.stack([rw_hi, (router_w - rw_hi.astype(F32)).astype(BF16)])
        router_b = jnp.concatenate(
            [router_grp_b[l], router_exp_b[l], jnp.zeros((LANES - n_r,), F32)]).reshape(1, LANES)
        x1, h2, info, cnt = _merge(
            x2, o_a, o_r, o_g, p_gate, proj_attn[l].astype(BF16), proj_rwkv[l].astype(BF16),
            proj_gla[l].astype(BF16), w_out[l].astype(BF16), mod[l], norm2_g[l],
            router_w, router_b, S, tm_mm)
        dest0, dest1, blk_exp, n_used, blk_grp, blk_nxt, n_rows = _route_tables(info, cnt)
        if xin is None:
            xin = jnp.zeros((n_rows * NSUB, LANES), U32)
        xin = _dispatch(h2, dest0, dest1, xin, tm)
        y = _moe(xin, blk_exp, n_used, blk_grp, blk_nxt, exp_w_gate, exp_w_up, exp_w_down, l)
        x2 = _combine(x1, y, info, dest0, dest1, mod[l], S, tm)
    return x2.reshape(B, S, D)
```
